```python
import jax, jax.numpy as jnp
from jax import lax
import numpy as np


D_MODEL = 1024
BATCH = 4
SEQ = 4096
DEPTH = 1

EPS = 1e-6
D_RNN = D_MODEL
RNN_BLOCKS = 8
RNN_BW = D_RNN // RNN_BLOCKS
CONV_W = 4
LRU_C = 8.0
N_HEADS = 8
HEAD_DIM = 128
D_ATT = N_HEADS * HEAD_DIM
MOBA_BLOCK = 256
MOBA_TOPK = 3
Q_CHUNK = 16
ROPE_DIMS = HEAD_DIM // 4
ROPE_THETA = 500000.0
N_BRANCH = 2
D_BRANCH = D_MODEL
D_IN = 2 * D_RNN + 3 * D_ATT + N_BRANCH * D_MODEL
N_EXPERTS = 32
TOP_K = 4
D_FF = D_MODEL
SWIGLU_LIMIT = 7.0
SWIGLU_ALPHA = 1.702
EXPERT_ROWS = 256

kernel_name = 'hybrid_rglru_moba_moe_adaln'


def rms_norm(x, g):
    x32 = x.astype(jnp.float32)
    y = x32 * lax.rsqrt(jnp.mean(x32 * x32, axis=-1, keepdims=True) + EPS)
    return (y * g.astype(jnp.float32)).astype(x.dtype)


def partial_rope(x, positions):
    half = ROPE_DIMS // 2
    freqs = ROPE_THETA ** (-jnp.arange(half, dtype=jnp.float32) / half)
    ang = positions.astype(jnp.float32)[:, :, None] * freqs
    cos = jnp.cos(ang)[:, :, None, :]
    sin = jnp.sin(ang)[:, :, None, :]
    x1 = x[..., :half].astype(jnp.float32)
    x2 = x[..., half:ROPE_DIMS].astype(jnp.float32)
    rot = jnp.concatenate([x1 * cos - x2 * sin, x2 * cos + x1 * sin], axis=-1).astype(x.dtype)
    return jnp.concatenate([rot, x[..., ROPE_DIMS:]], axis=-1)


def _lin_combine(left, right):
    a_l, b_l = left
    a_r, b_r = right
    return a_l * a_r, a_r * b_l + b_r


def recurrent_branch(xr, gr, conv_w, conv_b, w_rg_a, b_rg_a, w_rg_x, b_rg_x, lru_lambda):
    B, S, _ = xr.shape
    xp = jnp.pad(xr, ((0, 0), (CONV_W - 1, 0), (0, 0)))
    xc = conv_b + conv_w[0] * xp[:, CONV_W - 1:CONV_W - 1 + S]
    for i in range(1, CONV_W):
        xc = xc + conv_w[i] * xp[:, CONV_W - 1 - i:CONV_W - 1 - i + S]
    xb = xc.reshape(B, S, RNN_BLOCKS, RNN_BW)
    r = jax.nn.sigmoid(jnp.einsum('bsnw,nwv->bsnv', xb, w_rg_a).reshape(B, S, D_RNN) + b_rg_a)
    ig = jax.nn.sigmoid(jnp.einsum('bsnw,nwv->bsnv', xb, w_rg_x).reshape(B, S, D_RNN) + b_rg_x)
    log_a = -LRU_C * r.astype(jnp.float32) * jax.nn.softplus(-lru_lambda.astype(jnp.float32))
    a = jnp.exp(log_a)
    mult = jnp.sqrt(-jnp.expm1(2.0 * log_a))
    mult = mult.at[:, 0].set(1.0)
    u = mult * (ig * xc).astype(jnp.float32)
    _, h = lax.associative_scan(_lin_combine, (a, u), axis=1)
    return h.astype(xr.dtype) * jax.nn.gelu(gr)


def moba_attention(q, k, v):
    B, H, S, dh = q.shape
    nb = -(-S // MOBA_BLOCK)
    pad = nb * MOBA_BLOCK - S
    kp = jnp.pad(k, ((0, 0), (0, 0), (0, pad), (0, 0))).reshape(B, H, nb, MOBA_BLOCK, dh)
    vp = jnp.pad(v, ((0, 0), (0, 0), (0, pad), (0, 0))).reshape(B, H, nb, MOBA_BLOCK, dh)
    k_mean = jnp.mean(kp.astype(jnp.float32), axis=3)
    q_blk = jnp.arange(S) // MOBA_BLOCK
    gate = jnp.einsum('bhsd,bhnd->bhsn', q.astype(jnp.float32), k_mean)
    past = jnp.arange(nb)[None, :] < q_blk[:, None]
    gate = jnp.where(past, gate, -jnp.inf)
    topk = min(MOBA_TOPK, nb)
    _, sel = lax.top_k(gate, topk)
    sel_ok = jnp.arange(topk)[None, :] < jnp.minimum(q_blk, MOBA_TOPK)[:, None]
    nc = S // Q_CHUNK

    def to_chunks(t):
        t = t.reshape(t.shape[:2] + (nc, Q_CHUNK) + t.shape[3:])
        return jnp.moveaxis(t, 2, 0)

    b_ix = jnp.arange(B)[:, None, None, None]
    h_ix = jnp.arange(H)[None, :, None, None]
    scale = dh ** -0.5

    def chunk_attn(args):
        ci, qc, sc, okc = args
        start = ci * Q_CHUNK
        blk = start // MOBA_BLOCK
        k_own = lax.dynamic_index_in_dim(kp, blk, axis=2, keepdims=False)
        v_own = lax.dynamic_index_in_dim(vp, blk, axis=2, keepdims=False)
        k_sel = kp[b_ix, h_ix, sc]
        v_sel = vp[b_ix, h_ix, sc]
        q_pos = start + jnp.arange(Q_CHUNK)
        k_pos = blk * MOBA_BLOCK + jnp.arange(MOBA_BLOCK)
        s_own = jnp.einsum('bhqd,bhkd->bhqk', qc, k_own).astype(jnp.float32) * scale
        s_own = jnp.where(k_pos[None, :] <= q_pos[:, None], s_own, -jnp.inf)
        s_sel = jnp.einsum('bhqd,bhqjkd->bhqjk', qc, k_sel).astype(jnp.float32) * scale
        s_sel = jnp.where(okc[:, :, None], s_sel, -jnp.inf).reshape(B, H, Q_CHUNK, topk * MOBA_BLOCK)
        p = jax.nn.softmax(jnp.concatenate([s_own, s_sel], axis=-1), axis=-1).astype(v.dtype)
        p_own = p[..., :MOBA_BLOCK]
        p_sel = p[..., MOBA_BLOCK:].reshape(B, H, Q_CHUNK, topk, MOBA_BLOCK)
        return (jnp.einsum('bhqk,bhkd->bhqd', p_own, v_own)
                + jnp.einsum('bhqjk,bhqjkd->bhqd', p_sel, v_sel))

    out = lax.map(chunk_attn, (jnp.arange(nc), to_chunks(q), to_chunks(sel),
                               sel_ok.reshape(nc, Q_CHUNK, topk)))
    out = jnp.moveaxis(out, 0, 2).reshape(B, H, S, dh)
    return out.transpose(0, 2, 1, 3).reshape(B, S, H * dh)


def moe_ffn(h, w_router, b_router, w_up, b_up, w_down, b_down):
    B, S, D = h.shape
    N = B * S
    xt = h.reshape(N, D)
    logits = (xt @ w_router + b_router).astype(jnp.float32)
    top_val, top_idx = lax.top_k(logits, TOP_K)
    gates = jax.nn.softmax(top_val, axis=-1)
    A = N * TOP_K
    e_flat = top_idx.reshape(A)
    tok_flat = jnp.arange(A, dtype=jnp.int32) // TOP_K
    g_flat = gates.reshape(A)
    order = jnp.argsort(e_flat)
    e_sorted = e_flat[order]
    counts = jnp.zeros((N_EXPERTS,), jnp.int32).at[e_flat].add(1)
    padded = (counts + EXPERT_ROWS - 1) // EXPERT_ROWS * EXPERT_ROWS
    start = jnp.cumsum(counts) - counts
    pad_end = jnp.cumsum(padded)
    pad_start = pad_end - padded
    dest = pad_start[e_sorted] + jnp.arange(A, dtype=jnp.int32) - start[e_sorted]
    n_blocks = -(-A // EXPERT_ROWS) + N_EXPERTS
    R = n_blocks * EXPERT_ROWS
    row_tok = jnp.full((R,), N, jnp.int32).at[dest].set(tok_flat[order])
    row_gate = jnp.zeros((R,), jnp.float32).at[dest].set(g_flat[order])
    block_start = jnp.arange(n_blocks, dtype=jnp.int32) * EXPERT_ROWS
    block_exp = jnp.minimum(jnp.sum(block_start[:, None] >= pad_end[None, :], axis=1), N_EXPERTS - 1)
    x_pad = jnp.concatenate([xt, jnp.zeros((1, D), xt.dtype)], axis=0)
    x_rows = x_pad[row_tok].reshape(n_blocks, EXPERT_ROWS, D)

    def expert_block(args):
        xb, e = args
        hc = xb @ w_up[e] + b_up[e]
        g = jnp.minimum(hc[:, :D_FF], SWIGLU_LIMIT)
        lin = jnp.clip(hc[:, D_FF:], -SWIGLU_LIMIT, SWIGLU_LIMIT)
        act = (lin + 1.0) * g * jax.nn.sigmoid(SWIGLU_ALPHA * g)
        return act @ w_down[e] + b_down[e]

    y_rows = lax.map(expert_block, (x_rows, block_exp)).reshape(R, D)
    y_rows = y_rows * row_gate[:, None].astype(y_rows.dtype)
    y = jax.ops.segment_sum(y_rows, row_tok, num_segments=N + 1)[:N]
    return y.reshape(B, S, D)


def setup_inputs(seed: int = 0) -> dict:
    key = jax.random.key(seed)
    ks = jax.random.split(key, 32)
    f32 = jnp.float32
    L = DEPTH

    def nrm(k, shape, scale):
        return jax.random.normal(k, shape, f32) * scale

    u = jax.random.uniform(ks[13], (L, D_RNN), f32, 0.9, 0.999)
    a0 = u ** (1.0 / LRU_C)
    return {
        'x': nrm(ks[0], (BATCH, SEQ, D_MODEL), 1.0),
        'c': nrm(ks[1], (BATCH, D_MODEL), 1.0),
        'positions': jnp.tile(jnp.arange(SEQ, dtype=jnp.int32)[None, :], (BATCH, 1)),
        'ada_w': nrm(ks[2], (L, D_MODEL, 6 * D_MODEL), 0.5 * D_MODEL ** -0.5),
        'ada_b': nrm(ks[3], (L, 6 * D_MODEL), 0.01),
        'norm1_g': 1.0 + nrm(ks[4], (L, D_MODEL), 0.02),
        'norm2_g': 1.0 + nrm(ks[5], (L, D_MODEL), 0.02),
        'w_in': nrm(ks[6], (L, D_MODEL, D_IN), D_MODEL ** -0.5),
        'conv_w': nrm(ks[7], (L, CONV_W, D_RNN), CONV_W ** -0.5),
        'conv_b': nrm(ks[8], (L, D_RNN), 0.01),
        'w_rg_a': nrm(ks[9], (L, RNN_BLOCKS, RNN_BW, RNN_BW), RNN_BW ** -0.5),
        'b_rg_a': nrm(ks[10], (L, D_RNN), 0.01),
        'w_rg_x': nrm(ks[11], (L, RNN_BLOCKS, RNN_BW, RNN_BW), RNN_BW ** -0.5),
        'b_rg_x': nrm(ks[12], (L, D_RNN), 0.01),
        'lru_lambda': jnp.log(a0) - jnp.log1p(-a0),
        'q_norm_g': 1.0 + nrm(ks[14], (L, HEAD_DIM), 0.02),
        'k_norm_g': 1.0 + nrm(ks[15], (L, HEAD_DIM), 0.02),
        'b_gate': nrm(ks[16], (L, N_BRANCH, D_MODEL), 0.01),
        'w_branch': nrm(ks[17], (L, N_BRANCH, D_BRANCH, D_MODEL), D_BRANCH ** -0.5),
        'w_out': nrm(ks[18], (L, D_MODEL, D_MODEL), D_MODEL ** -0.5),
        'w_router': nrm(ks[19], (L, D_MODEL, N_EXPERTS), D_MODEL ** -0.5),
        'b_router': nrm(ks[20], (L, N_EXPERTS), 0.01),
        'w_up': nrm(ks[21], (L, N_EXPERTS, D_MODEL, 2 * D_FF), D_MODEL ** -0.5),
        'b_up': nrm(ks[22], (L, N_EXPERTS, 2 * D_FF), 0.01),
        'w_down': nrm(ks[23], (L, N_EXPERTS, D_FF, D_MODEL), D_FF ** -0.5),
        'b_down': nrm(ks[24], (L, N_EXPERTS, D_MODEL), 0.01),
    }


def reference(x, c, positions, ada_w, ada_b, norm1_g, norm2_g, w_in, conv_w, conv_b,
              w_rg_a, b_rg_a, w_rg_x, b_rg_x, lru_lambda, q_norm_g, k_norm_g, b_gate,
              w_branch, w_out, w_router, b_router, w_up, b_up, w_down, b_down):
    B, S, D = x.shape
    split_at = [D_RNN, 2 * D_RNN, 2 * D_RNN + D_ATT, 2 * D_RNN + 2 * D_ATT, 2 * D_RNN + 3 * D_ATT]
    for l in range(DEPTH):
        mod = jax.nn.silu(c) @ ada_w[l] + ada_b[l]
        sh1, sc1, g1, sh2, sc2, g2 = jnp.split(mod[:, None, :], 6, axis=-1)
        h = rms_norm(x, norm1_g[l]) * (1.0 + sc1) + sh1
        proj = h @ w_in[l]
        xr, gr, q, k, v, gl = jnp.split(proj, split_at, axis=-1)
        y_rnn = recurrent_branch(xr, gr, conv_w[l], conv_b[l], w_rg_a[l], b_rg_a[l],
                                 w_rg_x[l], b_rg_x[l], lru_lambda[l])
        q = partial_rope(rms_norm(q.reshape(B, S, N_HEADS, HEAD_DIM), q_norm_g[l]), positions)
        k = partial_rope(rms_norm(k.reshape(B, S, N_HEADS, HEAD_DIM), k_norm_g[l]), positions)
        v = v.reshape(B, S, N_HEADS, HEAD_DIM)
        y_att = moba_attention(q.transpose(0, 2, 1, 3), k.transpose(0, 2, 1, 3),
                               v.transpose(0, 2, 1, 3))
        branches = jnp.stack([y_rnn, y_att], axis=2)
        z = jnp.einsum('bsgi,gid->bsgd', branches, w_branch[l])
        gates = jax.nn.sigmoid(gl.reshape(B, S, N_BRANCH, D_MODEL) + b_gate[l])
        mixed = jnp.sum(gates * z, axis=2) @ w_out[l]
        x = x + g1 * mixed
        h2 = rms_norm(x, norm2_g[l]) * (1.0 + sc2) + sh2
        x = x + g2 * moe_ffn(h2, w_router[l], b_router[l], w_up[l], b_up[l], w_down[l], b_down[l])
    return x
```

```python
import functools

import jax
import jax.numpy as jnp
from jax import lax
from jax.experimental import pallas as pl
from jax.experimental.pallas import tpu as pltpu

F32 = jnp.float32
BF16 = jnp.bfloat16
HIGHEST = lax.Precision.HIGHEST

EPS = 1e-6
LANES = 128
N_HEADS = 8
HEAD_DIM = 128
RNN_BLOCKS = 8
RNN_BW = 128
CONV_W = 4
LRU_C = 8.0
MOBA_BLOCK = 256
MOBA_TOPK = 3
ROPE_DIMS = HEAD_DIM // 4
ROPE_THETA = 500000.0
N_EXPERTS = 32
TOP_K = 4
SWIGLU_LIMIT = 7.0
SWIGLU_ALPHA = 1.702
EXPERT_ROWS = 256
NEG_BIG = -1e30

VMEM_LIMIT = 56 * 1024 * 1024


def _params(*sem):
    return pltpu.CompilerParams(dimension_semantics=sem, vmem_limit_bytes=VMEM_LIMIT)


def _adaln_body(c_ref, w_ref, b_ref, o_ref):
    cs = c_ref[...]
    cs = cs * jax.nn.sigmoid(cs)
    o_ref[...] = jnp.dot(cs, w_ref[...], preferred_element_type=F32, precision=HIGHEST) + b_ref[...]


def _adaln(c, ada_w, ada_b):
    B, D = c.shape
    W = ada_w.shape[1]
    cpad = jnp.zeros((8, D), F32).at[:B].set(c)
    tn = 1024
    mod = pl.pallas_call(
        _adaln_body,
        grid=(W // tn,),
        in_specs=[pl.BlockSpec((8, D), lambda j: (0, 0)),
                  pl.BlockSpec((D, tn), lambda j: (0, j)),
                  pl.BlockSpec((1, tn), lambda j: (0, j))],
        out_specs=pl.BlockSpec((8, tn), lambda j: (0, j)),
        out_shape=jax.ShapeDtypeStruct((8, W), F32),
        compiler_params=_params("parallel"),
        name="adaln",
    )(cpad, ada_w, ada_b.reshape(1, W))
    return mod[:B]


def _rope_body(pos_ref, freq_ref, c_ref, sa_ref, sb_ref):
    ang = pos_ref[...].astype(F32) * freq_ref[...]
    lane = lax.broadcasted_iota(jnp.int32, ang.shape, 1)
    half = ROPE_DIMS // 2
    s = jnp.sin(ang)
    c_ref[...] = jnp.cos(ang)
    sa_ref[...] = jnp.where(lane < half, -s, 0.0)
    sb_ref[...] = jnp.where((lane >= half) & (lane < ROPE_DIMS), s, 0.0)


def _rope_tables(positions):
    n = positions.size
    half = ROPE_DIMS // 2
    freqs = ROPE_THETA ** (-jnp.arange(half, dtype=F32) / half)
    freq_lane = jnp.zeros((1, LANES), F32).at[0, :ROPE_DIMS].set(jnp.concatenate([freqs, freqs]))
    tm = 1024
    tab = jax.ShapeDtypeStruct((n, LANES), F32)
    return pl.pallas_call(
        _rope_body,
        grid=(n // tm,),
        in_specs=[pl.BlockSpec((tm, 1), lambda i: (i, 0)),
                  pl.BlockSpec((1, LANES), lambda i: (0, 0))],
        out_specs=[pl.BlockSpec((tm, LANES), lambda i: (i, 0))] * 3,
        out_shape=[tab, tab, tab],
        compiler_params=_params("parallel"),
        name="rope_tab",
    )(positions.reshape(n, 1), freq_lane)


def _norm_mod(x, g, sh, sc):
    ms = jnp.mean(x * x, axis=-1, keepdims=True)
    y = x * lax.rsqrt(ms + EPS)
    return (y * g) * (1.0 + sc) + sh


def _norm_mod_body(x_ref, mod_ref, g_ref, o_ref):
    o_ref[...] = _norm_mod(x_ref[...], g_ref[...], mod_ref[0, 0:1, :], mod_ref[0, 1:2, :]).astype(o_ref.dtype)


def _norm_mod_call(x2, mod3, g, rows_per_batch):
    n, D = x2.shape
    tm = 512
    tpb = rows_per_batch // tm
    return pl.pallas_call(
        _norm_mod_body,
        grid=(n // tm,),
        in_specs=[pl.BlockSpec((tm, D), lambda i: (i, 0)),
                  pl.BlockSpec((1, 6, D), lambda i: (i // tpb, 0, 0)),
                  pl.BlockSpec((1, D), lambda i: (0, 0))],
        out_specs=pl.BlockSpec((tm, D), lambda i: (i, 0)),
        out_shape=jax.ShapeDtypeStruct((n, D), BF16),
        compiler_params=_params("parallel"),
        name="norm_mod",
    )(x2, mod3, g.reshape(1, D))


def _gelu_tanh(x):
    return 0.5 * x * (1.0 + jnp.tanh(0.7978845608028654 * (x + 0.044715 * (x * x * x))))


def _proj_body(kind, h_ref, w_ref, *rest):
    acc = jnp.dot(h_ref[...], w_ref[...], preferred_element_type=F32)
    if kind == "plain":
        (o_ref,) = rest
        o_ref[...] = acc.astype(o_ref.dtype)
    elif kind == "gelu":
        (o_ref,) = rest
        o_ref[...] = _gelu_tanh(acc).astype(o_ref.dtype)
    elif kind == "gate":
        b_ref, o_ref = rest
        o_ref[...] = jax.nn.sigmoid(acc + b_ref[...]).astype(o_ref.dtype)
    else:
        g_ref, c_ref, sa_ref, sb_ref, o_ref = rest
        cos, sa, sb = c_ref[...], sa_ref[...], sb_ref[...]
        half = ROPE_DIMS // 2
        for hh in range(acc.shape[1] // HEAD_DIM):
            sl = slice(hh * HEAD_DIM, (hh + 1) * HEAD_DIM)
            seg = acc[:, sl]
            ms = jnp.mean(seg * seg, axis=-1, keepdims=True)
            y = seg * lax.rsqrt(ms + EPS) * g_ref[:, sl]
            up = pltpu.roll(y, HEAD_DIM - half, axis=1)
            dn = pltpu.roll(y, half, axis=1)
            o_ref[:, sl] = (y * cos + up * sa + dn * sb).astype(o_ref.dtype)


def _proj(kind, h, w, extras, out_dtype, name):
    n, D = h.shape
    W = w.shape[1]
    tm, tn = 512, 1024
    in_specs = [pl.BlockSpec((tm, D), lambda j, i: (i, 0)),
                pl.BlockSpec((D, tn), lambda j, i: (0, j))]
    args = [h, w]
    for arr in extras:
        if arr.shape[0] == 1:
            in_specs.append(pl.BlockSpec((1, tn), lambda j, i: (0, j)))
        else:
            in_specs.append(pl.BlockSpec((tm, LANES), lambda j, i: (i, 0)))
        args.append(arr)
    return pl.pallas_call(
        functools.partial(_proj_body, kind),
        grid=(W // tn, n // tm),
        in_specs=in_specs,
        out_specs=pl.BlockSpec((tm, tn), lambda j, i: (i, j)),
        out_shape=jax.ShapeDtypeStruct((n, W), out_dtype),
        compiler_params=_params("parallel", "parallel"),
        name=name,
    )(*args)


def _rglru_body(xr_ref, gg_ref, cw_ref, cb_ref, wa_ref, ba_ref, wx_ref, bx_ref, lam_ref,
                o_ref, xbuf, hcar, a_s, u_s):
    s = pl.program_id(1)
    ts, D = xr_ref.shape

    @pl.when(s == 0)
    def _():
        xbuf[0:8, :] = jnp.zeros((8, D), F32)
        hcar[...] = jnp.zeros_like(hcar)

    @pl.when(s > 0)
    def _():
        xbuf[0:8, :] = xbuf[ts:ts + 8, :]

    xbuf[8:8 + ts, :] = xr_ref[...]
    xc = cb_ref[...] + cw_ref[0:1, :] * xbuf[8:8 + ts, :]
    for i in range(1, CONV_W):
        xc = xc + cw_ref[i:i + 1, :] * xbuf[8 - i:8 - i + ts, :]

    ra, rx = [], []
    for n in range(RNN_BLOCKS):
        xb = xc[:, n * RNN_BW:(n + 1) * RNN_BW].astype(BF16)
        ra.append(jnp.dot(xb, wa_ref[n], preferred_element_type=F32))
        rx.append(jnp.dot(xb, wx_ref[n], preferred_element_type=F32))
    r = jax.nn.sigmoid(jnp.concatenate(ra, axis=1) + ba_ref[...])
    ig = jax.nn.sigmoid(jnp.concatenate(rx, axis=1) + bx_ref[...])

    z = -lam_ref[...]
    softplus = jnp.maximum(z, 0.0) + jnp.log1p(jnp.exp(-jnp.abs(z)))
    log_a = -LRU_C * r * softplus
    a = jnp.exp(log_a)
    mult = jnp.sqrt(1.0 - a * a)
    row = lax.broadcasted_iota(jnp.int32, (ts, D), 0)
    mult = jnp.where((row == 0) & (s == 0), 1.0, mult)
    u = mult * (ig * xc)

    rm = row & 7
    for d in (1, 2, 4):
        keep = rm >= d
        a_sh = pltpu.roll(a, d, axis=0)
        u_sh = pltpu.roll(u, d, axis=0)
        u = jnp.where(keep, a * u_sh + u, u)
        a = jnp.where(keep, a * a_sh, a)
    a_s[...] = a
    u_s[...] = u

    def group(g, h):
        r0 = pl.multiple_of(g * 8, 8)
        hg = u_s[pl.ds(r0, 8), :] + a_s[pl.ds(r0, 8), :] * h
        u_s[pl.ds(r0, 8), :] = hg
        return hg[7:8, :]

    hcar[...] = lax.fori_loop(0, ts // 8, group, hcar[...])
    o_ref[...] = (u_s[...] * gg_ref[...]).astype(o_ref.dtype)


def _rglru(xr, gg, conv_w, conv_b, w_a, b_a, w_x, b_x, lam, B, S):
    n, D = xr.shape
    ts = 256
    spb = S // ts
    row = lambda b, s: (b * spb + s, 0)
    vec = lambda b, s: (0, 0)
    return pl.pallas_call(
        _rglru_body,
        grid=(B, spb),
        in_specs=[pl.BlockSpec((ts, D), row),
                  pl.BlockSpec((ts, D), row),
                  pl.BlockSpec((CONV_W, D), vec),
                  pl.BlockSpec((1, D), vec),
                  pl.BlockSpec((RNN_BLOCKS, RNN_BW, RNN_BW), lambda b, s: (0, 0, 0)),
                  pl.BlockSpec((1, D), vec),
                  pl.BlockSpec((RNN_BLOCKS, RNN_BW, RNN_BW), lambda b, s: (0, 0, 0)),
                  pl.BlockSpec((1, D), vec),
                  pl.BlockSpec((1, D), vec)],
        out_specs=pl.BlockSpec((ts, D), row),
        out_shape=jax.ShapeDtypeStruct((n, D), BF16),
        scratch_shapes=[pltpu.VMEM((ts + 8, D), F32), pltpu.VMEM((1, D), F32),
                        pltpu.VMEM((ts, D), F32), pltpu.VMEM((ts, D), F32)],
        compiler_params=_params("arbitrary", "arbitrary"),
        name="rglru",
    )(xr, gg, conv_w, conv_b.reshape(1, D), w_a.astype(BF16), b_a.reshape(1, D),
      w_x.astype(BF16), b_x.reshape(1, D), lam.reshape(1, D))


def _moba_body(q_ref, k_ref, v_ref, o_ref, kbf, kmean):
    qb = pl.program_id(2)
    nb = k_ref.shape[0] // MOBA_BLOCK
    bs = MOBA_BLOCK

    @pl.when(qb == 0)
    def _():
        kbf[...] = k_ref[...].astype(BF16)
        kmean[...] = jnp.zeros_like(kmean)
        for j in range(nb):
            kmean[j:j + 1, :] = jnp.mean(k_ref[j * bs:(j + 1) * bs, :], axis=0, keepdims=True)

    q = q_ref[...]
    qbf = q.astype(BF16)
    scale = HEAD_DIM ** -0.5
    nt = (((1,), (1,)), ((), ()))

    gate = lax.dot_general(q, kmean[...], nt, preferred_element_type=F32, precision=HIGHEST)
    lane = lax.broadcasted_iota(jnp.int32, gate.shape, 1)
    lane_f = lane.astype(F32)
    past = lane < qb
    g = jnp.where(past, gate, -jnp.inf)
    sel = jnp.zeros(gate.shape, F32)
    for _ in range(MOBA_TOPK):
        m = jnp.max(g, axis=1, keepdims=True)
        idx = jnp.min(jnp.where(g == m, lane_f, float(LANES)), axis=1, keepdims=True)
        pick = (lane_f == idx) & past
        sel = jnp.where(pick, 1.0, sel)
        g = jnp.where(pick, -jnp.inf, g)

    r0 = pl.multiple_of(qb * bs, bs)
    s = lax.dot_general(qbf, kbf[pl.ds(r0, bs), :], nt, preferred_element_type=F32) * scale
    rr = lax.broadcasted_iota(jnp.int32, s.shape, 0)
    cc = lax.broadcasted_iota(jnp.int32, s.shape, 1)
    s = jnp.where(cc <= rr, s, NEG_BIG)
    m0 = jnp.max(s, axis=1, keepdims=True)
    p = jnp.exp(s - m0)
    l0 = jnp.sum(p, axis=1, keepdims=True)
    acc0 = jnp.dot(p.astype(BF16), v_ref[pl.ds(r0, bs), :], preferred_element_type=F32)

    def past_block(j, carry):
        m, l, acc = carry
        c0 = pl.multiple_of(j * bs, bs)
        sj = lax.dot_general(qbf, kbf[pl.ds(c0, bs), :], nt, preferred_element_type=F32) * scale
        chosen = jnp.sum(jnp.where(lane == j, sel, 0.0), axis=1, keepdims=True) > 0.5
        sj = jnp.where(chosen, sj, NEG_BIG)
        m_new = jnp.maximum(m, jnp.max(sj, axis=1, keepdims=True))
        alpha = jnp.exp(m - m_new)
        pj = jnp.exp(sj - m_new)
        l = alpha * l + jnp.sum(pj, axis=1, keepdims=True)
        acc = alpha * acc + jnp.dot(pj.astype(BF16), v_ref[pl.ds(c0, bs), :], preferred_element_type=F32)
        return m_new, l, acc

    _, l, acc = lax.fori_loop(0, qb, past_block, (m0, l0, acc0))
    o_ref[...] = (acc / l).astype(o_ref.dtype)


def _moba(qk, v, B, S):
    n, D = v.shape
    nq = S // MOBA_BLOCK
    return pl.pallas_call(
        _moba_body,
        grid=(B, N_HEADS, nq),
        in_specs=[pl.BlockSpec((MOBA_BLOCK, HEAD_DIM), lambda b, h, i: (b * nq + i, h)),
                  pl.BlockSpec((S, HEAD_DIM), lambda b, h, i: (b, N_HEADS + h)),
                  pl.BlockSpec((S, HEAD_DIM), lambda b, h, i: (b, h))],
        out_specs=pl.BlockSpec((MOBA_BLOCK, HEAD_DIM), lambda b, h, i: (b * nq + i, h)),
        out_shape=jax.ShapeDtypeStruct((n, D), BF16),
        scratch_shapes=[pltpu.VMEM((S, HEAD_DIM), BF16), pltpu.VMEM((LANES, HEAD_DIM), F32)],
        compiler_params=_params("parallel", "parallel", "arbitrary"),
        name="moba",
    )(qk, qk, v)


def _merge_body(yr_ref, ya_ref, gr_ref, ga_ref, x_ref, mod_ref, wb_ref, wo_ref, g2_ref, wr_ref, br_ref,
                x1_ref, h2_ref, topi_ref, gate_ref, rank_ref, cnt_ref, carry):
    i = pl.program_id(0)
    tm = x_ref.shape[0]

    @pl.when(i == 0)
    def _():
        carry[...] = jnp.zeros_like(carry)

    zr = jnp.dot(yr_ref[...], wb_ref[0], preferred_element_type=F32)
    za = jnp.dot(ya_ref[...], wb_ref[1], preferred_element_type=F32)
    mix = (gr_ref[...] * zr + ga_ref[...] * za).astype(BF16)
    mixed = jnp.dot(mix, wo_ref[...], preferred_element_type=F32)
    x1 = x_ref[...] + mod_ref[0, 2:3, :] * mixed
    x1_ref[...] = x1
    h2 = _norm_mod(x1, g2_ref[...], mod_ref[0, 3:4, :], mod_ref[0, 4:5, :])
    h2_ref[...] = h2

    logits = jnp.dot(h2, wr_ref[...], preferred_element_type=F32, precision=HIGHEST) + br_ref[...]
    lane = lax.broadcasted_iota(jnp.int32, logits.shape, 1)
    lane_f = lane.astype(F32)
    lg = jnp.where(lane < N_EXPERTS, logits, -jnp.inf)
    vals, idxs = [], []
    onehot = jnp.zeros(logits.shape, F32)
    for _ in range(TOP_K):
        m = jnp.max(lg, axis=1, keepdims=True)
        idx = jnp.min(jnp.where(lg == m, lane_f, float(LANES)), axis=1, keepdims=True)
        pick = lane_f == idx
        onehot = jnp.where(pick, 1.0, onehot)
        lg = jnp.where(pick, -jnp.inf, lg)
        vals.append(m)
        idxs.append(idx)
    es = [jnp.exp(v - vals[0]) for v in vals]
    den = es[0] + es[1] + es[2] + es[3]

    rr = lax.broadcasted_iota(jnp.int32, (tm, tm), 0)
    cc = lax.broadcasted_iota(jnp.int32, (tm, tm), 1)
    tri = jnp.where(cc < rr, 1.0, 0.0).astype(BF16)
    before = jnp.dot(tri, onehot.astype(BF16), preferred_element_type=F32) + carry[...]

    topi = jnp.zeros(logits.shape, F32)
    gates = jnp.zeros(logits.shape, F32)
    rank = jnp.zeros(logits.shape, F32)
    for k in range(TOP_K):
        rk = jnp.sum(jnp.where(lane_f == idxs[k], before, 0.0), axis=1, keepdims=True)
        topi = jnp.where(lane == k, idxs[k], topi)
        gates = jnp.where(lane == k, es[k] / den, gates)
        rank = jnp.where(lane == k, rk, rank)
    topi_ref[...] = topi.astype(jnp.int32)
    gate_ref[...] = gates
    rank_ref[...] = rank.astype(jnp.int32)
    total = carry[...] + jnp.sum(onehot, axis=0, keepdims=True)
    carry[...] = total
    cnt_ref[...] = jnp.broadcast_to(total, cnt_ref.shape).astype(jnp.int32)


def _merge(y_rnn, y_att, gl, x2, mod3, w_branch, w_out, norm2_g, w_router, b_router, rows_per_batch):
    n, D = x2.shape
    tm = 256
    tpb = rows_per_batch // tm
    wr = jnp.zeros((D, LANES), F32).at[:, :N_EXPERTS].set(w_router)
    br = jnp.zeros((1, LANES), F32).at[0, :N_EXPERTS].set(b_router)
    row = lambda i: (i, 0)
    fixed = lambda i: (0, 0)
    wide = jax.ShapeDtypeStruct((n, D), F32)
    narrow_i = jax.ShapeDtypeStruct((n, LANES), jnp.int32)
    return pl.pallas_call(
        _merge_body,
        grid=(n // tm,),
        in_specs=[pl.BlockSpec((tm, D), row),
                  pl.BlockSpec((tm, D), row),
                  pl.BlockSpec((tm, D), lambda i: (i, 0)),
                  pl.BlockSpec((tm, D), lambda i: (i, 1)),
                  pl.BlockSpec((tm, D), row),
                  pl.BlockSpec((1, 6, D), lambda i: (i // tpb, 0, 0)),
                  pl.BlockSpec((2, D, D), lambda i: (0, 0, 0)),
                  pl.BlockSpec((D, D), fixed),
                  pl.BlockSpec((1, D), fixed),
                  pl.BlockSpec((D, LANES), fixed),
                  pl.BlockSpec((1, LANES), fixed)],
        out_specs=[pl.BlockSpec((tm, D), row),
                   pl.BlockSpec((tm, D), row),
                   pl.BlockSpec((tm, LANES), row),
                   pl.BlockSpec((tm, LANES), row),
                   pl.BlockSpec((tm, LANES), row),
                   pl.BlockSpec((8, LANES), fixed)],
        out_shape=[wide, wide, narrow_i, jax.ShapeDtypeStruct((n, LANES), F32), narrow_i,
                   jax.ShapeDtypeStruct((8, LANES), jnp.int32)],
        scratch_shapes=[pltpu.VMEM((1, LANES), F32)],
        compiler_params=_params("arbitrary"),
        name="merge",
    )(y_rnn, y_att, gl, gl, x2, mod3, w_branch.astype(BF16), w_out.astype(BF16),
      norm2_g.reshape(1, D), wr, br)


DISPATCH_CHUNK = 512


def _dispatch_body(dest_ref, zblk_ref, src_hbm, dst_hbm, zbuf, sems, zsem):
    ch = DISPATCH_CHUNK
    c = pl.program_id(0)
    slot = c % 2

    @pl.when(c == 0)
    def _():
        zbuf[...] = jnp.zeros_like(zbuf)

        def zero_copy(b):
            r0 = pl.multiple_of(b * EXPERT_ROWS, EXPERT_ROWS)
            return pltpu.make_async_copy(zbuf, dst_hbm.at[pl.ds(r0, EXPERT_ROWS), :], zsem)

        def zissue(j, carry):
            @pl.when(zblk_ref[j] >= 0)
            def _():
                zero_copy(zblk_ref[j]).start()
            return carry

        def zwait(j, carry):
            @pl.when(zblk_ref[j] >= 0)
            def _():
                zero_copy(zblk_ref[j]).wait()
            return carry

        lax.fori_loop(0, zblk_ref.shape[0], zissue, 0)
        lax.fori_loop(0, zblk_ref.shape[0], zwait, 0)

    def chunk_wait(s):
        pltpu.make_async_copy(src_hbm.at[pl.ds(0, ch), :], dst_hbm.at[pl.ds(0, ch), :], sems.at[s]).wait()

    def issue(r, carry):
        a = c * ch + r
        pltpu.make_async_copy(src_hbm.at[pl.ds(a // TOP_K, 1), :],
                              dst_hbm.at[pl.ds(dest_ref[a], 1), :], sems.at[slot]).start()
        return carry

    lax.fori_loop(0, ch, issue, 0)

    @pl.when(c > 0)
    def _():
        chunk_wait(1 - slot)

    @pl.when(c == pl.num_programs(0) - 1)
    def _():
        chunk_wait(slot)


def _dispatch(h2, dest, zero_blocks, n_rows):
    n, D = h2.shape
    return pl.pallas_call(
        _dispatch_body,
        grid_spec=pltpu.PrefetchScalarGridSpec(
            num_scalar_prefetch=2,
            grid=(dest.shape[0] // DISPATCH_CHUNK,),
            in_specs=[pl.BlockSpec(memory_space=pl.ANY)],
            out_specs=pl.BlockSpec(memory_space=pl.ANY),
            scratch_shapes=[pltpu.VMEM((EXPERT_ROWS, D), F32), pltpu.SemaphoreType.DMA((2,)),
                            pltpu.SemaphoreType.DMA],
        ),
        out_shape=jax.ShapeDtypeStruct((n_rows, D), F32),
        compiler_params=_params("arbitrary"),
        name="dispatch",
    )(dest, zero_blocks, h2)


def _experts_body(bexp_ref, nused_ref, x_ref, wu_ref, bu_ref, wd_ref, bd_ref, o_ref):
    del bexp_ref
    i = pl.program_id(0)
    dff = wd_ref.shape[1]

    @pl.when(i < nused_ref[0])
    def _():
        hc = jnp.dot(x_ref[...].astype(BF16), wu_ref[0], preferred_element_type=F32) + bu_ref[0]
        g = jnp.minimum(hc[:, :dff], SWIGLU_LIMIT)
        lin = jnp.clip(hc[:, dff:], -SWIGLU_LIMIT, SWIGLU_LIMIT)
        act = (lin + 1.0) * g * jax.nn.sigmoid(SWIGLU_ALPHA * g)
        o_ref[...] = jnp.dot(act.astype(BF16), wd_ref[0], preferred_element_type=F32) + bd_ref[0]

    @pl.when(i >= nused_ref[0])
    def _():
        o_ref[...] = jnp.zeros_like(o_ref)


def _experts(x_rows, block_exp, n_used, w_up, b_up, w_down, b_down, n_blocks):
    R, D = x_rows.shape
    E, _, F2 = w_up.shape
    dff = w_down.shape[1]
    blk = lambda i, be, nu: (i, 0)
    wsel = lambda i, be, nu: (be[i], 0, 0)
    return pl.pallas_call(
        _experts_body,
        grid_spec=pltpu.PrefetchScalarGridSpec(
            num_scalar_prefetch=2,
            grid=(n_blocks,),
            in_specs=[pl.BlockSpec((EXPERT_ROWS, D), blk),
                      pl.BlockSpec((1, D, F2), wsel),
                      pl.BlockSpec((1, 1, F2), wsel),
                      pl.BlockSpec((1, dff, D), wsel),
                      pl.BlockSpec((1, 1, D), wsel)],
            out_specs=pl.BlockSpec((EXPERT_ROWS, D), blk),
        ),
        out_shape=jax.ShapeDtypeStruct((R, D), F32),
        compiler_params=_params("arbitrary"),
        name="experts",
    )(block_exp, n_used, x_rows, w_up.astype(BF16), b_up.reshape(E, 1, F2),
      w_down.astype(BF16), b_down.reshape(E, 1, D))


COMBINE_ROWS = 128


def _combine_body(dest_ref, y_hbm, x1_ref, gate_ref, mod_ref, o_ref, buf, sem):
    i = pl.program_id(0)
    t = COMBINE_ROWS

    def issue(r, carry):
        for k in range(TOP_K):
            d = dest_ref[(i * t + r) * TOP_K + k]
            pltpu.make_async_copy(y_hbm.at[pl.ds(d, 1), :], buf.at[k, pl.ds(r, 1), :], sem).start()
        return carry

    lax.fori_loop(0, t, issue, 0)
    for k in range(TOP_K):
        pltpu.make_async_copy(y_hbm.at[pl.ds(0, t), :], buf.at[k], sem).wait()
    gates = gate_ref[...]
    y = gates[:, 0:1] * buf[0]
    for k in range(1, TOP_K):
        y = y + gates[:, k:k + 1] * buf[k]
    o_ref[...] = x1_ref[...] + mod_ref[0, 5:6, :] * y


def _combine(y_rows, dest, x1, gates, mod3, rows_per_batch):
    n, D = x1.shape
    t = COMBINE_ROWS
    tpb = rows_per_batch // t
    return pl.pallas_call(
        _combine_body,
        grid_spec=pltpu.PrefetchScalarGridSpec(
            num_scalar_prefetch=1,
            grid=(n // t,),
            in_specs=[pl.BlockSpec(memory_space=pl.ANY),
                      pl.BlockSpec((t, D), lambda i, d: (i, 0)),
                      pl.BlockSpec((t, LANES), lambda i, d: (i, 0)),
                      pl.BlockSpec((1, 6, D), lambda i, d: (i // tpb, 0, 0))],
            out_specs=pl.BlockSpec((t, D), lambda i, d: (i, 0)),
            scratch_shapes=[pltpu.VMEM((TOP_K, t, D), F32), pltpu.SemaphoreType.DMA],
        ),
        out_shape=jax.ShapeDtypeStruct((n, D), F32),
        compiler_params=_params("arbitrary"),
        name="combine",
    )(dest, y_rows, x1, gates, mod3)


def kernel(x, c, positions, ada_w, ada_b, norm1_g, norm2_g, w_in, conv_w, conv_b, w_rg_a, b_rg_a, w_rg_x, b_rg_x, lru_lambda, q_norm_g, k_norm_g, b_gate, w_branch, w_out, w_router, b_router, w_up, b_up, w_down, b_down):
    B, S, D = x.shape
    n = B * S
    depth = ada_w.shape[0]
    x2 = x.reshape(n, D)
    cos_t, sa_t, sb_t = _rope_tables(positions)
    for l in range(depth):
        mod3 = _adaln(c, ada_w[l], ada_b[l]).reshape(B, 6, D)
        h1 = _norm_mod_call(x2, mod3, norm1_g[l], S)
        w = w_in[l].astype(BF16)
        xr = _proj("plain", h1, w[:, 0:D], [], F32, "proj_xr")
        gg = _proj("gelu", h1, w[:, D:2 * D], [], F32, "proj_gr")
        qk_g = jnp.concatenate([jnp.tile(q_norm_g[l], N_HEADS), jnp.tile(k_norm_g[l], N_HEADS)]).reshape(1, 2 * D)
        qk = _proj("qk", h1, w[:, 2 * D:4 * D], [qk_g, cos_t, sa_t, sb_t], F32, "proj_qk")
        v = _proj("plain", h1, w[:, 4 * D:5 * D], [], BF16, "proj_v")
        gl = _proj("gate", h1, w[:, 5 * D:7 * D], [b_gate[l].reshape(1, 2 * D)], F32, "proj_gl")

        y_rnn = _rglru(xr, gg, conv_w[l], conv_b[l], w_rg_a[l], b_rg_a[l], w_rg_x[l], b_rg_x[l],
                       lru_lambda[l], B, S)
        y_att = _moba(qk, v, B, S)

        x1, h2, topi, gates, rank, cnt = _merge(y_rnn, y_att, gl, x2, mod3, w_branch[l], w_out[l],
                                                norm2_g[l], w_router[l], b_router[l], S)

        counts = cnt[0, :N_EXPERTS]
        padded = (counts + EXPERT_ROWS - 1) // EXPERT_ROWS * EXPERT_ROWS
        pad_end = jnp.cumsum(padded)
        pad_start = pad_end - padded
        top_idx = topi[:, :TOP_K]
        dest = (pad_start[top_idx] + rank[:, :TOP_K]).reshape(n * TOP_K).astype(jnp.int32)
        n_blocks = (n * TOP_K) // EXPERT_ROWS + N_EXPERTS
        block_start = jnp.arange(n_blocks, dtype=jnp.int32) * EXPERT_ROWS
        block_exp = jnp.minimum(jnp.sum(block_start[:, None] >= pad_end[None, :], axis=1),
                                N_EXPERTS - 1).astype(jnp.int32)
        n_used = (pad_end[-1] // EXPERT_ROWS).astype(jnp.int32)
        last_blk = jnp.where(padded > 0, pad_end // EXPERT_ROWS - 1, -1)
        spare = n_used + jnp.arange(N_EXPERTS, dtype=jnp.int32)
        spare = jnp.where(spare < n_blocks, spare, -1)
        zero_blocks = jnp.concatenate([last_blk, spare]).astype(jnp.int32)

        x_rows = _dispatch(h2, dest, zero_blocks, n_blocks * EXPERT_ROWS)
        y_rows = _experts(x_rows, block_exp, n_used.reshape(1), w_up[l], b_up[l], w_down[l], b_down[l], n_blocks)
        x2 = _combine(y_rows, dest, x1, gates, mod3, S)
    return x2.reshape(B, S, D)
```

```python
import functools

import jax
import jax.numpy as jnp
from jax import lax
from jax.experimental import pallas as pl
from jax.experimental.pallas import tpu as pltpu

F32 = jnp.float32
BF16 = jnp.bfloat16
HIGHEST = lax.Precision.HIGHEST

EPS = 1e-6
LANES = 128
N_HEADS = 8
HEAD_DIM = 128
RNN_BLOCKS = 8
RNN_BW = 128
CONV_W = 4
LRU_C = 8.0
MOBA_BLOCK = 256
MOBA_TOPK = 3
MOBA_CHUNK = 4
MOBA_HEADS = 4
LOG2_E = 1.4426950408889634
ROPE_DIMS = HEAD_DIM // 4
ROPE_THETA = 500000.0
N_EXPERTS = 32
TOP_K = 4
SWIGLU_LIMIT = 7.0
SWIGLU_ALPHA = 1.702
EXPERT_ROWS = 256
NEG_BIG = -1e30

VMEM_LIMIT = 56 * 1024 * 1024


def _params(*sem):
    return pltpu.CompilerParams(dimension_semantics=sem, vmem_limit_bytes=VMEM_LIMIT)


def _adaln_body(c_ref, w_ref, b_ref, o_ref):
    cs = c_ref[...]
    cs = cs * jax.nn.sigmoid(cs)
    o_ref[...] = jnp.dot(cs, w_ref[...], preferred_element_type=F32, precision=HIGHEST) + b_ref[...]


def _adaln(c, ada_w, ada_b):
    B, D = c.shape
    W = ada_w.shape[1]
    cpad = jnp.zeros((8, D), F32).at[:B].set(c)
    tn = 1024
    mod = pl.pallas_call(
        _adaln_body,
        grid=(W // tn,),
        in_specs=[pl.BlockSpec((8, D), lambda j: (0, 0)),
                  pl.BlockSpec((D, tn), lambda j: (0, j)),
                  pl.BlockSpec((1, tn), lambda j: (0, j))],
        out_specs=pl.BlockSpec((8, tn), lambda j: (0, j)),
        out_shape=jax.ShapeDtypeStruct((8, W), F32),
        compiler_params=_params("parallel"),
        name="adaln",
    )(cpad, ada_w, ada_b.reshape(1, W))
    return mod[:B]


def _rope_body(pos_ref, freq_ref, c_ref, sa_ref, sb_ref):
    ang = pos_ref[...].astype(F32) * freq_ref[...]
    lane = lax.broadcasted_iota(jnp.int32, ang.shape, 1)
    half = ROPE_DIMS // 2
    s = jnp.sin(ang)
    c_ref[...] = jnp.cos(ang)
    sa_ref[...] = jnp.where(lane < half, -s, 0.0)
    sb_ref[...] = jnp.where((lane >= half) & (lane < ROPE_DIMS), s, 0.0)


def _rope_tables(positions):
    n = positions.size
    half = ROPE_DIMS // 2
    freqs = ROPE_THETA ** (-jnp.arange(half, dtype=F32) / half)
    freq_lane = jnp.zeros((1, LANES), F32).at[0, :ROPE_DIMS].set(jnp.concatenate([freqs, freqs]))
    tm = 1024
    tab = jax.ShapeDtypeStruct((n, LANES), F32)
    return pl.pallas_call(
        _rope_body,
        grid=(n // tm,),
        in_specs=[pl.BlockSpec((tm, 1), lambda i: (i, 0)),
                  pl.BlockSpec((1, LANES), lambda i: (0, 0))],
        out_specs=[pl.BlockSpec((tm, LANES), lambda i: (i, 0))] * 3,
        out_shape=[tab, tab, tab],
        compiler_params=_params("parallel"),
        name="rope_tab",
    )(positions.reshape(n, 1), freq_lane)


def _norm_mod(x, g, sh, sc):
    ms = jnp.mean(x * x, axis=-1, keepdims=True)
    y = x * lax.rsqrt(ms + EPS)
    return (y * g) * (1.0 + sc) + sh


def _norm_mod_body(x_ref, mod_ref, g_ref, o_ref):
    o_ref[...] = _norm_mod(x_ref[...], g_ref[...], mod_ref[0, 0:1, :], mod_ref[0, 1:2, :]).astype(o_ref.dtype)


def _norm_mod_call(x2, mod3, g, rows_per_batch):
    n, D = x2.shape
    tm = 512
    tpb = rows_per_batch // tm
    return pl.pallas_call(
        _norm_mod_body,
        grid=(n // tm,),
        in_specs=[pl.BlockSpec((tm, D), lambda i: (i, 0)),
                  pl.BlockSpec((1, 6, D), lambda i: (i // tpb, 0, 0)),
                  pl.BlockSpec((1, D), lambda i: (0, 0))],
        out_specs=pl.BlockSpec((tm, D), lambda i: (i, 0)),
        out_shape=jax.ShapeDtypeStruct((n, D), BF16),
        compiler_params=_params("parallel"),
        name="norm_mod",
    )(x2, mod3, g.reshape(1, D))


def _gelu_tanh(x):
    return 0.5 * x * (1.0 + jnp.tanh(0.7978845608028654 * (x + 0.044715 * (x * x * x))))


def _proj_body(kind, h_ref, w_ref, *rest):
    acc = jnp.dot(h_ref[...], w_ref[...], preferred_element_type=F32)
    if kind == "plain":
        (o_ref,) = rest
        o_ref[...] = acc.astype(o_ref.dtype)
    elif kind == "gelu":
        (o_ref,) = rest
        o_ref[...] = _gelu_tanh(acc).astype(o_ref.dtype)
    elif kind == "gate":
        b_ref, o_ref = rest
        o_ref[...] = jax.nn.sigmoid(acc + b_ref[...]).astype(o_ref.dtype)
    else:
        g_ref, c_ref, sa_ref, sb_ref, o_ref = rest
        cos, sa, sb = c_ref[...], sa_ref[...], sb_ref[...]
        half = ROPE_DIMS // 2
        for hh in range(acc.shape[1] // HEAD_DIM):
            sl = slice(hh * HEAD_DIM, (hh + 1) * HEAD_DIM)
            seg = acc[:, sl]
            ms = jnp.mean(seg * seg, axis=-1, keepdims=True)
            y = seg * lax.rsqrt(ms + EPS) * g_ref[:, sl]
            up = pltpu.roll(y, HEAD_DIM - half, axis=1)
            dn = pltpu.roll(y, half, axis=1)
            o_ref[:, sl] = (y * cos + up * sa + dn * sb).astype(o_ref.dtype)


def _proj(kind, h, w, extras, out_dtype, name):
    n, D = h.shape
    W = w.shape[1]
    tm, tn = 512, 1024
    in_specs = [pl.BlockSpec((tm, D), lambda j, i: (i, 0)),
                pl.BlockSpec((D, tn), lambda j, i: (0, j))]
    args = [h, w]
    for arr in extras:
        if arr.shape[0] == 1:
            in_specs.append(pl.BlockSpec((1, tn), lambda j, i: (0, j)))
        else:
            in_specs.append(pl.BlockSpec((tm, LANES), lambda j, i: (i, 0)))
        args.append(arr)
    return pl.pallas_call(
        functools.partial(_proj_body, kind),
        grid=(W // tn, n // tm),
        in_specs=in_specs,
        out_specs=pl.BlockSpec((tm, tn), lambda j, i: (i, j)),
        out_shape=jax.ShapeDtypeStruct((n, W), out_dtype),
        compiler_params=_params("parallel", "parallel"),
        name=name,
    )(*args)


def _rglru_body(xr_ref, gg_ref, cw_ref, cb_ref, wa_ref, ba_ref, wx_ref, bx_ref, lam_ref,
                o_ref, xbuf, hcar, a_s, u_s):
    s = pl.program_id(1)
    ts, D = xr_ref.shape

    @pl.when(s == 0)
    def _():
        xbuf[0:8, :] = jnp.zeros((8, D), F32)
        hcar[...] = jnp.zeros_like(hcar)

    @pl.when(s > 0)
    def _():
        xbuf[0:8, :] = xbuf[ts:ts + 8, :]

    xbuf[8:8 + ts, :] = xr_ref[...]
    xc = cb_ref[...] + cw_ref[0:1, :] * xbuf[8:8 + ts, :]
    for i in range(1, CONV_W):
        xc = xc + cw_ref[i:i + 1, :] * xbuf[8 - i:8 - i + ts, :]

    ra, rx = [], []
    for n in range(RNN_BLOCKS):
        xb = xc[:, n * RNN_BW:(n + 1) * RNN_BW].astype(BF16)
        ra.append(jnp.dot(xb, wa_ref[n], preferred_element_type=F32))
        rx.append(jnp.dot(xb, wx_ref[n], preferred_element_type=F32))
    r = jax.nn.sigmoid(jnp.concatenate(ra, axis=1) + ba_ref[...])
    ig = jax.nn.sigmoid(jnp.concatenate(rx, axis=1) + bx_ref[...])

    z = -lam_ref[...]
    softplus = jnp.maximum(z, 0.0) + jnp.log1p(jnp.exp(-jnp.abs(z)))
    log_a = -LRU_C * r * softplus
    a = jnp.exp(log_a)
    mult = jnp.sqrt(1.0 - a * a)
    row = lax.broadcasted_iota(jnp.int32, (ts, D), 0)
    mult = jnp.where((row == 0) & (s == 0), 1.0, mult)
    u = mult * (ig * xc)

    rm = row & 7
    for d in (1, 2, 4):
        keep = rm >= d
        a_sh = pltpu.roll(a, d, axis=0)
        u_sh = pltpu.roll(u, d, axis=0)
        u = jnp.where(keep, a * u_sh + u, u)
        a = jnp.where(keep, a * a_sh, a)
    a_s[...] = a
    u_s[...] = u

    def group(g, h):
        r0 = pl.multiple_of(g * 8, 8)
        hg = u_s[pl.ds(r0, 8), :] + a_s[pl.ds(r0, 8), :] * h
        u_s[pl.ds(r0, 8), :] = hg
        return hg[7:8, :]

    hcar[...] = lax.fori_loop(0, ts // 8, group, hcar[...])
    o_ref[...] = (u_s[...] * gg_ref[...]).astype(o_ref.dtype)


def _rglru(xr, gg, conv_w, conv_b, w_a, b_a, w_x, b_x, lam, B, S):
    n, D = xr.shape
    ts = 256
    spb = S // ts
    row = lambda b, s: (b * spb + s, 0)
    vec = lambda b, s: (0, 0)
    return pl.pallas_call(
        _rglru_body,
        grid=(B, spb),
        in_specs=[pl.BlockSpec((ts, D), row),
                  pl.BlockSpec((ts, D), row),
                  pl.BlockSpec((CONV_W, D), vec),
                  pl.BlockSpec((1, D), vec),
                  pl.BlockSpec((RNN_BLOCKS, RNN_BW, RNN_BW), lambda b, s: (0, 0, 0)),
                  pl.BlockSpec((1, D), vec),
                  pl.BlockSpec((RNN_BLOCKS, RNN_BW, RNN_BW), lambda b, s: (0, 0, 0)),
                  pl.BlockSpec((1, D), vec),
                  pl.BlockSpec((1, D), vec)],
        out_specs=pl.BlockSpec((ts, D), row),
        out_shape=jax.ShapeDtypeStruct((n, D), BF16),
        scratch_shapes=[pltpu.VMEM((ts + 8, D), F32), pltpu.VMEM((1, D), F32),
                        pltpu.VMEM((ts, D), F32), pltpu.VMEM((ts, D), F32)],
        compiler_params=_params("arbitrary", "arbitrary"),
        name="rglru",
    )(xr, gg, conv_w, conv_b.reshape(1, D), w_a.astype(BF16), b_a.reshape(1, D),
      w_x.astype(BF16), b_x.reshape(1, D), lam.reshape(1, D))


def _moba_body(q_ref, k_ref, v_ref, o_ref, kaug, vaug, kmean, m_s, acc_s):
    qb = pl.program_id(2)
    S = k_ref.shape[0]
    nb = S // MOBA_BLOCK
    bs = MOBA_BLOCK
    dh = HEAD_DIM
    heads = range(MOBA_HEADS)
    hcol = lambda hh: slice(hh * dh, (hh + 1) * dh)

    @pl.when(qb == 0)
    def _():
        blk = lax.broadcasted_iota(jnp.int32, (S, dh), 0) // bs
        col = lax.broadcasted_iota(jnp.int32, (S, dh), 1)
        onehot = jnp.where(col == blk, 1.0, 0.0).astype(BF16)
        for hh in heads:
            kaug[hh, :, 0:dh] = k_ref[:, hcol(hh)].astype(BF16)
            kaug[hh, :, dh:2 * dh] = onehot
            vaug[hh, :, 0:dh] = v_ref[:, hcol(hh)]
            vaug[hh, :, dh:2 * dh] = jnp.ones((S, dh), BF16)
            kmean[hh] = jnp.zeros((LANES, dh), F32)
            for j in range(nb):
                kmean[hh, j:j + 1, :] = jnp.mean(k_ref[j * bs:(j + 1) * bs, hcol(hh)], axis=0, keepdims=True)

    scale = HEAD_DIM ** -0.5
    nt = (((1,), (1,)), ((), ()))
    c2 = scale * LOG2_E
    r0 = pl.multiple_of(qb * bs, bs)
    lane = lax.broadcasted_iota(jnp.int32, (bs, LANES), 1)
    lane_f = lane.astype(F32)
    past = lane < qb
    rr = lax.broadcasted_iota(jnp.int32, (bs, bs), 0)
    cc = lax.broadcasted_iota(jnp.int32, (bs, bs), 1)

    qaugs = []
    for hh in heads:
        q = q_ref[:, hcol(hh)]
        qbf = q.astype(BF16)

        gate = lax.dot_general(q, kmean[hh], nt, preferred_element_type=F32, precision=HIGHEST)
        g = jnp.where(past, gate, -jnp.inf)
        sel = jnp.zeros(gate.shape, F32)
        for _ in range(MOBA_TOPK):
            m = jnp.max(g, axis=1, keepdims=True)
            idx = jnp.min(jnp.where(g == m, lane_f, float(LANES)), axis=1, keepdims=True)
            pick = (lane_f == idx) & past
            sel = jnp.where(pick, 1.0, sel)
            g = jnp.where(pick, -jnp.inf, g)
        bias = jnp.where(sel > 0.5, 0.0, NEG_BIG).astype(BF16)
        qaugs.append(jnp.concatenate([qbf, bias], axis=1))

        s = lax.dot_general(qbf, kaug[hh, pl.ds(r0, bs), 0:dh], nt, preferred_element_type=F32)
        s = jnp.where(cc <= rr, s, NEG_BIG)
        m0 = jnp.max(s, axis=1, keepdims=True)
        p = jnp.exp2((s - m0) * c2)
        m_s[hh] = m0
        acc_s[hh] = jnp.dot(p.astype(BF16), vaug[hh, pl.ds(r0, bs), :], preferred_element_type=F32)

    cw = MOBA_CHUNK * bs
    for c in range(nb // MOBA_CHUNK):
        @pl.when(c * MOBA_CHUNK < qb)
        def _(c=c):
            for hh in heads:
                sc = lax.dot_general(qaugs[hh], kaug[hh, c * cw:(c + 1) * cw, :], nt,
                                     preferred_element_type=F32)
                m_old = m_s[hh]
                m_new = jnp.maximum(m_old, jnp.max(sc, axis=1, keepdims=True))
                alpha = jnp.exp2((m_old - m_new) * c2)
                pc = jnp.exp2((sc - m_new) * c2)
                m_s[hh] = m_new
                acc_s[hh] = alpha * acc_s[hh] + jnp.dot(pc.astype(BF16), vaug[hh, c * cw:(c + 1) * cw, :],
                                                        preferred_element_type=F32)

    for hh in heads:
        o_ref[:, hcol(hh)] = (acc_s[hh, :, 0:dh] / acc_s[hh, :, dh:dh + 1]).astype(o_ref.dtype)


def _moba(qk, v, B, S):
    n, D = v.shape
    nq = S // MOBA_BLOCK
    hs = MOBA_HEADS
    w = hs * HEAD_DIM
    return pl.pallas_call(
        _moba_body,
        grid=(B, N_HEADS // hs, nq),
        in_specs=[pl.BlockSpec((MOBA_BLOCK, w), lambda b, h, i: (b * nq + i, h)),
                  pl.BlockSpec((S, w), lambda b, h, i: (b, N_HEADS // hs + h)),
                  pl.BlockSpec((S, w), lambda b, h, i: (b, h))],
        out_specs=pl.BlockSpec((MOBA_BLOCK, w), lambda b, h, i: (b * nq + i, h)),
        out_shape=jax.ShapeDtypeStruct((n, D), BF16),
        scratch_shapes=[pltpu.VMEM((hs, S, 2 * HEAD_DIM), BF16), pltpu.VMEM((hs, S, 2 * HEAD_DIM), BF16),
                        pltpu.VMEM((hs, LANES, HEAD_DIM), F32), pltpu.VMEM((hs, MOBA_BLOCK, 1), F32),
                        pltpu.VMEM((hs, MOBA_BLOCK, 2 * HEAD_DIM), F32)],
        compiler_params=_params("parallel", "parallel", "arbitrary"),
        name="moba",
    )(qk, qk, v)


def _merge_body(yr_ref, ya_ref, gr_ref, ga_ref, x_ref, mod_ref, wb_ref, wo_ref, g2_ref, wr_ref, br_ref,
                x1_ref, h2_ref, topi_ref, gate_ref, rank_ref, cnt_ref, carry):
    i = pl.program_id(0)
    tm = x_ref.shape[0]

    @pl.when(i == 0)
    def _():
        carry[...] = jnp.zeros_like(carry)

    zr = jnp.dot(yr_ref[...], wb_ref[0], preferred_element_type=F32)
    za = jnp.dot(ya_ref[...], wb_ref[1], preferred_element_type=F32)
    mix = (gr_ref[...] * zr + ga_ref[...] * za).astype(BF16)
    mixed = jnp.dot(mix, wo_ref[...], preferred_element_type=F32)
    x1 = x_ref[...] + mod_ref[0, 2:3, :] * mixed
    x1_ref[...] = x1
    h2 = _norm_mod(x1, g2_ref[...], mod_ref[0, 3:4, :], mod_ref[0, 4:5, :])
    h2_ref[...] = h2

    logits = jnp.dot(h2, wr_ref[...], preferred_element_type=F32, precision=HIGHEST) + br_ref[...]
    lane = lax.broadcasted_iota(jnp.int32, logits.shape, 1)
    lane_f = lane.astype(F32)
    lg = jnp.where(lane < N_EXPERTS, logits, -jnp.inf)
    vals, idxs = [], []
    onehot = jnp.zeros(logits.shape, F32)
    for _ in range(TOP_K):
        m = jnp.max(lg, axis=1, keepdims=True)
        idx = jnp.min(jnp.where(lg == m, lane_f, float(LANES)), axis=1, keepdims=True)
        pick = lane_f == idx
        onehot = jnp.where(pick, 1.0, onehot)
        lg = jnp.where(pick, -jnp.inf, lg)
        vals.append(m)
        idxs.append(idx)
    es = [jnp.exp(v - vals[0]) for v in vals]
    den = es[0] + es[1] + es[2] + es[3]

    rr = lax.broadcasted_iota(jnp.int32, (tm, tm), 0)
    cc = lax.broadcasted_iota(jnp.int32, (tm, tm), 1)
    tri = jnp.where(cc < rr, 1.0, 0.0).astype(BF16)
    before = jnp.dot(tri, onehot.astype(BF16), preferred_element_type=F32) + carry[...]

    topi = jnp.zeros(logits.shape, F32)
    gates = jnp.zeros(logits.shape, F32)
    rank = jnp.zeros(logits.shape, F32)
    for k in range(TOP_K):
        rk = jnp.sum(jnp.where(lane_f == idxs[k], before, 0.0), axis=1, keepdims=True)
        topi = jnp.where(lane == k, idxs[k], topi)
        gates = jnp.where(lane == k, es[k] / den, gates)
        rank = jnp.where(lane == k, rk, rank)
    topi_ref[...] = topi.astype(jnp.int32)
    gate_ref[...] = gates
    rank_ref[...] = rank.astype(jnp.int32)
    total = carry[...] + jnp.sum(onehot, axis=0, keepdims=True)
    carry[...] = total
    cnt_ref[...] = jnp.broadcast_to(total, cnt_ref.shape).astype(jnp.int32)


def _merge(y_rnn, y_att, gl, x2, mod3, w_branch, w_out, norm2_g, w_router, b_router, rows_per_batch):
    n, D = x2.shape
    tm = 256
    tpb = rows_per_batch // tm
    wr = jnp.zeros((D, LANES), F32).at[:, :N_EXPERTS].set(w_router)
    br = jnp.zeros((1, LANES), F32).at[0, :N_EXPERTS].set(b_router)
    row = lambda i: (i, 0)
    fixed = lambda i: (0, 0)
    wide = jax.ShapeDtypeStruct((n, D), F32)
    narrow_i = jax.ShapeDtypeStruct((n, LANES), jnp.int32)
    return pl.pallas_call(
        _merge_body,
        grid=(n // tm,),
        in_specs=[pl.BlockSpec((tm, D), row),
                  pl.BlockSpec((tm, D), row),
                  pl.BlockSpec((tm, D), lambda i: (i, 0)),
                  pl.BlockSpec((tm, D), lambda i: (i, 1)),
                  pl.BlockSpec((tm, D), row),
                  pl.BlockSpec((1, 6, D), lambda i: (i // tpb, 0, 0)),
                  pl.BlockSpec((2, D, D), lambda i: (0, 0, 0)),
                  pl.BlockSpec((D, D), fixed),
                  pl.BlockSpec((1, D), fixed),
                  pl.BlockSpec((D, LANES), fixed),
                  pl.BlockSpec((1, LANES), fixed)],
        out_specs=[pl.BlockSpec((tm, D), row),
                   pl.BlockSpec((tm, D), row),
                   pl.BlockSpec((tm, LANES), row),
                   pl.BlockSpec((tm, LANES), row),
                   pl.BlockSpec((tm, LANES), row),
                   pl.BlockSpec((8, LANES), fixed)],
        out_shape=[wide, wide, narrow_i, jax.ShapeDtypeStruct((n, LANES), F32), narrow_i,
                   jax.ShapeDtypeStruct((8, LANES), jnp.int32)],
        scratch_shapes=[pltpu.VMEM((1, LANES), F32)],
        compiler_params=_params("arbitrary"),
        name="merge",
    )(y_rnn, y_att, gl, gl, x2, mod3, w_branch.astype(BF16), w_out.astype(BF16),
      norm2_g.reshape(1, D), wr, br)


DISPATCH_TOKENS = 256


def _dispatch_body(dest_ref, zblk_ref, src_ref, dst_hbm, zbuf, sem, zsem):
    t = DISPATCH_TOKENS
    c = pl.program_id(0)

    @pl.when(c == 0)
    def _():
        zbuf[...] = jnp.zeros_like(zbuf)

        def zero_copy(b):
            r0 = pl.multiple_of(b * EXPERT_ROWS, EXPERT_ROWS)
            return pltpu.make_async_copy(zbuf, dst_hbm.at[pl.ds(r0, EXPERT_ROWS), :], zsem)

        def zissue(j, carry):
            @pl.when(zblk_ref[j] >= 0)
            def _():
                zero_copy(zblk_ref[j]).start()
            return carry

        def zwait(j, carry):
            @pl.when(zblk_ref[j] >= 0)
            def _():
                zero_copy(zblk_ref[j]).wait()
            return carry

        lax.fori_loop(0, zblk_ref.shape[0], zissue, 0)
        lax.fori_loop(0, zblk_ref.shape[0], zwait, 0)

    def issue(r, carry):
        for k in range(TOP_K):
            d = dest_ref[(c * t + r) * TOP_K + k]
            pltpu.make_async_copy(src_ref.at[pl.ds(r, 1), :], dst_hbm.at[pl.ds(d, 1), :], sem).start()
        return carry

    lax.fori_loop(0, t, issue, 0)
    for k in range(TOP_K):
        pltpu.make_async_copy(src_ref, dst_hbm.at[pl.ds(0, t), :], sem).wait()


def _dispatch(h2, dest, zero_blocks, n_rows):
    n, D = h2.shape
    return pl.pallas_call(
        _dispatch_body,
        grid_spec=pltpu.PrefetchScalarGridSpec(
            num_scalar_prefetch=2,
            grid=(n // DISPATCH_TOKENS,),
            in_specs=[pl.BlockSpec((DISPATCH_TOKENS, D), lambda i, d, z: (i, 0))],
            out_specs=pl.BlockSpec(memory_space=pl.ANY),
            scratch_shapes=[pltpu.VMEM((EXPERT_ROWS, D), F32), pltpu.SemaphoreType.DMA,
                            pltpu.SemaphoreType.DMA],
        ),
        out_shape=jax.ShapeDtypeStruct((n_rows, D), F32),
        compiler_params=_params("arbitrary"),
        name="dispatch",
    )(dest, zero_blocks, h2)


def _experts_body(bexp_ref, nused_ref, x_ref, wu_ref, bu_ref, wd_ref, bd_ref, o_ref):
    del bexp_ref
    i = pl.program_id(0)
    dff = wd_ref.shape[1]

    @pl.when(i < nused_ref[0])
    def _():
        hc = jnp.dot(x_ref[...].astype(BF16), wu_ref[0], preferred_element_type=F32) + bu_ref[0]
        g = jnp.minimum(hc[:, :dff], SWIGLU_LIMIT)
        lin = jnp.clip(hc[:, dff:], -SWIGLU_LIMIT, SWIGLU_LIMIT)
        act = (lin + 1.0) * g * jax.nn.sigmoid(SWIGLU_ALPHA * g)
        o_ref[...] = jnp.dot(act.astype(BF16), wd_ref[0], preferred_element_type=F32) + bd_ref[0]

    @pl.when(i >= nused_ref[0])
    def _():
        o_ref[...] = jnp.zeros_like(o_ref)


def _experts(x_rows, block_exp, n_used, w_up, b_up, w_down, b_down, n_blocks):
    R, D = x_rows.shape
    E, _, F2 = w_up.shape
    dff = w_down.shape[1]
    blk = lambda i, be, nu: (i, 0)
    wsel = lambda i, be, nu: (be[i], 0, 0)
    return pl.pallas_call(
        _experts_body,
        grid_spec=pltpu.PrefetchScalarGridSpec(
            num_scalar_prefetch=2,
            grid=(n_blocks,),
            in_specs=[pl.BlockSpec((EXPERT_ROWS, D), blk),
                      pl.BlockSpec((1, D, F2), wsel),
                      pl.BlockSpec((1, 1, F2), wsel),
                      pl.BlockSpec((1, dff, D), wsel),
                      pl.BlockSpec((1, 1, D), wsel)],
            out_specs=pl.BlockSpec((EXPERT_ROWS, D), blk),
        ),
        out_shape=jax.ShapeDtypeStruct((R, D), F32),
        compiler_params=_params("arbitrary"),
        name="experts",
    )(block_exp, n_used, x_rows, w_up.astype(BF16), b_up.reshape(E, 1, F2),
      w_down.astype(BF16), b_down.reshape(E, 1, D))


COMBINE_ROWS = 128


def _combine_body(dest_ref, y_hbm, x1_ref, gate_ref, mod_ref, o_ref, buf, sem):
    i = pl.program_id(0)
    t = COMBINE_ROWS

    def issue(r, carry):
        for k in range(TOP_K):
            d = dest_ref[(i * t + r) * TOP_K + k]
            pltpu.make_async_copy(y_hbm.at[pl.ds(d, 1), :], buf.at[k, pl.ds(r, 1), :], sem).start()
        return carry

    lax.fori_loop(0, t, issue, 0)
    for k in range(TOP_K):
        pltpu.make_async_copy(y_hbm.at[pl.ds(0, t), :], buf.at[k], sem).wait()
    gates = gate_ref[...]
    y = gates[:, 0:1] * buf[0]
    for k in range(1, TOP_K):
        y = y + gates[:, k:k + 1] * buf[k]
    o_ref[...] = x1_ref[...] + mod_ref[0, 5:6, :] * y


def _combine(y_rows, dest, x1, gates, mod3, rows_per_batch):
    n, D = x1.shape
    t = COMBINE_ROWS
    tpb = rows_per_batch // t
    return pl.pallas_call(
        _combine_body,
        grid_spec=pltpu.PrefetchScalarGridSpec(
            num_scalar_prefetch=1,
            grid=(n // t,),
            in_specs=[pl.BlockSpec(memory_space=pl.ANY),
                      pl.BlockSpec((t, D), lambda i, d: (i, 0)),
                      pl.BlockSpec((t, LANES), lambda i, d: (i, 0)),
                      pl.BlockSpec((1, 6, D), lambda i, d: (i // tpb, 0, 0))],
            out_specs=pl.BlockSpec((t, D), lambda i, d: (i, 0)),
            scratch_shapes=[pltpu.VMEM((TOP_K, t, D), F32), pltpu.SemaphoreType.DMA],
        ),
        out_shape=jax.ShapeDtypeStruct((n, D), F32),
        compiler_params=_params("arbitrary"),
        name="combine",
    )(dest, y_rows, x1, gates, mod3)


def kernel(x, c, positions, ada_w, ada_b, norm1_g, norm2_g, w_in, conv_w, conv_b, w_rg_a, b_rg_a, w_rg_x, b_rg_x, lru_lambda, q_norm_g, k_norm_g, b_gate, w_branch, w_out, w_router, b_router, w_up, b_up, w_down, b_down):
    B, S, D = x.shape
    n = B * S
    depth = ada_w.shape[0]
    x2 = x.reshape(n, D)
    cos_t, sa_t, sb_t = _rope_tables(positions)
    for l in range(depth):
        mod3 = _adaln(c, ada_w[l], ada_b[l]).reshape(B, 6, D)
        h1 = _norm_mod_call(x2, mod3, norm1_g[l], S)
        w = w_in[l].astype(BF16)
        xr = _proj("plain", h1, w[:, 0:D], [], F32, "proj_xr")
        gg = _proj("gelu", h1, w[:, D:2 * D], [], F32, "proj_gr")
        qk_g = jnp.concatenate([jnp.tile(q_norm_g[l], N_HEADS), jnp.tile(k_norm_g[l], N_HEADS)]).reshape(1, 2 * D)
        qk = _proj("qk", h1, w[:, 2 * D:4 * D], [qk_g, cos_t, sa_t, sb_t], F32, "proj_qk")
        v = _proj("plain", h1, w[:, 4 * D:5 * D], [], BF16, "proj_v")
        gl = _proj("gate", h1, w[:, 5 * D:7 * D], [b_gate[l].reshape(1, 2 * D)], F32, "proj_gl")

        y_rnn = _rglru(xr, gg, conv_w[l], conv_b[l], w_rg_a[l], b_rg_a[l], w_rg_x[l], b_rg_x[l],
                       lru_lambda[l], B, S)
        y_att = _moba(qk, v, B, S)

        x1, h2, topi, gates, rank, cnt = _merge(y_rnn, y_att, gl, x2, mod3, w_branch[l], w_out[l],
                                                norm2_g[l], w_router[l], b_router[l], S)

        counts = cnt[0, :N_EXPERTS]
        padded = (counts + EXPERT_ROWS - 1) // EXPERT_ROWS * EXPERT_ROWS
        pad_end = jnp.cumsum(padded)
        pad_start = pad_end - padded
        top_idx = topi[:, :TOP_K]
        dest = (pad_start[top_idx] + rank[:, :TOP_K]).reshape(n * TOP_K).astype(jnp.int32)
        n_blocks = (n * TOP_K) // EXPERT_ROWS + N_EXPERTS
        block_start = jnp.arange(n_blocks, dtype=jnp.int32) * EXPERT_ROWS
        block_exp = jnp.minimum(jnp.sum(block_start[:, None] >= pad_end[None, :], axis=1),
                                N_EXPERTS - 1).astype(jnp.int32)
        n_used = (pad_end[-1] // EXPERT_ROWS).astype(jnp.int32)
        last_blk = jnp.where(padded > 0, pad_end // EXPERT_ROWS - 1, -1)
        spare = n_used + jnp.arange(N_EXPERTS, dtype=jnp.int32)
        spare = jnp.where(spare < n_blocks, spare, -1)
        zero_blocks = jnp.concatenate([last_blk, spare]).astype(jnp.int32)

        x_rows = _dispatch(h2, dest, zero_blocks, n_blocks * EXPERT_ROWS)
        y_rows = _experts(x_rows, block_exp, n_used.reshape(1), w_up[l], b_up[l], w_down[l], b_down[l], n_blocks)
        x2 = _combine(y_rows, dest, x1, gates, mod3, S)
    return x2.reshape(B, S, D)
```

```python
import functools

import jax
import jax.numpy as jnp
from jax import lax
from jax.experimental import pallas as pl
from jax.experimental.pallas import tpu as pltpu

F32 = jnp.float32
BF16 = jnp.bfloat16
HIGHEST = lax.Precision.HIGHEST

EPS = 1e-6
LANES = 128
N_HEADS = 8
HEAD_DIM = 128
RNN_BLOCKS = 8
RNN_BW = 128
CONV_W = 4
LRU_C = 8.0
MOBA_BLOCK = 256
MOBA_TOPK = 3
MOBA_CHUNK = 4
MOBA_HEADS = 4
MERGE_SUB = 256
LOG2_E = 1.4426950408889634
ROPE_DIMS = HEAD_DIM // 4
ROPE_THETA = 500000.0
N_EXPERTS = 32
TOP_K = 4
SWIGLU_LIMIT = 7.0
SWIGLU_ALPHA = 1.702
EXPERT_ROWS = 256
NEG_BIG = -1e30

VMEM_LIMIT = 56 * 1024 * 1024


def _params(*sem):
    return pltpu.CompilerParams(dimension_semantics=sem, vmem_limit_bytes=VMEM_LIMIT)


def _adaln_body(c_ref, w_ref, b_ref, o_ref):
    cs = c_ref[...]
    cs = cs * jax.nn.sigmoid(cs)
    o_ref[...] = jnp.dot(cs, w_ref[...], preferred_element_type=F32, precision=HIGHEST) + b_ref[...]


def _adaln(c, ada_w, ada_b):
    B, D = c.shape
    W = ada_w.shape[1]
    cpad = jnp.zeros((8, D), F32).at[:B].set(c)
    tn = 1024
    mod = pl.pallas_call(
        _adaln_body,
        grid=(W // tn,),
        in_specs=[pl.BlockSpec((8, D), lambda j: (0, 0)),
                  pl.BlockSpec((D, tn), lambda j: (0, j)),
                  pl.BlockSpec((1, tn), lambda j: (0, j))],
        out_specs=pl.BlockSpec((8, tn), lambda j: (0, j)),
        out_shape=jax.ShapeDtypeStruct((8, W), F32),
        compiler_params=_params("parallel"),
        name="adaln",
    )(cpad, ada_w, ada_b.reshape(1, W))
    return mod[:B]


def _rope_body(pos_ref, freq_ref, c_ref, sa_ref, sb_ref):
    ang = pos_ref[...].astype(F32) * freq_ref[...]
    lane = lax.broadcasted_iota(jnp.int32, ang.shape, 1)
    half = ROPE_DIMS // 2
    s = jnp.sin(ang)
    c_ref[...] = jnp.cos(ang)
    sa_ref[...] = jnp.where(lane < half, -s, 0.0)
    sb_ref[...] = jnp.where((lane >= half) & (lane < ROPE_DIMS), s, 0.0)


def _rope_tables(positions):
    n = positions.size
    half = ROPE_DIMS // 2
    freqs = ROPE_THETA ** (-jnp.arange(half, dtype=F32) / half)
    freq_lane = jnp.zeros((1, LANES), F32).at[0, :ROPE_DIMS].set(jnp.concatenate([freqs, freqs]))
    tm = 1024
    tab = jax.ShapeDtypeStruct((n, LANES), F32)
    return pl.pallas_call(
        _rope_body,
        grid=(n // tm,),
        in_specs=[pl.BlockSpec((tm, 1), lambda i: (i, 0)),
                  pl.BlockSpec((1, LANES), lambda i: (0, 0))],
        out_specs=[pl.BlockSpec((tm, LANES), lambda i: (i, 0))] * 3,
        out_shape=[tab, tab, tab],
        compiler_params=_params("parallel"),
        name="rope_tab",
    )(positions.reshape(n, 1), freq_lane)


def _norm_mod(x, g, sh, sc):
    ms = jnp.mean(x * x, axis=-1, keepdims=True)
    y = x * lax.rsqrt(ms + EPS)
    return (y * g) * (1.0 + sc) + sh


def _norm_mod_body(x_ref, mod_ref, g_ref, o_ref):
    o_ref[...] = _norm_mod(x_ref[...], g_ref[...], mod_ref[0, 0:1, :], mod_ref[0, 1:2, :]).astype(o_ref.dtype)


def _norm_mod_call(x2, mod3, g, rows_per_batch):
    n, D = x2.shape
    tm = 512
    tpb = rows_per_batch // tm
    return pl.pallas_call(
        _norm_mod_body,
        grid=(n // tm,),
        in_specs=[pl.BlockSpec((tm, D), lambda i: (i, 0)),
                  pl.BlockSpec((1, 6, D), lambda i: (i // tpb, 0, 0)),
                  pl.BlockSpec((1, D), lambda i: (0, 0))],
        out_specs=pl.BlockSpec((tm, D), lambda i: (i, 0)),
        out_shape=jax.ShapeDtypeStruct((n, D), BF16),
        compiler_params=_params("parallel"),
        name="norm_mod",
    )(x2, mod3, g.reshape(1, D))


def _gelu_tanh(x):
    return 0.5 * x * (1.0 + jnp.tanh(0.7978845608028654 * (x + 0.044715 * (x * x * x))))


def _proj_body(kind, h_ref, w_ref, *rest):
    acc = jnp.dot(h_ref[...], w_ref[...], preferred_element_type=F32)
    if kind == "plain":
        (o_ref,) = rest
        o_ref[...] = acc.astype(o_ref.dtype)
    elif kind == "gelu":
        (o_ref,) = rest
        o_ref[...] = _gelu_tanh(acc).astype(o_ref.dtype)
    elif kind == "gate":
        b_ref, o_ref = rest
        o_ref[...] = jax.nn.sigmoid(acc + b_ref[...]).astype(o_ref.dtype)
    else:
        g_ref, c_ref, sa_ref, sb_ref, o_ref = rest
        cos, sa, sb = c_ref[...], sa_ref[...], sb_ref[...]
        half = ROPE_DIMS // 2
        for hh in range(acc.shape[1] // HEAD_DIM):
            sl = slice(hh * HEAD_DIM, (hh + 1) * HEAD_DIM)
            seg = acc[:, sl]
            ms = jnp.mean(seg * seg, axis=-1, keepdims=True)
            y = seg * lax.rsqrt(ms + EPS) * g_ref[:, sl]
            up = pltpu.roll(y, HEAD_DIM - half, axis=1)
            dn = pltpu.roll(y, half, axis=1)
            o_ref[:, sl] = (y * cos + up * sa + dn * sb).astype(o_ref.dtype)


def _proj(kind, h, w, extras, out_dtype, name):
    n, D = h.shape
    W = w.shape[1]
    tm, tn = 512, 1024
    in_specs = [pl.BlockSpec((tm, D), lambda j, i: (i, 0)),
                pl.BlockSpec((D, tn), lambda j, i: (0, j))]
    args = [h, w]
    for arr in extras:
        if arr.shape[0] == 1:
            in_specs.append(pl.BlockSpec((1, tn), lambda j, i: (0, j)))
        else:
            in_specs.append(pl.BlockSpec((tm, LANES), lambda j, i: (i, 0)))
        args.append(arr)
    return pl.pallas_call(
        functools.partial(_proj_body, kind),
        grid=(W // tn, n // tm),
        in_specs=in_specs,
        out_specs=pl.BlockSpec((tm, tn), lambda j, i: (i, j)),
        out_shape=jax.ShapeDtypeStruct((n, W), out_dtype),
        compiler_params=_params("parallel", "parallel"),
        name=name,
    )(*args)


def _rglru_body(xr_ref, gg_ref, cw_ref, cb_ref, wa_ref, ba_ref, wx_ref, bx_ref, lam_ref,
                o_ref, xbuf, hcar, a_s, u_s):
    s = pl.program_id(1)
    ts, D = xr_ref.shape

    @pl.when(s == 0)
    def _():
        xbuf[0:8, :] = jnp.zeros((8, D), F32)
        hcar[...] = jnp.zeros_like(hcar)

    @pl.when(s > 0)
    def _():
        xbuf[0:8, :] = xbuf[ts:ts + 8, :]

    xbuf[8:8 + ts, :] = xr_ref[...]
    xc = cb_ref[...] + cw_ref[0:1, :] * xbuf[8:8 + ts, :]
    for i in range(1, CONV_W):
        xc = xc + cw_ref[i:i + 1, :] * xbuf[8 - i:8 - i + ts, :]

    ra, rx = [], []
    for n in range(RNN_BLOCKS):
        xb = xc[:, n * RNN_BW:(n + 1) * RNN_BW].astype(BF16)
        ra.append(jnp.dot(xb, wa_ref[n], preferred_element_type=F32))
        rx.append(jnp.dot(xb, wx_ref[n], preferred_element_type=F32))
    r = jax.nn.sigmoid(jnp.concatenate(ra, axis=1) + ba_ref[...])
    ig = jax.nn.sigmoid(jnp.concatenate(rx, axis=1) + bx_ref[...])

    z = -lam_ref[...]
    softplus = jnp.maximum(z, 0.0) + jnp.log1p(jnp.exp(-jnp.abs(z)))
    log_a = -LRU_C * r * softplus
    a = jnp.exp(log_a)
    mult = jnp.sqrt(1.0 - a * a)
    row = lax.broadcasted_iota(jnp.int32, (ts, D), 0)
    mult = jnp.where((row == 0) & (s == 0), 1.0, mult)
    u = mult * (ig * xc)

    rm = row & 7
    for d in (1, 2, 4):
        keep = rm >= d
        a_sh = pltpu.roll(a, d, axis=0)
        u_sh = pltpu.roll(u, d, axis=0)
        u = jnp.where(keep, a * u_sh + u, u)
        a = jnp.where(keep, a * a_sh, a)
    a_s[...] = a
    u_s[...] = u

    def group(g, h):
        r0 = pl.multiple_of(g * 8, 8)
        hg = u_s[pl.ds(r0, 8), :] + a_s[pl.ds(r0, 8), :] * h
        u_s[pl.ds(r0, 8), :] = hg
        return hg[7:8, :]

    hcar[...] = lax.fori_loop(0, ts // 8, group, hcar[...])
    o_ref[...] = (u_s[...] * gg_ref[...]).astype(o_ref.dtype)


def _rglru(xr, gg, conv_w, conv_b, w_a, b_a, w_x, b_x, lam, B, S):
    n, D = xr.shape
    ts = 256
    spb = S // ts
    row = lambda b, s: (b * spb + s, 0)
    vec = lambda b, s: (0, 0)
    return pl.pallas_call(
        _rglru_body,
        grid=(B, spb),
        in_specs=[pl.BlockSpec((ts, D), row),
                  pl.BlockSpec((ts, D), row),
                  pl.BlockSpec((CONV_W, D), vec),
                  pl.BlockSpec((1, D), vec),
                  pl.BlockSpec((RNN_BLOCKS, RNN_BW, RNN_BW), lambda b, s: (0, 0, 0)),
                  pl.BlockSpec((1, D), vec),
                  pl.BlockSpec((RNN_BLOCKS, RNN_BW, RNN_BW), lambda b, s: (0, 0, 0)),
                  pl.BlockSpec((1, D), vec),
                  pl.BlockSpec((1, D), vec)],
        out_specs=pl.BlockSpec((ts, D), row),
        out_shape=jax.ShapeDtypeStruct((n, D), BF16),
        scratch_shapes=[pltpu.VMEM((ts + 8, D), F32), pltpu.VMEM((1, D), F32),
                        pltpu.VMEM((ts, D), F32), pltpu.VMEM((ts, D), F32)],
        compiler_params=_params("arbitrary", "arbitrary"),
        name="rglru",
    )(xr, gg, conv_w, conv_b.reshape(1, D), w_a.astype(BF16), b_a.reshape(1, D),
      w_x.astype(BF16), b_x.reshape(1, D), lam.reshape(1, D))


def _moba_body(q_ref, k_ref, v_ref, o_ref, kaug, vaug, kmean, m_s, acc_s):
    qb = pl.program_id(2)
    S = k_ref.shape[0]
    nb = S // MOBA_BLOCK
    bs = MOBA_BLOCK
    dh = HEAD_DIM
    heads = range(MOBA_HEADS)
    hcol = lambda hh: slice(hh * dh, (hh + 1) * dh)

    @pl.when(qb == 0)
    def _():
        blk = lax.broadcasted_iota(jnp.int32, (S, dh), 0) // bs
        col = lax.broadcasted_iota(jnp.int32, (S, dh), 1)
        onehot = jnp.where(col == blk, 1.0, 0.0).astype(BF16)
        for hh in heads:
            kaug[hh, :, 0:dh] = k_ref[:, hcol(hh)].astype(BF16)
            kaug[hh, :, dh:2 * dh] = onehot
            vaug[hh, :, 0:dh] = v_ref[:, hcol(hh)]
            vaug[hh, :, dh:2 * dh] = jnp.ones((S, dh), BF16)
            kmean[hh] = jnp.zeros((LANES, dh), F32)
            for j in range(nb):
                kmean[hh, j:j + 1, :] = jnp.mean(k_ref[j * bs:(j + 1) * bs, hcol(hh)], axis=0, keepdims=True)

    scale = HEAD_DIM ** -0.5
    nt = (((1,), (1,)), ((), ()))
    c2 = scale * LOG2_E
    r0 = pl.multiple_of(qb * bs, bs)
    lane = lax.broadcasted_iota(jnp.int32, (bs, LANES), 1)
    lane_f = lane.astype(F32)
    past = lane < qb
    rr = lax.broadcasted_iota(jnp.int32, (bs, bs), 0)
    cc = lax.broadcasted_iota(jnp.int32, (bs, bs), 1)

    qaugs = []
    for hh in heads:
        q = q_ref[:, hcol(hh)]
        qbf = q.astype(BF16)

        gate = lax.dot_general(q, kmean[hh], nt, preferred_element_type=F32, precision=HIGHEST)
        g = jnp.where(past, gate, -jnp.inf)
        sel = jnp.zeros(gate.shape, F32)
        for _ in range(MOBA_TOPK):
            m = jnp.max(g, axis=1, keepdims=True)
            idx = jnp.min(jnp.where(g == m, lane_f, float(LANES)), axis=1, keepdims=True)
            pick = (lane_f == idx) & past
            sel = jnp.where(pick, 1.0, sel)
            g = jnp.where(pick, -jnp.inf, g)
        bias = jnp.where(sel > 0.5, 0.0, NEG_BIG).astype(BF16)
        qaugs.append(jnp.concatenate([qbf, bias], axis=1))

        s = lax.dot_general(qbf, kaug[hh, pl.ds(r0, bs), 0:dh], nt, preferred_element_type=F32)
        s = jnp.where(cc <= rr, s, NEG_BIG)
        m0 = jnp.max(s, axis=1, keepdims=True)
        p = jnp.exp2((s - m0) * c2)
        m_s[hh] = m0
        acc_s[hh] = jnp.dot(p.astype(BF16), vaug[hh, pl.ds(r0, bs), :], preferred_element_type=F32)

    cw = MOBA_CHUNK * bs
    for c in range(nb // MOBA_CHUNK):
        @pl.when(c * MOBA_CHUNK < qb)
        def _(c=c):
            for hh in heads:
                sc = lax.dot_general(qaugs[hh], kaug[hh, c * cw:(c + 1) * cw, :], nt,
                                     preferred_element_type=F32)
                m_old = m_s[hh]
                m_new = jnp.maximum(m_old, jnp.max(sc, axis=1, keepdims=True))
                alpha = jnp.exp2((m_old - m_new) * c2)
                pc = jnp.exp2((sc - m_new) * c2)
                m_s[hh] = m_new
                acc_s[hh] = alpha * acc_s[hh] + jnp.dot(pc.astype(BF16), vaug[hh, c * cw:(c + 1) * cw, :],
                                                        preferred_element_type=F32)

    for hh in heads:
        o_ref[:, hcol(hh)] = (acc_s[hh, :, 0:dh] / acc_s[hh, :, dh:dh + 1]).astype(o_ref.dtype)


def _moba(qk, v, B, S):
    n, D = v.shape
    nq = S // MOBA_BLOCK
    hs = MOBA_HEADS
    w = hs * HEAD_DIM
    return pl.pallas_call(
        _moba_body,
        grid=(B, N_HEADS // hs, nq),
        in_specs=[pl.BlockSpec((MOBA_BLOCK, w), lambda b, h, i: (b * nq + i, h)),
                  pl.BlockSpec((S, w), lambda b, h, i: (b, N_HEADS // hs + h)),
                  pl.BlockSpec((S, w), lambda b, h, i: (b, h))],
        out_specs=pl.BlockSpec((MOBA_BLOCK, w), lambda b, h, i: (b * nq + i, h)),
        out_shape=jax.ShapeDtypeStruct((n, D), BF16),
        scratch_shapes=[pltpu.VMEM((hs, S, 2 * HEAD_DIM), BF16), pltpu.VMEM((hs, S, 2 * HEAD_DIM), BF16),
                        pltpu.VMEM((hs, LANES, HEAD_DIM), F32), pltpu.VMEM((hs, MOBA_BLOCK, 1), F32),
                        pltpu.VMEM((hs, MOBA_BLOCK, 2 * HEAD_DIM), F32)],
        compiler_params=_params("parallel", "parallel", "arbitrary"),
        name="moba",
    )(qk, qk, v)


def _merge_body(yr_ref, ya_ref, gr_ref, ga_ref, x_ref, mod_ref, wb_ref, wo_ref, g2_ref, wr_ref, br_ref,
                x1_ref, h2_ref, topi_ref, gate_ref, rank_ref, cnt_ref, carry):
    i = pl.program_id(0)

    @pl.when(i == 0)
    def _():
        carry[...] = jnp.zeros_like(carry)

    tm = MERGE_SUB
    counts = {"total": carry[...]}
    first, second = [
        _merge_stages(slice(h * tm, (h + 1) * tm), counts, yr_ref, ya_ref, gr_ref, ga_ref, x_ref, mod_ref,
                      wb_ref, wo_ref, g2_ref, wr_ref, br_ref, x1_ref, h2_ref, topi_ref, gate_ref, rank_ref)
        for h in range(2)]
    for _ in range(MERGE_MATMUL_STAGES):
        next(first)
    for stage in first:
        next(second)
    for stage in second:
        pass
    carry[...] = counts["total"]
    cnt_ref[...] = jnp.broadcast_to(counts["total"], cnt_ref.shape).astype(jnp.int32)


MERGE_MATMUL_STAGES = 4


def _merge_stages(rows, counts, yr_ref, ya_ref, gr_ref, ga_ref, x_ref, mod_ref, wb_ref, wo_ref, g2_ref, wr_ref,
                  br_ref, x1_ref, h2_ref, topi_ref, gate_ref, rank_ref):
    tm = rows.stop - rows.start
    zr = jnp.dot(yr_ref[rows, :], wb_ref[0], preferred_element_type=F32)
    yield
    za = jnp.dot(ya_ref[rows, :], wb_ref[1], preferred_element_type=F32)
    mix = (gr_ref[rows, :] * zr + ga_ref[rows, :] * za).astype(BF16)
    yield
    mixed = jnp.dot(mix, wo_ref[...], preferred_element_type=F32)
    x1 = x_ref[rows, :] + mod_ref[0, 2:3, :] * mixed
    x1_ref[rows, :] = x1
    h2 = _norm_mod(x1, g2_ref[...], mod_ref[0, 3:4, :], mod_ref[0, 4:5, :])
    h2_ref[rows, :] = h2
    yield

    hi = h2.astype(BF16)
    lo = (h2 - hi.astype(F32)).astype(BF16)
    both = jnp.dot(hi, wr_ref[...], preferred_element_type=F32)
    logits = (both[:, :LANES] + both[:, LANES:]
              + jnp.dot(lo, wr_ref[:, :LANES], preferred_element_type=F32) + br_ref[...])
    yield

    lane = lax.broadcasted_iota(jnp.int32, logits.shape, 1)
    lane_f = lane.astype(F32)
    lg = jnp.where(lane < N_EXPERTS, logits, -jnp.inf)
    vals, idxs = [], []
    onehot = jnp.zeros(logits.shape, F32)
    for _ in range(TOP_K):
        m = jnp.max(lg, axis=1, keepdims=True)
        idx = jnp.min(jnp.where(lg == m, lane_f, float(LANES)), axis=1, keepdims=True)
        pick = lane_f == idx
        onehot = jnp.where(pick, 1.0, onehot)
        lg = jnp.where(pick, -jnp.inf, lg)
        vals.append(m)
        idxs.append(idx)
        yield
    es = [jnp.exp(v - vals[0]) for v in vals]
    den = es[0] + es[1] + es[2] + es[3]
    before0 = counts["total"]

    rr = lax.broadcasted_iota(jnp.int32, (tm, tm), 0)
    cc = lax.broadcasted_iota(jnp.int32, (tm, tm), 1)
    tri = jnp.where(cc < rr, 1.0, 0.0).astype(BF16)
    before = jnp.dot(tri, onehot.astype(BF16), preferred_element_type=F32) + before0

    topi = jnp.zeros(logits.shape, F32)
    gates = jnp.zeros(logits.shape, F32)
    rank = jnp.zeros(logits.shape, F32)
    for k in range(TOP_K):
        rk = jnp.sum(jnp.where(lane_f == idxs[k], before, 0.0), axis=1, keepdims=True)
        topi = jnp.where(lane == k, idxs[k], topi)
        gates = jnp.where(lane == k, es[k] / den, gates)
        rank = jnp.where(lane == k, rk, rank)
    topi_ref[rows, :] = topi.astype(jnp.int32)
    gate_ref[rows, :] = gates
    rank_ref[rows, :] = rank.astype(jnp.int32)
    counts["total"] = before0 + jnp.sum(onehot, axis=0, keepdims=True)
    yield


def _merge(y_rnn, y_att, gl, x2, mod3, w_branch, w_out, norm2_g, w_router, b_router, rows_per_batch):
    n, D = x2.shape
    tm = 2 * MERGE_SUB
    tpb = rows_per_batch // tm
    wr32 = jnp.zeros((D, LANES), F32).at[:, :N_EXPERTS].set(w_router)
    wr_hi = wr32.astype(BF16)
    wr = jnp.concatenate([wr_hi, (wr32 - wr_hi.astype(F32)).astype(BF16)], axis=1)
    br = jnp.zeros((1, LANES), F32).at[0, :N_EXPERTS].set(b_router)
    row = lambda i: (i, 0)
    fixed = lambda i: (0, 0)
    once = pl.Buffered(1)
    wide = jax.ShapeDtypeStruct((n, D), F32)
    narrow_i = jax.ShapeDtypeStruct((n, LANES), jnp.int32)
    return pl.pallas_call(
        _merge_body,
        grid=(n // tm,),
        in_specs=[pl.BlockSpec((tm, D), row),
                  pl.BlockSpec((tm, D), row),
                  pl.BlockSpec((tm, D), lambda i: (i, 0)),
                  pl.BlockSpec((tm, D), lambda i: (i, 1)),
                  pl.BlockSpec((tm, D), row),
                  pl.BlockSpec((1, 6, D), lambda i: (i // tpb, 0, 0)),
                  pl.BlockSpec((2, D, D), lambda i: (0, 0, 0), pipeline_mode=once),
                  pl.BlockSpec((D, D), fixed, pipeline_mode=once),
                  pl.BlockSpec((1, D), fixed),
                  pl.BlockSpec((D, 2 * LANES), fixed, pipeline_mode=once),
                  pl.BlockSpec((1, LANES), fixed)],
        out_specs=[pl.BlockSpec((tm, D), row),
                   pl.BlockSpec((tm, D), row),
                   pl.BlockSpec((tm, LANES), row),
                   pl.BlockSpec((tm, LANES), row),
                   pl.BlockSpec((tm, LANES), row),
                   pl.BlockSpec((8, LANES), fixed)],
        out_shape=[wide, wide, narrow_i, jax.ShapeDtypeStruct((n, LANES), F32), narrow_i,
                   jax.ShapeDtypeStruct((8, LANES), jnp.int32)],
        scratch_shapes=[pltpu.VMEM((1, LANES), F32)],
        compiler_params=_params("arbitrary"),
        name="merge",
    )(y_rnn, y_att, gl, gl, x2, mod3, w_branch.astype(BF16), w_out.astype(BF16),
      norm2_g.reshape(1, D), wr, br)


DISPATCH_TOKENS = 256
ROW_DMA_UNROLL = 8


def _dispatch_body(dest_ref, zblk_ref, src_ref, dst_hbm, zbuf, sem, zsem):
    t = DISPATCH_TOKENS
    c = pl.program_id(0)

    @pl.when(c == 0)
    def _():
        zbuf[...] = jnp.zeros_like(zbuf)

        def zero_copy(b):
            r0 = pl.multiple_of(b * EXPERT_ROWS, EXPERT_ROWS)
            return pltpu.make_async_copy(zbuf, dst_hbm.at[pl.ds(r0, EXPERT_ROWS), :], zsem)

        def zissue(j, carry):
            @pl.when(zblk_ref[j] >= 0)
            def _():
                zero_copy(zblk_ref[j]).start()
            return carry

        def zwait(j, carry):
            @pl.when(zblk_ref[j] >= 0)
            def _():
                zero_copy(zblk_ref[j]).wait()
            return carry

        lax.fori_loop(0, zblk_ref.shape[0], zissue, 0)
        lax.fori_loop(0, zblk_ref.shape[0], zwait, 0)

    def issue(g, carry):
        for u in range(ROW_DMA_UNROLL):
            r = g * ROW_DMA_UNROLL + u
            for k in range(TOP_K):
                d = dest_ref[(c * t + r) * TOP_K + k]
                pltpu.make_async_copy(src_ref.at[pl.ds(r, 1), :], dst_hbm.at[pl.ds(d, 1), :], sem).start()
        return carry

    lax.fori_loop(0, t // ROW_DMA_UNROLL, issue, 0)
    for k in range(TOP_K):
        pltpu.make_async_copy(src_ref, dst_hbm.at[pl.ds(0, t), :], sem).wait()


def _dispatch(h2, dest, zero_blocks, n_rows):
    n, D = h2.shape
    return pl.pallas_call(
        _dispatch_body,
        grid_spec=pltpu.PrefetchScalarGridSpec(
            num_scalar_prefetch=2,
            grid=(n // DISPATCH_TOKENS,),
            in_specs=[pl.BlockSpec((DISPATCH_TOKENS, D), lambda i, d, z: (i, 0))],
            out_specs=pl.BlockSpec(memory_space=pl.ANY),
            scratch_shapes=[pltpu.VMEM((EXPERT_ROWS, D), F32), pltpu.SemaphoreType.DMA,
                            pltpu.SemaphoreType.DMA],
        ),
        out_shape=jax.ShapeDtypeStruct((n_rows, D), F32),
        compiler_params=_params("arbitrary"),
        name="dispatch",
    )(dest, zero_blocks, h2)


def _experts_body(bexp_ref, nused_ref, x_ref, wu_ref, bu_ref, wd_ref, bd_ref, o_ref, wu_bf, wd_bf):
    i = pl.program_id(0)
    dff = wd_ref.shape[1]
    used = i < nused_ref[0]

    new_expert = (i == 0) | (bexp_ref[i] != bexp_ref[jnp.maximum(i - 1, 0)])

    @pl.when(used & new_expert)
    def _():
        wu_bf[...] = wu_ref[0].astype(BF16)
        wd_bf[...] = wd_ref[0].astype(BF16)

    @pl.when(used)
    def _():
        hc = jnp.dot(x_ref[...].astype(BF16), wu_bf[...], preferred_element_type=F32) + bu_ref[0]
        g = jnp.minimum(hc[:, :dff], SWIGLU_LIMIT)
        lin = jnp.clip(hc[:, dff:], -SWIGLU_LIMIT, SWIGLU_LIMIT)
        act = (lin + 1.0) * g * jax.nn.sigmoid(SWIGLU_ALPHA * g)
        o_ref[...] = jnp.dot(act.astype(BF16), wd_bf[...], preferred_element_type=F32) + bd_ref[0]

    @pl.when(i >= nused_ref[0])
    def _():
        o_ref[...] = jnp.zeros_like(o_ref)


def _experts(x_rows, block_exp, n_used, w_up, b_up, w_down, b_down, n_blocks):
    R, D = x_rows.shape
    E, _, F2 = w_up.shape
    dff = w_down.shape[1]
    blk = lambda i, be, nu: (i, 0)
    wsel = lambda i, be, nu: (be[i], 0, 0)
    return pl.pallas_call(
        _experts_body,
        grid_spec=pltpu.PrefetchScalarGridSpec(
            num_scalar_prefetch=2,
            grid=(n_blocks,),
            in_specs=[pl.BlockSpec((EXPERT_ROWS, D), blk),
                      pl.BlockSpec((1, D, F2), wsel),
                      pl.BlockSpec((1, 1, F2), wsel),
                      pl.BlockSpec((1, dff, D), wsel),
                      pl.BlockSpec((1, 1, D), wsel)],
            out_specs=pl.BlockSpec((EXPERT_ROWS, D), blk),
            scratch_shapes=[pltpu.VMEM((D, F2), BF16), pltpu.VMEM((dff, D), BF16)],
        ),
        out_shape=jax.ShapeDtypeStruct((R, D), F32),
        compiler_params=_params("arbitrary"),
        name="experts",
    )(block_exp, n_used, x_rows, w_up, b_up.reshape(E, 1, F2), w_down, b_down.reshape(E, 1, D))


COMBINE_ROWS = 128


def _combine_body(dest_ref, y_hbm, x1_ref, gate_ref, mod_ref, o_ref, buf, sem):
    i = pl.program_id(0)
    t = COMBINE_ROWS

    def issue(g, carry):
        for u in range(ROW_DMA_UNROLL):
            r = g * ROW_DMA_UNROLL + u
            for k in range(TOP_K):
                d = dest_ref[(i * t + r) * TOP_K + k]
                pltpu.make_async_copy(y_hbm.at[pl.ds(d, 1), :], buf.at[pl.ds(r, 1), pl.ds(k * D, D)], sem).start()
        return carry

    D = o_ref.shape[1]
    lax.fori_loop(0, t // ROW_DMA_UNROLL, issue, 0)
    for k in range(TOP_K):
        pltpu.make_async_copy(y_hbm.at[pl.ds(0, t), :], buf.at[:, pl.ds(k * D, D)], sem).wait()
    gates = gate_ref[...]
    y = gates[:, 0:1] * buf[:, 0:D]
    for k in range(1, TOP_K):
        y = y + gates[:, k:k + 1] * buf[:, k * D:(k + 1) * D]
    o_ref[...] = x1_ref[...] + mod_ref[0, 5:6, :] * y


def _combine(y_rows, dest, x1, gates, mod3, rows_per_batch):
    n, D = x1.shape
    t = COMBINE_ROWS
    tpb = rows_per_batch // t
    return pl.pallas_call(
        _combine_body,
        grid_spec=pltpu.PrefetchScalarGridSpec(
            num_scalar_prefetch=1,
            grid=(n // t,),
            in_specs=[pl.BlockSpec(memory_space=pl.ANY),
                      pl.BlockSpec((t, D), lambda i, d: (i, 0)),
                      pl.BlockSpec((t, LANES), lambda i, d: (i, 0)),
                      pl.BlockSpec((1, 6, D), lambda i, d: (i // tpb, 0, 0))],
            out_specs=pl.BlockSpec((t, D), lambda i, d: (i, 0)),
            scratch_shapes=[pltpu.VMEM((t, TOP_K * D), F32), pltpu.SemaphoreType.DMA],
        ),
        out_shape=jax.ShapeDtypeStruct((n, D), F32),
        compiler_params=_params("arbitrary"),
        name="combine",
    )(dest, y_rows, x1, gates, mod3)


def kernel(x, c, positions, ada_w, ada_b, norm1_g, norm2_g, w_in, conv_w, conv_b, w_rg_a, b_rg_a, w_rg_x, b_rg_x, lru_lambda, q_norm_g, k_norm_g, b_gate, w_branch, w_out, w_router, b_router, w_up, b_up, w_down, b_down):
    B, S, D = x.shape
    n = B * S
    depth = ada_w.shape[0]
    x2 = x.reshape(n, D)
    cos_t, sa_t, sb_t = _rope_tables(positions)
    for l in range(depth):
        mod3 = _adaln(c, ada_w[l], ada_b[l]).reshape(B, 6, D)
        h1 = _norm_mod_call(x2, mod3, norm1_g[l], S)
        w = w_in[l].astype(BF16)
        xr = _proj("plain", h1, w[:, 0:D], [], F32, "proj_xr")
        gg = _proj("gelu", h1, w[:, D:2 * D], [], F32, "proj_gr")
        qk_g = jnp.concatenate([jnp.tile(q_norm_g[l], N_HEADS), jnp.tile(k_norm_g[l], N_HEADS)]).reshape(1, 2 * D)
        qk = _proj("qk", h1, w[:, 2 * D:4 * D], [qk_g, cos_t, sa_t, sb_t], F32, "proj_qk")
        v = _proj("plain", h1, w[:, 4 * D:5 * D], [], BF16, "proj_v")
        gl = _proj("gate", h1, w[:, 5 * D:7 * D], [b_gate[l].reshape(1, 2 * D)], F32, "proj_gl")

        y_rnn = _rglru(xr, gg, conv_w[l], conv_b[l], w_rg_a[l], b_rg_a[l], w_rg_x[l], b_rg_x[l],
                       lru_lambda[l], B, S)
        y_att = _moba(qk, v, B, S)

        x1, h2, topi, gates, rank, cnt = _merge(y_rnn, y_att, gl, x2, mod3, w_branch[l], w_out[l],
                                                norm2_g[l], w_router[l], b_router[l], S)

        counts = cnt[0, :N_EXPERTS]
        padded = (counts + EXPERT_ROWS - 1) // EXPERT_ROWS * EXPERT_ROWS
        pad_end = jnp.cumsum(padded)
        pad_start = pad_end - padded
        top_idx = topi[:, :TOP_K]
        dest = (pad_start[top_idx] + rank[:, :TOP_K]).reshape(n * TOP_K).astype(jnp.int32)
        n_blocks = (n * TOP_K) // EXPERT_ROWS + N_EXPERTS
        block_start = jnp.arange(n_blocks, dtype=jnp.int32) * EXPERT_ROWS
        block_exp = jnp.minimum(jnp.sum(block_start[:, None] >= pad_end[None, :], axis=1),
                                N_EXPERTS - 1).astype(jnp.int32)
        n_used = (pad_end[-1] // EXPERT_ROWS).astype(jnp.int32)
        last_blk = jnp.where(padded > 0, pad_end // EXPERT_ROWS - 1, -1)
        spare = n_used + jnp.arange(N_EXPERTS, dtype=jnp.int32)
        spare = jnp.where(spare < n_blocks, spare, -1)
        zero_blocks = jnp.concatenate([last_blk, spare]).astype(jnp.int32)

        x_rows = _dispatch(h2, dest, zero_blocks, n_blocks * EXPERT_ROWS)
        y_rows = _experts(x_rows, block_exp, n_used.reshape(1), w_up[l], b_up[l], w_down[l], b_down[l], n_blocks)
        x2 = _combine(y_rows, dest, x1, gates, mod3, S)
    return x2.reshape(B, S, D)
```

```python
import functools

import jax
import jax.numpy as jnp
from jax import lax
from jax.experimental import pallas as pl
from jax.experimental.pallas import tpu as pltpu

F32 = jnp.float32
BF16 = jnp.bfloat16
HIGHEST = lax.Precision.HIGHEST

EPS = 1e-6
LANES = 128
N_HEADS = 8
HEAD_DIM = 128
RNN_BLOCKS = 8
RNN_BW = 128
CONV_W = 4
LRU_C = 8.0
MOBA_BLOCK = 256
MOBA_TOPK = 3
MOBA_CHUNK = 4
MOBA_HEADS = 4
MERGE_SUB = 256
LOG2_E = 1.4426950408889634
ROPE_DIMS = HEAD_DIM // 4
ROPE_THETA = 500000.0
N_EXPERTS = 32
TOP_K = 4
SWIGLU_LIMIT = 7.0
SWIGLU_ALPHA = 1.702
EXPERT_ROWS = 256
NEG_BIG = -1e30

VMEM_LIMIT = 56 * 1024 * 1024


def _params(*sem):
    return pltpu.CompilerParams(dimension_semantics=sem, vmem_limit_bytes=VMEM_LIMIT)


def _adaln_body(c_ref, w_ref, b_ref, o_ref):
    cs = c_ref[...]
    cs = cs * jax.nn.sigmoid(cs)
    o_ref[...] = jnp.dot(cs, w_ref[...], preferred_element_type=F32, precision=HIGHEST) + b_ref[...]


def _adaln(c, ada_w, ada_b):
    B, D = c.shape
    W = ada_w.shape[1]
    cpad = jnp.zeros((8, D), F32).at[:B].set(c)
    tn = 1024
    mod = pl.pallas_call(
        _adaln_body,
        grid=(W // tn,),
        in_specs=[pl.BlockSpec((8, D), lambda j: (0, 0)),
                  pl.BlockSpec((D, tn), lambda j: (0, j)),
                  pl.BlockSpec((1, tn), lambda j: (0, j))],
        out_specs=pl.BlockSpec((8, tn), lambda j: (0, j)),
        out_shape=jax.ShapeDtypeStruct((8, W), F32),
        compiler_params=_params("parallel"),
        name="adaln",
    )(cpad, ada_w, ada_b.reshape(1, W))
    return mod[:B]


def _rope_body(pos_ref, freq_ref, c_ref, sa_ref, sb_ref):
    ang = pos_ref[...].astype(F32) * freq_ref[...]
    lane = lax.broadcasted_iota(jnp.int32, ang.shape, 1)
    half = ROPE_DIMS // 2
    s = jnp.sin(ang)
    c_ref[...] = jnp.cos(ang)
    sa_ref[...] = jnp.where(lane < half, -s, 0.0)
    sb_ref[...] = jnp.where((lane >= half) & (lane < ROPE_DIMS), s, 0.0)


def _rope_tables(positions):
    n = positions.size
    half = ROPE_DIMS // 2
    freqs = ROPE_THETA ** (-jnp.arange(half, dtype=F32) / half)
    freq_lane = jnp.zeros((1, LANES), F32).at[0, :ROPE_DIMS].set(jnp.concatenate([freqs, freqs]))
    tm = 1024
    tab = jax.ShapeDtypeStruct((n, LANES), F32)
    return pl.pallas_call(
        _rope_body,
        grid=(n // tm,),
        in_specs=[pl.BlockSpec((tm, 1), lambda i: (i, 0)),
                  pl.BlockSpec((1, LANES), lambda i: (0, 0))],
        out_specs=[pl.BlockSpec((tm, LANES), lambda i: (i, 0))] * 3,
        out_shape=[tab, tab, tab],
        compiler_params=_params("parallel"),
        name="rope_tab",
    )(positions.reshape(n, 1), freq_lane)


def _norm_mod(x, g, sh, sc):
    ms = jnp.mean(x * x, axis=-1, keepdims=True)
    y = x * lax.rsqrt(ms + EPS)
    return (y * g) * (1.0 + sc) + sh


def _norm_mod_body(x_ref, mod_ref, g_ref, o_ref):
    o_ref[...] = _norm_mod(x_ref[...], g_ref[...], mod_ref[0, 0:1, :], mod_ref[0, 1:2, :]).astype(o_ref.dtype)


def _norm_mod_call(x2, mod3, g, rows_per_batch):
    n, D = x2.shape
    tm = 512
    tpb = rows_per_batch // tm
    return pl.pallas_call(
        _norm_mod_body,
        grid=(n // tm,),
        in_specs=[pl.BlockSpec((tm, D), lambda i: (i, 0)),
                  pl.BlockSpec((1, 6, D), lambda i: (i // tpb, 0, 0)),
                  pl.BlockSpec((1, D), lambda i: (0, 0))],
        out_specs=pl.BlockSpec((tm, D), lambda i: (i, 0)),
        out_shape=jax.ShapeDtypeStruct((n, D), BF16),
        compiler_params=_params("parallel"),
        name="norm_mod",
    )(x2, mod3, g.reshape(1, D))


def _gelu_tanh(x):
    return 0.5 * x * (1.0 + jnp.tanh(0.7978845608028654 * (x + 0.044715 * (x * x * x))))


def _proj_body(kind, h_ref, w_ref, *rest):
    acc = jnp.dot(h_ref[...], w_ref[...], preferred_element_type=F32)
    if kind == "plain":
        (o_ref,) = rest
        o_ref[...] = acc.astype(o_ref.dtype)
    elif kind == "gelu":
        (o_ref,) = rest
        o_ref[...] = _gelu_tanh(acc).astype(o_ref.dtype)
    elif kind == "gate":
        b_ref, o_ref = rest
        o_ref[...] = jax.nn.sigmoid(acc + b_ref[...]).astype(o_ref.dtype)
    else:
        g_ref, c_ref, sa_ref, sb_ref, o_ref = rest
        cos, sa, sb = c_ref[...], sa_ref[...], sb_ref[...]
        half = ROPE_DIMS // 2
        for hh in range(acc.shape[1] // HEAD_DIM):
            sl = slice(hh * HEAD_DIM, (hh + 1) * HEAD_DIM)
            seg = acc[:, sl]
            ms = jnp.mean(seg * seg, axis=-1, keepdims=True)
            y = seg * lax.rsqrt(ms + EPS) * g_ref[:, sl]
            up = pltpu.roll(y, HEAD_DIM - half, axis=1)
            dn = pltpu.roll(y, half, axis=1)
            o_ref[:, sl] = (y * cos + up * sa + dn * sb).astype(o_ref.dtype)


def _proj(kind, h, w, extras, out_dtype, name):
    n, D = h.shape
    W = w.shape[1]
    tm, tn = 512, 1024
    in_specs = [pl.BlockSpec((tm, D), lambda j, i: (i, 0)),
                pl.BlockSpec((D, tn), lambda j, i: (0, j))]
    args = [h, w]
    for arr in extras:
        if arr.shape[0] == 1:
            in_specs.append(pl.BlockSpec((1, tn), lambda j, i: (0, j)))
        else:
            in_specs.append(pl.BlockSpec((tm, LANES), lambda j, i: (i, 0)))
        args.append(arr)
    return pl.pallas_call(
        functools.partial(_proj_body, kind),
        grid=(W // tn, n // tm),
        in_specs=in_specs,
        out_specs=pl.BlockSpec((tm, tn), lambda j, i: (i, j)),
        out_shape=jax.ShapeDtypeStruct((n, W), out_dtype),
        compiler_params=_params("parallel", "parallel"),
        name=name,
    )(*args)


def _rglru_body(xr_ref, gg_ref, cw_ref, cb_ref, wa_ref, ba_ref, wx_ref, bx_ref, lam_ref,
                o_ref, xbuf, hcar, a_s, u_s):
    s = pl.program_id(1)
    ts, D = xr_ref.shape

    @pl.when(s == 0)
    def _():
        xbuf[0:8, :] = jnp.zeros((8, D), F32)
        hcar[...] = jnp.zeros_like(hcar)

    @pl.when(s > 0)
    def _():
        xbuf[0:8, :] = xbuf[ts:ts + 8, :]

    xbuf[8:8 + ts, :] = xr_ref[...]
    xc = cb_ref[...] + cw_ref[0:1, :] * xbuf[8:8 + ts, :]
    for i in range(1, CONV_W):
        xc = xc + cw_ref[i:i + 1, :] * xbuf[8 - i:8 - i + ts, :]

    ra, rx = [], []
    for n in range(RNN_BLOCKS):
        xb = xc[:, n * RNN_BW:(n + 1) * RNN_BW].astype(BF16)
        ra.append(jnp.dot(xb, wa_ref[n], preferred_element_type=F32))
        rx.append(jnp.dot(xb, wx_ref[n], preferred_element_type=F32))
    r = jax.nn.sigmoid(jnp.concatenate(ra, axis=1) + ba_ref[...])
    ig = jax.nn.sigmoid(jnp.concatenate(rx, axis=1) + bx_ref[...])

    z = -lam_ref[...]
    softplus = jnp.maximum(z, 0.0) + jnp.log1p(jnp.exp(-jnp.abs(z)))
    log_a = -LRU_C * r * softplus
    a = jnp.exp(log_a)
    mult = jnp.sqrt(1.0 - a * a)
    row = lax.broadcasted_iota(jnp.int32, (ts, D), 0)
    mult = jnp.where((row == 0) & (s == 0), 1.0, mult)
    u = mult * (ig * xc)

    rm = row & 7
    for d in (1, 2, 4):
        keep = rm >= d
        a_sh = pltpu.roll(a, d, axis=0)
        u_sh = pltpu.roll(u, d, axis=0)
        u = jnp.where(keep, a * u_sh + u, u)
        a = jnp.where(keep, a * a_sh, a)
    a_s[...] = a
    u_s[...] = u

    def group(g, h):
        r0 = pl.multiple_of(g * 8, 8)
        hg = u_s[pl.ds(r0, 8), :] + a_s[pl.ds(r0, 8), :] * h
        u_s[pl.ds(r0, 8), :] = hg
        return hg[7:8, :]

    hcar[...] = lax.fori_loop(0, ts // 8, group, hcar[...])
    o_ref[...] = (u_s[...] * gg_ref[...]).astype(o_ref.dtype)


def _rglru(xr, gg, conv_w, conv_b, w_a, b_a, w_x, b_x, lam, B, S):
    n, D = xr.shape
    ts = 256
    spb = S // ts
    row = lambda b, s: (b * spb + s, 0)
    vec = lambda b, s: (0, 0)
    return pl.pallas_call(
        _rglru_body,
        grid=(B, spb),
        in_specs=[pl.BlockSpec((ts, D), row),
                  pl.BlockSpec((ts, D), row),
                  pl.BlockSpec((CONV_W, D), vec),
                  pl.BlockSpec((1, D), vec),
                  pl.BlockSpec((RNN_BLOCKS, RNN_BW, RNN_BW), lambda b, s: (0, 0, 0)),
                  pl.BlockSpec((1, D), vec),
                  pl.BlockSpec((RNN_BLOCKS, RNN_BW, RNN_BW), lambda b, s: (0, 0, 0)),
                  pl.BlockSpec((1, D), vec),
                  pl.BlockSpec((1, D), vec)],
        out_specs=pl.BlockSpec((ts, D), row),
        out_shape=jax.ShapeDtypeStruct((n, D), BF16),
        scratch_shapes=[pltpu.VMEM((ts + 8, D), F32), pltpu.VMEM((1, D), F32),
                        pltpu.VMEM((ts, D), F32), pltpu.VMEM((ts, D), F32)],
        compiler_params=_params("arbitrary", "arbitrary"),
        name="rglru",
    )(xr, gg, conv_w, conv_b.reshape(1, D), w_a.astype(BF16), b_a.reshape(1, D),
      w_x.astype(BF16), b_x.reshape(1, D), lam.reshape(1, D))


def _moba_body(q_ref, k_ref, v_ref, o_ref, kaug, vaug, kmean, m_s, acc_s):
    qb = pl.program_id(2)
    S = k_ref.shape[0]
    nb = S // MOBA_BLOCK
    bs = MOBA_BLOCK
    dh = HEAD_DIM
    heads = range(MOBA_HEADS)
    hcol = lambda hh: slice(hh * dh, (hh + 1) * dh)

    @pl.when(qb == 0)
    def _():
        blk = lax.broadcasted_iota(jnp.int32, (S, dh), 0) // bs
        col = lax.broadcasted_iota(jnp.int32, (S, dh), 1)
        onehot = jnp.where(col == blk, 1.0, 0.0).astype(BF16)
        for hh in heads:
            kaug[hh, :, 0:dh] = k_ref[:, hcol(hh)].astype(BF16)
            kaug[hh, :, dh:2 * dh] = onehot
            vaug[hh, :, 0:dh] = v_ref[:, hcol(hh)]
            vaug[hh, :, dh:2 * dh] = jnp.ones((S, dh), BF16)
            kmean[hh] = jnp.zeros((LANES, dh), F32)
            for j in range(nb):
                kmean[hh, j:j + 1, :] = jnp.mean(k_ref[j * bs:(j + 1) * bs, hcol(hh)], axis=0, keepdims=True)

    scale = HEAD_DIM ** -0.5
    nt = (((1,), (1,)), ((), ()))
    c2 = scale * LOG2_E
    r0 = pl.multiple_of(qb * bs, bs)
    lane = lax.broadcasted_iota(jnp.int32, (bs, LANES), 1)
    lane_f = lane.astype(F32)
    past = lane < qb
    rr = lax.broadcasted_iota(jnp.int32, (bs, bs), 0)
    cc = lax.broadcasted_iota(jnp.int32, (bs, bs), 1)

    qaugs = []
    for hh in heads:
        q = q_ref[:, hcol(hh)]
        qbf = q.astype(BF16)

        gate = lax.dot_general(q, kmean[hh], nt, preferred_element_type=F32, precision=HIGHEST)
        g = jnp.where(past, gate, -jnp.inf)
        sel = jnp.zeros(gate.shape, F32)
        for _ in range(MOBA_TOPK):
            m = jnp.max(g, axis=1, keepdims=True)
            idx = jnp.min(jnp.where(g == m, lane_f, float(LANES)), axis=1, keepdims=True)
            pick = (lane_f == idx) & past
            sel = jnp.where(pick, 1.0, sel)
            g = jnp.where(pick, -jnp.inf, g)
        bias = jnp.where(sel > 0.5, 0.0, NEG_BIG).astype(BF16)
        qaugs.append(jnp.concatenate([qbf, bias], axis=1))

        s = lax.dot_general(qbf, kaug[hh, pl.ds(r0, bs), 0:dh], nt, preferred_element_type=F32)
        s = jnp.where(cc <= rr, s, NEG_BIG)
        m0 = jnp.max(s, axis=1, keepdims=True)
        p = jnp.exp2((s - m0) * c2)
        m_s[hh] = m0
        acc_s[hh] = jnp.dot(p.astype(BF16), vaug[hh, pl.ds(r0, bs), :], preferred_element_type=F32)

    cw = MOBA_CHUNK * bs
    for c in range(nb // MOBA_CHUNK):
        @pl.when(c * MOBA_CHUNK < qb)
        def _(c=c):
            for hh in heads:
                sc = lax.dot_general(qaugs[hh], kaug[hh, c * cw:(c + 1) * cw, :], nt,
                                     preferred_element_type=F32)
                m_old = m_s[hh]
                m_new = jnp.maximum(m_old, jnp.max(sc, axis=1, keepdims=True))
                alpha = jnp.exp2((m_old - m_new) * c2)
                pc = jnp.exp2((sc - m_new) * c2)
                m_s[hh] = m_new
                acc_s[hh] = alpha * acc_s[hh] + jnp.dot(pc.astype(BF16), vaug[hh, c * cw:(c + 1) * cw, :],
                                                        preferred_element_type=F32)

    for hh in heads:
        o_ref[:, hcol(hh)] = (acc_s[hh, :, 0:dh] / acc_s[hh, :, dh:dh + 1]).astype(o_ref.dtype)


def _moba(qk, v, B, S):
    n, D = v.shape
    nq = S // MOBA_BLOCK
    hs = MOBA_HEADS
    w = hs * HEAD_DIM
    return pl.pallas_call(
        _moba_body,
        grid=(B, N_HEADS // hs, nq),
        in_specs=[pl.BlockSpec((MOBA_BLOCK, w), lambda b, h, i: (b * nq + i, h)),
                  pl.BlockSpec((S, w), lambda b, h, i: (b, N_HEADS // hs + h)),
                  pl.BlockSpec((S, w), lambda b, h, i: (b, h))],
        out_specs=pl.BlockSpec((MOBA_BLOCK, w), lambda b, h, i: (b * nq + i, h)),
        out_shape=jax.ShapeDtypeStruct((n, D), BF16),
        scratch_shapes=[pltpu.VMEM((hs, S, 2 * HEAD_DIM), BF16), pltpu.VMEM((hs, S, 2 * HEAD_DIM), BF16),
                        pltpu.VMEM((hs, LANES, HEAD_DIM), F32), pltpu.VMEM((hs, MOBA_BLOCK, 1), F32),
                        pltpu.VMEM((hs, MOBA_BLOCK, 2 * HEAD_DIM), F32)],
        compiler_params=_params("parallel", "parallel", "arbitrary"),
        name="moba",
    )(qk, qk, v)


def _merge_body(yr_ref, ya_ref, gr_ref, ga_ref, x_ref, mod_ref, wb_ref, wo_ref, g2_ref, wr_ref, br_ref,
                x1_ref, h2_ref, topi_ref, gate_ref, rank_ref, cnt_ref, carry):
    i = pl.program_id(0)

    @pl.when(i == 0)
    def _():
        carry[...] = jnp.zeros_like(carry)

    tm = MERGE_SUB
    counts = {"total": carry[...]}
    first, second = [
        _merge_stages(slice(h * tm, (h + 1) * tm), counts, yr_ref, ya_ref, gr_ref, ga_ref, x_ref, mod_ref,
                      wb_ref, wo_ref, g2_ref, wr_ref, br_ref, x1_ref, h2_ref, topi_ref, gate_ref, rank_ref)
        for h in range(2)]
    for _ in range(MERGE_MATMUL_STAGES):
        next(first)
    for stage in first:
        next(second)
    for stage in second:
        pass
    carry[...] = counts["total"]
    cnt_ref[...] = jnp.broadcast_to(counts["total"], cnt_ref.shape).astype(jnp.int32)


MERGE_MATMUL_STAGES = 4


def _merge_stages(rows, counts, yr_ref, ya_ref, gr_ref, ga_ref, x_ref, mod_ref, wb_ref, wo_ref, g2_ref, wr_ref,
                  br_ref, x1_ref, h2_ref, topi_ref, gate_ref, rank_ref):
    tm = rows.stop - rows.start
    zr = jnp.dot(yr_ref[rows, :], wb_ref[0], preferred_element_type=F32)
    yield
    za = jnp.dot(ya_ref[rows, :], wb_ref[1], preferred_element_type=F32)
    mix = (gr_ref[rows, :] * zr + ga_ref[rows, :] * za).astype(BF16)
    yield
    mixed = jnp.dot(mix, wo_ref[...], preferred_element_type=F32)
    x1 = x_ref[rows, :] + mod_ref[0, 2:3, :] * mixed
    x1_ref[rows, :] = x1
    h2 = _norm_mod(x1, g2_ref[...], mod_ref[0, 3:4, :], mod_ref[0, 4:5, :])
    h2_ref[rows, :] = h2
    yield

    hi = h2.astype(BF16)
    lo = (h2 - hi.astype(F32)).astype(BF16)
    both = jnp.dot(hi, wr_ref[...], preferred_element_type=F32)
    logits = (both[:, :LANES] + both[:, LANES:]
              + jnp.dot(lo, wr_ref[:, :LANES], preferred_element_type=F32) + br_ref[...])
    yield

    lane = lax.broadcasted_iota(jnp.int32, logits.shape, 1)
    lane_f = lane.astype(F32)
    lg = jnp.where(lane < N_EXPERTS, logits, -jnp.inf)
    vals, idxs = [], []
    onehot = jnp.zeros(logits.shape, F32)
    for _ in range(TOP_K):
        m = jnp.max(lg, axis=1, keepdims=True)
        idx = jnp.min(jnp.where(lg == m, lane_f, float(LANES)), axis=1, keepdims=True)
        pick = lane_f == idx
        onehot = jnp.where(pick, 1.0, onehot)
        lg = jnp.where(pick, -jnp.inf, lg)
        vals.append(m)
        idxs.append(idx)
        yield
    es = [jnp.exp(v - vals[0]) for v in vals]
    den = es[0] + es[1] + es[2] + es[3]
    before0 = counts["total"]

    rr = lax.broadcasted_iota(jnp.int32, (tm, tm), 0)
    cc = lax.broadcasted_iota(jnp.int32, (tm, tm), 1)
    tri = jnp.where(cc < rr, 1.0, 0.0).astype(BF16)
    before = jnp.dot(tri, onehot.astype(BF16), preferred_element_type=F32) + before0

    topi = jnp.zeros(logits.shape, F32)
    gates = jnp.zeros(logits.shape, F32)
    rank = jnp.zeros(logits.shape, F32)
    for k in range(TOP_K):
        rk = jnp.sum(jnp.where(lane_f == idxs[k], before, 0.0), axis=1, keepdims=True)
        topi = jnp.where(lane == k, idxs[k], topi)
        gates = jnp.where(lane == k, es[k] / den, gates)
        rank = jnp.where(lane == k, rk, rank)
    topi_ref[rows, :] = topi.astype(jnp.int32)
    gate_ref[rows, :] = gates
    rank_ref[rows, :] = rank.astype(jnp.int32)
    counts["total"] = before0 + jnp.sum(onehot, axis=0, keepdims=True)
    yield


def _merge(y_rnn, y_att, gl, x2, mod3, w_branch, w_out, norm2_g, w_router, b_router, rows_per_batch):
    n, D = x2.shape
    tm = 2 * MERGE_SUB
    tpb = rows_per_batch // tm
    wr32 = jnp.zeros((D, LANES), F32).at[:, :N_EXPERTS].set(w_router)
    wr_hi = wr32.astype(BF16)
    wr = jnp.concatenate([wr_hi, (wr32 - wr_hi.astype(F32)).astype(BF16)], axis=1)
    br = jnp.zeros((1, LANES), F32).at[0, :N_EXPERTS].set(b_router)
    row = lambda i: (i, 0)
    fixed = lambda i: (0, 0)
    once = pl.Buffered(1)
    wide = jax.ShapeDtypeStruct((n, D), F32)
    narrow_i = jax.ShapeDtypeStruct((n, LANES), jnp.int32)
    return pl.pallas_call(
        _merge_body,
        grid=(n // tm,),
        in_specs=[pl.BlockSpec((tm, D), row),
                  pl.BlockSpec((tm, D), row),
                  pl.BlockSpec((tm, D), lambda i: (i, 0)),
                  pl.BlockSpec((tm, D), lambda i: (i, 1)),
                  pl.BlockSpec((tm, D), row),
                  pl.BlockSpec((1, 6, D), lambda i: (i // tpb, 0, 0)),
                  pl.BlockSpec((2, D, D), lambda i: (0, 0, 0), pipeline_mode=once),
                  pl.BlockSpec((D, D), fixed, pipeline_mode=once),
                  pl.BlockSpec((1, D), fixed),
                  pl.BlockSpec((D, 2 * LANES), fixed, pipeline_mode=once),
                  pl.BlockSpec((1, LANES), fixed)],
        out_specs=[pl.BlockSpec((tm, D), row),
                   pl.BlockSpec((tm, D), row),
                   pl.BlockSpec((tm, LANES), row),
                   pl.BlockSpec((tm, LANES), row),
                   pl.BlockSpec((tm, LANES), row),
                   pl.BlockSpec((8, LANES), fixed)],
        out_shape=[wide, wide, narrow_i, jax.ShapeDtypeStruct((n, LANES), F32), narrow_i,
                   jax.ShapeDtypeStruct((8, LANES), jnp.int32)],
        scratch_shapes=[pltpu.VMEM((1, LANES), F32)],
        compiler_params=_params("arbitrary"),
        name="merge",
    )(y_rnn, y_att, gl, gl, x2, mod3, w_branch.astype(BF16), w_out.astype(BF16),
      norm2_g.reshape(1, D), wr, br)


DISPATCH_TOKENS = 256
ROW_DMA_UNROLL = 8


def _dispatch_body(dest_ref, zblk_ref, src_ref, dst_hbm, zbuf, sem, zsem):
    t = DISPATCH_TOKENS
    c = pl.program_id(0)

    @pl.when(c == 0)
    def _():
        zbuf[...] = jnp.zeros_like(zbuf)

        def zero_copy(b):
            r0 = pl.multiple_of(b * EXPERT_ROWS, EXPERT_ROWS)
            return pltpu.make_async_copy(zbuf, dst_hbm.at[pl.ds(r0, EXPERT_ROWS), :], zsem)

        def zissue(j, carry):
            @pl.when(zblk_ref[j] >= 0)
            def _():
                zero_copy(zblk_ref[j]).start()
            return carry

        def zwait(j, carry):
            @pl.when(zblk_ref[j] >= 0)
            def _():
                zero_copy(zblk_ref[j]).wait()
            return carry

        lax.fori_loop(0, zblk_ref.shape[0], zissue, 0)
        lax.fori_loop(0, zblk_ref.shape[0], zwait, 0)

    def issue(g, carry):
        for u in range(ROW_DMA_UNROLL):
            r = g * ROW_DMA_UNROLL + u
            for k in range(TOP_K):
                d = dest_ref[(c * t + r) * TOP_K + k]
                pltpu.make_async_copy(src_ref.at[pl.ds(r, 1), :], dst_hbm.at[pl.ds(d, 1), :], sem).start()
        return carry

    lax.fori_loop(0, t // ROW_DMA_UNROLL, issue, 0)
    for k in range(TOP_K):
        pltpu.make_async_copy(src_ref, dst_hbm.at[pl.ds(0, t), :], sem).wait()


def _dispatch(h2, dest, zero_blocks, n_rows):
    n, D = h2.shape
    return pl.pallas_call(
        _dispatch_body,
        grid_spec=pltpu.PrefetchScalarGridSpec(
            num_scalar_prefetch=2,
            grid=(n // DISPATCH_TOKENS,),
            in_specs=[pl.BlockSpec((DISPATCH_TOKENS, D), lambda i, d, z: (i, 0))],
            out_specs=pl.BlockSpec(memory_space=pl.ANY),
            scratch_shapes=[pltpu.VMEM((EXPERT_ROWS, D), F32), pltpu.SemaphoreType.DMA,
                            pltpu.SemaphoreType.DMA],
        ),
        out_shape=jax.ShapeDtypeStruct((n_rows, D), F32),
        compiler_params=_params("arbitrary"),
        name="dispatch",
    )(dest, zero_blocks, h2)


def _experts_body(bexp_ref, nused_ref, x_ref, wu_ref, bu_ref, wd_ref, bd_ref, o_ref, wu_bf, wd_bf):
    i = pl.program_id(0)
    dff = wd_ref.shape[1]
    used = i < nused_ref[0]

    new_expert = (i == 0) | (bexp_ref[i] != bexp_ref[jnp.maximum(i - 1, 0)])

    @pl.when(used & new_expert)
    def _():
        wu_bf[...] = wu_ref[0].astype(BF16)
        wd_bf[...] = wd_ref[0].astype(BF16)

    @pl.when(used)
    def _():
        hc = jnp.dot(x_ref[...].astype(BF16), wu_bf[...], preferred_element_type=F32) + bu_ref[0]
        g = jnp.minimum(hc[:, :dff], SWIGLU_LIMIT)
        lin = jnp.clip(hc[:, dff:], -SWIGLU_LIMIT, SWIGLU_LIMIT)
        act = (lin + 1.0) * g * jax.nn.sigmoid(SWIGLU_ALPHA * g)
        o_ref[...] = jnp.dot(act.astype(BF16), wd_bf[...], preferred_element_type=F32) + bd_ref[0]

    @pl.when(i >= nused_ref[0])
    def _():
        o_ref[...] = jnp.zeros_like(o_ref)


def _experts(x_rows, block_exp, n_used, w_up, b_up, w_down, b_down, n_blocks):
    R, D = x_rows.shape
    E, _, F2 = w_up.shape
    dff = w_down.shape[1]
    blk = lambda i, be, nu: (i, 0)
    wsel = lambda i, be, nu: (be[i], 0, 0)
    return pl.pallas_call(
        _experts_body,
        grid_spec=pltpu.PrefetchScalarGridSpec(
            num_scalar_prefetch=2,
            grid=(n_blocks,),
            in_specs=[pl.BlockSpec((EXPERT_ROWS, D), blk),
                      pl.BlockSpec((1, D, F2), wsel),
                      pl.BlockSpec((1, 1, F2), wsel),
                      pl.BlockSpec((1, dff, D), wsel),
                      pl.BlockSpec((1, 1, D), wsel)],
            out_specs=pl.BlockSpec((EXPERT_ROWS, D), blk),
            scratch_shapes=[pltpu.VMEM((D, F2), BF16), pltpu.VMEM((dff, D), BF16)],
        ),
        out_shape=jax.ShapeDtypeStruct((R, D), F32),
        compiler_params=_params("arbitrary"),
        name="experts",
    )(block_exp, n_used, x_rows, w_up, b_up.reshape(E, 1, F2), w_down, b_down.reshape(E, 1, D))


COMBINE_ROWS = 128


def _combine_body(dest_ref, y_hbm, x1_ref, gate_ref, mod_ref, o_ref, buf, sem):
    i = pl.program_id(0)
    t = COMBINE_ROWS

    def issue(g, carry):
        for u in range(ROW_DMA_UNROLL):
            r = g * ROW_DMA_UNROLL + u
            for k in range(TOP_K):
                d = dest_ref[(i * t + r) * TOP_K + k]
                pltpu.make_async_copy(y_hbm.at[pl.ds(d, 1), :], buf.at[pl.ds(r, 1), pl.ds(k * D, D)], sem).start()
        return carry

    D = o_ref.shape[1]
    lax.fori_loop(0, t // ROW_DMA_UNROLL, issue, 0)
    for k in range(TOP_K):
        pltpu.make_async_copy(y_hbm.at[pl.ds(0, t), :], buf.at[:, pl.ds(k * D, D)], sem).wait()
    gates = gate_ref[...]
    y = gates[:, 0:1] * buf[:, 0:D]
    for k in range(1, TOP_K):
        y = y + gates[:, k:k + 1] * buf[:, k * D:(k + 1) * D]
    o_ref[...] = x1_ref[...] + mod_ref[0, 5:6, :] * y


def _combine(y_rows, dest, x1, gates, mod3, rows_per_batch):
    n, D = x1.shape
    t = COMBINE_ROWS
    tpb = rows_per_batch // t
    return pl.pallas_call(
        _combine_body,
        grid_spec=pltpu.PrefetchScalarGridSpec(
            num_scalar_prefetch=1,
            grid=(n // t,),
            in_specs=[pl.BlockSpec(memory_space=pl.ANY),
                      pl.BlockSpec((t, D), lambda i, d: (i, 0)),
                      pl.BlockSpec((t, LANES), lambda i, d: (i, 0)),
                      pl.BlockSpec((1, 6, D), lambda i, d: (i // tpb, 0, 0))],
            out_specs=pl.BlockSpec((t, D), lambda i, d: (i, 0)),
            scratch_shapes=[pltpu.VMEM((t, TOP_K * D), F32), pltpu.SemaphoreType.DMA],
        ),
        out_shape=jax.ShapeDtypeStruct((n, D), F32),
        compiler_params=_params("arbitrary"),
        name="combine",
    )(dest, y_rows, x1, gates, mod3)


MAP_SRC_BITS = 14
MOE_PIECES = 4


def _moe_body(bexp_ref, map_ref, h2_hbm, wu_ref, bu_ref, wd_ref, bd_ref, ytok_hbm,
              xbuf, ybuf, xb, yacc, wu_bf, wd_bf, gsem, ssem, zsem):
    i = pl.program_id(0)
    last = pl.num_programs(0) - 1
    R = EXPERT_ROWS
    dff = wd_ref.shape[1]
    pw = dff // MOE_PIECES
    cur, nxt = i % 2, 1 - i % 2
    xcur, xnew = i % 3, (i + 2) % 3
    src_mask = (1 << MAP_SRC_BITS) - 1
    n_tok_rows = ytok_hbm.shape[0] - 2 * R

    def gather_row(b, r, slot):
        src = map_ref[(b + 1) * R + r] & src_mask
        pltpu.make_async_copy(h2_hbm.at[pl.ds(src, 1), :], xbuf.at[slot, pl.ds(r, 1), :], gsem.at[slot]).start()

    def scatter_row(b, r, slot):
        dst = map_ref[(b + 1) * R + r] >> MAP_SRC_BITS
        pltpu.make_async_copy(ybuf.at[slot, pl.ds(r, 1), :], ytok_hbm.at[pl.ds(dst, 1), :], ssem.at[slot]).start()

    def gather_wait(slot):
        pltpu.make_async_copy(h2_hbm.at[pl.ds(0, R), :], xbuf.at[slot], gsem.at[slot]).wait()

    def scatter_wait(slot):
        pltpu.make_async_copy(ybuf.at[slot], ytok_hbm.at[pl.ds(0, R), :], ssem.at[slot]).wait()

    def for_rows(fn):
        def body(g, carry):
            for u in range(ROW_DMA_UNROLL):
                fn(g * ROW_DMA_UNROLL + u)
            return carry
        lax.fori_loop(0, R // ROW_DMA_UNROLL, body, 0)

    @pl.when(i == 0)
    def _():
        ybuf[...] = jnp.zeros_like(ybuf)
        for s in range(2):
            zc = pltpu.make_async_copy(ybuf.at[s], ytok_hbm.at[pl.ds(n_tok_rows + s * R, R), :], zsem)
            zc.start()
            zc.wait()
        for_rows(lambda r: gather_row(0, r, 0))
        for_rows(lambda r: gather_row(1, r, 1))

    @pl.when((i == 0) | (bexp_ref[i] != bexp_ref[jnp.maximum(i - 1, 0)]))
    def _():
        wu_bf[...] = wu_ref[0].astype(BF16)
        wd_bf[...] = wd_ref[0].astype(BF16)

    gather_wait(xcur)
    xb[...] = xbuf[xcur].astype(BF16)

    rows_per_piece = R // MOE_PIECES
    for c in range(MOE_PIECES):
        @pl.when(bexp_ref[c] >= 0)
        def _(c=c):
            cols = slice(c * pw, (c + 1) * pw)
            cols_lin = slice(dff + c * pw, dff + (c + 1) * pw)
            for r in range(c * rows_per_piece, (c + 1) * rows_per_piece):
                gather_row(i + 2, r, xnew)
                scatter_row(i - 1, r, nxt)
            x = xb[...]
            hg = jnp.dot(x, wu_bf[:, cols], preferred_element_type=F32) + bu_ref[0, :, cols]
            hl = jnp.dot(x, wu_bf[:, cols_lin], preferred_element_type=F32) + bu_ref[0, :, cols_lin]
            g = jnp.minimum(hg, SWIGLU_LIMIT)
            lin = jnp.clip(hl, -SWIGLU_LIMIT, SWIGLU_LIMIT)
            act = ((lin + 1.0) * g * jax.nn.sigmoid(SWIGLU_ALPHA * g)).astype(BF16)
            part = jnp.dot(act, wd_bf[cols, :], preferred_element_type=F32)
            if c == 0:
                yacc[...] = part + bd_ref[0]
            else:
                yacc[...] += part

    @pl.when(i >= 1)
    def _():
        scatter_wait(cur)

    ybuf[cur] = yacc[...]

    @pl.when(i == last)
    def _():
        for_rows(lambda r: scatter_row(i, r, cur))
        scatter_wait(nxt)
        scatter_wait(cur)
        gather_wait((i + 1) % 3)
        gather_wait((i + 2) % 3)


def _moe(h2, block_exp, row_map, w_up, b_up, w_down, b_down, n_blocks):
    n, D = h2.shape
    E, _, F2 = w_up.shape
    dff = w_down.shape[1]
    wsel = lambda i, be, rm: (be[i], 0, 0)
    return pl.pallas_call(
        _moe_body,
        grid_spec=pltpu.PrefetchScalarGridSpec(
            num_scalar_prefetch=2,
            grid=(n_blocks,),
            in_specs=[pl.BlockSpec(memory_space=pl.ANY),
                      pl.BlockSpec((1, D, F2), wsel),
                      pl.BlockSpec((1, 1, F2), wsel),
                      pl.BlockSpec((1, dff, D), wsel),
                      pl.BlockSpec((1, 1, D), wsel)],
            out_specs=pl.BlockSpec(memory_space=pl.ANY),
            scratch_shapes=[pltpu.VMEM((3, EXPERT_ROWS, D), F32), pltpu.VMEM((2, EXPERT_ROWS, D), F32),
                            pltpu.VMEM((EXPERT_ROWS, D), BF16), pltpu.VMEM((EXPERT_ROWS, D), F32),
                            pltpu.VMEM((D, F2), BF16), pltpu.VMEM((dff, D), BF16),
                            pltpu.SemaphoreType.DMA((3,)), pltpu.SemaphoreType.DMA((2,)),
                            pltpu.SemaphoreType.DMA],
        ),
        out_shape=jax.ShapeDtypeStruct((TOP_K * n + 2 * EXPERT_ROWS, D), F32),
        compiler_params=_params("arbitrary"),
        name="moe",
    )(block_exp, row_map, h2, w_up, b_up.reshape(E, 1, F2), w_down, b_down.reshape(E, 1, D))


def _moe_sum_body(y0_ref, y1_ref, y2_ref, y3_ref, x1_ref, gate_ref, mod_ref, o_ref):
    gates = gate_ref[...]
    y = gates[:, 0:1] * y0_ref[...]
    for k, y_ref in enumerate((y1_ref, y2_ref, y3_ref), start=1):
        y = y + gates[:, k:k + 1] * y_ref[...]
    o_ref[...] = x1_ref[...] + mod_ref[0, 5:6, :] * y


def _moe_sum(y_tok, x1, gates, mod3, rows_per_batch):
    n, D = x1.shape
    t = 512
    tpb = rows_per_batch // t
    nt = n // t
    slab = lambda k: pl.BlockSpec((t, D), lambda i: (k * nt + i, 0))
    return pl.pallas_call(
        _moe_sum_body,
        grid=(nt,),
        in_specs=[slab(0), slab(1), slab(2), slab(3),
                  pl.BlockSpec((t, D), lambda i: (i, 0)),
                  pl.BlockSpec((t, LANES), lambda i: (i, 0)),
                  pl.BlockSpec((1, 6, D), lambda i: (i // tpb, 0, 0))],
        out_specs=pl.BlockSpec((t, D), lambda i: (i, 0)),
        out_shape=jax.ShapeDtypeStruct((n, D), F32),
        compiler_params=_params("parallel"),
        name="moe_sum",
    )(y_tok, y_tok, y_tok, y_tok, x1, gates, mod3)


def kernel(x, c, positions, ada_w, ada_b, norm1_g, norm2_g, w_in, conv_w, conv_b, w_rg_a, b_rg_a, w_rg_x, b_rg_x, lru_lambda, q_norm_g, k_norm_g, b_gate, w_branch, w_out, w_router, b_router, w_up, b_up, w_down, b_down):
    B, S, D = x.shape
    n = B * S
    depth = ada_w.shape[0]
    x2 = x.reshape(n, D)
    cos_t, sa_t, sb_t = _rope_tables(positions)
    for l in range(depth):
        mod3 = _adaln(c, ada_w[l], ada_b[l]).reshape(B, 6, D)
        h1 = _norm_mod_call(x2, mod3, norm1_g[l], S)
        w = w_in[l].astype(BF16)
        xr = _proj("plain", h1, w[:, 0:D], [], F32, "proj_xr")
        gg = _proj("gelu", h1, w[:, D:2 * D], [], F32, "proj_gr")
        qk_g = jnp.concatenate([jnp.tile(q_norm_g[l], N_HEADS), jnp.tile(k_norm_g[l], N_HEADS)]).reshape(1, 2 * D)
        qk = _proj("qk", h1, w[:, 2 * D:4 * D], [qk_g, cos_t, sa_t, sb_t], F32, "proj_qk")
        v = _proj("plain", h1, w[:, 4 * D:5 * D], [], BF16, "proj_v")
        gl = _proj("gate", h1, w[:, 5 * D:7 * D], [b_gate[l].reshape(1, 2 * D)], F32, "proj_gl")

        y_rnn = _rglru(xr, gg, conv_w[l], conv_b[l], w_rg_a[l], b_rg_a[l], w_rg_x[l], b_rg_x[l],
                       lru_lambda[l], B, S)
        y_att = _moba(qk, v, B, S)

        x1, h2, topi, gates, rank, cnt = _merge(y_rnn, y_att, gl, x2, mod3, w_branch[l], w_out[l],
                                                norm2_g[l], w_router[l], b_router[l], S)

        counts = cnt[0, :N_EXPERTS]
        padded = (counts + EXPERT_ROWS - 1) // EXPERT_ROWS * EXPERT_ROWS
        pad_end = jnp.cumsum(padded)
        pad_start = pad_end - padded
        top_idx = topi[:, :TOP_K]
        dest = (pad_start[top_idx] + rank[:, :TOP_K]).reshape(n * TOP_K).astype(jnp.int32)
        n_blocks = (n * TOP_K) // EXPERT_ROWS + N_EXPERTS
        block_start = jnp.arange(n_blocks, dtype=jnp.int32) * EXPERT_ROWS
        block_exp = jnp.minimum(jnp.sum(block_start[:, None] >= pad_end[None, :], axis=1),
                                N_EXPERTS - 1).astype(jnp.int32)
        n_rows = n_blocks * EXPERT_ROWS
        assert n <= 1 << MAP_SRC_BITS
        choice =jnp.full((n_rows,), -1, jnp.int32).at[dest].set(jnp.arange(n * TOP_K, dtype=jnp.int32))
        choice = jnp.concatenate([jnp.full((EXPERT_ROWS,), -1, jnp.int32), choice,
                                  jnp.full((2 * EXPERT_ROWS,), -1, jnp.int32)])
        rix = jnp.arange(choice.shape[0], dtype=jnp.int32)
        spare_row = n * TOP_K + (rix // EXPERT_ROWS) % 2 * EXPERT_ROWS + rix % EXPERT_ROWS
        tok, slot = choice // TOP_K, choice % TOP_K
        src_row = jnp.where(choice >= 0, tok, 0)
        dst_row = jnp.where(choice >= 0, slot * n + tok, spare_row)
        row_map = (src_row | (dst_row << MAP_SRC_BITS)).astype(jnp.int32)

        y_tok = _moe(h2, block_exp, row_map, w_up[l], b_up[l], w_down[l], b_down[l], n_blocks)
        x2 = _moe_sum(y_tok, x1, gates, mod3, S)
    return x2.reshape(B, S, D)
```

```python
import functools

import jax
import jax.numpy as jnp
from jax import lax
from jax.experimental import pallas as pl
from jax.experimental.pallas import tpu as pltpu

F32 = jnp.float32
BF16 = jnp.bfloat16
HIGHEST = lax.Precision.HIGHEST

EPS = 1e-6
LANES = 128
N_HEADS = 8
HEAD_DIM = 128
RNN_BLOCKS = 8
RNN_BW = 128
CONV_W = 4
LRU_C = 8.0
MOBA_BLOCK = 256
MOBA_TOPK = 3
MOBA_CHUNK = 4
MOBA_HEADS = 4
MERGE_SUB = 256
LOG2_E = 1.4426950408889634
ROPE_DIMS = HEAD_DIM // 4
ROPE_THETA = 500000.0
N_EXPERTS = 32
TOP_K = 4
SWIGLU_LIMIT = 7.0
SWIGLU_ALPHA = 1.702
EXPERT_ROWS = 256
NEG_BIG = -1e30

VMEM_LIMIT = 56 * 1024 * 1024


def _params(*sem):
    return pltpu.CompilerParams(dimension_semantics=sem, vmem_limit_bytes=VMEM_LIMIT)


def _adaln_body(c_ref, w_ref, b_ref, o_ref):
    cs = c_ref[...]
    cs = cs * jax.nn.sigmoid(cs)
    o_ref[...] = jnp.dot(cs, w_ref[...], preferred_element_type=F32, precision=HIGHEST) + b_ref[...]


def _adaln(c, ada_w, ada_b):
    B, D = c.shape
    W = ada_w.shape[1]
    cpad = jnp.zeros((8, D), F32).at[:B].set(c)
    tn = 1024
    mod = pl.pallas_call(
        _adaln_body,
        grid=(W // tn,),
        in_specs=[pl.BlockSpec((8, D), lambda j: (0, 0)),
                  pl.BlockSpec((D, tn), lambda j: (0, j)),
                  pl.BlockSpec((1, tn), lambda j: (0, j))],
        out_specs=pl.BlockSpec((8, tn), lambda j: (0, j)),
        out_shape=jax.ShapeDtypeStruct((8, W), F32),
        compiler_params=_params("parallel"),
        name="adaln",
    )(cpad, ada_w, ada_b.reshape(1, W))
    return mod[:B]


def _rope_head_perm():
    half = ROPE_DIMS // 2
    mid = HEAD_DIM // 2
    return jnp.concatenate([jnp.arange(0, half), jnp.arange(ROPE_DIMS, mid + half),
                            jnp.arange(half, ROPE_DIMS), jnp.arange(mid + half, HEAD_DIM)])


def _rope_body(pos_ref, freq_ref, c_ref, s_ref):
    ang = pos_ref[...].astype(F32) * freq_ref[...]
    lane = lax.broadcasted_iota(jnp.int32, ang.shape, 1)
    half = ROPE_DIMS // 2
    mid = HEAD_DIM // 2
    s = jnp.sin(ang)
    c_ref[...] = jnp.cos(ang)
    s_ref[...] = jnp.where(lane < half, -s, jnp.where((lane >= mid) & (lane < mid + half), s, 0.0))


def _rope_tables(positions):
    n = positions.size
    half = ROPE_DIMS // 2
    mid = HEAD_DIM // 2
    freqs = ROPE_THETA ** (-jnp.arange(half, dtype=F32) / half)
    freq_lane = jnp.zeros((1, LANES), F32).at[0, :half].set(freqs).at[0, mid:mid + half].set(freqs)
    tm = 1024
    tab = jax.ShapeDtypeStruct((n, LANES), F32)
    return pl.pallas_call(
        _rope_body,
        grid=(n // tm,),
        in_specs=[pl.BlockSpec((tm, 1), lambda i: (i, 0)),
                  pl.BlockSpec((1, LANES), lambda i: (0, 0))],
        out_specs=[pl.BlockSpec((tm, LANES), lambda i: (i, 0))] * 2,
        out_shape=[tab, tab],
        compiler_params=_params("parallel"),
        name="rope_tab",
    )(positions.reshape(n, 1), freq_lane)


def _norm_mod(x, g, sh, sc):
    ms = jnp.mean(x * x, axis=-1, keepdims=True)
    y = x * lax.rsqrt(ms + EPS)
    return (y * g) * (1.0 + sc) + sh


def _norm_mod_body(x_ref, mod_ref, g_ref, o_ref):
    o_ref[...] = _norm_mod(x_ref[...], g_ref[...], mod_ref[0, 0:1, :], mod_ref[0, 1:2, :]).astype(o_ref.dtype)


def _norm_mod_call(x2, mod3, g, rows_per_batch):
    n, D = x2.shape
    tm = 512
    tpb = rows_per_batch // tm
    return pl.pallas_call(
        _norm_mod_body,
        grid=(n // tm,),
        in_specs=[pl.BlockSpec((tm, D), lambda i: (i, 0)),
                  pl.BlockSpec((1, 6, D), lambda i: (i // tpb, 0, 0)),
                  pl.BlockSpec((1, D), lambda i: (0, 0))],
        out_specs=pl.BlockSpec((tm, D), lambda i: (i, 0)),
        out_shape=jax.ShapeDtypeStruct((n, D), BF16),
        compiler_params=_params("parallel"),
        name="norm_mod",
    )(x2, mod3, g.reshape(1, D))


def _gelu_tanh(x):
    return 0.5 * x * (1.0 + jnp.tanh(0.7978845608028654 * (x + 0.044715 * (x * x * x))))


PROJ_SUB = 256
PROJ_PIECE = 256


def _qk_head(seg, g, cos, sin):
    ms = jnp.mean(seg * seg, axis=-1, keepdims=True)
    y = seg * lax.rsqrt(ms + EPS) * g
    return y * cos + pltpu.roll(y, HEAD_DIM // 2, axis=1) * sin


def _proj_stages(kind, rows, h_ref, w_ref, extras, o_ref):
    n_pieces = w_ref.shape[1] // PROJ_PIECE
    pieces = []
    for j in range(n_pieces):
        pieces.append(jnp.dot(h_ref[rows, :], w_ref[:, j * PROJ_PIECE:(j + 1) * PROJ_PIECE],
                              preferred_element_type=F32))
        yield
    for j in range(n_pieces):
        cols = slice(j * PROJ_PIECE, (j + 1) * PROJ_PIECE)
        acc = pieces[j]
        if kind == "gelu":
            acc = _gelu_tanh(acc)
        elif kind == "gate":
            (b_ref,) = extras
            acc = jax.nn.sigmoid(acc + b_ref[:, cols])
        elif kind == "qk":
            g_ref, c_ref, s_ref = extras
            cos, sin = c_ref[rows, :], s_ref[rows, :]
            acc = jnp.concatenate(
                [_qk_head(acc[:, o:o + HEAD_DIM], g_ref[:, cols.start + o:cols.start + o + HEAD_DIM], cos, sin)
                 for o in range(0, PROJ_PIECE, HEAD_DIM)], axis=1)
        o_ref[rows, cols] = acc.astype(o_ref.dtype)
        yield


def _proj_body(kind, h_ref, w_ref, *rest):
    *extras, o_ref = rest
    first, second = [_proj_stages(kind, slice(t * PROJ_SUB, (t + 1) * PROJ_SUB), h_ref, w_ref, extras, o_ref)
                     for t in range(2)]
    n_pieces = w_ref.shape[1] // PROJ_PIECE
    for _ in range(n_pieces):
        next(first)
    for _ in range(n_pieces):
        next(second)
        next(first)
    for stage in second:
        pass


def _proj(kind, h, w, extras, out_dtype, name):
    n, D = h.shape
    W = w.shape[1]
    tm, tn = 512, 1024
    in_specs = [pl.BlockSpec((tm, D), lambda j, i: (i, 0)),
                pl.BlockSpec((D, tn), lambda j, i: (0, j))]
    args = [h, w]
    for arr in extras:
        if arr.shape[0] == 1:
            in_specs.append(pl.BlockSpec((1, tn), lambda j, i: (0, j)))
        else:
            in_specs.append(pl.BlockSpec((tm, LANES), lambda j, i: (i, 0)))
        args.append(arr)
    return pl.pallas_call(
        functools.partial(_proj_body, kind),
        grid=(W // tn, n // tm),
        in_specs=in_specs,
        out_specs=pl.BlockSpec((tm, tn), lambda j, i: (i, j)),
        out_shape=jax.ShapeDtypeStruct((n, W), out_dtype),
        compiler_params=_params("parallel", "parallel"),
        name=name,
    )(*args)


def _rglru_body(xr_ref, gg_ref, cw_ref, cb_ref, wa_ref, ba_ref, wx_ref, bx_ref, lam_ref,
                o_ref, xbuf, hcar, a_s, u_s):
    s = pl.program_id(1)
    ts, D = xr_ref.shape

    @pl.when(s == 0)
    def _():
        xbuf[0:8, :] = jnp.zeros((8, D), F32)
        hcar[...] = jnp.zeros_like(hcar)

    @pl.when(s > 0)
    def _():
        xbuf[0:8, :] = xbuf[ts:ts + 8, :]

    xbuf[8:8 + ts, :] = xr_ref[...]
    xc = cb_ref[...] + cw_ref[0:1, :] * xbuf[8:8 + ts, :]
    for i in range(1, CONV_W):
        xc = xc + cw_ref[i:i + 1, :] * xbuf[8 - i:8 - i + ts, :]

    ra, rx = [], []
    for n in range(RNN_BLOCKS):
        xb = xc[:, n * RNN_BW:(n + 1) * RNN_BW].astype(BF16)
        ra.append(jnp.dot(xb, wa_ref[n], preferred_element_type=F32))
        rx.append(jnp.dot(xb, wx_ref[n], preferred_element_type=F32))
    r = jax.nn.sigmoid(jnp.concatenate(ra, axis=1) + ba_ref[...])
    ig = jax.nn.sigmoid(jnp.concatenate(rx, axis=1) + bx_ref[...])

    z = -lam_ref[...]
    softplus = jnp.maximum(z, 0.0) + jnp.log1p(jnp.exp(-jnp.abs(z)))
    log_a = -LRU_C * r * softplus
    a = jnp.exp(log_a)
    mult = jnp.sqrt(1.0 - a * a)
    row = lax.broadcasted_iota(jnp.int32, (ts, D), 0)
    mult = jnp.where((row == 0) & (s == 0), 1.0, mult)
    u = mult * (ig * xc)

    rm = row & 7
    for d in (1, 2, 4):
        keep = rm >= d
        a_sh = pltpu.roll(a, d, axis=0)
        u_sh = pltpu.roll(u, d, axis=0)
        u = jnp.where(keep, a * u_sh + u, u)
        a = jnp.where(keep, a * a_sh, a)
    a_s[...] = a
    u_s[...] = u

    def group(g, h):
        r0 = pl.multiple_of(g * 8, 8)
        hg = u_s[pl.ds(r0, 8), :] + a_s[pl.ds(r0, 8), :] * h
        u_s[pl.ds(r0, 8), :] = hg
        return hg[7:8, :]

    hcar[...] = lax.fori_loop(0, ts // 8, group, hcar[...])
    o_ref[...] = (u_s[...] * gg_ref[...]).astype(o_ref.dtype)


def _rglru(xr, gg, conv_w, conv_b, w_a, b_a, w_x, b_x, lam, B, S):
    n, D = xr.shape
    ts = 256
    spb = S // ts
    row = lambda b, s: (b * spb + s, 0)
    vec = lambda b, s: (0, 0)
    return pl.pallas_call(
        _rglru_body,
        grid=(B, spb),
        in_specs=[pl.BlockSpec((ts, D), row),
                  pl.BlockSpec((ts, D), row),
                  pl.BlockSpec((CONV_W, D), vec),
                  pl.BlockSpec((1, D), vec),
                  pl.BlockSpec((RNN_BLOCKS, RNN_BW, RNN_BW), lambda b, s: (0, 0, 0)),
                  pl.BlockSpec((1, D), vec),
                  pl.BlockSpec((RNN_BLOCKS, RNN_BW, RNN_BW), lambda b, s: (0, 0, 0)),
                  pl.BlockSpec((1, D), vec),
                  pl.BlockSpec((1, D), vec)],
        out_specs=pl.BlockSpec((ts, D), row),
        out_shape=jax.ShapeDtypeStruct((n, D), BF16),
        scratch_shapes=[pltpu.VMEM((ts + 8, D), F32), pltpu.VMEM((1, D), F32),
                        pltpu.VMEM((ts, D), F32), pltpu.VMEM((ts, D), F32)],
        compiler_params=_params("arbitrary", "arbitrary"),
        name="rglru",
    )(xr, gg, conv_w, conv_b.reshape(1, D), w_a.astype(BF16), b_a.reshape(1, D),
      w_x.astype(BF16), b_x.reshape(1, D), lam.reshape(1, D))


def _moba_body(q_ref, k_ref, v_ref, o_ref, kaug, vaug, kmean, m_s, acc_s):
    qb = pl.program_id(2)
    S = k_ref.shape[0]
    nb = S // MOBA_BLOCK
    bs = MOBA_BLOCK
    dh = HEAD_DIM
    heads = range(MOBA_HEADS)
    hcol = lambda hh: slice(hh * dh, (hh + 1) * dh)

    @pl.when(qb == 0)
    def _():
        blk = lax.broadcasted_iota(jnp.int32, (S, dh), 0) // bs
        col = lax.broadcasted_iota(jnp.int32, (S, dh), 1)
        onehot = jnp.where(col == blk, 1.0, 0.0).astype(BF16)
        for hh in heads:
            kaug[hh, :, 0:dh] = k_ref[:, hcol(hh)].astype(BF16)
            kaug[hh, :, dh:2 * dh] = onehot
            vaug[hh, :, 0:dh] = v_ref[:, hcol(hh)]
            vaug[hh, :, dh:2 * dh] = jnp.ones((S, dh), BF16)
            kmean[hh] = jnp.zeros((LANES, dh), F32)
            for j in range(nb):
                kmean[hh, j:j + 1, :] = jnp.mean(k_ref[j * bs:(j + 1) * bs, hcol(hh)], axis=0, keepdims=True)

    scale = HEAD_DIM ** -0.5
    nt = (((1,), (1,)), ((), ()))
    c2 = scale * LOG2_E
    r0 = pl.multiple_of(qb * bs, bs)
    lane = lax.broadcasted_iota(jnp.int32, (bs, LANES), 1)
    lane_f = lane.astype(F32)
    past = lane < qb
    rr = lax.broadcasted_iota(jnp.int32, (bs, bs), 0)
    cc = lax.broadcasted_iota(jnp.int32, (bs, bs), 1)

    qaugs = []
    for hh in heads:
        q = q_ref[:, hcol(hh)]
        qbf = q.astype(BF16)

        gate = lax.dot_general(q, kmean[hh], nt, preferred_element_type=F32, precision=HIGHEST)
        g = jnp.where(past, gate, -jnp.inf)
        sel = jnp.zeros(gate.shape, F32)
        for _ in range(MOBA_TOPK):
            m = jnp.max(g, axis=1, keepdims=True)
            idx = jnp.min(jnp.where(g == m, lane_f, float(LANES)), axis=1, keepdims=True)
            pick = (lane_f == idx) & past
            sel = jnp.where(pick, 1.0, sel)
            g = jnp.where(pick, -jnp.inf, g)
        bias = jnp.where(sel > 0.5, 0.0, NEG_BIG).astype(BF16)
        qaugs.append(jnp.concatenate([qbf, bias], axis=1))

        s = lax.dot_general(qbf, kaug[hh, pl.ds(r0, bs), 0:dh], nt, preferred_element_type=F32)
        s = jnp.where(cc <= rr, s, NEG_BIG)
        m0 = jnp.max(s, axis=1, keepdims=True)
        p = jnp.exp2((s - m0) * c2)
        m_s[hh] = m0
        acc_s[hh] = jnp.dot(p.astype(BF16), vaug[hh, pl.ds(r0, bs), :], preferred_element_type=F32)

    cw = MOBA_CHUNK * bs
    for c in range(nb // MOBA_CHUNK):
        @pl.when(c * MOBA_CHUNK < qb)
        def _(c=c):
            for hh in heads:
                sc = lax.dot_general(qaugs[hh], kaug[hh, c * cw:(c + 1) * cw, :], nt,
                                     preferred_element_type=F32)
                m_old = m_s[hh]
                m_new = jnp.maximum(m_old, jnp.max(sc, axis=1, keepdims=True))
                alpha = jnp.exp2((m_old - m_new) * c2)
                pc = jnp.exp2((sc - m_new) * c2)
                m_s[hh] = m_new
                acc_s[hh] = alpha * acc_s[hh] + jnp.dot(pc.astype(BF16), vaug[hh, c * cw:(c + 1) * cw, :],
                                                        preferred_element_type=F32)

    for hh in heads:
        o_ref[:, hcol(hh)] = (acc_s[hh, :, 0:dh] / acc_s[hh, :, dh:dh + 1]).astype(o_ref.dtype)


def _moba(qk, v, B, S):
    n, D = v.shape
    nq = S // MOBA_BLOCK
    hs = MOBA_HEADS
    w = hs * HEAD_DIM
    return pl.pallas_call(
        _moba_body,
        grid=(B, N_HEADS // hs, nq),
        in_specs=[pl.BlockSpec((MOBA_BLOCK, w), lambda b, h, i: (b * nq + i, h)),
                  pl.BlockSpec((S, w), lambda b, h, i: (b, N_HEADS // hs + h)),
                  pl.BlockSpec((S, w), lambda b, h, i: (b, h))],
        out_specs=pl.BlockSpec((MOBA_BLOCK, w), lambda b, h, i: (b * nq + i, h)),
        out_shape=jax.ShapeDtypeStruct((n, D), BF16),
        scratch_shapes=[pltpu.VMEM((hs, S, 2 * HEAD_DIM), BF16), pltpu.VMEM((hs, S, 2 * HEAD_DIM), BF16),
                        pltpu.VMEM((hs, LANES, HEAD_DIM), F32), pltpu.VMEM((hs, MOBA_BLOCK, 1), F32),
                        pltpu.VMEM((hs, MOBA_BLOCK, 2 * HEAD_DIM), F32)],
        compiler_params=_params("parallel", "parallel", "arbitrary"),
        name="moba",
    )(qk, qk, v)


def _merge_body(yr_ref, ya_ref, gr_ref, ga_ref, x_ref, mod_ref, wb_ref, wo_ref, g2_ref, wr_ref, br_ref,
                x1_ref, h2_ref, topi_ref, gate_ref, rank_ref, cnt_ref, carry):
    i = pl.program_id(0)

    @pl.when(i == 0)
    def _():
        carry[...] = jnp.zeros_like(carry)

    tm = MERGE_SUB
    counts = {"total": carry[...]}
    first, second = [
        _merge_stages(slice(h * tm, (h + 1) * tm), counts, yr_ref, ya_ref, gr_ref, ga_ref, x_ref, mod_ref,
                      wb_ref, wo_ref, g2_ref, wr_ref, br_ref, x1_ref, h2_ref, topi_ref, gate_ref, rank_ref)
        for h in range(2)]
    for _ in range(MERGE_MATMUL_STAGES):
        next(first)
    for stage in first:
        next(second)
    for stage in second:
        pass
    carry[...] = counts["total"]
    cnt_ref[...] = jnp.broadcast_to(counts["total"], cnt_ref.shape).astype(jnp.int32)


MERGE_MATMUL_STAGES = 4


def _merge_stages(rows, counts, yr_ref, ya_ref, gr_ref, ga_ref, x_ref, mod_ref, wb_ref, wo_ref, g2_ref, wr_ref,
                  br_ref, x1_ref, h2_ref, topi_ref, gate_ref, rank_ref):
    tm = rows.stop - rows.start
    zr = jnp.dot(yr_ref[rows, :], wb_ref[0], preferred_element_type=F32)
    yield
    za = jnp.dot(ya_ref[rows, :], wb_ref[1], preferred_element_type=F32)
    mix = (gr_ref[rows, :] * zr + ga_ref[rows, :] * za).astype(BF16)
    yield
    mixed = jnp.dot(mix, wo_ref[...], preferred_element_type=F32)
    x1 = x_ref[rows, :] + mod_ref[0, 2:3, :] * mixed
    x1_ref[rows, :] = x1
    h2 = _norm_mod(x1, g2_ref[...], mod_ref[0, 3:4, :], mod_ref[0, 4:5, :])
    h2_ref[rows, :] = h2
    yield

    hi = h2.astype(BF16)
    lo = (h2 - hi.astype(F32)).astype(BF16)
    both = jnp.dot(hi, wr_ref[...], preferred_element_type=F32)
    logits = (both[:, :LANES] + both[:, LANES:]
              + jnp.dot(lo, wr_ref[:, :LANES], preferred_element_type=F32) + br_ref[...])
    yield

    lane = lax.broadcasted_iota(jnp.int32, logits.shape, 1)
    lane_f = lane.astype(F32)
    lg = jnp.where(lane < N_EXPERTS, logits, -jnp.inf)
    vals, idxs = [], []
    onehot = jnp.zeros(logits.shape, F32)
    for _ in range(TOP_K):
        m = jnp.max(lg, axis=1, keepdims=True)
        idx = jnp.min(jnp.where(lg == m, lane_f, float(LANES)), axis=1, keepdims=True)
        pick = lane_f == idx
        onehot = jnp.where(pick, 1.0, onehot)
        lg = jnp.where(pick, -jnp.inf, lg)
        vals.append(m)
        idxs.append(idx)
        yield
    es = [jnp.exp(v - vals[0]) for v in vals]
    den = es[0] + es[1] + es[2] + es[3]
    before0 = counts["total"]

    rr = lax.broadcasted_iota(jnp.int32, (tm, tm), 0)
    cc = lax.broadcasted_iota(jnp.int32, (tm, tm), 1)
    tri = jnp.where(cc < rr, 1.0, 0.0).astype(BF16)
    before = jnp.dot(tri, onehot.astype(BF16), preferred_element_type=F32) + before0

    topi = jnp.zeros(logits.shape, F32)
    gates = jnp.zeros(logits.shape, F32)
    rank = jnp.zeros(logits.shape, F32)
    for k in range(TOP_K):
        rk = jnp.sum(jnp.where(lane_f == idxs[k], before, 0.0), axis=1, keepdims=True)
        topi = jnp.where(lane == k, idxs[k], topi)
        gates = jnp.where(lane == k, es[k] / den, gates)
        rank = jnp.where(lane == k, rk, rank)
    topi_ref[rows, :] = topi.astype(jnp.int32)
    gate_ref[rows, :] = gates
    rank_ref[rows, :] = rank.astype(jnp.int32)
    counts["total"] = before0 + jnp.sum(onehot, axis=0, keepdims=True)
    yield


def _merge(y_rnn, y_att, gl, x2, mod3, w_branch, w_out, norm2_g, w_router, b_router, rows_per_batch):
    n, D = x2.shape
    tm = 2 * MERGE_SUB
    tpb = rows_per_batch // tm
    wr32 = jnp.zeros((D, LANES), F32).at[:, :N_EXPERTS].set(w_router)
    wr_hi = wr32.astype(BF16)
    wr = jnp.concatenate([wr_hi, (wr32 - wr_hi.astype(F32)).astype(BF16)], axis=1)
    br = jnp.zeros((1, LANES), F32).at[0, :N_EXPERTS].set(b_router)
    row = lambda i: (i, 0)
    fixed = lambda i: (0, 0)
    once = pl.Buffered(1)
    wide = jax.ShapeDtypeStruct((n, D), F32)
    narrow_i = jax.ShapeDtypeStruct((n, LANES), jnp.int32)
    return pl.pallas_call(
        _merge_body,
        grid=(n // tm,),
        in_specs=[pl.BlockSpec((tm, D), row),
                  pl.BlockSpec((tm, D), row),
                  pl.BlockSpec((tm, D), lambda i: (i, 0)),
                  pl.BlockSpec((tm, D), lambda i: (i, 1)),
                  pl.BlockSpec((tm, D), row),
                  pl.BlockSpec((1, 6, D), lambda i: (i // tpb, 0, 0)),
                  pl.BlockSpec((2, D, D), lambda i: (0, 0, 0), pipeline_mode=once),
                  pl.BlockSpec((D, D), fixed, pipeline_mode=once),
                  pl.BlockSpec((1, D), fixed),
                  pl.BlockSpec((D, 2 * LANES), fixed, pipeline_mode=once),
                  pl.BlockSpec((1, LANES), fixed)],
        out_specs=[pl.BlockSpec((tm, D), row),
                   pl.BlockSpec((tm, D), row),
                   pl.BlockSpec((tm, LANES), row),
                   pl.BlockSpec((tm, LANES), row),
                   pl.BlockSpec((tm, LANES), row),
                   pl.BlockSpec((8, LANES), fixed)],
        out_shape=[wide, wide, narrow_i, jax.ShapeDtypeStruct((n, LANES), F32), narrow_i,
                   jax.ShapeDtypeStruct((8, LANES), jnp.int32)],
        scratch_shapes=[pltpu.VMEM((1, LANES), F32)],
        compiler_params=_params("arbitrary"),
        name="merge",
    )(y_rnn, y_att, gl, gl, x2, mod3, w_branch.astype(BF16), w_out.astype(BF16),
      norm2_g.reshape(1, D), wr, br)


DISPATCH_TOKENS = 256
ROW_DMA_UNROLL = 8


def _dispatch_body(dest_ref, zblk_ref, src_ref, dst_hbm, zbuf, sem, zsem):
    t = DISPATCH_TOKENS
    c = pl.program_id(0)

    @pl.when(c == 0)
    def _():
        zbuf[...] = jnp.zeros_like(zbuf)

        def zero_copy(b):
            r0 = pl.multiple_of(b * EXPERT_ROWS, EXPERT_ROWS)
            return pltpu.make_async_copy(zbuf, dst_hbm.at[pl.ds(r0, EXPERT_ROWS), :], zsem)

        def zissue(j, carry):
            @pl.when(zblk_ref[j] >= 0)
            def _():
                zero_copy(zblk_ref[j]).start()
            return carry

        def zwait(j, carry):
            @pl.when(zblk_ref[j] >= 0)
            def _():
                zero_copy(zblk_ref[j]).wait()
            return carry

        lax.fori_loop(0, zblk_ref.shape[0], zissue, 0)
        lax.fori_loop(0, zblk_ref.shape[0], zwait, 0)

    def issue(g, carry):
        for u in range(ROW_DMA_UNROLL):
            r = g * ROW_DMA_UNROLL + u
            for k in range(TOP_K):
                d = dest_ref[(c * t + r) * TOP_K + k]
                pltpu.make_async_copy(src_ref.at[pl.ds(r, 1), :], dst_hbm.at[pl.ds(d, 1), :], sem).start()
        return carry

    lax.fori_loop(0, t // ROW_DMA_UNROLL, issue, 0)
    for k in range(TOP_K):
        pltpu.make_async_copy(src_ref, dst_hbm.at[pl.ds(0, t), :], sem).wait()


def _dispatch(h2, dest, zero_blocks, n_rows):
    n, D = h2.shape
    return pl.pallas_call(
        _dispatch_body,
        grid_spec=pltpu.PrefetchScalarGridSpec(
            num_scalar_prefetch=2,
            grid=(n // DISPATCH_TOKENS,),
            in_specs=[pl.BlockSpec((DISPATCH_TOKENS, D), lambda i, d, z: (i, 0))],
            out_specs=pl.BlockSpec(memory_space=pl.ANY),
            scratch_shapes=[pltpu.VMEM((EXPERT_ROWS, D), F32), pltpu.SemaphoreType.DMA,
                            pltpu.SemaphoreType.DMA],
        ),
        out_shape=jax.ShapeDtypeStruct((n_rows, D), F32),
        compiler_params=_params("arbitrary"),
        name="dispatch",
    )(dest, zero_blocks, h2)


def _experts_body(bexp_ref, nused_ref, x_ref, wu_ref, bu_ref, wd_ref, bd_ref, o_ref, wu_bf, wd_bf):
    i = pl.program_id(0)
    dff = wd_ref.shape[1]
    used = i < nused_ref[0]

    new_expert = (i == 0) | (bexp_ref[i] != bexp_ref[jnp.maximum(i - 1, 0)])

    @pl.when(used & new_expert)
    def _():
        wu_bf[...] = wu_ref[0].astype(BF16)
        wd_bf[...] = wd_ref[0].astype(BF16)

    @pl.when(used)
    def _():
        hc = jnp.dot(x_ref[...].astype(BF16), wu_bf[...], preferred_element_type=F32) + bu_ref[0]
        g = jnp.minimum(hc[:, :dff], SWIGLU_LIMIT)
        lin = jnp.clip(hc[:, dff:], -SWIGLU_LIMIT, SWIGLU_LIMIT)
        act = (lin + 1.0) * g * jax.nn.sigmoid(SWIGLU_ALPHA * g)
        o_ref[...] = jnp.dot(act.astype(BF16), wd_bf[...], preferred_element_type=F32) + bd_ref[0]

    @pl.when(i >= nused_ref[0])
    def _():
        o_ref[...] = jnp.zeros_like(o_ref)


def _experts(x_rows, block_exp, n_used, w_up, b_up, w_down, b_down, n_blocks):
    R, D = x_rows.shape
    E, _, F2 = w_up.shape
    dff = w_down.shape[1]
    blk = lambda i, be, nu: (i, 0)
    wsel = lambda i, be, nu: (be[i], 0, 0)
    return pl.pallas_call(
        _experts_body,
        grid_spec=pltpu.PrefetchScalarGridSpec(
            num_scalar_prefetch=2,
            grid=(n_blocks,),
            in_specs=[pl.BlockSpec((EXPERT_ROWS, D), blk),
                      pl.BlockSpec((1, D, F2), wsel),
                      pl.BlockSpec((1, 1, F2), wsel),
                      pl.BlockSpec((1, dff, D), wsel),
                      pl.BlockSpec((1, 1, D), wsel)],
            out_specs=pl.BlockSpec((EXPERT_ROWS, D), blk),
            scratch_shapes=[pltpu.VMEM((D, F2), BF16), pltpu.VMEM((dff, D), BF16)],
        ),
        out_shape=jax.ShapeDtypeStruct((R, D), F32),
        compiler_params=_params("arbitrary"),
        name="experts",
    )(block_exp, n_used, x_rows, w_up, b_up.reshape(E, 1, F2), w_down, b_down.reshape(E, 1, D))


COMBINE_ROWS = 128


def _combine_body(dest_ref, y_hbm, x1_ref, gate_ref, mod_ref, o_ref, buf, sem):
    i = pl.program_id(0)
    t = COMBINE_ROWS

    def issue(g, carry):
        for u in range(ROW_DMA_UNROLL):
            r = g * ROW_DMA_UNROLL + u
            for k in range(TOP_K):
                d = dest_ref[(i * t + r) * TOP_K + k]
                pltpu.make_async_copy(y_hbm.at[pl.ds(d, 1), :], buf.at[pl.ds(r, 1), pl.ds(k * D, D)], sem).start()
        return carry

    D = o_ref.shape[1]
    lax.fori_loop(0, t // ROW_DMA_UNROLL, issue, 0)
    for k in range(TOP_K):
        pltpu.make_async_copy(y_hbm.at[pl.ds(0, t), :], buf.at[:, pl.ds(k * D, D)], sem).wait()
    gates = gate_ref[...]
    y = gates[:, 0:1] * buf[:, 0:D]
    for k in range(1, TOP_K):
        y = y + gates[:, k:k + 1] * buf[:, k * D:(k + 1) * D]
    o_ref[...] = x1_ref[...] + mod_ref[0, 5:6, :] * y


def _combine(y_rows, dest, x1, gates, mod3, rows_per_batch):
    n, D = x1.shape
    t = COMBINE_ROWS
    tpb = rows_per_batch // t
    return pl.pallas_call(
        _combine_body,
        grid_spec=pltpu.PrefetchScalarGridSpec(
            num_scalar_prefetch=1,
            grid=(n // t,),
            in_specs=[pl.BlockSpec(memory_space=pl.ANY),
                      pl.BlockSpec((t, D), lambda i, d: (i, 0)),
                      pl.BlockSpec((t, LANES), lambda i, d: (i, 0)),
                      pl.BlockSpec((1, 6, D), lambda i, d: (i // tpb, 0, 0))],
            out_specs=pl.BlockSpec((t, D), lambda i, d: (i, 0)),
            scratch_shapes=[pltpu.VMEM((t, TOP_K * D), F32), pltpu.SemaphoreType.DMA],
        ),
        out_shape=jax.ShapeDtypeStruct((n, D), F32),
        compiler_params=_params("arbitrary"),
        name="combine",
    )(dest, y_rows, x1, gates, mod3)


def kernel(x, c, positions, ada_w, ada_b, norm1_g, norm2_g, w_in, conv_w, conv_b, w_rg_a, b_rg_a, w_rg_x, b_rg_x, lru_lambda, q_norm_g, k_norm_g, b_gate, w_branch, w_out, w_router, b_router, w_up, b_up, w_down, b_down):
    B, S, D = x.shape
    n = B * S
    depth = ada_w.shape[0]
    x2 = x.reshape(n, D)
    cos_t, sin_t = _rope_tables(positions)
    perm = _rope_head_perm()
    head_cols = (jnp.arange(2 * N_HEADS)[:, None] * HEAD_DIM + perm[None, :]).reshape(-1)
    for l in range(depth):
        mod3 = _adaln(c, ada_w[l], ada_b[l]).reshape(B, 6, D)
        h1 = _norm_mod_call(x2, mod3, norm1_g[l], S)
        w = w_in[l].astype(BF16)
        xr = _proj("plain", h1, w[:, 0:D], [], F32, "proj_xr")
        gg = _proj("gelu", h1, w[:, D:2 * D], [], F32, "proj_gr")
        qk_g = jnp.concatenate([jnp.tile(q_norm_g[l][perm], N_HEADS),
                                jnp.tile(k_norm_g[l][perm], N_HEADS)]).reshape(1, 2 * D)
        qk = _proj("qk", h1, w[:, 2 * D:4 * D][:, head_cols], [qk_g, cos_t, sin_t], F32, "proj_qk")
        v = _proj("plain", h1, w[:, 4 * D:5 * D], [], BF16, "proj_v")
        gl = _proj("gate", h1, w[:, 5 * D:7 * D], [b_gate[l].reshape(1, 2 * D)], F32, "proj_gl")

        y_rnn = _rglru(xr, gg, conv_w[l], conv_b[l], w_rg_a[l], b_rg_a[l], w_rg_x[l], b_rg_x[l],
                       lru_lambda[l], B, S)
        y_att = _moba(qk, v, B, S)

        x1, h2, topi, gates, rank, cnt = _merge(y_rnn, y_att, gl, x2, mod3, w_branch[l], w_out[l],
                                                norm2_g[l], w_router[l], b_router[l], S)

        counts = cnt[0, :N_EXPERTS]
        padded = (counts + EXPERT_ROWS - 1) // EXPERT_ROWS * EXPERT_ROWS
        pad_end = jnp.cumsum(padded)
        pad_start = pad_end - padded
        top_idx = topi[:, :TOP_K]
        dest = (pad_start[top_idx] + rank[:, :TOP_K]).reshape(n * TOP_K).astype(jnp.int32)
        n_blocks = (n * TOP_K) // EXPERT_ROWS + N_EXPERTS
        block_start = jnp.arange(n_blocks, dtype=jnp.int32) * EXPERT_ROWS
        block_exp = jnp.minimum(jnp.sum(block_start[:, None] >= pad_end[None, :], axis=1),
                                N_EXPERTS - 1).astype(jnp.int32)
        n_used = (pad_end[-1] // EXPERT_ROWS).astype(jnp.int32)
        last_blk = jnp.where(padded > 0, pad_end // EXPERT_ROWS - 1, -1)
        spare = n_used + jnp.arange(N_EXPERTS, dtype=jnp.int32)
        spare = jnp.where(spare < n_blocks, spare, -1)
        zero_blocks = jnp.concatenate([last_blk, spare]).astype(jnp.int32)

        x_rows = _dispatch(h2, dest, zero_blocks, n_blocks * EXPERT_ROWS)
        y_rows = _experts(x_rows, block_exp, n_used.reshape(1), w_up[l], b_up[l], w_down[l], b_down[l], n_blocks)
        x2 = _combine(y_rows, dest, x1, gates, mod3, S)
    return x2.reshape(B, S, D)
```

```python
import functools

import jax
import jax.numpy as jnp
from jax import lax
from jax.experimental import pallas as pl
from jax.experimental.pallas import tpu as pltpu

F32 = jnp.float32
BF16 = jnp.bfloat16
HIGHEST = lax.Precision.HIGHEST

EPS = 1e-6
LANES = 128
N_HEADS = 8
HEAD_DIM = 128
RNN_BLOCKS = 8
RNN_BW = 128
CONV_W = 4
LRU_C = 8.0
MOBA_BLOCK = 256
MOBA_TOPK = 3
MOBA_CHUNK = 4
MOBA_HEADS = 4
MERGE_SUB = 256
LOG2_E = 1.4426950408889634
ROPE_DIMS = HEAD_DIM // 4
ROPE_THETA = 500000.0
N_EXPERTS = 32
TOP_K = 4
SWIGLU_LIMIT = 7.0
SWIGLU_ALPHA = 1.702
EXPERT_ROWS = 256
NEG_BIG = -1e30

VMEM_LIMIT = 56 * 1024 * 1024


def _params(*sem):
    return pltpu.CompilerParams(dimension_semantics=sem, vmem_limit_bytes=VMEM_LIMIT)


def _adaln_body(c_ref, w_ref, b_ref, o_ref):
    cs = c_ref[...]
    cs = cs * jax.nn.sigmoid(cs)
    o_ref[...] = jnp.dot(cs, w_ref[...], preferred_element_type=F32, precision=HIGHEST) + b_ref[...]


def _adaln(c, ada_w, ada_b):
    B, D = c.shape
    W = ada_w.shape[1]
    cpad = jnp.zeros((8, D), F32).at[:B].set(c)
    tn = 1024
    mod = pl.pallas_call(
        _adaln_body,
        grid=(W // tn,),
        in_specs=[pl.BlockSpec((8, D), lambda j: (0, 0)),
                  pl.BlockSpec((D, tn), lambda j: (0, j)),
                  pl.BlockSpec((1, tn), lambda j: (0, j))],
        out_specs=pl.BlockSpec((8, tn), lambda j: (0, j)),
        out_shape=jax.ShapeDtypeStruct((8, W), F32),
        compiler_params=_params("parallel"),
        name="adaln",
    )(cpad, ada_w, ada_b.reshape(1, W))
    return mod[:B]


def _rope_head_perm():
    half = ROPE_DIMS // 2
    mid = HEAD_DIM // 2
    return jnp.concatenate([jnp.arange(0, half), jnp.arange(ROPE_DIMS, mid + half),
                            jnp.arange(half, ROPE_DIMS), jnp.arange(mid + half, HEAD_DIM)])


def _rope_body(pos_ref, freq_ref, c_ref, s_ref):
    ang = pos_ref[...].astype(F32) * freq_ref[...]
    lane = lax.broadcasted_iota(jnp.int32, ang.shape, 1)
    half = ROPE_DIMS // 2
    mid = HEAD_DIM // 2
    s = jnp.sin(ang)
    c_ref[...] = jnp.cos(ang)
    s_ref[...] = jnp.where(lane < half, -s, jnp.where((lane >= mid) & (lane < mid + half), s, 0.0))


def _rope_tables(positions):
    n = positions.size
    half = ROPE_DIMS // 2
    mid = HEAD_DIM // 2
    freqs = ROPE_THETA ** (-jnp.arange(half, dtype=F32) / half)
    freq_lane = jnp.zeros((1, LANES), F32).at[0, :half].set(freqs).at[0, mid:mid + half].set(freqs)
    tm = 1024
    tab = jax.ShapeDtypeStruct((n, LANES), F32)
    return pl.pallas_call(
        _rope_body,
        grid=(n // tm,),
        in_specs=[pl.BlockSpec((tm, 1), lambda i: (i, 0)),
                  pl.BlockSpec((1, LANES), lambda i: (0, 0))],
        out_specs=[pl.BlockSpec((tm, LANES), lambda i: (i, 0))] * 2,
        out_shape=[tab, tab],
        compiler_params=_params("parallel"),
        name="rope_tab",
    )(positions.reshape(n, 1), freq_lane)


def _norm_mod(x, g, sh, sc):
    ms = jnp.mean(x * x, axis=-1, keepdims=True)
    y = x * lax.rsqrt(ms + EPS)
    return (y * g) * (1.0 + sc) + sh


def _norm_mod_body(x_ref, mod_ref, g_ref, o_ref):
    o_ref[...] = _norm_mod(x_ref[...], g_ref[...], mod_ref[0, 0:1, :], mod_ref[0, 1:2, :]).astype(o_ref.dtype)


def _norm_mod_call(x2, mod3, g, rows_per_batch):
    n, D = x2.shape
    tm = 512
    tpb = rows_per_batch // tm
    return pl.pallas_call(
        _norm_mod_body,
        grid=(n // tm,),
        in_specs=[pl.BlockSpec((tm, D), lambda i: (i, 0)),
                  pl.BlockSpec((1, 6, D), lambda i: (i // tpb, 0, 0)),
                  pl.BlockSpec((1, D), lambda i: (0, 0))],
        out_specs=pl.BlockSpec((tm, D), lambda i: (i, 0)),
        out_shape=jax.ShapeDtypeStruct((n, D), BF16),
        compiler_params=_params("parallel"),
        name="norm_mod",
    )(x2, mod3, g.reshape(1, D))


def _gelu_tanh(x):
    return 0.5 * x * (1.0 + jnp.tanh(0.7978845608028654 * (x + 0.044715 * (x * x * x))))


PROJ_SUB = 256
PROJ_PIECE = 256


def _qk_head(seg, g, cos, sin):
    ms = jnp.mean(seg * seg, axis=-1, keepdims=True)
    y = seg * lax.rsqrt(ms + EPS) * g
    return y * cos + pltpu.roll(y, HEAD_DIM // 2, axis=1) * sin


def _proj_stages(kind, rows, h_ref, w_ref, extras, o_ref):
    n_pieces = w_ref.shape[1] // PROJ_PIECE
    pieces = []
    for j in range(n_pieces):
        pieces.append(jnp.dot(h_ref[rows, :], w_ref[:, j * PROJ_PIECE:(j + 1) * PROJ_PIECE],
                              preferred_element_type=F32))
        yield
    for j in range(n_pieces):
        cols = slice(j * PROJ_PIECE, (j + 1) * PROJ_PIECE)
        acc = pieces[j]
        if kind == "gelu":
            acc = _gelu_tanh(acc)
        elif kind == "gate":
            (b_ref,) = extras
            acc = jax.nn.sigmoid(acc + b_ref[:, cols])
        elif kind == "qk":
            g_ref, c_ref, s_ref = extras
            cos, sin = c_ref[rows, :], s_ref[rows, :]
            acc = jnp.concatenate(
                [_qk_head(acc[:, o:o + HEAD_DIM], g_ref[:, cols.start + o:cols.start + o + HEAD_DIM], cos, sin)
                 for o in range(0, PROJ_PIECE, HEAD_DIM)], axis=1)
        o_ref[rows, cols] = acc.astype(o_ref.dtype)
        yield


def _proj_body(kind, h_ref, w_ref, *rest):
    *extras, o_ref = rest
    first, second = [_proj_stages(kind, slice(t * PROJ_SUB, (t + 1) * PROJ_SUB), h_ref, w_ref, extras, o_ref)
                     for t in range(2)]
    n_pieces = w_ref.shape[1] // PROJ_PIECE
    for _ in range(n_pieces):
        next(first)
    for _ in range(n_pieces):
        next(second)
        next(first)
    for stage in second:
        pass


def _proj(kind, h, w, extras, out_dtype, name):
    n, D = h.shape
    W = w.shape[1]
    tm, tn = 512, 1024
    in_specs = [pl.BlockSpec((tm, D), lambda j, i: (i, 0)),
                pl.BlockSpec((D, tn), lambda j, i: (0, j))]
    args = [h, w]
    for arr in extras:
        if arr.shape[0] == 1:
            in_specs.append(pl.BlockSpec((1, tn), lambda j, i: (0, j)))
        else:
            in_specs.append(pl.BlockSpec((tm, LANES), lambda j, i: (i, 0)))
        args.append(arr)
    return pl.pallas_call(
        functools.partial(_proj_body, kind),
        grid=(W // tn, n // tm),
        in_specs=in_specs,
        out_specs=pl.BlockSpec((tm, tn), lambda j, i: (i, j)),
        out_shape=jax.ShapeDtypeStruct((n, W), out_dtype),
        compiler_params=_params("parallel", "parallel"),
        name=name,
    )(*args)


def _rglru_body(xr_ref, gg_ref, cw_ref, cb_ref, wa_ref, ba_ref, wx_ref, bx_ref, lam_ref,
                o_ref, xbuf, hcar, a_s, u_s):
    s = pl.program_id(1)
    ts, D = xr_ref.shape

    @pl.when(s == 0)
    def _():
        xbuf[0:8, :] = jnp.zeros((8, D), F32)
        hcar[...] = jnp.zeros_like(hcar)

    @pl.when(s > 0)
    def _():
        xbuf[0:8, :] = xbuf[ts:ts + 8, :]

    xbuf[8:8 + ts, :] = xr_ref[...]
    xc = cb_ref[...] + cw_ref[0:1, :] * xbuf[8:8 + ts, :]
    for i in range(1, CONV_W):
        xc = xc + cw_ref[i:i + 1, :] * xbuf[8 - i:8 - i + ts, :]

    ra, rx = [], []
    for n in range(RNN_BLOCKS):
        xb = xc[:, n * RNN_BW:(n + 1) * RNN_BW].astype(BF16)
        ra.append(jnp.dot(xb, wa_ref[n], preferred_element_type=F32))
        rx.append(jnp.dot(xb, wx_ref[n], preferred_element_type=F32))
    r = jax.nn.sigmoid(jnp.concatenate(ra, axis=1) + ba_ref[...])
    ig = jax.nn.sigmoid(jnp.concatenate(rx, axis=1) + bx_ref[...])

    z = -lam_ref[...]
    softplus = jnp.maximum(z, 0.0) + jnp.log1p(jnp.exp(-jnp.abs(z)))
    log_a = -LRU_C * r * softplus
    a = jnp.exp(log_a)
    mult = jnp.sqrt(1.0 - a * a)
    row = lax.broadcasted_iota(jnp.int32, (ts, D), 0)
    mult = jnp.where((row == 0) & (s == 0), 1.0, mult)
    u = mult * (ig * xc)

    rm = row & 7
    for d in (1, 2, 4):
        keep = rm >= d
        a_sh = pltpu.roll(a, d, axis=0)
        u_sh = pltpu.roll(u, d, axis=0)
        u = jnp.where(keep, a * u_sh + u, u)
        a = jnp.where(keep, a * a_sh, a)
    a_s[...] = a
    u_s[...] = u

    def group(g, h):
        r0 = pl.multiple_of(g * 8, 8)
        hg = u_s[pl.ds(r0, 8), :] + a_s[pl.ds(r0, 8), :] * h
        u_s[pl.ds(r0, 8), :] = hg
        return hg[7:8, :]

    hcar[...] = lax.fori_loop(0, ts // 8, group, hcar[...])
    o_ref[...] = (u_s[...] * gg_ref[...]).astype(o_ref.dtype)


def _rglru(xr, gg, conv_w, conv_b, w_a, b_a, w_x, b_x, lam, B, S):
    n, D = xr.shape
    ts = 256
    spb = S // ts
    row = lambda b, s: (b * spb + s, 0)
    vec = lambda b, s: (0, 0)
    return pl.pallas_call(
        _rglru_body,
        grid=(B, spb),
        in_specs=[pl.BlockSpec((ts, D), row),
                  pl.BlockSpec((ts, D), row),
                  pl.BlockSpec((CONV_W, D), vec),
                  pl.BlockSpec((1, D), vec),
                  pl.BlockSpec((RNN_BLOCKS, RNN_BW, RNN_BW), lambda b, s: (0, 0, 0)),
                  pl.BlockSpec((1, D), vec),
                  pl.BlockSpec((RNN_BLOCKS, RNN_BW, RNN_BW), lambda b, s: (0, 0, 0)),
                  pl.BlockSpec((1, D), vec),
                  pl.BlockSpec((1, D), vec)],
        out_specs=pl.BlockSpec((ts, D), row),
        out_shape=jax.ShapeDtypeStruct((n, D), BF16),
        scratch_shapes=[pltpu.VMEM((ts + 8, D), F32), pltpu.VMEM((1, D), F32),
                        pltpu.VMEM((ts, D), F32), pltpu.VMEM((ts, D), F32)],
        compiler_params=_params("arbitrary", "arbitrary"),
        name="rglru",
    )(xr, gg, conv_w, conv_b.reshape(1, D), w_a.astype(BF16), b_a.reshape(1, D),
      w_x.astype(BF16), b_x.reshape(1, D), lam.reshape(1, D))


def _moba_body(q_ref, k_ref, v_ref, o_ref, kaug, vaug, kmean, m_s, acc_s):
    qb = pl.program_id(2)
    S = k_ref.shape[0]
    nb = S // MOBA_BLOCK
    bs = MOBA_BLOCK
    dh = HEAD_DIM
    heads = range(MOBA_HEADS)
    hcol = lambda hh: slice(hh * dh, (hh + 1) * dh)

    @pl.when(qb == 0)
    def _():
        blk = lax.broadcasted_iota(jnp.int32, (S, dh), 0) // bs
        col = lax.broadcasted_iota(jnp.int32, (S, dh), 1)
        onehot = jnp.where(col == blk, 1.0, 0.0).astype(BF16)
        for hh in heads:
            kaug[hh, :, 0:dh] = k_ref[:, hcol(hh)].astype(BF16)
            kaug[hh, :, dh:2 * dh] = onehot
            vaug[hh, :, 0:dh] = v_ref[:, hcol(hh)]
            vaug[hh, :, dh:2 * dh] = jnp.ones((S, dh), BF16)
            kmean[hh] = jnp.zeros((LANES, dh), F32)
            for j in range(nb):
                kmean[hh, j:j + 1, :] = jnp.mean(k_ref[j * bs:(j + 1) * bs, hcol(hh)], axis=0, keepdims=True)

    scale = HEAD_DIM ** -0.5
    nt = (((1,), (1,)), ((), ()))
    c2 = scale * LOG2_E
    r0 = pl.multiple_of(qb * bs, bs)
    lane = lax.broadcasted_iota(jnp.int32, (bs, LANES), 1)
    lane_f = lane.astype(F32)
    past = lane < qb
    rr = lax.broadcasted_iota(jnp.int32, (bs, bs), 0)
    cc = lax.broadcasted_iota(jnp.int32, (bs, bs), 1)

    qaugs = []
    for hh in heads:
        q = q_ref[:, hcol(hh)]
        qbf = q.astype(BF16)

        gate = lax.dot_general(q, kmean[hh], nt, preferred_element_type=F32, precision=HIGHEST)
        g = jnp.where(past, gate, -jnp.inf)
        sel = jnp.zeros(gate.shape, F32)
        for _ in range(MOBA_TOPK):
            m = jnp.max(g, axis=1, keepdims=True)
            idx = jnp.min(jnp.where(g == m, lane_f, float(LANES)), axis=1, keepdims=True)
            pick = (lane_f == idx) & past
            sel = jnp.where(pick, 1.0, sel)
            g = jnp.where(pick, -jnp.inf, g)
        bias = jnp.where(sel > 0.5, 0.0, NEG_BIG).astype(BF16)
        qaugs.append(jnp.concatenate([qbf, bias], axis=1))

        s = lax.dot_general(qbf, kaug[hh, pl.ds(r0, bs), 0:dh], nt, preferred_element_type=F32)
        s = jnp.where(cc <= rr, s, NEG_BIG)
        m0 = jnp.max(s, axis=1, keepdims=True)
        p = jnp.exp2((s - m0) * c2)
        m_s[hh] = m0
        acc_s[hh] = jnp.dot(p.astype(BF16), vaug[hh, pl.ds(r0, bs), :], preferred_element_type=F32)

    cw = MOBA_CHUNK * bs
    for c in range(nb // MOBA_CHUNK):
        @pl.when(c * MOBA_CHUNK < qb)
        def _(c=c):
            for hh in heads:
                sc = lax.dot_general(qaugs[hh], kaug[hh, c * cw:(c + 1) * cw, :], nt,
                                     preferred_element_type=F32)
                m_old = m_s[hh]
                m_new = jnp.maximum(m_old, jnp.max(sc, axis=1, keepdims=True))
                alpha = jnp.exp2((m_old - m_new) * c2)
                pc = jnp.exp2((sc - m_new) * c2)
                m_s[hh] = m_new
                acc_s[hh] = alpha * acc_s[hh] + jnp.dot(pc.astype(BF16), vaug[hh, c * cw:(c + 1) * cw, :],
                                                        preferred_element_type=F32)

    for hh in heads:
        o_ref[:, hcol(hh)] = (acc_s[hh, :, 0:dh] / acc_s[hh, :, dh:dh + 1]).astype(o_ref.dtype)


def _moba(qk, v, B, S):
    n, D = v.shape
    nq = S // MOBA_BLOCK
    hs = MOBA_HEADS
    w = hs * HEAD_DIM
    return pl.pallas_call(
        _moba_body,
        grid=(B, N_HEADS // hs, nq),
        in_specs=[pl.BlockSpec((MOBA_BLOCK, w), lambda b, h, i: (b * nq + i, h)),
                  pl.BlockSpec((S, w), lambda b, h, i: (b, N_HEADS // hs + h)),
                  pl.BlockSpec((S, w), lambda b, h, i: (b, h))],
        out_specs=pl.BlockSpec((MOBA_BLOCK, w), lambda b, h, i: (b * nq + i, h)),
        out_shape=jax.ShapeDtypeStruct((n, D), BF16),
        scratch_shapes=[pltpu.VMEM((hs, S, 2 * HEAD_DIM), BF16), pltpu.VMEM((hs, S, 2 * HEAD_DIM), BF16),
                        pltpu.VMEM((hs, LANES, HEAD_DIM), F32), pltpu.VMEM((hs, MOBA_BLOCK, 1), F32),
                        pltpu.VMEM((hs, MOBA_BLOCK, 2 * HEAD_DIM), F32)],
        compiler_params=_params("parallel", "parallel", "arbitrary"),
        name="moba",
    )(qk, qk, v)


def _merge_body(yr_ref, ya_ref, gr_ref, ga_ref, x_ref, mod_ref, wb_ref, wo_ref, g2_ref, wr_ref, br_ref,
                x1_ref, h2_ref, topi_ref, gate_ref, rank_ref, cnt_ref, carry):
    i = pl.program_id(0)

    @pl.when(i == 0)
    def _():
        carry[...] = jnp.zeros_like(carry)

    tm = MERGE_SUB
    counts = {"total": carry[...]}
    first, second = [
        _merge_stages(slice(h * tm, (h + 1) * tm), counts, yr_ref, ya_ref, gr_ref, ga_ref, x_ref, mod_ref,
                      wb_ref, wo_ref, g2_ref, wr_ref, br_ref, x1_ref, h2_ref, topi_ref, gate_ref, rank_ref)
        for h in range(2)]
    for _ in range(MERGE_MATMUL_STAGES):
        next(first)
    for stage in first:
        next(second)
    for stage in second:
        pass
    carry[...] = counts["total"]
    cnt_ref[...] = jnp.broadcast_to(counts["total"], cnt_ref.shape).astype(jnp.int32)


MERGE_MATMUL_STAGES = 4


def _merge_stages(rows, counts, yr_ref, ya_ref, gr_ref, ga_ref, x_ref, mod_ref, wb_ref, wo_ref, g2_ref, wr_ref,
                  br_ref, x1_ref, h2_ref, topi_ref, gate_ref, rank_ref):
    tm = rows.stop - rows.start
    zr = jnp.dot(yr_ref[rows, :], wb_ref[0], preferred_element_type=F32)
    yield
    za = jnp.dot(ya_ref[rows, :], wb_ref[1], preferred_element_type=F32)
    mix = (gr_ref[rows, :] * zr + ga_ref[rows, :] * za).astype(BF16)
    yield
    mixed = jnp.dot(mix, wo_ref[...], preferred_element_type=F32)
    x1 = x_ref[rows, :] + mod_ref[0, 2:3, :] * mixed
    x1_ref[rows, :] = x1
    h2 = _norm_mod(x1, g2_ref[...], mod_ref[0, 3:4, :], mod_ref[0, 4:5, :])
    h2_ref[rows, :] = h2
    yield

    hi = h2.astype(BF16)
    lo = (h2 - hi.astype(F32)).astype(BF16)
    both = jnp.dot(hi, wr_ref[...], preferred_element_type=F32)
    logits = (both[:, :LANES] + both[:, LANES:]
              + jnp.dot(lo, wr_ref[:, :LANES], preferred_element_type=F32) + br_ref[...])
    yield

    lane = lax.broadcasted_iota(jnp.int32, logits.shape, 1)
    lane_f = lane.astype(F32)
    lg = jnp.where(lane < N_EXPERTS, logits, -jnp.inf)
    vals, idxs = [], []
    onehot = jnp.zeros(logits.shape, F32)
    for _ in range(TOP_K):
        m = jnp.max(lg, axis=1, keepdims=True)
        idx = jnp.min(jnp.where(lg == m, lane_f, float(LANES)), axis=1, keepdims=True)
        pick = lane_f == idx
        onehot = jnp.where(pick, 1.0, onehot)
        lg = jnp.where(pick, -jnp.inf, lg)
        vals.append(m)
        idxs.append(idx)
        yield
    es = [jnp.exp(v - vals[0]) for v in vals]
    den = es[0] + es[1] + es[2] + es[3]
    before0 = counts["total"]

    rr = lax.broadcasted_iota(jnp.int32, (tm, tm), 0)
    cc = lax.broadcasted_iota(jnp.int32, (tm, tm), 1)
    tri = jnp.where(cc < rr, 1.0, 0.0).astype(BF16)
    before = jnp.dot(tri, onehot.astype(BF16), preferred_element_type=F32) + before0

    topi = jnp.zeros(logits.shape, F32)
    gates = jnp.zeros(logits.shape, F32)
    rank = jnp.zeros(logits.shape, F32)
    for k in range(TOP_K):
        rk = jnp.sum(jnp.where(lane_f == idxs[k], before, 0.0), axis=1, keepdims=True)
        topi = jnp.where(lane == k, idxs[k], topi)
        gates = jnp.where(lane == k, es[k] / den, gates)
        rank = jnp.where(lane == k, rk, rank)
    topi_ref[rows, :] = topi.astype(jnp.int32)
    gate_ref[rows, :] = gates
    rank_ref[rows, :] = rank.astype(jnp.int32)
    counts["total"] = before0 + jnp.sum(onehot, axis=0, keepdims=True)
    yield


def _merge(y_rnn, y_att, gl, x2, mod3, w_branch, w_out, norm2_g, w_router, b_router, rows_per_batch):
    n, D = x2.shape
    tm = 2 * MERGE_SUB
    tpb = rows_per_batch // tm
    wr32 = jnp.zeros((D, LANES), F32).at[:, :N_EXPERTS].set(w_router)
    wr_hi = wr32.astype(BF16)
    wr = jnp.concatenate([wr_hi, (wr32 - wr_hi.astype(F32)).astype(BF16)], axis=1)
    br = jnp.zeros((1, LANES), F32).at[0, :N_EXPERTS].set(b_router)
    row = lambda i: (i, 0)
    fixed = lambda i: (0, 0)
    once = pl.Buffered(1)
    wide = jax.ShapeDtypeStruct((n, D), F32)
    narrow_i = jax.ShapeDtypeStruct((n, LANES), jnp.int32)
    return pl.pallas_call(
        _merge_body,
        grid=(n // tm,),
        in_specs=[pl.BlockSpec((tm, D), row),
                  pl.BlockSpec((tm, D), row),
                  pl.BlockSpec((tm, D), lambda i: (i, 0)),
                  pl.BlockSpec((tm, D), lambda i: (i, 1)),
                  pl.BlockSpec((tm, D), row),
                  pl.BlockSpec((1, 6, D), lambda i: (i // tpb, 0, 0)),
                  pl.BlockSpec((2, D, D), lambda i: (0, 0, 0), pipeline_mode=once),
                  pl.BlockSpec((D, D), fixed, pipeline_mode=once),
                  pl.BlockSpec((1, D), fixed),
                  pl.BlockSpec((D, 2 * LANES), fixed, pipeline_mode=once),
                  pl.BlockSpec((1, LANES), fixed)],
        out_specs=[pl.BlockSpec((tm, D), row),
                   pl.BlockSpec((tm, D), row),
                   pl.BlockSpec((tm, LANES), row),
                   pl.BlockSpec((tm, LANES), row),
                   pl.BlockSpec((tm, LANES), row),
                   pl.BlockSpec((8, LANES), fixed)],
        out_shape=[wide, wide, narrow_i, jax.ShapeDtypeStruct((n, LANES), F32), narrow_i,
                   jax.ShapeDtypeStruct((8, LANES), jnp.int32)],
        scratch_shapes=[pltpu.VMEM((1, LANES), F32)],
        compiler_params=_params("arbitrary"),
        name="merge",
    )(y_rnn, y_att, gl, gl, x2, mod3, w_branch.astype(BF16), w_out.astype(BF16),
      norm2_g.reshape(1, D), wr, br)


DISPATCH_TOKENS = 256
ROW_DMA_UNROLL = 8


def _dispatch_body(dest_ref, zblk_ref, src_ref, dst_hbm, zbuf, tiles, sem, zsem):
    t = DISPATCH_TOKENS
    c = pl.program_id(0)
    tiles[...] = src_ref[...].reshape(tiles.shape)

    @pl.when(c == 0)
    def _():
        zbuf[...] = jnp.zeros_like(zbuf)

        def zero_copy(b):
            r0 = pl.multiple_of(b * EXPERT_ROWS, EXPERT_ROWS)
            return pltpu.make_async_copy(zbuf, dst_hbm.at[pl.ds(r0, EXPERT_ROWS)], zsem)

        def zissue(j, carry):
            @pl.when(zblk_ref[j] >= 0)
            def _():
                zero_copy(zblk_ref[j]).start()
            return carry

        def zwait(j, carry):
            @pl.when(zblk_ref[j] >= 0)
            def _():
                zero_copy(zblk_ref[j]).wait()
            return carry

        lax.fori_loop(0, zblk_ref.shape[0], zissue, 0)
        lax.fori_loop(0, zblk_ref.shape[0], zwait, 0)

    def issue(g, carry):
        for u in range(ROW_DMA_UNROLL):
            r = g * ROW_DMA_UNROLL + u
            for k in range(TOP_K):
                d = dest_ref[(c * t + r) * TOP_K + k]
                pltpu.make_async_copy(tiles.at[r], dst_hbm.at[d], sem).start()
        return carry

    lax.fori_loop(0, t // ROW_DMA_UNROLL, issue, 0)
    for k in range(TOP_K):
        pltpu.make_async_copy(tiles, dst_hbm.at[pl.ds(0, t)], sem).wait()


def _dispatch(h2, dest, zero_blocks, n_rows):
    n, D = h2.shape
    return pl.pallas_call(
        _dispatch_body,
        grid_spec=pltpu.PrefetchScalarGridSpec(
            num_scalar_prefetch=2,
            grid=(n // DISPATCH_TOKENS,),
            in_specs=[pl.BlockSpec((DISPATCH_TOKENS, D), lambda i, d, z: (i, 0))],
            out_specs=pl.BlockSpec(memory_space=pl.ANY),
            scratch_shapes=[pltpu.VMEM((EXPERT_ROWS, D // LANES, LANES), F32),
                            pltpu.VMEM((DISPATCH_TOKENS, D // LANES, LANES), F32),
                            pltpu.SemaphoreType.DMA, pltpu.SemaphoreType.DMA],
        ),
        out_shape=jax.ShapeDtypeStruct((n_rows, D // LANES, LANES), F32),
        compiler_params=_params("arbitrary"),
        name="dispatch",
    )(dest, zero_blocks, h2)


def _experts_body(bexp_ref, nused_ref, x_ref, wu_ref, bu_ref, wd_ref, bd_ref, o_ref, wu_bf, wd_bf):
    i = pl.program_id(0)
    dff = wd_ref.shape[1]
    used = i < nused_ref[0]

    new_expert = (i == 0) | (bexp_ref[i] != bexp_ref[jnp.maximum(i - 1, 0)])

    @pl.when(used & new_expert)
    def _():
        wu_bf[...] = wu_ref[0].astype(BF16)
        wd_bf[...] = wd_ref[0].astype(BF16)

    @pl.when(used)
    def _():
        x = x_ref[...].reshape(x_ref.shape[0], -1)
        hc = jnp.dot(x.astype(BF16), wu_bf[...], preferred_element_type=F32) + bu_ref[0]
        g = jnp.minimum(hc[:, :dff], SWIGLU_LIMIT)
        lin = jnp.clip(hc[:, dff:], -SWIGLU_LIMIT, SWIGLU_LIMIT)
        act = (lin + 1.0) * g * jax.nn.sigmoid(SWIGLU_ALPHA * g)
        y = jnp.dot(act.astype(BF16), wd_bf[...], preferred_element_type=F32) + bd_ref[0]
        o_ref[...] = y.reshape(o_ref.shape)

    @pl.when(i >= nused_ref[0])
    def _():
        o_ref[...] = jnp.zeros_like(o_ref)


def _experts(x_rows, block_exp, n_used, w_up, b_up, w_down, b_down, n_blocks):
    R, sub, _ = x_rows.shape
    E, D, F2 = w_up.shape
    dff = w_down.shape[1]
    blk = lambda i, be, nu: (i, 0, 0)
    wsel = lambda i, be, nu: (be[i], 0, 0)
    return pl.pallas_call(
        _experts_body,
        grid_spec=pltpu.PrefetchScalarGridSpec(
            num_scalar_prefetch=2,
            grid=(n_blocks,),
            in_specs=[pl.BlockSpec((EXPERT_ROWS, sub, LANES), blk),
                      pl.BlockSpec((1, D, F2), wsel),
                      pl.BlockSpec((1, 1, F2), wsel),
                      pl.BlockSpec((1, dff, D), wsel),
                      pl.BlockSpec((1, 1, D), wsel)],
            out_specs=pl.BlockSpec((EXPERT_ROWS, sub, LANES), blk),
            scratch_shapes=[pltpu.VMEM((D, F2), BF16), pltpu.VMEM((dff, D), BF16)],
        ),
        out_shape=jax.ShapeDtypeStruct((R, sub, LANES), F32),
        compiler_params=_params("arbitrary"),
        name="experts",
    )(block_exp, n_used, x_rows, w_up, b_up.reshape(E, 1, F2), w_down, b_down.reshape(E, 1, D))


COMBINE_ROWS = 128


def _combine_body(dest_ref, y_hbm, x1_ref, gate_ref, mod_ref, o_ref, buf, sem):
    i = pl.program_id(0)
    t = COMBINE_ROWS

    def issue(g, carry):
        for u in range(ROW_DMA_UNROLL):
            r = g * ROW_DMA_UNROLL + u
            for k in range(TOP_K):
                d = dest_ref[(i * t + r) * TOP_K + k]
                pltpu.make_async_copy(y_hbm.at[d], buf.at[k, r], sem).start()
        return carry

    lax.fori_loop(0, t // ROW_DMA_UNROLL, issue, 0)
    for k in range(TOP_K):
        pltpu.make_async_copy(y_hbm.at[pl.ds(0, t)], buf.at[k], sem).wait()
    gates = gate_ref[...]
    y = gates[:, 0:1] * buf[0].reshape(t, -1)
    for k in range(1, TOP_K):
        y = y + gates[:, k:k + 1] * buf[k].reshape(t, -1)
    o_ref[...] = x1_ref[...] + mod_ref[0, 5:6, :] * y


def _combine(y_rows, dest, x1, gates, mod3, rows_per_batch):
    n, D = x1.shape
    t = COMBINE_ROWS
    tpb = rows_per_batch // t
    return pl.pallas_call(
        _combine_body,
        grid_spec=pltpu.PrefetchScalarGridSpec(
            num_scalar_prefetch=1,
            grid=(n // t,),
            in_specs=[pl.BlockSpec(memory_space=pl.ANY),
                      pl.BlockSpec((t, D), lambda i, d: (i, 0)),
                      pl.BlockSpec((t, LANES), lambda i, d: (i, 0)),
                      pl.BlockSpec((1, 6, D), lambda i, d: (i // tpb, 0, 0))],
            out_specs=pl.BlockSpec((t, D), lambda i, d: (i, 0)),
            scratch_shapes=[pltpu.VMEM((TOP_K, t, D // LANES, LANES), F32), pltpu.SemaphoreType.DMA],
        ),
        out_shape=jax.ShapeDtypeStruct((n, D), F32),
        compiler_params=_params("arbitrary"),
        name="combine",
    )(dest, y_rows, x1, gates, mod3)


def kernel(x, c, positions, ada_w, ada_b, norm1_g, norm2_g, w_in, conv_w, conv_b, w_rg_a, b_rg_a, w_rg_x, b_rg_x, lru_lambda, q_norm_g, k_norm_g, b_gate, w_branch, w_out, w_router, b_router, w_up, b_up, w_down, b_down):
    B, S, D = x.shape
    n = B * S
    depth = ada_w.shape[0]
    x2 = x.reshape(n, D)
    cos_t, sin_t = _rope_tables(positions)
    perm = _rope_head_perm()
    head_cols = (jnp.arange(2 * N_HEADS)[:, None] * HEAD_DIM + perm[None, :]).reshape(-1)
    for l in range(depth):
        mod3 = _adaln(c, ada_w[l], ada_b[l]).reshape(B, 6, D)
        h1 = _norm_mod_call(x2, mod3, norm1_g[l], S)
        w = w_in[l].astype(BF16)
        xr = _proj("plain", h1, w[:, 0:D], [], F32, "proj_xr")
        gg = _proj("gelu", h1, w[:, D:2 * D], [], F32, "proj_gr")
        qk_g = jnp.concatenate([jnp.tile(q_norm_g[l][perm], N_HEADS),
                                jnp.tile(k_norm_g[l][perm], N_HEADS)]).reshape(1, 2 * D)
        qk = _proj("qk", h1, w[:, 2 * D:4 * D][:, head_cols], [qk_g, cos_t, sin_t], F32, "proj_qk")
        v = _proj("plain", h1, w[:, 4 * D:5 * D], [], BF16, "proj_v")
        gl = _proj("gate", h1, w[:, 5 * D:7 * D], [b_gate[l].reshape(1, 2 * D)], F32, "proj_gl")

        y_rnn = _rglru(xr, gg, conv_w[l], conv_b[l], w_rg_a[l], b_rg_a[l], w_rg_x[l], b_rg_x[l],
                       lru_lambda[l], B, S)
        y_att = _moba(qk, v, B, S)

        x1, h2, topi, gates, rank, cnt = _merge(y_rnn, y_att, gl, x2, mod3, w_branch[l], w_out[l],
                                                norm2_g[l], w_router[l], b_router[l], S)

        counts = cnt[0, :N_EXPERTS]
        padded = (counts + EXPERT_ROWS - 1) // EXPERT_ROWS * EXPERT_ROWS
        pad_end = jnp.cumsum(padded)
        pad_start = pad_end - padded
        top_idx = topi[:, :TOP_K]
        dest = (pad_start[top_idx] + rank[:, :TOP_K]).reshape(n * TOP_K).astype(jnp.int32)
        n_blocks = (n * TOP_K) // EXPERT_ROWS + N_EXPERTS
        block_start = jnp.arange(n_blocks, dtype=jnp.int32) * EXPERT_ROWS
        block_exp = jnp.minimum(jnp.sum(block_start[:, None] >= pad_end[None, :], axis=1),
                                N_EXPERTS - 1).astype(jnp.int32)
        n_used = (pad_end[-1] // EXPERT_ROWS).astype(jnp.int32)
        last_blk = jnp.where(padded > 0, pad_end // EXPERT_ROWS - 1, -1)
        spare = n_used + jnp.arange(N_EXPERTS, dtype=jnp.int32)
        spare = jnp.where(spare < n_blocks, spare, -1)
        zero_blocks = jnp.concatenate([last_blk, spare]).astype(jnp.int32)

        x_rows = _dispatch(h2, dest, zero_blocks, n_blocks * EXPERT_ROWS)
        y_rows = _experts(x_rows, block_exp, n_used.reshape(1), w_up[l], b_up[l], w_down[l], b_down[l], n_blocks)
        x2 = _combine(y_rows, dest, x1, gates, mod3, S)
    return x2.reshape(B, S, D)
```

```python
import functools

import jax
import jax.numpy as jnp
from jax import lax
from jax.experimental import pallas as pl
from jax.experimental.pallas import tpu as pltpu

F32 = jnp.float32
BF16 = jnp.bfloat16
HIGHEST = lax.Precision.HIGHEST

EPS = 1e-6
LANES = 128
N_HEADS = 8
HEAD_DIM = 128
RNN_BLOCKS = 8
RNN_BW = 128
CONV_W = 4
LRU_C = 8.0
MOBA_BLOCK = 256
MOBA_TOPK = 3
MOBA_CHUNK = 4
MOBA_HEADS = 4
MERGE_SUB = 256
LOG2_E = 1.4426950408889634
ROPE_DIMS = HEAD_DIM // 4
ROPE_THETA = 500000.0
N_EXPERTS = 32
TOP_K = 4
SWIGLU_LIMIT = 7.0
SWIGLU_ALPHA = 1.702
EXPERT_ROWS = 256
NEG_BIG = -1e30

VMEM_LIMIT = 56 * 1024 * 1024


def _params(*sem):
    return pltpu.CompilerParams(dimension_semantics=sem, vmem_limit_bytes=VMEM_LIMIT)


def _adaln_body(c_ref, w_ref, b_ref, o_ref):
    cs = c_ref[...]
    cs = cs * jax.nn.sigmoid(cs)
    o_ref[...] = jnp.dot(cs, w_ref[...], preferred_element_type=F32, precision=HIGHEST) + b_ref[...]


def _adaln(c, ada_w, ada_b):
    B, D = c.shape
    W = ada_w.shape[1]
    cpad = jnp.zeros((8, D), F32).at[:B].set(c)
    tn = 1024
    mod = pl.pallas_call(
        _adaln_body,
        grid=(W // tn,),
        in_specs=[pl.BlockSpec((8, D), lambda j: (0, 0)),
                  pl.BlockSpec((D, tn), lambda j: (0, j)),
                  pl.BlockSpec((1, tn), lambda j: (0, j))],
        out_specs=pl.BlockSpec((8, tn), lambda j: (0, j)),
        out_shape=jax.ShapeDtypeStruct((8, W), F32),
        compiler_params=_params("parallel"),
        name="adaln",
    )(cpad, ada_w, ada_b.reshape(1, W))
    return mod[:B]


def _rope_head_perm():
    half = ROPE_DIMS // 2
    mid = HEAD_DIM // 2
    return jnp.concatenate([jnp.arange(0, half), jnp.arange(ROPE_DIMS, mid + half),
                            jnp.arange(half, ROPE_DIMS), jnp.arange(mid + half, HEAD_DIM)])


def _rope_body(pos_ref, freq_ref, c_ref, s_ref):
    ang = pos_ref[...].astype(F32) * freq_ref[...]
    lane = lax.broadcasted_iota(jnp.int32, ang.shape, 1)
    half = ROPE_DIMS // 2
    mid = HEAD_DIM // 2
    s = jnp.sin(ang)
    c_ref[...] = jnp.cos(ang)
    s_ref[...] = jnp.where(lane < half, -s, jnp.where((lane >= mid) & (lane < mid + half), s, 0.0))


def _rope_tables(positions):
    n = positions.size
    half = ROPE_DIMS // 2
    mid = HEAD_DIM // 2
    freqs = ROPE_THETA ** (-jnp.arange(half, dtype=F32) / half)
    freq_lane = jnp.zeros((1, LANES), F32).at[0, :half].set(freqs).at[0, mid:mid + half].set(freqs)
    tm = 1024
    tab = jax.ShapeDtypeStruct((n, LANES), F32)
    return pl.pallas_call(
        _rope_body,
        grid=(n // tm,),
        in_specs=[pl.BlockSpec((tm, 1), lambda i: (i, 0)),
                  pl.BlockSpec((1, LANES), lambda i: (0, 0))],
        out_specs=[pl.BlockSpec((tm, LANES), lambda i: (i, 0))] * 2,
        out_shape=[tab, tab],
        compiler_params=_params("parallel"),
        name="rope_tab",
    )(positions.reshape(n, 1), freq_lane)


def _norm_mod(x, g, sh, sc):
    ms = jnp.mean(x * x, axis=-1, keepdims=True)
    y = x * lax.rsqrt(ms + EPS)
    return (y * g) * (1.0 + sc) + sh


def _norm_mod_body(x_ref, mod_ref, g_ref, o_ref):
    o_ref[...] = _norm_mod(x_ref[...], g_ref[...], mod_ref[0, 0:1, :], mod_ref[0, 1:2, :]).astype(o_ref.dtype)


def _norm_mod_call(x2, mod3, g, rows_per_batch):
    n, D = x2.shape
    tm = 512
    tpb = rows_per_batch // tm
    return pl.pallas_call(
        _norm_mod_body,
        grid=(n // tm,),
        in_specs=[pl.BlockSpec((tm, D), lambda i: (i, 0)),
                  pl.BlockSpec((1, 6, D), lambda i: (i // tpb, 0, 0)),
                  pl.BlockSpec((1, D), lambda i: (0, 0))],
        out_specs=pl.BlockSpec((tm, D), lambda i: (i, 0)),
        out_shape=jax.ShapeDtypeStruct((n, D), BF16),
        compiler_params=_params("parallel"),
        name="norm_mod",
    )(x2, mod3, g.reshape(1, D))


def _gelu_tanh(x):
    return 0.5 * x * (1.0 + jnp.tanh(0.7978845608028654 * (x + 0.044715 * (x * x * x))))


PROJ_SUB = 256
PROJ_PIECE = 256


def _qk_head(seg, g, cos, sin):
    ms = jnp.mean(seg * seg, axis=-1, keepdims=True)
    y = seg * lax.rsqrt(ms + EPS) * g
    return y * cos + pltpu.roll(y, HEAD_DIM // 2, axis=1) * sin


def _proj_stages(kind, rows, h_ref, w_ref, extras, o_ref):
    n_pieces = w_ref.shape[1] // PROJ_PIECE
    pieces = []
    for j in range(n_pieces):
        pieces.append(jnp.dot(h_ref[rows, :], w_ref[:, j * PROJ_PIECE:(j + 1) * PROJ_PIECE],
                              preferred_element_type=F32))
        yield
    for j in range(n_pieces):
        cols = slice(j * PROJ_PIECE, (j + 1) * PROJ_PIECE)
        acc = pieces[j]
        if kind == "gelu":
            acc = _gelu_tanh(acc)
        elif kind == "gate":
            (b_ref,) = extras
            acc = jax.nn.sigmoid(acc + b_ref[:, cols])
        elif kind == "qk":
            g_ref, c_ref, s_ref = extras
            cos, sin = c_ref[rows, :], s_ref[rows, :]
            acc = jnp.concatenate(
                [_qk_head(acc[:, o:o + HEAD_DIM], g_ref[:, cols.start + o:cols.start + o + HEAD_DIM], cos, sin)
                 for o in range(0, PROJ_PIECE, HEAD_DIM)], axis=1)
        o_ref[rows, cols] = acc.astype(o_ref.dtype)
        yield


def _proj_body(kind, h_ref, w_ref, *rest):
    *extras, o_ref = rest
    first, second = [_proj_stages(kind, slice(t * PROJ_SUB, (t + 1) * PROJ_SUB), h_ref, w_ref, extras, o_ref)
                     for t in range(2)]
    n_pieces = w_ref.shape[1] // PROJ_PIECE
    for _ in range(n_pieces):
        next(first)
    for _ in range(n_pieces):
        next(second)
        next(first)
    for stage in second:
        pass


def _proj(kind, h, w, extras, out_dtype, name):
    n, D = h.shape
    W = w.shape[1]
    tm, tn = 512, 1024
    in_specs = [pl.BlockSpec((tm, D), lambda j, i: (i, 0)),
                pl.BlockSpec((D, tn), lambda j, i: (0, j))]
    args = [h, w]
    for arr in extras:
        if arr.shape[0] == 1:
            in_specs.append(pl.BlockSpec((1, tn), lambda j, i: (0, j)))
        else:
            in_specs.append(pl.BlockSpec((tm, LANES), lambda j, i: (i, 0)))
        args.append(arr)
    return pl.pallas_call(
        functools.partial(_proj_body, kind),
        grid=(W // tn, n // tm),
        in_specs=in_specs,
        out_specs=pl.BlockSpec((tm, tn), lambda j, i: (i, j)),
        out_shape=jax.ShapeDtypeStruct((n, W), out_dtype),
        compiler_params=_params("parallel", "parallel"),
        name=name,
    )(*args)


def _rglru_body(xr_ref, gg_ref, cw_ref, cb_ref, wa_ref, ba_ref, wx_ref, bx_ref, lam_ref,
                o_ref, xbuf, hcar, a_s, u_s):
    s = pl.program_id(1)
    ts, D = xr_ref.shape

    @pl.when(s == 0)
    def _():
        xbuf[0:8, :] = jnp.zeros((8, D), F32)
        hcar[...] = jnp.zeros_like(hcar)

    @pl.when(s > 0)
    def _():
        xbuf[0:8, :] = xbuf[ts:ts + 8, :]

    xbuf[8:8 + ts, :] = xr_ref[...]
    xc = cb_ref[...] + cw_ref[0:1, :] * xbuf[8:8 + ts, :]
    for i in range(1, CONV_W):
        xc = xc + cw_ref[i:i + 1, :] * xbuf[8 - i:8 - i + ts, :]

    ra, rx = [], []
    for n in range(RNN_BLOCKS):
        xb = xc[:, n * RNN_BW:(n + 1) * RNN_BW].astype(BF16)
        ra.append(jnp.dot(xb, wa_ref[n], preferred_element_type=F32))
        rx.append(jnp.dot(xb, wx_ref[n], preferred_element_type=F32))
    r = jax.nn.sigmoid(jnp.concatenate(ra, axis=1) + ba_ref[...])
    ig = jax.nn.sigmoid(jnp.concatenate(rx, axis=1) + bx_ref[...])

    z = -lam_ref[...]
    softplus = jnp.maximum(z, 0.0) + jnp.log1p(jnp.exp(-jnp.abs(z)))
    log_a = -LRU_C * r * softplus
    a = jnp.exp(log_a)
    mult = jnp.sqrt(1.0 - a * a)
    row = lax.broadcasted_iota(jnp.int32, (ts, D), 0)
    mult = jnp.where((row == 0) & (s == 0), 1.0, mult)
    u = mult * (ig * xc)

    rm = row & 7
    for d in (1, 2, 4):
        keep = rm >= d
        a_sh = pltpu.roll(a, d, axis=0)
        u_sh = pltpu.roll(u, d, axis=0)
        u = jnp.where(keep, a * u_sh + u, u)
        a = jnp.where(keep, a * a_sh, a)
    a_s[...] = a
    u_s[...] = u

    def group(g, h):
        r0 = pl.multiple_of(g * 8, 8)
        hg = u_s[pl.ds(r0, 8), :] + a_s[pl.ds(r0, 8), :] * h
        u_s[pl.ds(r0, 8), :] = hg
        return hg[7:8, :]

    hcar[...] = lax.fori_loop(0, ts // 8, group, hcar[...])
    o_ref[...] = (u_s[...] * gg_ref[...]).astype(o_ref.dtype)


def _rglru(xr, gg, conv_w, conv_b, w_a, b_a, w_x, b_x, lam, B, S):
    n, D = xr.shape
    ts = 256
    spb = S // ts
    row = lambda b, s: (b * spb + s, 0)
    vec = lambda b, s: (0, 0)
    return pl.pallas_call(
        _rglru_body,
        grid=(B, spb),
        in_specs=[pl.BlockSpec((ts, D), row),
                  pl.BlockSpec((ts, D), row),
                  pl.BlockSpec((CONV_W, D), vec),
                  pl.BlockSpec((1, D), vec),
                  pl.BlockSpec((RNN_BLOCKS, RNN_BW, RNN_BW), lambda b, s: (0, 0, 0)),
                  pl.BlockSpec((1, D), vec),
                  pl.BlockSpec((RNN_BLOCKS, RNN_BW, RNN_BW), lambda b, s: (0, 0, 0)),
                  pl.BlockSpec((1, D), vec),
                  pl.BlockSpec((1, D), vec)],
        out_specs=pl.BlockSpec((ts, D), row),
        out_shape=jax.ShapeDtypeStruct((n, D), BF16),
        scratch_shapes=[pltpu.VMEM((ts + 8, D), F32), pltpu.VMEM((1, D), F32),
                        pltpu.VMEM((ts, D), F32), pltpu.VMEM((ts, D), F32)],
        compiler_params=_params("arbitrary", "arbitrary"),
        name="rglru",
    )(xr, gg, conv_w, conv_b.reshape(1, D), w_a.astype(BF16), b_a.reshape(1, D),
      w_x.astype(BF16), b_x.reshape(1, D), lam.reshape(1, D))


def _moba_body(q_ref, k_ref, v_ref, o_ref, kaug, vaug, kmean, m_s, acc_s):
    qb = pl.program_id(2)
    S = k_ref.shape[0]
    nb = S // MOBA_BLOCK
    bs = MOBA_BLOCK
    dh = HEAD_DIM
    heads = range(MOBA_HEADS)
    hcol = lambda hh: slice(hh * dh, (hh + 1) * dh)

    @pl.when(qb == 0)
    def _():
        blk = lax.broadcasted_iota(jnp.int32, (S, dh), 0) // bs
        col = lax.broadcasted_iota(jnp.int32, (S, dh), 1)
        onehot = jnp.where(col == blk, 1.0, 0.0).astype(BF16)
        for hh in heads:
            kaug[hh, :, 0:dh] = k_ref[:, hcol(hh)].astype(BF16)
            kaug[hh, :, dh:2 * dh] = onehot
            vaug[hh, :, 0:dh] = v_ref[:, hcol(hh)]
            vaug[hh, :, dh:2 * dh] = jnp.ones((S, dh), BF16)
            kmean[hh] = jnp.zeros((LANES, dh), F32)
            for j in range(nb):
                kmean[hh, j:j + 1, :] = jnp.mean(k_ref[j * bs:(j + 1) * bs, hcol(hh)], axis=0, keepdims=True)

    scale = HEAD_DIM ** -0.5
    nt = (((1,), (1,)), ((), ()))
    c2 = scale * LOG2_E
    r0 = pl.multiple_of(qb * bs, bs)
    lane = lax.broadcasted_iota(jnp.int32, (bs, LANES), 1)
    lane_f = lane.astype(F32)
    past = lane < qb
    rr = lax.broadcasted_iota(jnp.int32, (bs, bs), 0)
    cc = lax.broadcasted_iota(jnp.int32, (bs, bs), 1)

    qaugs = []
    for hh in heads:
        q = q_ref[:, hcol(hh)]
        qbf = q.astype(BF16)

        gate = lax.dot_general(q, kmean[hh], nt, preferred_element_type=F32, precision=HIGHEST)
        g = jnp.where(past, gate, -jnp.inf)
        sel = jnp.zeros(gate.shape, F32)
        for _ in range(MOBA_TOPK):
            m = jnp.max(g, axis=1, keepdims=True)
            idx = jnp.min(jnp.where(g == m, lane_f, float(LANES)), axis=1, keepdims=True)
            pick = (lane_f == idx) & past
            sel = jnp.where(pick, 1.0, sel)
            g = jnp.where(pick, -jnp.inf, g)
        bias = jnp.where(sel > 0.5, 0.0, NEG_BIG).astype(BF16)
        qaugs.append(jnp.concatenate([qbf, bias], axis=1))

        s = lax.dot_general(qbf, kaug[hh, pl.ds(r0, bs), 0:dh], nt, preferred_element_type=F32)
        s = jnp.where(cc <= rr, s, NEG_BIG)
        m0 = jnp.max(s, axis=1, keepdims=True)
        p = jnp.exp2((s - m0) * c2)
        m_s[hh] = m0
        acc_s[hh] = jnp.dot(p.astype(BF16), vaug[hh, pl.ds(r0, bs), :], preferred_element_type=F32)

    cw = MOBA_CHUNK * bs
    for c in range(nb // MOBA_CHUNK):
        @pl.when(c * MOBA_CHUNK < qb)
        def _(c=c):
            for hh in heads:
                sc = lax.dot_general(qaugs[hh], kaug[hh, c * cw:(c + 1) * cw, :], nt,
                                     preferred_element_type=F32)
                m_old = m_s[hh]
                m_new = jnp.maximum(m_old, jnp.max(sc, axis=1, keepdims=True))
                alpha = jnp.exp2((m_old - m_new) * c2)
                pc = jnp.exp2((sc - m_new) * c2)
                m_s[hh] = m_new
                acc_s[hh] = alpha * acc_s[hh] + jnp.dot(pc.astype(BF16), vaug[hh, c * cw:(c + 1) * cw, :],
                                                        preferred_element_type=F32)

    for hh in heads:
        o_ref[:, hcol(hh)] = (acc_s[hh, :, 0:dh] / acc_s[hh, :, dh:dh + 1]).astype(o_ref.dtype)


def _moba(qk, v, B, S):
    n, D = v.shape
    nq = S // MOBA_BLOCK
    hs = MOBA_HEADS
    w = hs * HEAD_DIM
    return pl.pallas_call(
        _moba_body,
        grid=(B, N_HEADS // hs, nq),
        in_specs=[pl.BlockSpec((MOBA_BLOCK, w), lambda b, h, i: (b * nq + i, h)),
                  pl.BlockSpec((S, w), lambda b, h, i: (b, N_HEADS // hs + h)),
                  pl.BlockSpec((S, w), lambda b, h, i: (b, h))],
        out_specs=pl.BlockSpec((MOBA_BLOCK, w), lambda b, h, i: (b * nq + i, h)),
        out_shape=jax.ShapeDtypeStruct((n, D), BF16),
        scratch_shapes=[pltpu.VMEM((hs, S, 2 * HEAD_DIM), BF16), pltpu.VMEM((hs, S, 2 * HEAD_DIM), BF16),
                        pltpu.VMEM((hs, LANES, HEAD_DIM), F32), pltpu.VMEM((hs, MOBA_BLOCK, 1), F32),
                        pltpu.VMEM((hs, MOBA_BLOCK, 2 * HEAD_DIM), F32)],
        compiler_params=_params("parallel", "parallel", "arbitrary"),
        name="moba",
    )(qk, qk, v)


def _merge_body(yr_ref, ya_ref, gr_ref, ga_ref, x_ref, mod_ref, wb_ref, wo_ref, g2_ref, wr_ref, br_ref,
                x1_ref, h2_ref, topi_ref, gate_ref, rank_ref, cnt_ref, carry):
    i = pl.program_id(0)

    @pl.when(i == 0)
    def _():
        carry[...] = jnp.zeros_like(carry)

    tm = MERGE_SUB
    counts = {"total": carry[...]}
    first, second = [
        _merge_stages(slice(h * tm, (h + 1) * tm), counts, yr_ref, ya_ref, gr_ref, ga_ref, x_ref, mod_ref,
                      wb_ref, wo_ref, g2_ref, wr_ref, br_ref, x1_ref, h2_ref, topi_ref, gate_ref, rank_ref)
        for h in range(2)]
    for _ in range(MERGE_MATMUL_STAGES):
        next(first)
    for stage in first:
        next(second)
    for stage in second:
        pass
    carry[...] = counts["total"]
    cnt_ref[...] = jnp.broadcast_to(counts["total"], cnt_ref.shape).astype(jnp.int32)


MERGE_MATMUL_STAGES = 4


def _merge_stages(rows, counts, yr_ref, ya_ref, gr_ref, ga_ref, x_ref, mod_ref, wb_ref, wo_ref, g2_ref, wr_ref,
                  br_ref, x1_ref, h2_ref, topi_ref, gate_ref, rank_ref):
    tm = rows.stop - rows.start
    zr = jnp.dot(yr_ref[rows, :], wb_ref[0], preferred_element_type=F32)
    yield
    za = jnp.dot(ya_ref[rows, :], wb_ref[1], preferred_element_type=F32)
    mix = (gr_ref[rows, :] * zr + ga_ref[rows, :] * za).astype(BF16)
    yield
    mixed = jnp.dot(mix, wo_ref[...], preferred_element_type=F32)
    x1 = x_ref[rows, :] + mod_ref[0, 2:3, :] * mixed
    x1_ref[rows, :] = x1
    h2 = _norm_mod(x1, g2_ref[...], mod_ref[0, 3:4, :], mod_ref[0, 4:5, :])
    h2_ref[rows] = h2.reshape(tm, -1, LANES)
    yield

    hi = h2.astype(BF16)
    lo = (h2 - hi.astype(F32)).astype(BF16)
    both = jnp.dot(hi, wr_ref[...], preferred_element_type=F32)
    logits = (both[:, :LANES] + both[:, LANES:]
              + jnp.dot(lo, wr_ref[:, :LANES], preferred_element_type=F32) + br_ref[...])
    yield

    lane = lax.broadcasted_iota(jnp.int32, logits.shape, 1)
    lane_f = lane.astype(F32)
    lg = jnp.where(lane < N_EXPERTS, logits, -jnp.inf)
    vals, idxs = [], []
    onehot = jnp.zeros(logits.shape, F32)
    for _ in range(TOP_K):
        m = jnp.max(lg, axis=1, keepdims=True)
        idx = jnp.min(jnp.where(lg == m, lane_f, float(LANES)), axis=1, keepdims=True)
        pick = lane_f == idx
        onehot = jnp.where(pick, 1.0, onehot)
        lg = jnp.where(pick, -jnp.inf, lg)
        vals.append(m)
        idxs.append(idx)
        yield
    es = [jnp.exp(v - vals[0]) for v in vals]
    den = es[0] + es[1] + es[2] + es[3]
    before0 = counts["total"]

    rr = lax.broadcasted_iota(jnp.int32, (tm, tm), 0)
    cc = lax.broadcasted_iota(jnp.int32, (tm, tm), 1)
    tri = jnp.where(cc < rr, 1.0, 0.0).astype(BF16)
    before = jnp.dot(tri, onehot.astype(BF16), preferred_element_type=F32) + before0

    topi = jnp.zeros(logits.shape, F32)
    gates = jnp.zeros(logits.shape, F32)
    rank = jnp.zeros(logits.shape, F32)
    for k in range(TOP_K):
        rk = jnp.sum(jnp.where(lane_f == idxs[k], before, 0.0), axis=1, keepdims=True)
        topi = jnp.where(lane == k, idxs[k], topi)
        gates = jnp.where(lane == k, es[k] / den, gates)
        rank = jnp.where(lane == k, rk, rank)
    topi_ref[rows, :] = topi.astype(jnp.int32)
    gate_ref[rows, :] = gates
    rank_ref[rows, :] = rank.astype(jnp.int32)
    counts["total"] = before0 + jnp.sum(onehot, axis=0, keepdims=True)
    yield


def _merge(y_rnn, y_att, gl, x2, mod3, w_branch, w_out, norm2_g, w_router, b_router, rows_per_batch):
    n, D = x2.shape
    tm = 2 * MERGE_SUB
    tpb = rows_per_batch // tm
    wr32 = jnp.zeros((D, LANES), F32).at[:, :N_EXPERTS].set(w_router)
    wr_hi = wr32.astype(BF16)
    wr = jnp.concatenate([wr_hi, (wr32 - wr_hi.astype(F32)).astype(BF16)], axis=1)
    br = jnp.zeros((1, LANES), F32).at[0, :N_EXPERTS].set(b_router)
    row = lambda i: (i, 0)
    fixed = lambda i: (0, 0)
    once = pl.Buffered(1)
    wide = jax.ShapeDtypeStruct((n, D), F32)
    narrow_i = jax.ShapeDtypeStruct((n, LANES), jnp.int32)
    return pl.pallas_call(
        _merge_body,
        grid=(n // tm,),
        in_specs=[pl.BlockSpec((tm, D), row),
                  pl.BlockSpec((tm, D), row),
                  pl.BlockSpec((tm, D), lambda i: (i, 0)),
                  pl.BlockSpec((tm, D), lambda i: (i, 1)),
                  pl.BlockSpec((tm, D), row),
                  pl.BlockSpec((1, 6, D), lambda i: (i // tpb, 0, 0)),
                  pl.BlockSpec((2, D, D), lambda i: (0, 0, 0), pipeline_mode=once),
                  pl.BlockSpec((D, D), fixed, pipeline_mode=once),
                  pl.BlockSpec((1, D), fixed),
                  pl.BlockSpec((D, 2 * LANES), fixed, pipeline_mode=once),
                  pl.BlockSpec((1, LANES), fixed)],
        out_specs=[pl.BlockSpec((tm, D), row),
                   pl.BlockSpec((tm, D // LANES, LANES), lambda i: (i, 0, 0)),
                   pl.BlockSpec((tm, LANES), row),
                   pl.BlockSpec((tm, LANES), row),
                   pl.BlockSpec((tm, LANES), row),
                   pl.BlockSpec((8, LANES), fixed)],
        out_shape=[wide, jax.ShapeDtypeStruct((n, D // LANES, LANES), F32), narrow_i,
                   jax.ShapeDtypeStruct((n, LANES), F32), narrow_i,
                   jax.ShapeDtypeStruct((8, LANES), jnp.int32)],
        scratch_shapes=[pltpu.VMEM((1, LANES), F32)],
        compiler_params=_params("arbitrary"),
        name="merge",
    )(y_rnn, y_att, gl, gl, x2, mod3, w_branch.astype(BF16), w_out.astype(BF16),
      norm2_g.reshape(1, D), wr, br)


DISPATCH_TOKENS = 256
ROW_DMA_UNROLL = 8


def _dispatch_body(dest_ref, zblk_ref, src_ref, dst_hbm, zbuf, sem, zsem):
    t = DISPATCH_TOKENS
    c = pl.program_id(0)

    @pl.when(c == 0)
    def _():
        zbuf[...] = jnp.zeros_like(zbuf)

        def zero_copy(b):
            r0 = pl.multiple_of(b * EXPERT_ROWS, EXPERT_ROWS)
            return pltpu.make_async_copy(zbuf, dst_hbm.at[pl.ds(r0, EXPERT_ROWS), :], zsem)

        def zissue(j, carry):
            @pl.when(zblk_ref[j] >= 0)
            def _():
                zero_copy(zblk_ref[j]).start()
            return carry

        def zwait(j, carry):
            @pl.when(zblk_ref[j] >= 0)
            def _():
                zero_copy(zblk_ref[j]).wait()
            return carry

        lax.fori_loop(0, zblk_ref.shape[0], zissue, 0)
        lax.fori_loop(0, zblk_ref.shape[0], zwait, 0)

    def issue(g, carry):
        for u in range(ROW_DMA_UNROLL):
            r = g * ROW_DMA_UNROLL + u
            for k in range(TOP_K):
                d = dest_ref[(c * t + r) * TOP_K + k]
                pltpu.make_async_copy(src_ref.at[pl.ds(r, 1), :], dst_hbm.at[pl.ds(d, 1), :], sem).start()
        return carry

    lax.fori_loop(0, t // ROW_DMA_UNROLL, issue, 0)
    for k in range(TOP_K):
        pltpu.make_async_copy(src_ref, dst_hbm.at[pl.ds(0, t), :], sem).wait()


def _dispatch(h2, dest, zero_blocks, n_rows):
    n, D = h2.shape
    return pl.pallas_call(
        _dispatch_body,
        grid_spec=pltpu.PrefetchScalarGridSpec(
            num_scalar_prefetch=2,
            grid=(n // DISPATCH_TOKENS,),
            in_specs=[pl.BlockSpec((DISPATCH_TOKENS, D), lambda i, d, z: (i, 0))],
            out_specs=pl.BlockSpec(memory_space=pl.ANY),
            scratch_shapes=[pltpu.VMEM((EXPERT_ROWS, D), F32), pltpu.SemaphoreType.DMA,
                            pltpu.SemaphoreType.DMA],
        ),
        out_shape=jax.ShapeDtypeStruct((n_rows, D), F32),
        compiler_params=_params("arbitrary"),
        name="dispatch",
    )(dest, zero_blocks, h2)


def _experts_body(bexp_ref, nused_ref, x_ref, wu_ref, bu_ref, wd_ref, bd_ref, o_ref, wu_bf, wd_bf):
    i = pl.program_id(0)
    dff = wd_ref.shape[1]
    used = i < nused_ref[0]

    new_expert = (i == 0) | (bexp_ref[i] != bexp_ref[jnp.maximum(i - 1, 0)])

    @pl.when(used & new_expert)
    def _():
        wu_bf[...] = wu_ref[0].astype(BF16)
        wd_bf[...] = wd_ref[0].astype(BF16)

    @pl.when(used)
    def _():
        hc = jnp.dot(x_ref[...].astype(BF16), wu_bf[...], preferred_element_type=F32) + bu_ref[0]
        g = jnp.minimum(hc[:, :dff], SWIGLU_LIMIT)
        lin = jnp.clip(hc[:, dff:], -SWIGLU_LIMIT, SWIGLU_LIMIT)
        act = (lin + 1.0) * g * jax.nn.sigmoid(SWIGLU_ALPHA * g)
        o_ref[...] = jnp.dot(act.astype(BF16), wd_bf[...], preferred_element_type=F32) + bd_ref[0]

    @pl.when(i >= nused_ref[0])
    def _():
        o_ref[...] = jnp.zeros_like(o_ref)


def _experts(x_rows, block_exp, n_used, w_up, b_up, w_down, b_down, n_blocks):
    R, D = x_rows.shape
    E, _, F2 = w_up.shape
    dff = w_down.shape[1]
    blk = lambda i, be, nu: (i, 0)
    wsel = lambda i, be, nu: (be[i], 0, 0)
    return pl.pallas_call(
        _experts_body,
        grid_spec=pltpu.PrefetchScalarGridSpec(
            num_scalar_prefetch=2,
            grid=(n_blocks,),
            in_specs=[pl.BlockSpec((EXPERT_ROWS, D), blk),
                      pl.BlockSpec((1, D, F2), wsel),
                      pl.BlockSpec((1, 1, F2), wsel),
                      pl.BlockSpec((1, dff, D), wsel),
                      pl.BlockSpec((1, 1, D), wsel)],
            out_specs=pl.BlockSpec((EXPERT_ROWS, D), blk),
            scratch_shapes=[pltpu.VMEM((D, F2), BF16), pltpu.VMEM((dff, D), BF16)],
        ),
        out_shape=jax.ShapeDtypeStruct((R, D), F32),
        compiler_params=_params("arbitrary"),
        name="experts",
    )(block_exp, n_used, x_rows, w_up, b_up.reshape(E, 1, F2), w_down, b_down.reshape(E, 1, D))


COMBINE_ROWS = 128


def _combine_body(dest_ref, y_hbm, x1_ref, gate_ref, mod_ref, o_ref, buf, sem):
    i = pl.program_id(0)
    t = COMBINE_ROWS

    def issue(g, carry):
        for u in range(ROW_DMA_UNROLL):
            r = g * ROW_DMA_UNROLL + u
            for k in range(TOP_K):
                d = dest_ref[(i * t + r) * TOP_K + k]
                pltpu.make_async_copy(y_hbm.at[pl.ds(d, 1), :], buf.at[pl.ds(r, 1), pl.ds(k * D, D)], sem).start()
        return carry

    D = o_ref.shape[1]
    lax.fori_loop(0, t // ROW_DMA_UNROLL, issue, 0)
    for k in range(TOP_K):
        pltpu.make_async_copy(y_hbm.at[pl.ds(0, t), :], buf.at[:, pl.ds(k * D, D)], sem).wait()
    gates = gate_ref[...]
    y = gates[:, 0:1] * buf[:, 0:D]
    for k in range(1, TOP_K):
        y = y + gates[:, k:k + 1] * buf[:, k * D:(k + 1) * D]
    o_ref[...] = x1_ref[...] + mod_ref[0, 5:6, :] * y


def _combine(y_rows, dest, x1, gates, mod3, rows_per_batch):
    n, D = x1.shape
    t = COMBINE_ROWS
    tpb = rows_per_batch // t
    return pl.pallas_call(
        _combine_body,
        grid_spec=pltpu.PrefetchScalarGridSpec(
            num_scalar_prefetch=1,
            grid=(n // t,),
            in_specs=[pl.BlockSpec(memory_space=pl.ANY),
                      pl.BlockSpec((t, D), lambda i, d: (i, 0)),
                      pl.BlockSpec((t, LANES), lambda i, d: (i, 0)),
                      pl.BlockSpec((1, 6, D), lambda i, d: (i // tpb, 0, 0))],
            out_specs=pl.BlockSpec((t, D), lambda i, d: (i, 0)),
            scratch_shapes=[pltpu.VMEM((t, TOP_K * D), F32), pltpu.SemaphoreType.DMA],
        ),
        out_shape=jax.ShapeDtypeStruct((n, D), F32),
        compiler_params=_params("arbitrary"),
        name="combine",
    )(dest, y_rows, x1, gates, mod3)


MAP_SRC_BITS = 14
INV_STEPS = 16
INV_UNROLL = 16


def _row_choice_body(dest_ref, pad_blk_ref, inv_ref):
    s = pl.program_id(0)
    blocks_per_step = pad_blk_ref.shape[0] // INV_STEPS
    n_place = dest_ref.shape[0] // INV_STEPS

    @pl.when(s < INV_STEPS)
    def _():
        for j in range(blocks_per_step):
            blk = pad_blk_ref[s * blocks_per_step + j]

            @pl.when(blk >= 0)
            def _():
                def fill(g, carry):
                    for u in range(INV_UNROLL):
                        inv_ref[blk * EXPERT_ROWS + g * INV_UNROLL + u] = -1
                    return carry
                lax.fori_loop(0, EXPERT_ROWS // INV_UNROLL, fill, 0)

    @pl.when(s >= INV_STEPS)
    def _():
        def place(g, carry):
            for u in range(INV_UNROLL):
                a = (s - INV_STEPS) * n_place + g * INV_UNROLL + u
                inv_ref[dest_ref[a]] = a
            return carry
        lax.fori_loop(0, n_place // INV_UNROLL, place, 0)


def _row_choice(dest, pad_blocks, n_rows):
    assert pad_blocks.shape[0] % INV_STEPS == 0 and dest.shape[0] % (INV_STEPS * INV_UNROLL) == 0
    return pl.pallas_call(
        _row_choice_body,
        grid_spec=pltpu.PrefetchScalarGridSpec(
            num_scalar_prefetch=2,
            grid=(2 * INV_STEPS,),
            in_specs=[],
            out_specs=pl.BlockSpec(memory_space=pltpu.SMEM),
        ),
        out_shape=jax.ShapeDtypeStruct((n_rows,), jnp.int32),
        compiler_params=_params("arbitrary"),
        name="row_choice",
    )(dest, pad_blocks)


def _moe_body(bexp_ref, map_ref, h2_hbm, wu_ref, bu_ref, wd_ref, bd_ref, ytok_hbm,
              xbuf, ybuf, xb, yacc, wu_bf, wd_bf, gsem, ssem, zsem):
    i = pl.program_id(0)
    last = pl.num_programs(0) - 1
    R = EXPERT_ROWS
    dff = wd_ref.shape[1]
    cur, nxt = i % 2, 1 - i % 2
    xcur, xnew = i % 3, (i + 2) % 3
    src_mask = (1 << MAP_SRC_BITS) - 1
    n_tok_rows = ytok_hbm.shape[0] - 2 * R

    def gather_row(b, r, slot):
        src = map_ref[(b + 1) * R + r] & src_mask
        pltpu.make_async_copy(h2_hbm.at[src], xbuf.at[slot, r], gsem.at[slot]).start()

    def scatter_row(b, r, slot):
        dst = map_ref[(b + 1) * R + r] >> MAP_SRC_BITS
        pltpu.make_async_copy(ybuf.at[slot, r], ytok_hbm.at[dst], ssem.at[slot]).start()

    def gather_wait(slot):
        pltpu.make_async_copy(h2_hbm.at[pl.ds(0, R)], xbuf.at[slot], gsem.at[slot]).wait()

    def scatter_wait(slot):
        pltpu.make_async_copy(ybuf.at[slot], ytok_hbm.at[pl.ds(0, R)], ssem.at[slot]).wait()

    def for_rows(fn):
        def body(g, carry):
            for u in range(ROW_DMA_UNROLL):
                fn(g * ROW_DMA_UNROLL + u)
            return carry
        lax.fori_loop(0, R // ROW_DMA_UNROLL, body, 0)

    @pl.when(i == 0)
    def _():
        ybuf[...] = jnp.zeros_like(ybuf)
        for s in range(2):
            zc = pltpu.make_async_copy(ybuf.at[s], ytok_hbm.at[pl.ds(n_tok_rows + s * R, R)], zsem)
            zc.start()
            zc.wait()
        for_rows(lambda r: gather_row(0, r, 0))
        for_rows(lambda r: gather_row(1, r, 1))

    @pl.when((i == 0) | (bexp_ref[i] != bexp_ref[jnp.maximum(i - 1, 0)]))
    def _():
        wu_bf[...] = wu_ref[0].astype(BF16)
        wd_bf[...] = wd_ref[0].astype(BF16)

    gather_wait(xcur)
    xb[...] = xbuf[xcur].reshape(R, -1).astype(BF16)

    for r in range(R):
        gather_row(i + 2, r, xnew)
        scatter_row(i - 1, r, nxt)

    hc = jnp.dot(xb[...], wu_bf[...], preferred_element_type=F32) + bu_ref[0]
    g = jnp.minimum(hc[:, :dff], SWIGLU_LIMIT)
    lin = jnp.clip(hc[:, dff:], -SWIGLU_LIMIT, SWIGLU_LIMIT)
    act = (lin + 1.0) * g * jax.nn.sigmoid(SWIGLU_ALPHA * g)
    yacc[...] = jnp.dot(act.astype(BF16), wd_bf[...], preferred_element_type=F32) + bd_ref[0]

    @pl.when(i >= 1)
    def _():
        scatter_wait(cur)

    ybuf[cur] = yacc[...].reshape(ybuf.shape[1:])

    @pl.when(i == last)
    def _():
        for_rows(lambda r: scatter_row(i, r, cur))
        scatter_wait(nxt)
        scatter_wait(cur)
        gather_wait((i + 1) % 3)
        gather_wait((i + 2) % 3)


def _moe(h2_tiles, block_exp, row_map, w_up, b_up, w_down, b_down, n_blocks):
    n, sub, _ = h2_tiles.shape
    E, D, F2 = w_up.shape
    dff = w_down.shape[1]
    wsel = lambda i, be, rm: (be[i], 0, 0)
    return pl.pallas_call(
        _moe_body,
        grid_spec=pltpu.PrefetchScalarGridSpec(
            num_scalar_prefetch=2,
            grid=(n_blocks,),
            in_specs=[pl.BlockSpec(memory_space=pl.ANY),
                      pl.BlockSpec((1, D, F2), wsel),
                      pl.BlockSpec((1, 1, F2), wsel),
                      pl.BlockSpec((1, dff, D), wsel),
                      pl.BlockSpec((1, 1, D), wsel)],
            out_specs=pl.BlockSpec(memory_space=pl.ANY),
            scratch_shapes=[pltpu.VMEM((3, EXPERT_ROWS, sub, LANES), F32),
                            pltpu.VMEM((2, EXPERT_ROWS, sub, LANES), F32),
                            pltpu.VMEM((EXPERT_ROWS, D), BF16), pltpu.VMEM((EXPERT_ROWS, D), F32),
                            pltpu.VMEM((D, F2), BF16), pltpu.VMEM((dff, D), BF16),
                            pltpu.SemaphoreType.DMA((3,)), pltpu.SemaphoreType.DMA((2,)),
                            pltpu.SemaphoreType.DMA],
        ),
        out_shape=jax.ShapeDtypeStruct((TOP_K * n + 2 * EXPERT_ROWS, sub, LANES), F32),
        compiler_params=_params("arbitrary"),
        name="moe",
    )(block_exp, row_map, h2_tiles, w_up, b_up.reshape(E, 1, F2), w_down, b_down.reshape(E, 1, D))


def _moe_sum_body(y0_ref, y1_ref, y2_ref, y3_ref, x1_ref, gate_ref, mod_ref, o_ref):
    gates = gate_ref[...]
    t = o_ref.shape[0]
    y = gates[:, 0:1] * y0_ref[...].reshape(t, -1)
    for k, y_ref in enumerate((y1_ref, y2_ref, y3_ref), start=1):
        y = y + gates[:, k:k + 1] * y_ref[...].reshape(t, -1)
    o_ref[...] = x1_ref[...] + mod_ref[0, 5:6, :] * y


def _moe_sum(y_tok, x1, gates, mod3, rows_per_batch):
    n, D = x1.shape
    sub = y_tok.shape[1]
    t = 256
    tpb = rows_per_batch // t
    nt = n // t
    slab = lambda k: pl.BlockSpec((t, sub, LANES), lambda i: (k * nt + i, 0, 0))
    return pl.pallas_call(
        _moe_sum_body,
        grid=(nt,),
        in_specs=[slab(0), slab(1), slab(2), slab(3),
                  pl.BlockSpec((t, D), lambda i: (i, 0)),
                  pl.BlockSpec((t, LANES), lambda i: (i, 0)),
                  pl.BlockSpec((1, 6, D), lambda i: (i // tpb, 0, 0))],
        out_specs=pl.BlockSpec((t, D), lambda i: (i, 0)),
        out_shape=jax.ShapeDtypeStruct((n, D), F32),
        compiler_params=_params("parallel"),
        name="moe_sum",
    )(y_tok, y_tok, y_tok, y_tok, x1, gates, mod3)


def kernel(x, c, positions, ada_w, ada_b, norm1_g, norm2_g, w_in, conv_w, conv_b, w_rg_a, b_rg_a, w_rg_x, b_rg_x, lru_lambda, q_norm_g, k_norm_g, b_gate, w_branch, w_out, w_router, b_router, w_up, b_up, w_down, b_down):
    B, S, D = x.shape
    n = B * S
    depth = ada_w.shape[0]
    x2 = x.reshape(n, D)
    cos_t, sin_t = _rope_tables(positions)
    perm = _rope_head_perm()
    head_cols = (jnp.arange(2 * N_HEADS)[:, None] * HEAD_DIM + perm[None, :]).reshape(-1)
    for l in range(depth):
        mod3 = _adaln(c, ada_w[l], ada_b[l]).reshape(B, 6, D)
        h1 = _norm_mod_call(x2, mod3, norm1_g[l], S)
        w = w_in[l].astype(BF16)
        xr = _proj("plain", h1, w[:, 0:D], [], F32, "proj_xr")
        gg = _proj("gelu", h1, w[:, D:2 * D], [], F32, "proj_gr")
        qk_g = jnp.concatenate([jnp.tile(q_norm_g[l][perm], N_HEADS),
                                jnp.tile(k_norm_g[l][perm], N_HEADS)]).reshape(1, 2 * D)
        qk = _proj("qk", h1, w[:, 2 * D:4 * D][:, head_cols], [qk_g, cos_t, sin_t], F32, "proj_qk")
        v = _proj("plain", h1, w[:, 4 * D:5 * D], [], BF16, "proj_v")
        gl = _proj("gate", h1, w[:, 5 * D:7 * D], [b_gate[l].reshape(1, 2 * D)], F32, "proj_gl")

        y_rnn = _rglru(xr, gg, conv_w[l], conv_b[l], w_rg_a[l], b_rg_a[l], w_rg_x[l], b_rg_x[l],
                       lru_lambda[l], B, S)
        y_att = _moba(qk, v, B, S)

        x1, h2, topi, gates, rank, cnt = _merge(y_rnn, y_att, gl, x2, mod3, w_branch[l], w_out[l],
                                                norm2_g[l], w_router[l], b_router[l], S)

        counts = cnt[0, :N_EXPERTS]
        padded = (counts + EXPERT_ROWS - 1) // EXPERT_ROWS * EXPERT_ROWS
        pad_end = jnp.cumsum(padded)
        pad_start = pad_end - padded
        top_idx = topi[:, :TOP_K]
        dest = (pad_start[top_idx] + rank[:, :TOP_K]).reshape(n * TOP_K).astype(jnp.int32)
        n_blocks = (n * TOP_K) // EXPERT_ROWS + N_EXPERTS
        block_start = jnp.arange(n_blocks, dtype=jnp.int32) * EXPERT_ROWS
        block_exp = jnp.minimum(jnp.sum(block_start[:, None] >= pad_end[None, :], axis=1),
                                N_EXPERTS - 1).astype(jnp.int32)
        assert n <= 1 << MAP_SRC_BITS
        n_used = (pad_end[-1] // EXPERT_ROWS).astype(jnp.int32)
        last_blk = jnp.where(padded > 0, pad_end // EXPERT_ROWS - 1, -1)
        spare = n_used + jnp.arange(N_EXPERTS, dtype=jnp.int32)
        spare = jnp.where(spare < n_blocks, spare, -1)
        pad_blocks = jnp.concatenate([last_blk, spare]).astype(jnp.int32)
        choice = _row_choice(dest, pad_blocks, n_blocks * EXPERT_ROWS)
        choice = jnp.concatenate([jnp.full((EXPERT_ROWS,), -1, jnp.int32), choice,
                                  jnp.full((2 * EXPERT_ROWS,), -1, jnp.int32)])
        rix = jnp.arange(choice.shape[0], dtype=jnp.int32)
        spare_row = n * TOP_K + (rix // EXPERT_ROWS) % 2 * EXPERT_ROWS + rix % EXPERT_ROWS
        tok, slot = choice // TOP_K, choice % TOP_K
        src_row = jnp.where(choice >= 0, tok, 0)
        dst_row = jnp.where(choice >= 0, slot * n + tok, spare_row)
        row_map = (src_row | (dst_row << MAP_SRC_BITS)).astype(jnp.int32)

        y_tok = _moe(h2, block_exp, row_map, w_up[l], b_up[l], w_down[l], b_down[l], n_blocks)
        x2 = _moe_sum(y_tok, x1, gates, mod3, S)
    return x2.reshape(B, S, D)
```

```python
import functools

import jax
import jax.numpy as jnp
from jax import lax
from jax.experimental import pallas as pl
from jax.experimental.pallas import tpu as pltpu

F32 = jnp.float32
BF16 = jnp.bfloat16
HIGHEST = lax.Precision.HIGHEST

EPS = 1e-6
LANES = 128
N_HEADS = 8
HEAD_DIM = 128
RNN_BLOCKS = 8
RNN_BW = 128
CONV_W = 4
LRU_C = 8.0
MOBA_BLOCK = 256
MOBA_TOPK = 3
MOBA_CHUNK = 4
MOBA_HEADS = 4
MERGE_SUB = 256
LOG2_E = 1.4426950408889634
ROPE_DIMS = HEAD_DIM // 4
ROPE_THETA = 500000.0
N_EXPERTS = 32
TOP_K = 4
SWIGLU_LIMIT = 7.0
SWIGLU_ALPHA = 1.702
EXPERT_ROWS = 256
NEG_BIG = -1e30

VMEM_LIMIT = 56 * 1024 * 1024


def _params(*sem):
    return pltpu.CompilerParams(dimension_semantics=sem, vmem_limit_bytes=VMEM_LIMIT)


def _adaln_body(c_ref, w_ref, b_ref, o_ref):
    cs = c_ref[...]
    cs = cs * jax.nn.sigmoid(cs)
    o_ref[...] = jnp.dot(cs, w_ref[...], preferred_element_type=F32, precision=HIGHEST) + b_ref[...]


def _adaln(c, ada_w, ada_b):
    B, D = c.shape
    W = ada_w.shape[1]
    cpad = jnp.zeros((8, D), F32).at[:B].set(c)
    tn = 1024
    mod = pl.pallas_call(
        _adaln_body,
        grid=(W // tn,),
        in_specs=[pl.BlockSpec((8, D), lambda j: (0, 0)),
                  pl.BlockSpec((D, tn), lambda j: (0, j)),
                  pl.BlockSpec((1, tn), lambda j: (0, j))],
        out_specs=pl.BlockSpec((8, tn), lambda j: (0, j)),
        out_shape=jax.ShapeDtypeStruct((8, W), F32),
        compiler_params=_params("parallel"),
        name="adaln",
    )(cpad, ada_w, ada_b.reshape(1, W))
    return mod[:B]


def _rope_head_perm():
    half = ROPE_DIMS // 2
    mid = HEAD_DIM // 2
    return jnp.concatenate([jnp.arange(0, half), jnp.arange(ROPE_DIMS, mid + half),
                            jnp.arange(half, ROPE_DIMS), jnp.arange(mid + half, HEAD_DIM)])


def _rope_body(pos_ref, freq_ref, c_ref, s_ref):
    ang = pos_ref[...].astype(F32) * freq_ref[...]
    lane = lax.broadcasted_iota(jnp.int32, ang.shape, 1)
    half = ROPE_DIMS // 2
    mid = HEAD_DIM // 2
    s = jnp.sin(ang)
    c_ref[...] = jnp.cos(ang)
    s_ref[...] = jnp.where(lane < half, -s, jnp.where((lane >= mid) & (lane < mid + half), s, 0.0))


def _rope_tables(positions):
    n = positions.size
    half = ROPE_DIMS // 2
    mid = HEAD_DIM // 2
    freqs = ROPE_THETA ** (-jnp.arange(half, dtype=F32) / half)
    freq_lane = jnp.zeros((1, LANES), F32).at[0, :half].set(freqs).at[0, mid:mid + half].set(freqs)
    tm = 1024
    tab = jax.ShapeDtypeStruct((n, LANES), F32)
    return pl.pallas_call(
        _rope_body,
        grid=(n // tm,),
        in_specs=[pl.BlockSpec((tm, 1), lambda i: (i, 0)),
                  pl.BlockSpec((1, LANES), lambda i: (0, 0))],
        out_specs=[pl.BlockSpec((tm, LANES), lambda i: (i, 0))] * 2,
        out_shape=[tab, tab],
        compiler_params=_params("parallel"),
        name="rope_tab",
    )(positions.reshape(n, 1), freq_lane)


def _norm_mod(x, g, sh, sc):
    ms = jnp.mean(x * x, axis=-1, keepdims=True)
    y = x * lax.rsqrt(ms + EPS)
    return (y * g) * (1.0 + sc) + sh


def _norm_mod_body(x_ref, mod_ref, g_ref, o_ref):
    o_ref[...] = _norm_mod(x_ref[...], g_ref[...], mod_ref[0, 0:1, :], mod_ref[0, 1:2, :]).astype(o_ref.dtype)


def _norm_mod_call(x2, mod3, g, rows_per_batch):
    n, D = x2.shape
    tm = 512
    tpb = rows_per_batch // tm
    return pl.pallas_call(
        _norm_mod_body,
        grid=(n // tm,),
        in_specs=[pl.BlockSpec((tm, D), lambda i: (i, 0)),
                  pl.BlockSpec((1, 6, D), lambda i: (i // tpb, 0, 0)),
                  pl.BlockSpec((1, D), lambda i: (0, 0))],
        out_specs=pl.BlockSpec((tm, D), lambda i: (i, 0)),
        out_shape=jax.ShapeDtypeStruct((n, D), BF16),
        compiler_params=_params("parallel"),
        name="norm_mod",
    )(x2, mod3, g.reshape(1, D))


def _gelu_tanh(x):
    return 0.5 * x * (1.0 + jnp.tanh(0.7978845608028654 * (x + 0.044715 * (x * x * x))))


PROJ_SUB = 256
PROJ_PIECE = 256


def _qk_head(seg, g, cos, sin):
    ms = jnp.mean(seg * seg, axis=-1, keepdims=True)
    y = seg * lax.rsqrt(ms + EPS) * g
    return y * cos + pltpu.roll(y, HEAD_DIM // 2, axis=1) * sin


def _proj_stages(kind, rows, h_ref, w_ref, extras, o_ref):
    n_pieces = w_ref.shape[1] // PROJ_PIECE
    pieces = []
    for j in range(n_pieces):
        pieces.append(jnp.dot(h_ref[rows, :], w_ref[:, j * PROJ_PIECE:(j + 1) * PROJ_PIECE],
                              preferred_element_type=F32))
        yield
    for j in range(n_pieces):
        cols = slice(j * PROJ_PIECE, (j + 1) * PROJ_PIECE)
        acc = pieces[j]
        if kind == "gelu":
            acc = _gelu_tanh(acc)
        elif kind == "gate":
            (b_ref,) = extras
            acc = jax.nn.sigmoid(acc + b_ref[:, cols])
        elif kind == "qk":
            g_ref, c_ref, s_ref = extras
            cos, sin = c_ref[rows, :], s_ref[rows, :]
            acc = jnp.concatenate(
                [_qk_head(acc[:, o:o + HEAD_DIM], g_ref[:, cols.start + o:cols.start + o + HEAD_DIM], cos, sin)
                 for o in range(0, PROJ_PIECE, HEAD_DIM)], axis=1)
        o_ref[rows, cols] = acc.astype(o_ref.dtype)
        yield


def _proj_body(kind, h_ref, w_ref, *rest):
    *extras, o_ref = rest
    first, second = [_proj_stages(kind, slice(t * PROJ_SUB, (t + 1) * PROJ_SUB), h_ref, w_ref, extras, o_ref)
                     for t in range(2)]
    n_pieces = w_ref.shape[1] // PROJ_PIECE
    for _ in range(n_pieces):
        next(first)
    for _ in range(n_pieces):
        next(second)
        next(first)
    for stage in second:
        pass


def _proj(kind, h, w, extras, out_dtype, name):
    n, D = h.shape
    W = w.shape[1]
    tm, tn = 512, 1024
    in_specs = [pl.BlockSpec((tm, D), lambda j, i: (i, 0)),
                pl.BlockSpec((D, tn), lambda j, i: (0, j))]
    args = [h, w]
    for arr in extras:
        if arr.shape[0] == 1:
            in_specs.append(pl.BlockSpec((1, tn), lambda j, i: (0, j)))
        else:
            in_specs.append(pl.BlockSpec((tm, LANES), lambda j, i: (i, 0)))
        args.append(arr)
    return pl.pallas_call(
        functools.partial(_proj_body, kind),
        grid=(W // tn, n // tm),
        in_specs=in_specs,
        out_specs=pl.BlockSpec((tm, tn), lambda j, i: (i, j)),
        out_shape=jax.ShapeDtypeStruct((n, W), out_dtype),
        compiler_params=_params("parallel", "parallel"),
        name=name,
    )(*args)


def _rglru_body(xr_ref, gg_ref, cw_ref, cb_ref, wa_ref, ba_ref, wx_ref, bx_ref, lam_ref,
                o_ref, xbuf, hcar, a_s, u_s):
    s = pl.program_id(1)
    ts, D = xr_ref.shape

    @pl.when(s == 0)
    def _():
        xbuf[0:8, :] = jnp.zeros((8, D), F32)
        hcar[...] = jnp.zeros_like(hcar)

    @pl.when(s > 0)
    def _():
        xbuf[0:8, :] = xbuf[ts:ts + 8, :]

    xbuf[8:8 + ts, :] = xr_ref[...]
    xc = cb_ref[...] + cw_ref[0:1, :] * xbuf[8:8 + ts, :]
    for i in range(1, CONV_W):
        xc = xc + cw_ref[i:i + 1, :] * xbuf[8 - i:8 - i + ts, :]

    ra, rx = [], []
    for n in range(RNN_BLOCKS):
        xb = xc[:, n * RNN_BW:(n + 1) * RNN_BW].astype(BF16)
        ra.append(jnp.dot(xb, wa_ref[n], preferred_element_type=F32))
        rx.append(jnp.dot(xb, wx_ref[n], preferred_element_type=F32))
    r = jax.nn.sigmoid(jnp.concatenate(ra, axis=1) + ba_ref[...])
    ig = jax.nn.sigmoid(jnp.concatenate(rx, axis=1) + bx_ref[...])

    z = -lam_ref[...]
    softplus = jnp.maximum(z, 0.0) + jnp.log1p(jnp.exp(-jnp.abs(z)))
    log_a = -LRU_C * r * softplus
    a = jnp.exp(log_a)
    mult = jnp.sqrt(1.0 - a * a)
    row = lax.broadcasted_iota(jnp.int32, (ts, D), 0)
    mult = jnp.where((row == 0) & (s == 0), 1.0, mult)
    u = mult * (ig * xc)

    rm = row & 7
    for d in (1, 2, 4):
        keep = rm >= d
        a_sh = pltpu.roll(a, d, axis=0)
        u_sh = pltpu.roll(u, d, axis=0)
        u = jnp.where(keep, a * u_sh + u, u)
        a = jnp.where(keep, a * a_sh, a)
    a_s[...] = a
    u_s[...] = u

    def group(g, h):
        r0 = pl.multiple_of(g * 8, 8)
        hg = u_s[pl.ds(r0, 8), :] + a_s[pl.ds(r0, 8), :] * h
        u_s[pl.ds(r0, 8), :] = hg
        return hg[7:8, :]

    hcar[...] = lax.fori_loop(0, ts // 8, group, hcar[...])
    o_ref[...] = (u_s[...] * gg_ref[...]).astype(o_ref.dtype)


def _rglru(xr, gg, conv_w, conv_b, w_a, b_a, w_x, b_x, lam, B, S):
    n, D = xr.shape
    ts = 256
    spb = S // ts
    row = lambda b, s: (b * spb + s, 0)
    vec = lambda b, s: (0, 0)
    return pl.pallas_call(
        _rglru_body,
        grid=(B, spb),
        in_specs=[pl.BlockSpec((ts, D), row),
                  pl.BlockSpec((ts, D), row),
                  pl.BlockSpec((CONV_W, D), vec),
                  pl.BlockSpec((1, D), vec),
                  pl.BlockSpec((RNN_BLOCKS, RNN_BW, RNN_BW), lambda b, s: (0, 0, 0)),
                  pl.BlockSpec((1, D), vec),
                  pl.BlockSpec((RNN_BLOCKS, RNN_BW, RNN_BW), lambda b, s: (0, 0, 0)),
                  pl.BlockSpec((1, D), vec),
                  pl.BlockSpec((1, D), vec)],
        out_specs=pl.BlockSpec((ts, D), row),
        out_shape=jax.ShapeDtypeStruct((n, D), BF16),
        scratch_shapes=[pltpu.VMEM((ts + 8, D), F32), pltpu.VMEM((1, D), F32),
                        pltpu.VMEM((ts, D), F32), pltpu.VMEM((ts, D), F32)],
        compiler_params=_params("arbitrary", "arbitrary"),
        name="rglru",
    )(xr, gg, conv_w, conv_b.reshape(1, D), w_a.astype(BF16), b_a.reshape(1, D),
      w_x.astype(BF16), b_x.reshape(1, D), lam.reshape(1, D))


def _moba_body(q_ref, k_ref, v_ref, o_ref, kaug, vaug, kmean, m_s, acc_s):
    qb = pl.program_id(2)
    S = k_ref.shape[0]
    nb = S // MOBA_BLOCK
    bs = MOBA_BLOCK
    dh = HEAD_DIM
    heads = range(MOBA_HEADS)
    hcol = lambda hh: slice(hh * dh, (hh + 1) * dh)

    @pl.when(qb == 0)
    def _():
        blk = lax.broadcasted_iota(jnp.int32, (S, dh), 0) // bs
        col = lax.broadcasted_iota(jnp.int32, (S, dh), 1)
        onehot = jnp.where(col == blk, 1.0, 0.0).astype(BF16)
        for hh in heads:
            kaug[hh, :, 0:dh] = k_ref[:, hcol(hh)].astype(BF16)
            kaug[hh, :, dh:2 * dh] = onehot
            vaug[hh, :, 0:dh] = v_ref[:, hcol(hh)]
            vaug[hh, :, dh:2 * dh] = jnp.ones((S, dh), BF16)
            kmean[hh] = jnp.zeros((LANES, dh), F32)
            for j in range(nb):
                kmean[hh, j:j + 1, :] = jnp.mean(k_ref[j * bs:(j + 1) * bs, hcol(hh)], axis=0, keepdims=True)

    scale = HEAD_DIM ** -0.5
    nt = (((1,), (1,)), ((), ()))
    c2 = scale * LOG2_E
    r0 = pl.multiple_of(qb * bs, bs)
    lane = lax.broadcasted_iota(jnp.int32, (bs, LANES), 1)
    lane_f = lane.astype(F32)
    past = lane < qb
    rr = lax.broadcasted_iota(jnp.int32, (bs, bs), 0)
    cc = lax.broadcasted_iota(jnp.int32, (bs, bs), 1)

    qaugs = []
    for hh in heads:
        q = q_ref[:, hcol(hh)]
        qbf = q.astype(BF16)

        gate = lax.dot_general(q, kmean[hh], nt, preferred_element_type=F32, precision=HIGHEST)
        g = jnp.where(past, gate, -jnp.inf)
        sel = jnp.zeros(gate.shape, F32)
        for _ in range(MOBA_TOPK):
            m = jnp.max(g, axis=1, keepdims=True)
            idx = jnp.min(jnp.where(g == m, lane_f, float(LANES)), axis=1, keepdims=True)
            pick = (lane_f == idx) & past
            sel = jnp.where(pick, 1.0, sel)
            g = jnp.where(pick, -jnp.inf, g)
        bias = jnp.where(sel > 0.5, 0.0, NEG_BIG).astype(BF16)
        qaugs.append(jnp.concatenate([qbf, bias], axis=1))

        s = lax.dot_general(qbf, kaug[hh, pl.ds(r0, bs), 0:dh], nt, preferred_element_type=F32)
        s = jnp.where(cc <= rr, s, NEG_BIG)
        m0 = jnp.max(s, axis=1, keepdims=True)
        p = jnp.exp2((s - m0) * c2)
        m_s[hh] = m0
        acc_s[hh] = jnp.dot(p.astype(BF16), vaug[hh, pl.ds(r0, bs), :], preferred_element_type=F32)

    cw = MOBA_CHUNK * bs
    for c in range(nb // MOBA_CHUNK):
        @pl.when(c * MOBA_CHUNK < qb)
        def _(c=c):
            for hh in heads:
                sc = lax.dot_general(qaugs[hh], kaug[hh, c * cw:(c + 1) * cw, :], nt,
                                     preferred_element_type=F32)
                m_old = m_s[hh]
                m_new = jnp.maximum(m_old, jnp.max(sc, axis=1, keepdims=True))
                alpha = jnp.exp2((m_old - m_new) * c2)
                pc = jnp.exp2((sc - m_new) * c2)
                m_s[hh] = m_new
                acc_s[hh] = alpha * acc_s[hh] + jnp.dot(pc.astype(BF16), vaug[hh, c * cw:(c + 1) * cw, :],
                                                        preferred_element_type=F32)

    for hh in heads:
        o_ref[:, hcol(hh)] = (acc_s[hh, :, 0:dh] / acc_s[hh, :, dh:dh + 1]).astype(o_ref.dtype)


def _moba(qk, v, B, S):
    n, D = v.shape
    nq = S // MOBA_BLOCK
    hs = MOBA_HEADS
    w = hs * HEAD_DIM
    return pl.pallas_call(
        _moba_body,
        grid=(B, N_HEADS // hs, nq),
        in_specs=[pl.BlockSpec((MOBA_BLOCK, w), lambda b, h, i: (b * nq + i, h)),
                  pl.BlockSpec((S, w), lambda b, h, i: (b, N_HEADS // hs + h)),
                  pl.BlockSpec((S, w), lambda b, h, i: (b, h))],
        out_specs=pl.BlockSpec((MOBA_BLOCK, w), lambda b, h, i: (b * nq + i, h)),
        out_shape=jax.ShapeDtypeStruct((n, D), BF16),
        scratch_shapes=[pltpu.VMEM((hs, S, 2 * HEAD_DIM), BF16), pltpu.VMEM((hs, S, 2 * HEAD_DIM), BF16),
                        pltpu.VMEM((hs, LANES, HEAD_DIM), F32), pltpu.VMEM((hs, MOBA_BLOCK, 1), F32),
                        pltpu.VMEM((hs, MOBA_BLOCK, 2 * HEAD_DIM), F32)],
        compiler_params=_params("parallel", "parallel", "arbitrary"),
        name="moba",
    )(qk, qk, v)


def _merge_body(yr_ref, ya_ref, gr_ref, ga_ref, x_ref, mod_ref, wb_ref, wo_ref, g2_ref, wr_ref, br_ref,
                x1_ref, h2_ref, topi_ref, gate_ref, rank_ref, cnt_ref, carry):
    i = pl.program_id(0)

    @pl.when(i == 0)
    def _():
        carry[...] = jnp.zeros_like(carry)

    tm = MERGE_SUB
    counts = {"total": carry[...]}
    first, second = [
        _merge_stages(slice(h * tm, (h + 1) * tm), counts, yr_ref, ya_ref, gr_ref, ga_ref, x_ref, mod_ref,
                      wb_ref, wo_ref, g2_ref, wr_ref, br_ref, x1_ref, h2_ref, topi_ref, gate_ref, rank_ref)
        for h in range(2)]
    for _ in range(MERGE_MATMUL_STAGES):
        next(first)
    for stage in first:
        next(second)
    for stage in second:
        pass
    carry[...] = counts["total"]
    cnt_ref[...] = jnp.broadcast_to(counts["total"], cnt_ref.shape).astype(jnp.int32)


MERGE_MATMUL_STAGES = 4


def _merge_stages(rows, counts, yr_ref, ya_ref, gr_ref, ga_ref, x_ref, mod_ref, wb_ref, wo_ref, g2_ref, wr_ref,
                  br_ref, x1_ref, h2_ref, topi_ref, gate_ref, rank_ref):
    tm = rows.stop - rows.start
    zr = jnp.dot(yr_ref[rows, :], wb_ref[0], preferred_element_type=F32)
    yield
    za = jnp.dot(ya_ref[rows, :], wb_ref[1], preferred_element_type=F32)
    mix = (gr_ref[rows, :] * zr + ga_ref[rows, :] * za).astype(BF16)
    yield
    mixed = jnp.dot(mix, wo_ref[...], preferred_element_type=F32)
    x1 = x_ref[rows, :] + mod_ref[0, 2:3, :] * mixed
    x1_ref[rows, :] = x1
    h2 = _norm_mod(x1, g2_ref[...], mod_ref[0, 3:4, :], mod_ref[0, 4:5, :])
    h2_ref[rows] = h2.reshape(tm, -1, LANES)
    yield

    hi = h2.astype(BF16)
    lo = (h2 - hi.astype(F32)).astype(BF16)
    both = jnp.dot(hi, wr_ref[...], preferred_element_type=F32)
    logits = (both[:, :LANES] + both[:, LANES:]
              + jnp.dot(lo, wr_ref[:, :LANES], preferred_element_type=F32) + br_ref[...])
    yield

    lane = lax.broadcasted_iota(jnp.int32, logits.shape, 1)
    lane_f = lane.astype(F32)
    lg = jnp.where(lane < N_EXPERTS, logits, -jnp.inf)
    vals, idxs = [], []
    onehot = jnp.zeros(logits.shape, F32)
    for _ in range(TOP_K):
        m = jnp.max(lg, axis=1, keepdims=True)
        idx = jnp.min(jnp.where(lg == m, lane_f, float(LANES)), axis=1, keepdims=True)
        pick = lane_f == idx
        onehot = jnp.where(pick, 1.0, onehot)
        lg = jnp.where(pick, -jnp.inf, lg)
        vals.append(m)
        idxs.append(idx)
        yield
    es = [jnp.exp(v - vals[0]) for v in vals]
    den = es[0] + es[1] + es[2] + es[3]
    before0 = counts["total"]

    rr = lax.broadcasted_iota(jnp.int32, (tm, tm), 0)
    cc = lax.broadcasted_iota(jnp.int32, (tm, tm), 1)
    tri = jnp.where(cc < rr, 1.0, 0.0).astype(BF16)
    before = jnp.dot(tri, onehot.astype(BF16), preferred_element_type=F32) + before0

    topi = jnp.zeros(logits.shape, F32)
    gates = jnp.zeros(logits.shape, F32)
    rank = jnp.zeros(logits.shape, F32)
    for k in range(TOP_K):
        rk = jnp.sum(jnp.where(lane_f == idxs[k], before, 0.0), axis=1, keepdims=True)
        topi = jnp.where(lane == k, idxs[k], topi)
        gates = jnp.where(lane == k, es[k] / den, gates)
        rank = jnp.where(lane == k, rk, rank)
    topi_ref[rows, :] = topi.astype(jnp.int32)
    gate_ref[rows, :] = gates
    rank_ref[rows, :] = rank.astype(jnp.int32)
    counts["total"] = before0 + jnp.sum(onehot, axis=0, keepdims=True)
    yield


def _merge(y_rnn, y_att, gl, x2, mod3, w_branch, w_out, norm2_g, w_router, b_router, rows_per_batch):
    n, D = x2.shape
    tm = 2 * MERGE_SUB
    tpb = rows_per_batch // tm
    wr32 = jnp.zeros((D, LANES), F32).at[:, :N_EXPERTS].set(w_router)
    wr_hi = wr32.astype(BF16)
    wr = jnp.concatenate([wr_hi, (wr32 - wr_hi.astype(F32)).astype(BF16)], axis=1)
    br = jnp.zeros((1, LANES), F32).at[0, :N_EXPERTS].set(b_router)
    row = lambda i: (i, 0)
    fixed = lambda i: (0, 0)
    once = pl.Buffered(1)
    wide = jax.ShapeDtypeStruct((n, D), F32)
    narrow_i = jax.ShapeDtypeStruct((n, LANES), jnp.int32)
    return pl.pallas_call(
        _merge_body,
        grid=(n // tm,),
        in_specs=[pl.BlockSpec((tm, D), row),
                  pl.BlockSpec((tm, D), row),
                  pl.BlockSpec((tm, D), lambda i: (i, 0)),
                  pl.BlockSpec((tm, D), lambda i: (i, 1)),
                  pl.BlockSpec((tm, D), row),
                  pl.BlockSpec((1, 6, D), lambda i: (i // tpb, 0, 0)),
                  pl.BlockSpec((2, D, D), lambda i: (0, 0, 0), pipeline_mode=once),
                  pl.BlockSpec((D, D), fixed, pipeline_mode=once),
                  pl.BlockSpec((1, D), fixed),
                  pl.BlockSpec((D, 2 * LANES), fixed, pipeline_mode=once),
                  pl.BlockSpec((1, LANES), fixed)],
        out_specs=[pl.BlockSpec((tm, D), row),
                   pl.BlockSpec((tm, D // LANES, LANES), lambda i: (i, 0, 0)),
                   pl.BlockSpec((tm, LANES), row),
                   pl.BlockSpec((tm, LANES), row),
                   pl.BlockSpec((tm, LANES), row),
                   pl.BlockSpec((8, LANES), fixed)],
        out_shape=[wide, jax.ShapeDtypeStruct((n, D // LANES, LANES), F32), narrow_i,
                   jax.ShapeDtypeStruct((n, LANES), F32), narrow_i,
                   jax.ShapeDtypeStruct((8, LANES), jnp.int32)],
        scratch_shapes=[pltpu.VMEM((1, LANES), F32)],
        compiler_params=_params("arbitrary"),
        name="merge",
    )(y_rnn, y_att, gl, gl, x2, mod3, w_branch.astype(BF16), w_out.astype(BF16),
      norm2_g.reshape(1, D), wr, br)


DISPATCH_TOKENS = 256
ROW_DMA_UNROLL = 8


def _dispatch_body(dest_ref, zblk_ref, src_ref, dst_hbm, zbuf, sem, zsem):
    t = DISPATCH_TOKENS
    c = pl.program_id(0)

    @pl.when(c == 0)
    def _():
        zbuf[...] = jnp.zeros_like(zbuf)

        def zero_copy(b):
            r0 = pl.multiple_of(b * EXPERT_ROWS, EXPERT_ROWS)
            return pltpu.make_async_copy(zbuf, dst_hbm.at[pl.ds(r0, EXPERT_ROWS), :], zsem)

        def zissue(j, carry):
            @pl.when(zblk_ref[j] >= 0)
            def _():
                zero_copy(zblk_ref[j]).start()
            return carry

        def zwait(j, carry):
            @pl.when(zblk_ref[j] >= 0)
            def _():
                zero_copy(zblk_ref[j]).wait()
            return carry

        lax.fori_loop(0, zblk_ref.shape[0], zissue, 0)
        lax.fori_loop(0, zblk_ref.shape[0], zwait, 0)

    def issue(g, carry):
        for u in range(ROW_DMA_UNROLL):
            r = g * ROW_DMA_UNROLL + u
            for k in range(TOP_K):
                d = dest_ref[(c * t + r) * TOP_K + k]
                pltpu.make_async_copy(src_ref.at[pl.ds(r, 1), :], dst_hbm.at[pl.ds(d, 1), :], sem).start()
        return carry

    lax.fori_loop(0, t // ROW_DMA_UNROLL, issue, 0)
    for k in range(TOP_K):
        pltpu.make_async_copy(src_ref, dst_hbm.at[pl.ds(0, t), :], sem).wait()


def _dispatch(h2, dest, zero_blocks, n_rows):
    n, D = h2.shape
    return pl.pallas_call(
        _dispatch_body,
        grid_spec=pltpu.PrefetchScalarGridSpec(
            num_scalar_prefetch=2,
            grid=(n // DISPATCH_TOKENS,),
            in_specs=[pl.BlockSpec((DISPATCH_TOKENS, D), lambda i, d, z: (i, 0))],
            out_specs=pl.BlockSpec(memory_space=pl.ANY),
            scratch_shapes=[pltpu.VMEM((EXPERT_ROWS, D), F32), pltpu.SemaphoreType.DMA,
                            pltpu.SemaphoreType.DMA],
        ),
        out_shape=jax.ShapeDtypeStruct((n_rows, D), F32),
        compiler_params=_params("arbitrary"),
        name="dispatch",
    )(dest, zero_blocks, h2)


def _experts_body(bexp_ref, nused_ref, x_ref, wu_ref, bu_ref, wd_ref, bd_ref, o_ref, wu_bf, wd_bf):
    i = pl.program_id(0)
    dff = wd_ref.shape[1]
    used = i < nused_ref[0]

    new_expert = (i == 0) | (bexp_ref[i] != bexp_ref[jnp.maximum(i - 1, 0)])

    @pl.when(used & new_expert)
    def _():
        wu_bf[...] = wu_ref[0].astype(BF16)
        wd_bf[...] = wd_ref[0].astype(BF16)

    @pl.when(used)
    def _():
        hc = jnp.dot(x_ref[...].astype(BF16), wu_bf[...], preferred_element_type=F32) + bu_ref[0]
        g = jnp.minimum(hc[:, :dff], SWIGLU_LIMIT)
        lin = jnp.clip(hc[:, dff:], -SWIGLU_LIMIT, SWIGLU_LIMIT)
        act = (lin + 1.0) * g * jax.nn.sigmoid(SWIGLU_ALPHA * g)
        o_ref[...] = jnp.dot(act.astype(BF16), wd_bf[...], preferred_element_type=F32) + bd_ref[0]

    @pl.when(i >= nused_ref[0])
    def _():
        o_ref[...] = jnp.zeros_like(o_ref)


def _experts(x_rows, block_exp, n_used, w_up, b_up, w_down, b_down, n_blocks):
    R, D = x_rows.shape
    E, _, F2 = w_up.shape
    dff = w_down.shape[1]
    blk = lambda i, be, nu: (i, 0)
    wsel = lambda i, be, nu: (be[i], 0, 0)
    return pl.pallas_call(
        _experts_body,
        grid_spec=pltpu.PrefetchScalarGridSpec(
            num_scalar_prefetch=2,
            grid=(n_blocks,),
            in_specs=[pl.BlockSpec((EXPERT_ROWS, D), blk),
                      pl.BlockSpec((1, D, F2), wsel),
                      pl.BlockSpec((1, 1, F2), wsel),
                      pl.BlockSpec((1, dff, D), wsel),
                      pl.BlockSpec((1, 1, D), wsel)],
            out_specs=pl.BlockSpec((EXPERT_ROWS, D), blk),
            scratch_shapes=[pltpu.VMEM((D, F2), BF16), pltpu.VMEM((dff, D), BF16)],
        ),
        out_shape=jax.ShapeDtypeStruct((R, D), F32),
        compiler_params=_params("arbitrary"),
        name="experts",
    )(block_exp, n_used, x_rows, w_up, b_up.reshape(E, 1, F2), w_down, b_down.reshape(E, 1, D))


COMBINE_ROWS = 128


def _combine_body(dest_ref, y_hbm, x1_ref, gate_ref, mod_ref, o_ref, buf, sem):
    i = pl.program_id(0)
    t = COMBINE_ROWS

    def issue(g, carry):
        for u in range(ROW_DMA_UNROLL):
            r = g * ROW_DMA_UNROLL + u
            for k in range(TOP_K):
                d = dest_ref[(i * t + r) * TOP_K + k]
                pltpu.make_async_copy(y_hbm.at[pl.ds(d, 1), :], buf.at[pl.ds(r, 1), pl.ds(k * D, D)], sem).start()
        return carry

    D = o_ref.shape[1]
    lax.fori_loop(0, t // ROW_DMA_UNROLL, issue, 0)
    for k in range(TOP_K):
        pltpu.make_async_copy(y_hbm.at[pl.ds(0, t), :], buf.at[:, pl.ds(k * D, D)], sem).wait()
    gates = gate_ref[...]
    y = gates[:, 0:1] * buf[:, 0:D]
    for k in range(1, TOP_K):
        y = y + gates[:, k:k + 1] * buf[:, k * D:(k + 1) * D]
    o_ref[...] = x1_ref[...] + mod_ref[0, 5:6, :] * y


def _combine(y_rows, dest, x1, gates, mod3, rows_per_batch):
    n, D = x1.shape
    t = COMBINE_ROWS
    tpb = rows_per_batch // t
    return pl.pallas_call(
        _combine_body,
        grid_spec=pltpu.PrefetchScalarGridSpec(
            num_scalar_prefetch=1,
            grid=(n // t,),
            in_specs=[pl.BlockSpec(memory_space=pl.ANY),
                      pl.BlockSpec((t, D), lambda i, d: (i, 0)),
                      pl.BlockSpec((t, LANES), lambda i, d: (i, 0)),
                      pl.BlockSpec((1, 6, D), lambda i, d: (i // tpb, 0, 0))],
            out_specs=pl.BlockSpec((t, D), lambda i, d: (i, 0)),
            scratch_shapes=[pltpu.VMEM((t, TOP_K * D), F32), pltpu.SemaphoreType.DMA],
        ),
        out_shape=jax.ShapeDtypeStruct((n, D), F32),
        compiler_params=_params("arbitrary"),
        name="combine",
    )(dest, y_rows, x1, gates, mod3)


MAP_SRC_BITS = 14
MOE_PIECES = 4
INV_STEPS = 16
INV_UNROLL = 16


def _row_choice_body(dest_ref, pad_blk_ref, inv_ref):
    s = pl.program_id(0)
    blocks_per_step = pad_blk_ref.shape[0] // INV_STEPS
    n_place = dest_ref.shape[0] // INV_STEPS

    @pl.when(s < INV_STEPS)
    def _():
        for j in range(blocks_per_step):
            blk = pad_blk_ref[s * blocks_per_step + j]

            @pl.when(blk >= 0)
            def _():
                def fill(g, carry):
                    for u in range(INV_UNROLL):
                        inv_ref[blk * EXPERT_ROWS + g * INV_UNROLL + u] = -1
                    return carry
                lax.fori_loop(0, EXPERT_ROWS // INV_UNROLL, fill, 0)

    @pl.when(s >= INV_STEPS)
    def _():
        def place(g, carry):
            for u in range(INV_UNROLL):
                a = (s - INV_STEPS) * n_place + g * INV_UNROLL + u
                inv_ref[dest_ref[a]] = a
            return carry
        lax.fori_loop(0, n_place // INV_UNROLL, place, 0)


def _row_choice(dest, pad_blocks, n_rows):
    assert pad_blocks.shape[0] % INV_STEPS == 0 and dest.shape[0] % (INV_STEPS * INV_UNROLL) == 0
    return pl.pallas_call(
        _row_choice_body,
        grid_spec=pltpu.PrefetchScalarGridSpec(
            num_scalar_prefetch=2,
            grid=(2 * INV_STEPS,),
            in_specs=[],
            out_specs=pl.BlockSpec(memory_space=pltpu.SMEM),
        ),
        out_shape=jax.ShapeDtypeStruct((n_rows,), jnp.int32),
        compiler_params=_params("arbitrary"),
        name="row_choice",
    )(dest, pad_blocks)


def _moe_body(bexp_ref, map_ref, h2_hbm, wu_ref, bu_ref, wd_ref, bd_ref, ytok_hbm,
              xbuf, ybuf, xb, yacc, wu_bf, wd_bf, gsem, ssem, zsem):
    i = pl.program_id(0)
    last = pl.num_programs(0) - 1
    R = EXPERT_ROWS
    dff = wd_ref.shape[1]
    cur, nxt = i % 2, 1 - i % 2
    xcur, xnew = i % 3, (i + 2) % 3
    src_mask = (1 << MAP_SRC_BITS) - 1
    n_tok_rows = ytok_hbm.shape[0] - 2 * R

    def gather_row(b, r, slot):
        src = map_ref[(b + 1) * R + r] & src_mask
        pltpu.make_async_copy(h2_hbm.at[src], xbuf.at[slot, r], gsem.at[slot]).start()

    def scatter_row(b, r, slot):
        dst = map_ref[(b + 1) * R + r] >> MAP_SRC_BITS
        pltpu.make_async_copy(ybuf.at[slot, r], ytok_hbm.at[dst], ssem.at[slot]).start()

    def gather_wait(slot):
        pltpu.make_async_copy(h2_hbm.at[pl.ds(0, R)], xbuf.at[slot], gsem.at[slot]).wait()

    def scatter_wait(slot):
        pltpu.make_async_copy(ybuf.at[slot], ytok_hbm.at[pl.ds(0, R)], ssem.at[slot]).wait()

    def for_rows(fn):
        def body(g, carry):
            for u in range(ROW_DMA_UNROLL):
                fn(g * ROW_DMA_UNROLL + u)
            return carry
        lax.fori_loop(0, R // ROW_DMA_UNROLL, body, 0)

    @pl.when(i == 0)
    def _():
        ybuf[...] = jnp.zeros_like(ybuf)
        for s in range(2):
            zc = pltpu.make_async_copy(ybuf.at[s], ytok_hbm.at[pl.ds(n_tok_rows + s * R, R)], zsem)
            zc.start()
            zc.wait()
        for_rows(lambda r: gather_row(0, r, 0))
        for_rows(lambda r: gather_row(1, r, 1))

    @pl.when((i == 0) | (bexp_ref[i] != bexp_ref[jnp.maximum(i - 1, 0)]))
    def _():
        wu_bf[...] = wu_ref[0].astype(BF16)
        wd_bf[...] = wd_ref[0].astype(BF16)

    gather_wait(xcur)
    xb[...] = xbuf[xcur].reshape(R, -1).astype(BF16)

    pw = dff // MOE_PIECES
    rows_per_piece = R // MOE_PIECES
    for c in range(MOE_PIECES):
        @pl.when(bexp_ref[c] >= 0)
        def _(c=c):
            cols = slice(c * pw, (c + 1) * pw)
            cols_lin = slice(dff + c * pw, dff + (c + 1) * pw)
            for r in range(c * rows_per_piece, (c + 1) * rows_per_piece):
                gather_row(i + 2, r, xnew)
                scatter_row(i - 1, r, nxt)
            x = xb[...]
            hg = jnp.dot(x, wu_bf[:, cols], preferred_element_type=F32) + bu_ref[0, :, cols]
            hl = jnp.dot(x, wu_bf[:, cols_lin], preferred_element_type=F32) + bu_ref[0, :, cols_lin]
            g = jnp.minimum(hg, SWIGLU_LIMIT)
            lin = jnp.clip(hl, -SWIGLU_LIMIT, SWIGLU_LIMIT)
            act = ((lin + 1.0) * g * jax.nn.sigmoid(SWIGLU_ALPHA * g)).astype(BF16)
            part = jnp.dot(act, wd_bf[cols, :], preferred_element_type=F32)
            if c == 0:
                yacc[...] = part + bd_ref[0]
            else:
                yacc[...] += part

    @pl.when(i >= 1)
    def _():
        scatter_wait(cur)

    ybuf[cur] = yacc[...].reshape(ybuf.shape[1:])

    @pl.when(i == last)
    def _():
        for_rows(lambda r: scatter_row(i, r, cur))
        scatter_wait(nxt)
        scatter_wait(cur)
        gather_wait((i + 1) % 3)
        gather_wait((i + 2) % 3)


def _moe(h2_tiles, block_exp, row_map, w_up, b_up, w_down, b_down, n_blocks):
    n, sub, _ = h2_tiles.shape
    E, D, F2 = w_up.shape
    dff = w_down.shape[1]
    wsel = lambda i, be, rm: (be[i], 0, 0)
    return pl.pallas_call(
        _moe_body,
        grid_spec=pltpu.PrefetchScalarGridSpec(
            num_scalar_prefetch=2,
            grid=(n_blocks,),
            in_specs=[pl.BlockSpec(memory_space=pl.ANY),
                      pl.BlockSpec((1, D, F2), wsel),
                      pl.BlockSpec((1, 1, F2), wsel),
                      pl.BlockSpec((1, dff, D), wsel),
                      pl.BlockSpec((1, 1, D), wsel)],
            out_specs=pl.BlockSpec(memory_space=pl.ANY),
            scratch_shapes=[pltpu.VMEM((3, EXPERT_ROWS, sub, LANES), F32),
                            pltpu.VMEM((2, EXPERT_ROWS, sub, LANES), F32),
                            pltpu.VMEM((EXPERT_ROWS, D), BF16), pltpu.VMEM((EXPERT_ROWS, D), F32),
                            pltpu.VMEM((D, F2), BF16), pltpu.VMEM((dff, D), BF16),
                            pltpu.SemaphoreType.DMA((3,)), pltpu.SemaphoreType.DMA((2,)),
                            pltpu.SemaphoreType.DMA],
        ),
        out_shape=jax.ShapeDtypeStruct((TOP_K * n + 2 * EXPERT_ROWS, sub, LANES), F32),
        compiler_params=_params("arbitrary"),
        name="moe",
    )(block_exp, row_map, h2_tiles, w_up, b_up.reshape(E, 1, F2), w_down, b_down.reshape(E, 1, D))


def _moe_sum_body(y0_ref, y1_ref, y2_ref, y3_ref, x1_ref, gate_ref, mod_ref, o_ref):
    gates = gate_ref[...]
    t = o_ref.shape[0]
    y = gates[:, 0:1] * y0_ref[...].reshape(t, -1)
    for k, y_ref in enumerate((y1_ref, y2_ref, y3_ref), start=1):
        y = y + gates[:, k:k + 1] * y_ref[...].reshape(t, -1)
    o_ref[...] = x1_ref[...] + mod_ref[0, 5:6, :] * y


def _moe_sum(y_tok, x1, gates, mod3, rows_per_batch):
    n, D = x1.shape
    sub = y_tok.shape[1]
    t = 256
    tpb = rows_per_batch // t
    nt = n // t
    slab = lambda k: pl.BlockSpec((t, sub, LANES), lambda i: (k * nt + i, 0, 0))
    return pl.pallas_call(
        _moe_sum_body,
        grid=(nt,),
        in_specs=[slab(0), slab(1), slab(2), slab(3),
                  pl.BlockSpec((t, D), lambda i: (i, 0)),
                  pl.BlockSpec((t, LANES), lambda i: (i, 0)),
                  pl.BlockSpec((1, 6, D), lambda i: (i // tpb, 0, 0))],
        out_specs=pl.BlockSpec((t, D), lambda i: (i, 0)),
        out_shape=jax.ShapeDtypeStruct((n, D), F32),
        compiler_params=_params("parallel"),
        name="moe_sum",
    )(y_tok, y_tok, y_tok, y_tok, x1, gates, mod3)


def kernel(x, c, positions, ada_w, ada_b, norm1_g, norm2_g, w_in, conv_w, conv_b, w_rg_a, b_rg_a, w_rg_x, b_rg_x, lru_lambda, q_norm_g, k_norm_g, b_gate, w_branch, w_out, w_router, b_router, w_up, b_up, w_down, b_down):
    B, S, D = x.shape
    n = B * S
    depth = ada_w.shape[0]
    x2 = x.reshape(n, D)
    cos_t, sin_t = _rope_tables(positions)
    perm = _rope_head_perm()
    head_cols = (jnp.arange(2 * N_HEADS)[:, None] * HEAD_DIM + perm[None, :]).reshape(-1)
    for l in range(depth):
        mod3 = _adaln(c, ada_w[l], ada_b[l]).reshape(B, 6, D)
        h1 = _norm_mod_call(x2, mod3, norm1_g[l], S)
        w = w_in[l].astype(BF16)
        xr = _proj("plain", h1, w[:, 0:D], [], F32, "proj_xr")
        gg = _proj("gelu", h1, w[:, D:2 * D], [], F32, "proj_gr")
        qk_g = jnp.concatenate([jnp.tile(q_norm_g[l][perm], N_HEADS),
                                jnp.tile(k_norm_g[l][perm], N_HEADS)]).reshape(1, 2 * D)
        qk = _proj("qk", h1, w[:, 2 * D:4 * D][:, head_cols], [qk_g, cos_t, sin_t], F32, "proj_qk")
        v = _proj("plain", h1, w[:, 4 * D:5 * D], [], BF16, "proj_v")
        gl = _proj("gate", h1, w[:, 5 * D:7 * D], [b_gate[l].reshape(1, 2 * D)], F32, "proj_gl")

        y_rnn = _rglru(xr, gg, conv_w[l], conv_b[l], w_rg_a[l], b_rg_a[l], w_rg_x[l], b_rg_x[l],
                       lru_lambda[l], B, S)
        y_att = _moba(qk, v, B, S)

        x1, h2, topi, gates, rank, cnt = _merge(y_rnn, y_att, gl, x2, mod3, w_branch[l], w_out[l],
                                                norm2_g[l], w_router[l], b_router[l], S)

        counts = cnt[0, :N_EXPERTS]
        padded = (counts + EXPERT_ROWS - 1) // EXPERT_ROWS * EXPERT_ROWS
        pad_end = jnp.cumsum(padded)
        pad_start = pad_end - padded
        top_idx = topi[:, :TOP_K]
        dest = (pad_start[top_idx] + rank[:, :TOP_K]).reshape(n * TOP_K).astype(jnp.int32)
        n_blocks = (n * TOP_K) // EXPERT_ROWS + N_EXPERTS
        block_start = jnp.arange(n_blocks, dtype=jnp.int32) * EXPERT_ROWS
        block_exp = jnp.minimum(jnp.sum(block_start[:, None] >= pad_end[None, :], axis=1),
                                N_EXPERTS - 1).astype(jnp.int32)
        assert n <= 1 << MAP_SRC_BITS
        n_used = (pad_end[-1] // EXPERT_ROWS).astype(jnp.int32)
        last_blk = jnp.where(padded > 0, pad_end // EXPERT_ROWS - 1, -1)
        spare = n_used + jnp.arange(N_EXPERTS, dtype=jnp.int32)
        spare = jnp.where(spare < n_blocks, spare, -1)
        pad_blocks = jnp.concatenate([last_blk, spare]).astype(jnp.int32)
        choice = _row_choice(dest, pad_blocks, n_blocks * EXPERT_ROWS)
        choice = jnp.concatenate([jnp.full((EXPERT_ROWS,), -1, jnp.int32), choice,
                                  jnp.full((2 * EXPERT_ROWS,), -1, jnp.int32)])
        rix = jnp.arange(choice.shape[0], dtype=jnp.int32)
        spare_row = n * TOP_K + (rix // EXPERT_ROWS) % 2 * EXPERT_ROWS + rix % EXPERT_ROWS
        tok, slot = choice // TOP_K, choice % TOP_K
        src_row = jnp.where(choice >= 0, tok, 0)
        dst_row = jnp.where(choice >= 0, slot * n + tok, spare_row)
        row_map = (src_row | (dst_row << MAP_SRC_BITS)).astype(jnp.int32)

        y_tok = _moe(h2, block_exp, row_map, w_up[l], b_up[l], w_down[l], b_down[l], n_blocks)
        x2 = _moe_sum(y_tok, x1, gates, mod3, S)
    return x2.reshape(B, S, D)
```

```python
import functools

import jax
import jax.numpy as jnp
from jax import lax
from jax.experimental import pallas as pl
from jax.experimental.pallas import tpu as pltpu

F32 = jnp.float32
BF16 = jnp.bfloat16
HIGHEST = lax.Precision.HIGHEST

EPS = 1e-6
LANES = 128
N_HEADS = 8
HEAD_DIM = 128
RNN_BLOCKS = 8
RNN_BW = 128
CONV_W = 4
LRU_C = 8.0
MOBA_BLOCK = 256
MOBA_TOPK = 3
MOBA_CHUNK = 4
MOBA_HEADS = 4
MERGE_SUB = 256
LOG2_E = 1.4426950408889634
ROPE_DIMS = HEAD_DIM // 4
ROPE_THETA = 500000.0
N_EXPERTS = 32
TOP_K = 4
SWIGLU_LIMIT = 7.0
SWIGLU_ALPHA = 1.702
EXPERT_ROWS = 256
NEG_BIG = -1e30

VMEM_LIMIT = 56 * 1024 * 1024


def _params(*sem):
    return pltpu.CompilerParams(dimension_semantics=sem, vmem_limit_bytes=VMEM_LIMIT)


def _adaln_body(c_ref, w_ref, b_ref, o_ref):
    cs = c_ref[...]
    cs = cs * jax.nn.sigmoid(cs)
    o_ref[...] = jnp.dot(cs, w_ref[...], preferred_element_type=F32, precision=HIGHEST) + b_ref[...]


def _adaln(c, ada_w, ada_b):
    B, D = c.shape
    W = ada_w.shape[1]
    cpad = jnp.zeros((8, D), F32).at[:B].set(c)
    tn = 1024
    mod = pl.pallas_call(
        _adaln_body,
        grid=(W // tn,),
        in_specs=[pl.BlockSpec((8, D), lambda j: (0, 0)),
                  pl.BlockSpec((D, tn), lambda j: (0, j)),
                  pl.BlockSpec((1, tn), lambda j: (0, j))],
        out_specs=pl.BlockSpec((8, tn), lambda j: (0, j)),
        out_shape=jax.ShapeDtypeStruct((8, W), F32),
        compiler_params=_params("parallel"),
        name="adaln",
    )(cpad, ada_w, ada_b.reshape(1, W))
    return mod[:B]


def _rope_head_perm():
    half = ROPE_DIMS // 2
    mid = HEAD_DIM // 2
    return jnp.concatenate([jnp.arange(0, half), jnp.arange(ROPE_DIMS, mid + half),
                            jnp.arange(half, ROPE_DIMS), jnp.arange(mid + half, HEAD_DIM)])


def _rope_body(pos_ref, freq_ref, c_ref, s_ref):
    ang = pos_ref[...].astype(F32) * freq_ref[...]
    lane = lax.broadcasted_iota(jnp.int32, ang.shape, 1)
    half = ROPE_DIMS // 2
    mid = HEAD_DIM // 2
    s = jnp.sin(ang)
    c_ref[...] = jnp.cos(ang)
    s_ref[...] = jnp.where(lane < half, -s, jnp.where((lane >= mid) & (lane < mid + half), s, 0.0))


def _rope_tables(positions):
    n = positions.size
    half = ROPE_DIMS // 2
    mid = HEAD_DIM // 2
    freqs = ROPE_THETA ** (-jnp.arange(half, dtype=F32) / half)
    freq_lane = jnp.zeros((1, LANES), F32).at[0, :half].set(freqs).at[0, mid:mid + half].set(freqs)
    tm = 1024
    tab = jax.ShapeDtypeStruct((n, LANES), F32)
    return pl.pallas_call(
        _rope_body,
        grid=(n // tm,),
        in_specs=[pl.BlockSpec((tm, 1), lambda i: (i, 0)),
                  pl.BlockSpec((1, LANES), lambda i: (0, 0))],
        out_specs=[pl.BlockSpec((tm, LANES), lambda i: (i, 0))] * 2,
        out_shape=[tab, tab],
        compiler_params=_params("parallel"),
        name="rope_tab",
    )(positions.reshape(n, 1), freq_lane)


def _norm_mod(x, g, sh, sc):
    ms = jnp.mean(x * x, axis=-1, keepdims=True)
    y = x * lax.rsqrt(ms + EPS)
    return (y * g) * (1.0 + sc) + sh


def _norm_mod_body(x_ref, mod_ref, g_ref, o_ref):
    o_ref[...] = _norm_mod(x_ref[...], g_ref[...], mod_ref[0, 0:1, :], mod_ref[0, 1:2, :]).astype(o_ref.dtype)


def _norm_mod_call(x2, mod3, g, rows_per_batch):
    n, D = x2.shape
    tm = 512
    tpb = rows_per_batch // tm
    return pl.pallas_call(
        _norm_mod_body,
        grid=(n // tm,),
        in_specs=[pl.BlockSpec((tm, D), lambda i: (i, 0)),
                  pl.BlockSpec((1, 6, D), lambda i: (i // tpb, 0, 0)),
                  pl.BlockSpec((1, D), lambda i: (0, 0))],
        out_specs=pl.BlockSpec((tm, D), lambda i: (i, 0)),
        out_shape=jax.ShapeDtypeStruct((n, D), BF16),
        compiler_params=_params("parallel"),
        name="norm_mod",
    )(x2, mod3, g.reshape(1, D))


def _gelu_tanh(x):
    return 0.5 * x * (1.0 + jnp.tanh(0.7978845608028654 * (x + 0.044715 * (x * x * x))))


PROJ_SUB = 256
PROJ_PIECE = 256


def _qk_head(seg, g, cos, sin):
    ms = jnp.mean(seg * seg, axis=-1, keepdims=True)
    y = seg * lax.rsqrt(ms + EPS) * g
    return y * cos + pltpu.roll(y, HEAD_DIM // 2, axis=1) * sin


def _proj_stages(kind, rows, h_ref, w_ref, extras, o_ref):
    n_pieces = w_ref.shape[1] // PROJ_PIECE
    pieces = []
    for j in range(n_pieces):
        pieces.append(jnp.dot(h_ref[rows, :], w_ref[:, j * PROJ_PIECE:(j + 1) * PROJ_PIECE],
                              preferred_element_type=F32))
        yield
    for j in range(n_pieces):
        cols = slice(j * PROJ_PIECE, (j + 1) * PROJ_PIECE)
        acc = pieces[j]
        if kind == "gelu":
            acc = _gelu_tanh(acc)
        elif kind == "gate":
            (b_ref,) = extras
            acc = jax.nn.sigmoid(acc + b_ref[:, cols])
        elif kind == "qk":
            g_ref, c_ref, s_ref = extras
            cos, sin = c_ref[rows, :], s_ref[rows, :]
            acc = jnp.concatenate(
                [_qk_head(acc[:, o:o + HEAD_DIM], g_ref[:, cols.start + o:cols.start + o + HEAD_DIM], cos, sin)
                 for o in range(0, PROJ_PIECE, HEAD_DIM)], axis=1)
        o_ref[rows, cols] = acc.astype(o_ref.dtype)
        yield


def _proj_body(kind, h_ref, w_ref, *rest):
    *extras, o_ref = rest
    first, second = [_proj_stages(kind, slice(t * PROJ_SUB, (t + 1) * PROJ_SUB), h_ref, w_ref, extras, o_ref)
                     for t in range(2)]
    n_pieces = w_ref.shape[1] // PROJ_PIECE
    for _ in range(n_pieces):
        next(first)
    for _ in range(n_pieces):
        next(second)
        next(first)
    for stage in second:
        pass


def _proj(kind, h, w, extras, out_dtype, name):
    n, D = h.shape
    W = w.shape[1]
    tm, tn = 512, 1024
    in_specs = [pl.BlockSpec((tm, D), lambda j, i: (i, 0)),
                pl.BlockSpec((D, tn), lambda j, i: (0, j))]
    args = [h, w]
    for arr in extras:
        if arr.shape[0] == 1:
            in_specs.append(pl.BlockSpec((1, tn), lambda j, i: (0, j)))
        else:
            in_specs.append(pl.BlockSpec((tm, LANES), lambda j, i: (i, 0)))
        args.append(arr)
    return pl.pallas_call(
        functools.partial(_proj_body, kind),
        grid=(W // tn, n // tm),
        in_specs=in_specs,
        out_specs=pl.BlockSpec((tm, tn), lambda j, i: (i, j)),
        out_shape=jax.ShapeDtypeStruct((n, W), out_dtype),
        compiler_params=_params("parallel", "parallel"),
        name=name,
    )(*args)


def _rglru_body(xr_ref, gg_ref, cw_ref, cb_ref, wa_ref, ba_ref, wx_ref, bx_ref, lam_ref,
                o_ref, xbuf, hcar, a_s, u_s):
    s = pl.program_id(1)
    ts, D = xr_ref.shape

    @pl.when(s == 0)
    def _():
        xbuf[0:8, :] = jnp.zeros((8, D), F32)
        hcar[...] = jnp.zeros_like(hcar)

    @pl.when(s > 0)
    def _():
        xbuf[0:8, :] = xbuf[ts:ts + 8, :]

    xbuf[8:8 + ts, :] = xr_ref[...]
    xc = cb_ref[...] + cw_ref[0:1, :] * xbuf[8:8 + ts, :]
    for i in range(1, CONV_W):
        xc = xc + cw_ref[i:i + 1, :] * xbuf[8 - i:8 - i + ts, :]

    ra, rx = [], []
    for n in range(RNN_BLOCKS):
        xb = xc[:, n * RNN_BW:(n + 1) * RNN_BW].astype(BF16)
        ra.append(jnp.dot(xb, wa_ref[n], preferred_element_type=F32))
        rx.append(jnp.dot(xb, wx_ref[n], preferred_element_type=F32))
    r = jax.nn.sigmoid(jnp.concatenate(ra, axis=1) + ba_ref[...])
    ig = jax.nn.sigmoid(jnp.concatenate(rx, axis=1) + bx_ref[...])

    z = -lam_ref[...]
    softplus = jnp.maximum(z, 0.0) + jnp.log1p(jnp.exp(-jnp.abs(z)))
    log_a = -LRU_C * r * softplus
    a = jnp.exp(log_a)
    mult = jnp.sqrt(1.0 - a * a)
    row = lax.broadcasted_iota(jnp.int32, (ts, D), 0)
    mult = jnp.where((row == 0) & (s == 0), 1.0, mult)
    u = mult * (ig * xc)

    rm = row & 7
    for d in (1, 2, 4):
        keep = rm >= d
        a_sh = pltpu.roll(a, d, axis=0)
        u_sh = pltpu.roll(u, d, axis=0)
        u = jnp.where(keep, a * u_sh + u, u)
        a = jnp.where(keep, a * a_sh, a)
    a_s[...] = a
    u_s[...] = u

    def group(g, h):
        r0 = pl.multiple_of(g * 8, 8)
        hg = u_s[pl.ds(r0, 8), :] + a_s[pl.ds(r0, 8), :] * h
        u_s[pl.ds(r0, 8), :] = hg
        return hg[7:8, :]

    hcar[...] = lax.fori_loop(0, ts // 8, group, hcar[...])
    o_ref[...] = (u_s[...] * gg_ref[...]).astype(o_ref.dtype)


def _rglru(xr, gg, conv_w, conv_b, w_a, b_a, w_x, b_x, lam, B, S):
    n, D = xr.shape
    ts = 256
    spb = S // ts
    row = lambda b, s: (b * spb + s, 0)
    vec = lambda b, s: (0, 0)
    return pl.pallas_call(
        _rglru_body,
        grid=(B, spb),
        in_specs=[pl.BlockSpec((ts, D), row),
                  pl.BlockSpec((ts, D), row),
                  pl.BlockSpec((CONV_W, D), vec),
                  pl.BlockSpec((1, D), vec),
                  pl.BlockSpec((RNN_BLOCKS, RNN_BW, RNN_BW), lambda b, s: (0, 0, 0)),
                  pl.BlockSpec((1, D), vec),
                  pl.BlockSpec((RNN_BLOCKS, RNN_BW, RNN_BW), lambda b, s: (0, 0, 0)),
                  pl.BlockSpec((1, D), vec),
                  pl.BlockSpec((1, D), vec)],
        out_specs=pl.BlockSpec((ts, D), row),
        out_shape=jax.ShapeDtypeStruct((n, D), BF16),
        scratch_shapes=[pltpu.VMEM((ts + 8, D), F32), pltpu.VMEM((1, D), F32),
                        pltpu.VMEM((ts, D), F32), pltpu.VMEM((ts, D), F32)],
        compiler_params=_params("arbitrary", "arbitrary"),
        name="rglru",
    )(xr, gg, conv_w, conv_b.reshape(1, D), w_a.astype(BF16), b_a.reshape(1, D),
      w_x.astype(BF16), b_x.reshape(1, D), lam.reshape(1, D))


def _moba_body(q_ref, k_ref, v_ref, o_ref, kaug, vaug, kmean, m_s, acc_s):
    qb = pl.program_id(2)
    S = k_ref.shape[0]
    nb = S // MOBA_BLOCK
    bs = MOBA_BLOCK
    dh = HEAD_DIM
    heads = range(MOBA_HEADS)
    hcol = lambda hh: slice(hh * dh, (hh + 1) * dh)

    @pl.when(qb == 0)
    def _():
        blk = lax.broadcasted_iota(jnp.int32, (S, dh), 0) // bs
        col = lax.broadcasted_iota(jnp.int32, (S, dh), 1)
        onehot = jnp.where(col == blk, 1.0, 0.0).astype(BF16)
        for hh in heads:
            kaug[hh, :, 0:dh] = k_ref[:, hcol(hh)].astype(BF16)
            kaug[hh, :, dh:2 * dh] = onehot
            vaug[hh, :, 0:dh] = v_ref[:, hcol(hh)]
            vaug[hh, :, dh:2 * dh] = jnp.ones((S, dh), BF16)
            kmean[hh] = jnp.zeros((LANES, dh), F32)
            for j in range(nb):
                kmean[hh, j:j + 1, :] = jnp.mean(k_ref[j * bs:(j + 1) * bs, hcol(hh)], axis=0, keepdims=True)

    scale = HEAD_DIM ** -0.5
    nt = (((1,), (1,)), ((), ()))
    c2 = scale * LOG2_E
    r0 = pl.multiple_of(qb * bs, bs)
    lane = lax.broadcasted_iota(jnp.int32, (bs, LANES), 1)
    lane_f = lane.astype(F32)
    past = lane < qb
    rr = lax.broadcasted_iota(jnp.int32, (bs, bs), 0)
    cc = lax.broadcasted_iota(jnp.int32, (bs, bs), 1)

    qs = [q_ref[:, hcol(hh)] for hh in heads]
    qbfs = [q.astype(BF16) for q in qs]

    gs = [jnp.where(past, lax.dot_general(qs[hh], kmean[hh], nt, preferred_element_type=F32, precision=HIGHEST),
                    -jnp.inf) for hh in heads]
    ss = [jnp.where(cc <= rr, lax.dot_general(qbfs[hh], kaug[hh, pl.ds(r0, bs), 0:dh], nt,
                                              preferred_element_type=F32), NEG_BIG) for hh in heads]
    sels = [jnp.zeros((bs, LANES), F32) for _ in heads]
    for _ in range(MOBA_TOPK):
        for hh in heads:
            m = jnp.max(gs[hh], axis=1, keepdims=True)
            idx = jnp.min(jnp.where(gs[hh] == m, lane_f, float(LANES)), axis=1, keepdims=True)
            pick = (lane_f == idx) & past
            sels[hh] = jnp.where(pick, 1.0, sels[hh])
            gs[hh] = jnp.where(pick, -jnp.inf, gs[hh])
    qaugs = [jnp.concatenate([qbfs[hh], jnp.where(sels[hh] > 0.5, 0.0, NEG_BIG).astype(BF16)], axis=1)
             for hh in heads]

    ps = []
    for hh in heads:
        m0 = jnp.max(ss[hh], axis=1, keepdims=True)
        ps.append(jnp.exp2((ss[hh] - m0) * c2).astype(BF16))
        m_s[hh] = m0
    for hh in heads:
        acc_s[hh] = jnp.dot(ps[hh], vaug[hh, pl.ds(r0, bs), :], preferred_element_type=F32)

    cw = MOBA_CHUNK * bs
    for c in range(nb // MOBA_CHUNK):
        @pl.when(c * MOBA_CHUNK < qb)
        def _(c=c):
            scs = [lax.dot_general(qaugs[hh], kaug[hh, c * cw:(c + 1) * cw, :], nt, preferred_element_type=F32)
                   for hh in heads]
            pcs, alphas = [], []
            for hh in heads:
                m_old = m_s[hh]
                m_new = jnp.maximum(m_old, jnp.max(scs[hh], axis=1, keepdims=True))
                alphas.append(jnp.exp2((m_old - m_new) * c2))
                pcs.append(jnp.exp2((scs[hh] - m_new) * c2).astype(BF16))
                m_s[hh] = m_new
            for hh in heads:
                acc_s[hh] = alphas[hh] * acc_s[hh] + jnp.dot(pcs[hh], vaug[hh, c * cw:(c + 1) * cw, :],
                                                             preferred_element_type=F32)

    for hh in heads:
        o_ref[:, hcol(hh)] = (acc_s[hh, :, 0:dh] / acc_s[hh, :, dh:dh + 1]).astype(o_ref.dtype)


def _moba(qk, v, B, S):
    n, D = v.shape
    nq = S // MOBA_BLOCK
    hs = MOBA_HEADS
    w = hs * HEAD_DIM
    return pl.pallas_call(
        _moba_body,
        grid=(B, N_HEADS // hs, nq),
        in_specs=[pl.BlockSpec((MOBA_BLOCK, w), lambda b, h, i: (b * nq + i, h)),
                  pl.BlockSpec((S, w), lambda b, h, i: (b, N_HEADS // hs + h)),
                  pl.BlockSpec((S, w), lambda b, h, i: (b, h))],
        out_specs=pl.BlockSpec((MOBA_BLOCK, w), lambda b, h, i: (b * nq + i, h)),
        out_shape=jax.ShapeDtypeStruct((n, D), BF16),
        scratch_shapes=[pltpu.VMEM((hs, S, 2 * HEAD_DIM), BF16), pltpu.VMEM((hs, S, 2 * HEAD_DIM), BF16),
                        pltpu.VMEM((hs, LANES, HEAD_DIM), F32), pltpu.VMEM((hs, MOBA_BLOCK, 1), F32),
                        pltpu.VMEM((hs, MOBA_BLOCK, 2 * HEAD_DIM), F32)],
        compiler_params=_params("parallel", "parallel", "arbitrary"),
        name="moba",
    )(qk, qk, v)


def _merge_body(yr_ref, ya_ref, gr_ref, ga_ref, x_ref, mod_ref, wb_ref, wo_ref, g2_ref, wr_ref, br_ref,
                x1_ref, h2_ref, topi_ref, gate_ref, rank_ref, cnt_ref, carry):
    i = pl.program_id(0)

    @pl.when(i == 0)
    def _():
        carry[...] = jnp.zeros_like(carry)

    tm = MERGE_SUB
    counts = {"total": carry[...]}
    first, second = [
        _merge_stages(slice(h * tm, (h + 1) * tm), counts, yr_ref, ya_ref, gr_ref, ga_ref, x_ref, mod_ref,
                      wb_ref, wo_ref, g2_ref, wr_ref, br_ref, x1_ref, h2_ref, topi_ref, gate_ref, rank_ref)
        for h in range(2)]
    for _ in range(MERGE_MATMUL_STAGES):
        next(first)
    for stage in first:
        next(second)
    for stage in second:
        pass
    carry[...] = counts["total"]
    cnt_ref[...] = jnp.broadcast_to(counts["total"], cnt_ref.shape).astype(jnp.int32)


MERGE_MATMUL_STAGES = 4


def _merge_stages(rows, counts, yr_ref, ya_ref, gr_ref, ga_ref, x_ref, mod_ref, wb_ref, wo_ref, g2_ref, wr_ref,
                  br_ref, x1_ref, h2_ref, topi_ref, gate_ref, rank_ref):
    tm = rows.stop - rows.start
    zr = jnp.dot(yr_ref[rows, :], wb_ref[0], preferred_element_type=F32)
    yield
    za = jnp.dot(ya_ref[rows, :], wb_ref[1], preferred_element_type=F32)
    mix = (gr_ref[rows, :] * zr + ga_ref[rows, :] * za).astype(BF16)
    yield
    mixed = jnp.dot(mix, wo_ref[...], preferred_element_type=F32)
    x1 = x_ref[rows, :] + mod_ref[0, 2:3, :] * mixed
    x1_ref[rows, :] = x1
    h2 = _norm_mod(x1, g2_ref[...], mod_ref[0, 3:4, :], mod_ref[0, 4:5, :])
    h2_ref[rows, :] = h2
    yield

    hi = h2.astype(BF16)
    lo = (h2 - hi.astype(F32)).astype(BF16)
    both = jnp.dot(hi, wr_ref[...], preferred_element_type=F32)
    logits = (both[:, :LANES] + both[:, LANES:]
              + jnp.dot(lo, wr_ref[:, :LANES], preferred_element_type=F32) + br_ref[...])
    yield

    lane = lax.broadcasted_iota(jnp.int32, logits.shape, 1)
    lane_f = lane.astype(F32)
    lg = jnp.where(lane < N_EXPERTS, logits, -jnp.inf)
    vals, idxs = [], []
    onehot = jnp.zeros(logits.shape, F32)
    for _ in range(TOP_K):
        m = jnp.max(lg, axis=1, keepdims=True)
        idx = jnp.min(jnp.where(lg == m, lane_f, float(LANES)), axis=1, keepdims=True)
        pick = lane_f == idx
        onehot = jnp.where(pick, 1.0, onehot)
        lg = jnp.where(pick, -jnp.inf, lg)
        vals.append(m)
        idxs.append(idx)
        yield
    es = [jnp.exp(v - vals[0]) for v in vals]
    den = es[0] + es[1] + es[2] + es[3]
    before0 = counts["total"]

    rr = lax.broadcasted_iota(jnp.int32, (tm, tm), 0)
    cc = lax.broadcasted_iota(jnp.int32, (tm, tm), 1)
    tri = jnp.where(cc < rr, 1.0, 0.0).astype(BF16)
    before = jnp.dot(tri, onehot.astype(BF16), preferred_element_type=F32) + before0

    topi = jnp.zeros(logits.shape, F32)
    gates = jnp.zeros(logits.shape, F32)
    rank = jnp.zeros(logits.shape, F32)
    for k in range(TOP_K):
        rk = jnp.sum(jnp.where(lane_f == idxs[k], before, 0.0), axis=1, keepdims=True)
        topi = jnp.where(lane == k, idxs[k], topi)
        gates = jnp.where(lane == k, es[k] / den, gates)
        rank = jnp.where(lane == k, rk, rank)
    topi_ref[rows, :] = topi.astype(jnp.int32)
    gate_ref[rows, :] = gates
    rank_ref[rows, :] = rank.astype(jnp.int32)
    counts["total"] = before0 + jnp.sum(onehot, axis=0, keepdims=True)
    yield


def _merge(y_rnn, y_att, gl, x2, mod3, w_branch, w_out, norm2_g, w_router, b_router, rows_per_batch):
    n, D = x2.shape
    tm = 2 * MERGE_SUB
    tpb = rows_per_batch // tm
    wr32 = jnp.zeros((D, LANES), F32).at[:, :N_EXPERTS].set(w_router)
    wr_hi = wr32.astype(BF16)
    wr = jnp.concatenate([wr_hi, (wr32 - wr_hi.astype(F32)).astype(BF16)], axis=1)
    br = jnp.zeros((1, LANES), F32).at[0, :N_EXPERTS].set(b_router)
    row = lambda i: (i, 0)
    fixed = lambda i: (0, 0)
    once = pl.Buffered(1)
    wide = jax.ShapeDtypeStruct((n, D), F32)
    narrow_i = jax.ShapeDtypeStruct((n, LANES), jnp.int32)
    return pl.pallas_call(
        _merge_body,
        grid=(n // tm,),
        in_specs=[pl.BlockSpec((tm, D), row),
                  pl.BlockSpec((tm, D), row),
                  pl.BlockSpec((tm, D), lambda i: (i, 0)),
                  pl.BlockSpec((tm, D), lambda i: (i, 1)),
                  pl.BlockSpec((tm, D), row),
                  pl.BlockSpec((1, 6, D), lambda i: (i // tpb, 0, 0)),
                  pl.BlockSpec((2, D, D), lambda i: (0, 0, 0), pipeline_mode=once),
                  pl.BlockSpec((D, D), fixed, pipeline_mode=once),
                  pl.BlockSpec((1, D), fixed),
                  pl.BlockSpec((D, 2 * LANES), fixed, pipeline_mode=once),
                  pl.BlockSpec((1, LANES), fixed)],
        out_specs=[pl.BlockSpec((tm, D), row),
                   pl.BlockSpec((tm, D), row),
                   pl.BlockSpec((tm, LANES), row),
                   pl.BlockSpec((tm, LANES), row),
                   pl.BlockSpec((tm, LANES), row),
                   pl.BlockSpec((8, LANES), fixed)],
        out_shape=[wide, wide, narrow_i, jax.ShapeDtypeStruct((n, LANES), F32), narrow_i,
                   jax.ShapeDtypeStruct((8, LANES), jnp.int32)],
        scratch_shapes=[pltpu.VMEM((1, LANES), F32)],
        compiler_params=_params("arbitrary"),
        name="merge",
    )(y_rnn, y_att, gl, gl, x2, mod3, w_branch.astype(BF16), w_out.astype(BF16),
      norm2_g.reshape(1, D), wr, br)


DISPATCH_TOKENS = 256
ROW_DMA_UNROLL = 8


def _dispatch_body(dest_ref, zblk_ref, src_ref, dst_hbm, zbuf, sem, zsem):
    t = DISPATCH_TOKENS
    c = pl.program_id(0)

    @pl.when(c == 0)
    def _():
        zbuf[...] = jnp.zeros_like(zbuf)

        def zero_copy(b):
            r0 = pl.multiple_of(b * EXPERT_ROWS, EXPERT_ROWS)
            return pltpu.make_async_copy(zbuf, dst_hbm.at[pl.ds(r0, EXPERT_ROWS), :], zsem)

        def zissue(j, carry):
            @pl.when(zblk_ref[j] >= 0)
            def _():
                zero_copy(zblk_ref[j]).start()
            return carry

        def zwait(j, carry):
            @pl.when(zblk_ref[j] >= 0)
            def _():
                zero_copy(zblk_ref[j]).wait()
            return carry

        lax.fori_loop(0, zblk_ref.shape[0], zissue, 0)
        lax.fori_loop(0, zblk_ref.shape[0], zwait, 0)

    def issue(g, carry):
        for u in range(ROW_DMA_UNROLL):
            r = g * ROW_DMA_UNROLL + u
            for k in range(TOP_K):
                d = dest_ref[(c * t + r) * TOP_K + k]
                pltpu.make_async_copy(src_ref.at[pl.ds(r, 1), :], dst_hbm.at[pl.ds(d, 1), :], sem).start()
        return carry

    lax.fori_loop(0, t // ROW_DMA_UNROLL, issue, 0)
    for k in range(TOP_K):
        pltpu.make_async_copy(src_ref, dst_hbm.at[pl.ds(0, t), :], sem).wait()


def _dispatch(h2, dest, zero_blocks, n_rows):
    n, D = h2.shape
    return pl.pallas_call(
        _dispatch_body,
        grid_spec=pltpu.PrefetchScalarGridSpec(
            num_scalar_prefetch=2,
            grid=(n // DISPATCH_TOKENS,),
            in_specs=[pl.BlockSpec((DISPATCH_TOKENS, D), lambda i, d, z: (i, 0))],
            out_specs=pl.BlockSpec(memory_space=pl.ANY),
            scratch_shapes=[pltpu.VMEM((EXPERT_ROWS, D), F32), pltpu.SemaphoreType.DMA,
                            pltpu.SemaphoreType.DMA],
        ),
        out_shape=jax.ShapeDtypeStruct((n_rows, D), F32),
        compiler_params=_params("arbitrary"),
        name="dispatch",
    )(dest, zero_blocks, h2)


def _experts_body(bexp_ref, nused_ref, x_ref, wu_ref, bu_ref, wd_ref, bd_ref, o_ref, wu_bf, wd_bf):
    i = pl.program_id(0)
    dff = wd_ref.shape[1]
    used = i < nused_ref[0]

    new_expert = (i == 0) | (bexp_ref[i] != bexp_ref[jnp.maximum(i - 1, 0)])

    @pl.when(used & new_expert)
    def _():
        wu_bf[...] = wu_ref[0].astype(BF16)
        wd_bf[...] = wd_ref[0].astype(BF16)

    @pl.when(used)
    def _():
        hc = jnp.dot(x_ref[...].astype(BF16), wu_bf[...], preferred_element_type=F32) + bu_ref[0]
        g = jnp.minimum(hc[:, :dff], SWIGLU_LIMIT)
        lin = jnp.clip(hc[:, dff:], -SWIGLU_LIMIT, SWIGLU_LIMIT)
        act = (lin + 1.0) * g * jax.nn.sigmoid(SWIGLU_ALPHA * g)
        o_ref[...] = jnp.dot(act.astype(BF16), wd_bf[...], preferred_element_type=F32) + bd_ref[0]

    @pl.when(i >= nused_ref[0])
    def _():
        o_ref[...] = jnp.zeros_like(o_ref)


def _experts(x_rows, block_exp, n_used, w_up, b_up, w_down, b_down, n_blocks):
    R, D = x_rows.shape
    E, _, F2 = w_up.shape
    dff = w_down.shape[1]
    blk = lambda i, be, nu: (i, 0)
    wsel = lambda i, be, nu: (be[i], 0, 0)
    return pl.pallas_call(
        _experts_body,
        grid_spec=pltpu.PrefetchScalarGridSpec(
            num_scalar_prefetch=2,
            grid=(n_blocks,),
            in_specs=[pl.BlockSpec((EXPERT_ROWS, D), blk),
                      pl.BlockSpec((1, D, F2), wsel),
                      pl.BlockSpec((1, 1, F2), wsel),
                      pl.BlockSpec((1, dff, D), wsel),
                      pl.BlockSpec((1, 1, D), wsel)],
            out_specs=pl.BlockSpec((EXPERT_ROWS, D), blk),
            scratch_shapes=[pltpu.VMEM((D, F2), BF16), pltpu.VMEM((dff, D), BF16)],
        ),
        out_shape=jax.ShapeDtypeStruct((R, D), F32),
        compiler_params=_params("arbitrary"),
        name="experts",
    )(block_exp, n_used, x_rows, w_up, b_up.reshape(E, 1, F2), w_down, b_down.reshape(E, 1, D))


COMBINE_ROWS = 128


def _combine_body(dest_ref, y_hbm, x1_ref, gate_ref, mod_ref, o_ref, buf, sem):
    i = pl.program_id(0)
    t = COMBINE_ROWS

    def issue(g, carry):
        for u in range(ROW_DMA_UNROLL):
            r = g * ROW_DMA_UNROLL + u
            for k in range(TOP_K):
                d = dest_ref[(i * t + r) * TOP_K + k]
                pltpu.make_async_copy(y_hbm.at[pl.ds(d, 1), :], buf.at[pl.ds(r, 1), pl.ds(k * D, D)], sem).start()
        return carry

    D = o_ref.shape[1]
    lax.fori_loop(0, t // ROW_DMA_UNROLL, issue, 0)
    for k in range(TOP_K):
        pltpu.make_async_copy(y_hbm.at[pl.ds(0, t), :], buf.at[:, pl.ds(k * D, D)], sem).wait()
    gates = gate_ref[...]
    y = gates[:, 0:1] * buf[:, 0:D]
    for k in range(1, TOP_K):
        y = y + gates[:, k:k + 1] * buf[:, k * D:(k + 1) * D]
    o_ref[...] = x1_ref[...] + mod_ref[0, 5:6, :] * y


def _combine(y_rows, dest, x1, gates, mod3, rows_per_batch):
    n, D = x1.shape
    t = COMBINE_ROWS
    tpb = rows_per_batch // t
    return pl.pallas_call(
        _combine_body,
        grid_spec=pltpu.PrefetchScalarGridSpec(
            num_scalar_prefetch=1,
            grid=(n // t,),
            in_specs=[pl.BlockSpec(memory_space=pl.ANY),
                      pl.BlockSpec((t, D), lambda i, d: (i, 0)),
                      pl.BlockSpec((t, LANES), lambda i, d: (i, 0)),
                      pl.BlockSpec((1, 6, D), lambda i, d: (i // tpb, 0, 0))],
            out_specs=pl.BlockSpec((t, D), lambda i, d: (i, 0)),
            scratch_shapes=[pltpu.VMEM((t, TOP_K * D), F32), pltpu.SemaphoreType.DMA],
        ),
        out_shape=jax.ShapeDtypeStruct((n, D), F32),
        compiler_params=_params("arbitrary"),
        name="combine",
    )(dest, y_rows, x1, gates, mod3)


def kernel(x, c, positions, ada_w, ada_b, norm1_g, norm2_g, w_in, conv_w, conv_b, w_rg_a, b_rg_a, w_rg_x, b_rg_x, lru_lambda, q_norm_g, k_norm_g, b_gate, w_branch, w_out, w_router, b_router, w_up, b_up, w_down, b_down):
    B, S, D = x.shape
    n = B * S
    depth = ada_w.shape[0]
    x2 = x.reshape(n, D)
    cos_t, sin_t = _rope_tables(positions)
    perm = _rope_head_perm()
    head_cols = (jnp.arange(2 * N_HEADS)[:, None] * HEAD_DIM + perm[None, :]).reshape(-1)
    for l in range(depth):
        mod3 = _adaln(c, ada_w[l], ada_b[l]).reshape(B, 6, D)
        h1 = _norm_mod_call(x2, mod3, norm1_g[l], S)
        w = w_in[l].astype(BF16)
        xr = _proj("plain", h1, w[:, 0:D], [], F32, "proj_xr")
        gg = _proj("gelu", h1, w[:, D:2 * D], [], F32, "proj_gr")
        qk_g = jnp.concatenate([jnp.tile(q_norm_g[l][perm], N_HEADS),
                                jnp.tile(k_norm_g[l][perm], N_HEADS)]).reshape(1, 2 * D)
        qk = _proj("qk", h1, w[:, 2 * D:4 * D][:, head_cols], [qk_g, cos_t, sin_t], F32, "proj_qk")
        v = _proj("plain", h1, w[:, 4 * D:5 * D], [], BF16, "proj_v")
        gl = _proj("gate", h1, w[:, 5 * D:7 * D], [b_gate[l].reshape(1, 2 * D)], F32, "proj_gl")

        y_rnn = _rglru(xr, gg, conv_w[l], conv_b[l], w_rg_a[l], b_rg_a[l], w_rg_x[l], b_rg_x[l],
                       lru_lambda[l], B, S)
        y_att = _moba(qk, v, B, S)

        x1, h2, topi, gates, rank, cnt = _merge(y_rnn, y_att, gl, x2, mod3, w_branch[l], w_out[l],
                                                norm2_g[l], w_router[l], b_router[l], S)

        counts = cnt[0, :N_EXPERTS]
        padded = (counts + EXPERT_ROWS - 1) // EXPERT_ROWS * EXPERT_ROWS
        pad_end = jnp.cumsum(padded)
        pad_start = pad_end - padded
        top_idx = topi[:, :TOP_K]
        dest = (pad_start[top_idx] + rank[:, :TOP_K]).reshape(n * TOP_K).astype(jnp.int32)
        n_blocks = (n * TOP_K) // EXPERT_ROWS + N_EXPERTS
        block_start = jnp.arange(n_blocks, dtype=jnp.int32) * EXPERT_ROWS
        block_exp = jnp.minimum(jnp.sum(block_start[:, None] >= pad_end[None, :], axis=1),
                                N_EXPERTS - 1).astype(jnp.int32)
        n_used = (pad_end[-1] // EXPERT_ROWS).astype(jnp.int32)
        last_blk = jnp.where(padded > 0, pad_end // EXPERT_ROWS - 1, -1)
        spare = n_used + jnp.arange(N_EXPERTS, dtype=jnp.int32)
        spare = jnp.where(spare < n_blocks, spare, -1)
        zero_blocks = jnp.concatenate([last_blk, spare]).astype(jnp.int32)

        x_rows = _dispatch(h2, dest, zero_blocks, n_blocks * EXPERT_ROWS)
        y_rows = _experts(x_rows, block_exp, n_used.reshape(1), w_up[l], b_up[l], w_down[l], b_down[l], n_blocks)
        x2 = _combine(y_rows, dest, x1, gates, mod3, S)
    return x2.reshape(B, S, D)
```

```python
import functools

import jax
import jax.numpy as jnp
from jax import lax
from jax.experimental import pallas as pl
from jax.experimental.pallas import tpu as pltpu

F32 = jnp.float32
BF16 = jnp.bfloat16
HIGHEST = lax.Precision.HIGHEST

EPS = 1e-6
LANES = 128
N_HEADS = 8
HEAD_DIM = 128
RNN_BLOCKS = 8
RNN_BW = 128
CONV_W = 4
LRU_C = 8.0
MOBA_BLOCK = 256
MOBA_TOPK = 3
MOBA_CHUNK = 4
MOBA_HEADS = 4
MERGE_SUB = 256
LOG2_E = 1.4426950408889634
ROPE_DIMS = HEAD_DIM // 4
ROPE_THETA = 500000.0
N_EXPERTS = 32
TOP_K = 4
SWIGLU_LIMIT = 7.0
SWIGLU_ALPHA = 1.702
EXPERT_ROWS = 256
NEG_BIG = -1e30

VMEM_LIMIT = 56 * 1024 * 1024


def _params(*sem):
    return pltpu.CompilerParams(dimension_semantics=sem, vmem_limit_bytes=VMEM_LIMIT)


def _adaln_body(c_ref, w_ref, b_ref, o_ref):
    cs = c_ref[...]
    cs = cs * jax.nn.sigmoid(cs)
    o_ref[...] = jnp.dot(cs, w_ref[...], preferred_element_type=F32, precision=HIGHEST) + b_ref[...]


def _adaln(c, ada_w, ada_b):
    B, D = c.shape
    W = ada_w.shape[1]
    cpad = jnp.zeros((8, D), F32).at[:B].set(c)
    tn = 1024
    mod = pl.pallas_call(
        _adaln_body,
        grid=(W // tn,),
        in_specs=[pl.BlockSpec((8, D), lambda j: (0, 0)),
                  pl.BlockSpec((D, tn), lambda j: (0, j)),
                  pl.BlockSpec((1, tn), lambda j: (0, j))],
        out_specs=pl.BlockSpec((8, tn), lambda j: (0, j)),
        out_shape=jax.ShapeDtypeStruct((8, W), F32),
        compiler_params=_params("parallel"),
        name="adaln",
    )(cpad, ada_w, ada_b.reshape(1, W))
    return mod[:B]


def _rope_head_perm():
    half = ROPE_DIMS // 2
    mid = HEAD_DIM // 2
    return jnp.concatenate([jnp.arange(0, half), jnp.arange(ROPE_DIMS, mid + half),
                            jnp.arange(half, ROPE_DIMS), jnp.arange(mid + half, HEAD_DIM)])


def _rope_body(pos_ref, freq_ref, c_ref, s_ref):
    ang = pos_ref[...].astype(F32) * freq_ref[...]
    lane = lax.broadcasted_iota(jnp.int32, ang.shape, 1)
    half = ROPE_DIMS // 2
    mid = HEAD_DIM // 2
    s = jnp.sin(ang)
    c_ref[...] = jnp.cos(ang)
    s_ref[...] = jnp.where(lane < half, -s, jnp.where((lane >= mid) & (lane < mid + half), s, 0.0))


def _rope_tables(positions):
    n = positions.size
    half = ROPE_DIMS // 2
    mid = HEAD_DIM // 2
    freqs = ROPE_THETA ** (-jnp.arange(half, dtype=F32) / half)
    freq_lane = jnp.zeros((1, LANES), F32).at[0, :half].set(freqs).at[0, mid:mid + half].set(freqs)
    tm = 1024
    tab = jax.ShapeDtypeStruct((n, LANES), F32)
    return pl.pallas_call(
        _rope_body,
        grid=(n // tm,),
        in_specs=[pl.BlockSpec((tm, 1), lambda i: (i, 0)),
                  pl.BlockSpec((1, LANES), lambda i: (0, 0))],
        out_specs=[pl.BlockSpec((tm, LANES), lambda i: (i, 0))] * 2,
        out_shape=[tab, tab],
        compiler_params=_params("parallel"),
        name="rope_tab",
    )(positions.reshape(n, 1), freq_lane)


def _norm_mod(x, g, sh, sc):
    ms = jnp.mean(x * x, axis=-1, keepdims=True)
    y = x * lax.rsqrt(ms + EPS)
    return (y * g) * (1.0 + sc) + sh


def _norm_mod_body(x_ref, mod_ref, g_ref, o_ref):
    o_ref[...] = _norm_mod(x_ref[...], g_ref[...], mod_ref[0, 0:1, :], mod_ref[0, 1:2, :]).astype(o_ref.dtype)


def _norm_mod_call(x2, mod3, g, rows_per_batch):
    n, D = x2.shape
    tm = 512
    tpb = rows_per_batch // tm
    return pl.pallas_call(
        _norm_mod_body,
        grid=(n // tm,),
        in_specs=[pl.BlockSpec((tm, D), lambda i: (i, 0)),
                  pl.BlockSpec((1, 6, D), lambda i: (i // tpb, 0, 0)),
                  pl.BlockSpec((1, D), lambda i: (0, 0))],
        out_specs=pl.BlockSpec((tm, D), lambda i: (i, 0)),
        out_shape=jax.ShapeDtypeStruct((n, D), BF16),
        compiler_params=_params("parallel"),
        name="norm_mod",
    )(x2, mod3, g.reshape(1, D))


def _gelu_tanh(x):
    return 0.5 * x * (1.0 + jnp.tanh(0.7978845608028654 * (x + 0.044715 * (x * x * x))))


PROJ_SUB = 256
PROJ_PIECE = 256


def _qk_head(seg, g, cos, sin):
    ms = jnp.mean(seg * seg, axis=-1, keepdims=True)
    y = seg * lax.rsqrt(ms + EPS) * g
    return y * cos + pltpu.roll(y, HEAD_DIM // 2, axis=1) * sin


def _proj_stages(kind, rows, h_ref, w_ref, extras, o_ref):
    n_pieces = w_ref.shape[1] // PROJ_PIECE
    pieces = []
    for j in range(n_pieces):
        pieces.append(jnp.dot(h_ref[rows, :], w_ref[:, j * PROJ_PIECE:(j + 1) * PROJ_PIECE],
                              preferred_element_type=F32))
        yield
    for j in range(n_pieces):
        cols = slice(j * PROJ_PIECE, (j + 1) * PROJ_PIECE)
        acc = pieces[j]
        if kind == "gelu":
            acc = _gelu_tanh(acc)
        elif kind == "gate":
            (b_ref,) = extras
            acc = jax.nn.sigmoid(acc + b_ref[:, cols])
        elif kind == "qk":
            g_ref, c_ref, s_ref = extras
            cos, sin = c_ref[rows, :], s_ref[rows, :]
            acc = jnp.concatenate(
                [_qk_head(acc[:, o:o + HEAD_DIM], g_ref[:, cols.start + o:cols.start + o + HEAD_DIM], cos, sin)
                 for o in range(0, PROJ_PIECE, HEAD_DIM)], axis=1)
        o_ref[rows, cols] = acc.astype(o_ref.dtype)
        yield


def _proj_body(kind, h_ref, w_ref, *rest):
    *extras, o_ref = rest
    first, second = [_proj_stages(kind, slice(t * PROJ_SUB, (t + 1) * PROJ_SUB), h_ref, w_ref, extras, o_ref)
                     for t in range(2)]
    n_pieces = w_ref.shape[1] // PROJ_PIECE
    for _ in range(n_pieces):
        next(first)
    for _ in range(n_pieces):
        next(second)
        next(first)
    for stage in second:
        pass


def _proj(kind, h, w, extras, out_dtype, name):
    n, D = h.shape
    W = w.shape[1]
    tm, tn = 512, 1024
    in_specs = [pl.BlockSpec((tm, D), lambda j, i: (i, 0)),
                pl.BlockSpec((D, tn), lambda j, i: (0, j))]
    args = [h, w]
    for arr in extras:
        if arr.shape[0] == 1:
            in_specs.append(pl.BlockSpec((1, tn), lambda j, i: (0, j)))
        else:
            in_specs.append(pl.BlockSpec((tm, LANES), lambda j, i: (i, 0)))
        args.append(arr)
    return pl.pallas_call(
        functools.partial(_proj_body, kind),
        grid=(W // tn, n // tm),
        in_specs=in_specs,
        out_specs=pl.BlockSpec((tm, tn), lambda j, i: (i, j)),
        out_shape=jax.ShapeDtypeStruct((n, W), out_dtype),
        compiler_params=_params("parallel", "parallel"),
        name=name,
    )(*args)


def _rglru_body(xr_ref, gg_ref, cw_ref, cb_ref, wa_ref, ba_ref, wx_ref, bx_ref, lam_ref,
                o_ref, xbuf, hcar, a_s, u_s):
    s = pl.program_id(1)
    ts, D = xr_ref.shape

    @pl.when(s == 0)
    def _():
        xbuf[0:8, :] = jnp.zeros((8, D), F32)
        hcar[...] = jnp.zeros_like(hcar)

    @pl.when(s > 0)
    def _():
        xbuf[0:8, :] = xbuf[ts:ts + 8, :]

    xbuf[8:8 + ts, :] = xr_ref[...]
    xc = cb_ref[...] + cw_ref[0:1, :] * xbuf[8:8 + ts, :]
    for i in range(1, CONV_W):
        xc = xc + cw_ref[i:i + 1, :] * xbuf[8 - i:8 - i + ts, :]

    ra, rx = [], []
    for n in range(RNN_BLOCKS):
        xb = xc[:, n * RNN_BW:(n + 1) * RNN_BW].astype(BF16)
        ra.append(jnp.dot(xb, wa_ref[n], preferred_element_type=F32))
        rx.append(jnp.dot(xb, wx_ref[n], preferred_element_type=F32))
    r = jax.nn.sigmoid(jnp.concatenate(ra, axis=1) + ba_ref[...])
    ig = jax.nn.sigmoid(jnp.concatenate(rx, axis=1) + bx_ref[...])

    z = -lam_ref[...]
    softplus = jnp.maximum(z, 0.0) + jnp.log1p(jnp.exp(-jnp.abs(z)))
    log_a = -LRU_C * r * softplus
    a = jnp.exp(log_a)
    mult = jnp.sqrt(1.0 - a * a)
    row = lax.broadcasted_iota(jnp.int32, (ts, D), 0)
    mult = jnp.where((row == 0) & (s == 0), 1.0, mult)
    u = mult * (ig * xc)

    rm = row & 7
    for d in (1, 2, 4):
        keep = rm >= d
        a_sh = pltpu.roll(a, d, axis=0)
        u_sh = pltpu.roll(u, d, axis=0)
        u = jnp.where(keep, a * u_sh + u, u)
        a = jnp.where(keep, a * a_sh, a)
    a_s[...] = a
    u_s[...] = u

    def group(g, h):
        r0 = pl.multiple_of(g * 8, 8)
        hg = u_s[pl.ds(r0, 8), :] + a_s[pl.ds(r0, 8), :] * h
        u_s[pl.ds(r0, 8), :] = hg
        return hg[7:8, :]

    hcar[...] = lax.fori_loop(0, ts // 8, group, hcar[...])
    o_ref[...] = (u_s[...] * gg_ref[...]).astype(o_ref.dtype)


def _rglru(xr, gg, conv_w, conv_b, w_a, b_a, w_x, b_x, lam, B, S):
    n, D = xr.shape
    ts = 256
    spb = S // ts
    row = lambda b, s: (b * spb + s, 0)
    vec = lambda b, s: (0, 0)
    return pl.pallas_call(
        _rglru_body,
        grid=(B, spb),
        in_specs=[pl.BlockSpec((ts, D), row),
                  pl.BlockSpec((ts, D), row),
                  pl.BlockSpec((CONV_W, D), vec),
                  pl.BlockSpec((1, D), vec),
                  pl.BlockSpec((RNN_BLOCKS, RNN_BW, RNN_BW), lambda b, s: (0, 0, 0)),
                  pl.BlockSpec((1, D), vec),
                  pl.BlockSpec((RNN_BLOCKS, RNN_BW, RNN_BW), lambda b, s: (0, 0, 0)),
                  pl.BlockSpec((1, D), vec),
                  pl.BlockSpec((1, D), vec)],
        out_specs=pl.BlockSpec((ts, D), row),
        out_shape=jax.ShapeDtypeStruct((n, D), BF16),
        scratch_shapes=[pltpu.VMEM((ts + 8, D), F32), pltpu.VMEM((1, D), F32),
                        pltpu.VMEM((ts, D), F32), pltpu.VMEM((ts, D), F32)],
        compiler_params=_params("arbitrary", "arbitrary"),
        name="rglru",
    )(xr, gg, conv_w, conv_b.reshape(1, D), w_a.astype(BF16), b_a.reshape(1, D),
      w_x.astype(BF16), b_x.reshape(1, D), lam.reshape(1, D))


def _moba_body(q_ref, k_ref, v_ref, o_ref, kaug, vaug, kmean, m_s, acc_s):
    qb = pl.program_id(2)
    S = k_ref.shape[0]
    nb = S // MOBA_BLOCK
    bs = MOBA_BLOCK
    dh = HEAD_DIM
    heads = range(MOBA_HEADS)
    hcol = lambda hh: slice(hh * dh, (hh + 1) * dh)

    @pl.when(qb == 0)
    def _():
        blk = lax.broadcasted_iota(jnp.int32, (S, dh), 0) // bs
        col = lax.broadcasted_iota(jnp.int32, (S, dh), 1)
        onehot = jnp.where(col == blk, 1.0, 0.0).astype(BF16)
        for hh in heads:
            kaug[hh, :, 0:dh] = k_ref[:, hcol(hh)].astype(BF16)
            kaug[hh, :, dh:2 * dh] = onehot
            vaug[hh, :, 0:dh] = v_ref[:, hcol(hh)]
            vaug[hh, :, dh:2 * dh] = jnp.ones((S, dh), BF16)
            kmean[hh] = jnp.zeros((LANES, dh), F32)
            for j in range(nb):
                kmean[hh, j:j + 1, :] = jnp.mean(k_ref[j * bs:(j + 1) * bs, hcol(hh)], axis=0, keepdims=True)

    scale = HEAD_DIM ** -0.5
    nt = (((1,), (1,)), ((), ()))
    c2 = scale * LOG2_E
    r0 = pl.multiple_of(qb * bs, bs)
    nb_pad = -(-nb // 8) * 8
    rr = lax.broadcasted_iota(jnp.int32, (bs, bs), 0)
    cc = lax.broadcasted_iota(jnp.int32, (bs, bs), 1)

    qs = [q_ref[:, hcol(hh)] for hh in heads]
    qbfs = [q.astype(BF16) for q in qs]

    blk_i = lax.broadcasted_iota(jnp.int32, (nb_pad, bs), 0)
    blk_f = blk_i.astype(F32)
    past = blk_i < qb
    gs = [jnp.where(past, lax.dot_general(kmean[hh, 0:nb_pad, :], qs[hh], nt, preferred_element_type=F32,
                                          precision=HIGHEST), -jnp.inf) for hh in heads]
    ss = [jnp.where(cc <= rr, lax.dot_general(qbfs[hh], kaug[hh, pl.ds(r0, bs), 0:dh], nt,
                                              preferred_element_type=F32), NEG_BIG) for hh in heads]
    biases = [jnp.full((nb_pad, bs), NEG_BIG, F32) for _ in heads]
    for _ in range(MOBA_TOPK):
        for hh in heads:
            m = jnp.max(gs[hh], axis=0, keepdims=True)
            idx = jnp.min(jnp.where(gs[hh] == m, blk_f, float(LANES)), axis=0, keepdims=True)
            pick = (blk_f == idx) & past
            biases[hh] = jnp.where(pick, 0.0, biases[hh])
            gs[hh] = jnp.where(pick, -jnp.inf, gs[hh])
    fill = jnp.full((bs, dh - nb_pad), NEG_BIG, BF16)
    qaugs = [jnp.concatenate([qbfs[hh], biases[hh].T.astype(BF16), fill], axis=1) for hh in heads]

    ps = []
    for hh in heads:
        m0 = jnp.max(ss[hh], axis=1, keepdims=True)
        ps.append(jnp.exp2((ss[hh] - m0) * c2).astype(BF16))
        m_s[hh] = m0
    for hh in heads:
        acc_s[hh] = jnp.dot(ps[hh], vaug[hh, pl.ds(r0, bs), :], preferred_element_type=F32)

    cw = MOBA_CHUNK * bs
    for c in range(nb // MOBA_CHUNK):
        @pl.when(c * MOBA_CHUNK < qb)
        def _(c=c):
            scs = [lax.dot_general(qaugs[hh], kaug[hh, c * cw:(c + 1) * cw, :], nt, preferred_element_type=F32)
                   for hh in heads]
            pcs, alphas = [], []
            for hh in heads:
                m_old = m_s[hh]
                m_new = jnp.maximum(m_old, jnp.max(scs[hh], axis=1, keepdims=True))
                alphas.append(jnp.exp2((m_old - m_new) * c2))
                pcs.append(jnp.exp2((scs[hh] - m_new) * c2).astype(BF16))
                m_s[hh] = m_new
            for hh in heads:
                acc_s[hh] = alphas[hh] * acc_s[hh] + jnp.dot(pcs[hh], vaug[hh, c * cw:(c + 1) * cw, :],
                                                             preferred_element_type=F32)

    for hh in heads:
        o_ref[:, hcol(hh)] = (acc_s[hh, :, 0:dh] / acc_s[hh, :, dh:dh + 1]).astype(o_ref.dtype)


def _moba(qk, v, B, S):
    n, D = v.shape
    nq = S // MOBA_BLOCK
    hs = MOBA_HEADS
    w = hs * HEAD_DIM
    return pl.pallas_call(
        _moba_body,
        grid=(B, N_HEADS // hs, nq),
        in_specs=[pl.BlockSpec((MOBA_BLOCK, w), lambda b, h, i: (b * nq + i, h)),
                  pl.BlockSpec((S, w), lambda b, h, i: (b, N_HEADS // hs + h)),
                  pl.BlockSpec((S, w), lambda b, h, i: (b, h))],
        out_specs=pl.BlockSpec((MOBA_BLOCK, w), lambda b, h, i: (b * nq + i, h)),
        out_shape=jax.ShapeDtypeStruct((n, D), BF16),
        scratch_shapes=[pltpu.VMEM((hs, S, 2 * HEAD_DIM), BF16), pltpu.VMEM((hs, S, 2 * HEAD_DIM), BF16),
                        pltpu.VMEM((hs, LANES, HEAD_DIM), F32), pltpu.VMEM((hs, MOBA_BLOCK, 1), F32),
                        pltpu.VMEM((hs, MOBA_BLOCK, 2 * HEAD_DIM), F32)],
        compiler_params=_params("parallel", "parallel", "arbitrary"),
        name="moba",
    )(qk, qk, v)


def _merge_body(yr_ref, ya_ref, gr_ref, ga_ref, x_ref, mod_ref, wb_ref, wo_ref, g2_ref, wr_ref, br_ref,
                x1_ref, h2_ref, topi_ref, gate_ref, rank_ref, cnt_ref, carry):
    i = pl.program_id(0)

    @pl.when(i == 0)
    def _():
        carry[...] = jnp.zeros_like(carry)

    tm = MERGE_SUB
    counts = {"total": carry[...]}
    first, second = [
        _merge_stages(slice(h * tm, (h + 1) * tm), counts, yr_ref, ya_ref, gr_ref, ga_ref, x_ref, mod_ref,
                      wb_ref, wo_ref, g2_ref, wr_ref, br_ref, x1_ref, h2_ref, topi_ref, gate_ref, rank_ref)
        for h in range(2)]
    for _ in range(MERGE_MATMUL_STAGES):
        next(first)
    for stage in first:
        next(second)
    for stage in second:
        pass
    carry[...] = counts["total"]
    cnt_ref[...] = jnp.broadcast_to(counts["total"], cnt_ref.shape).astype(jnp.int32)


MERGE_MATMUL_STAGES = 4


def _merge_stages(rows, counts, yr_ref, ya_ref, gr_ref, ga_ref, x_ref, mod_ref, wb_ref, wo_ref, g2_ref, wr_ref,
                  br_ref, x1_ref, h2_ref, topi_ref, gate_ref, rank_ref):
    tm = rows.stop - rows.start
    zr = jnp.dot(yr_ref[rows, :], wb_ref[0], preferred_element_type=F32)
    yield
    za = jnp.dot(ya_ref[rows, :], wb_ref[1], preferred_element_type=F32)
    mix = (gr_ref[rows, :] * zr + ga_ref[rows, :] * za).astype(BF16)
    yield
    mixed = jnp.dot(mix, wo_ref[...], preferred_element_type=F32)
    x1 = x_ref[rows, :] + mod_ref[0, 2:3, :] * mixed
    x1_ref[rows, :] = x1
    h2 = _norm_mod(x1, g2_ref[...], mod_ref[0, 3:4, :], mod_ref[0, 4:5, :])
    h2_ref[rows, :] = h2
    yield

    hi = h2.astype(BF16)
    lo = (h2 - hi.astype(F32)).astype(BF16)
    both = jnp.dot(hi, wr_ref[...], preferred_element_type=F32)
    logits = (both[:, :LANES] + both[:, LANES:]
              + jnp.dot(lo, wr_ref[:, :LANES], preferred_element_type=F32) + br_ref[...])
    yield

    lane = lax.broadcasted_iota(jnp.int32, logits.shape, 1)
    lane_f = lane.astype(F32)
    lg = jnp.where(lane < N_EXPERTS, logits, -jnp.inf)
    vals, idxs = [], []
    onehot = jnp.zeros(logits.shape, F32)
    for _ in range(TOP_K):
        m = jnp.max(lg, axis=1, keepdims=True)
        idx = jnp.min(jnp.where(lg == m, lane_f, float(LANES)), axis=1, keepdims=True)
        pick = lane_f == idx
        onehot = jnp.where(pick, 1.0, onehot)
        lg = jnp.where(pick, -jnp.inf, lg)
        vals.append(m)
        idxs.append(idx)
        yield
    es = [jnp.exp(v - vals[0]) for v in vals]
    den = es[0] + es[1] + es[2] + es[3]
    before0 = counts["total"]

    rr = lax.broadcasted_iota(jnp.int32, (tm, tm), 0)
    cc = lax.broadcasted_iota(jnp.int32, (tm, tm), 1)
    tri = jnp.where(cc < rr, 1.0, 0.0).astype(BF16)
    before = jnp.dot(tri, onehot.astype(BF16), preferred_element_type=F32) + before0

    topi = jnp.zeros(logits.shape, F32)
    gates = jnp.zeros(logits.shape, F32)
    rank = jnp.zeros(logits.shape, F32)
    for k in range(TOP_K):
        rk = jnp.sum(jnp.where(lane_f == idxs[k], before, 0.0), axis=1, keepdims=True)
        topi = jnp.where(lane == k, idxs[k], topi)
        gates = jnp.where(lane == k, es[k] / den, gates)
        rank = jnp.where(lane == k, rk, rank)
    topi_ref[rows, :] = topi.astype(jnp.int32)
    gate_ref[rows, :] = gates
    rank_ref[rows, :] = rank.astype(jnp.int32)
    counts["total"] = before0 + jnp.sum(onehot, axis=0, keepdims=True)
    yield


def _merge(y_rnn, y_att, gl, x2, mod3, w_branch, w_out, norm2_g, w_router, b_router, rows_per_batch):
    n, D = x2.shape
    tm = 2 * MERGE_SUB
    tpb = rows_per_batch // tm
    wr32 = jnp.zeros((D, LANES), F32).at[:, :N_EXPERTS].set(w_router)
    wr_hi = wr32.astype(BF16)
    wr = jnp.concatenate([wr_hi, (wr32 - wr_hi.astype(F32)).astype(BF16)], axis=1)
    br = jnp.zeros((1, LANES), F32).at[0, :N_EXPERTS].set(b_router)
    row = lambda i: (i, 0)
    fixed = lambda i: (0, 0)
    once = pl.Buffered(1)
    wide = jax.ShapeDtypeStruct((n, D), F32)
    narrow_i = jax.ShapeDtypeStruct((n, LANES), jnp.int32)
    return pl.pallas_call(
        _merge_body,
        grid=(n // tm,),
        in_specs=[pl.BlockSpec((tm, D), row),
                  pl.BlockSpec((tm, D), row),
                  pl.BlockSpec((tm, D), lambda i: (i, 0)),
                  pl.BlockSpec((tm, D), lambda i: (i, 1)),
                  pl.BlockSpec((tm, D), row),
                  pl.BlockSpec((1, 6, D), lambda i: (i // tpb, 0, 0)),
                  pl.BlockSpec((2, D, D), lambda i: (0, 0, 0), pipeline_mode=once),
                  pl.BlockSpec((D, D), fixed, pipeline_mode=once),
                  pl.BlockSpec((1, D), fixed),
                  pl.BlockSpec((D, 2 * LANES), fixed, pipeline_mode=once),
                  pl.BlockSpec((1, LANES), fixed)],
        out_specs=[pl.BlockSpec((tm, D), row),
                   pl.BlockSpec((tm, D), row),
                   pl.BlockSpec((tm, LANES), row),
                   pl.BlockSpec((tm, LANES), row),
                   pl.BlockSpec((tm, LANES), row),
                   pl.BlockSpec((8, LANES), fixed)],
        out_shape=[wide, wide, narrow_i, jax.ShapeDtypeStruct((n, LANES), F32), narrow_i,
                   jax.ShapeDtypeStruct((8, LANES), jnp.int32)],
        scratch_shapes=[pltpu.VMEM((1, LANES), F32)],
        compiler_params=_params("arbitrary"),
        name="merge",
    )(y_rnn, y_att, gl, gl, x2, mod3, w_branch.astype(BF16), w_out.astype(BF16),
      norm2_g.reshape(1, D), wr, br)


DISPATCH_TOKENS = 256
ROW_DMA_UNROLL = 8


def _dispatch_body(dest_ref, zblk_ref, src_ref, dst_hbm, zbuf, sem, zsem):
    t = DISPATCH_TOKENS
    c = pl.program_id(0)

    @pl.when(c == 0)
    def _():
        zbuf[...] = jnp.zeros_like(zbuf)

        def zero_copy(b):
            r0 = pl.multiple_of(b * EXPERT_ROWS, EXPERT_ROWS)
            return pltpu.make_async_copy(zbuf, dst_hbm.at[pl.ds(r0, EXPERT_ROWS), :], zsem)

        def zissue(j, carry):
            @pl.when(zblk_ref[j] >= 0)
            def _():
                zero_copy(zblk_ref[j]).start()
            return carry

        def zwait(j, carry):
            @pl.when(zblk_ref[j] >= 0)
            def _():
                zero_copy(zblk_ref[j]).wait()
            return carry

        lax.fori_loop(0, zblk_ref.shape[0], zissue, 0)
        lax.fori_loop(0, zblk_ref.shape[0], zwait, 0)

    def issue(g, carry):
        for u in range(ROW_DMA_UNROLL):
            r = g * ROW_DMA_UNROLL + u
            for k in range(TOP_K):
                d = dest_ref[(c * t + r) * TOP_K + k]
                pltpu.make_async_copy(src_ref.at[pl.ds(r, 1), :], dst_hbm.at[pl.ds(d, 1), :], sem).start()
        return carry

    lax.fori_loop(0, t // ROW_DMA_UNROLL, issue, 0)
    for k in range(TOP_K):
        pltpu.make_async_copy(src_ref, dst_hbm.at[pl.ds(0, t), :], sem).wait()


def _dispatch(h2, dest, zero_blocks, n_rows):
    n, D = h2.shape
    return pl.pallas_call(
        _dispatch_body,
        grid_spec=pltpu.PrefetchScalarGridSpec(
            num_scalar_prefetch=2,
            grid=(n // DISPATCH_TOKENS,),
            in_specs=[pl.BlockSpec((DISPATCH_TOKENS, D), lambda i, d, z: (i, 0))],
            out_specs=pl.BlockSpec(memory_space=pl.ANY),
            scratch_shapes=[pltpu.VMEM((EXPERT_ROWS, D), F32), pltpu.SemaphoreType.DMA,
                            pltpu.SemaphoreType.DMA],
        ),
        out_shape=jax.ShapeDtypeStruct((n_rows, D), F32),
        compiler_params=_params("arbitrary"),
        name="dispatch",
    )(dest, zero_blocks, h2)


def _experts_body(bexp_ref, nused_ref, first_ref, slot_ref, next_ref, x_ref, wu_hbm, bu_ref, wd_hbm, bd_ref,
                  o_ref, wu_buf, wd_buf, wu_bf, wd_bf, sems):
    i = pl.program_id(0)
    dff = wd_hbm.shape[1]
    used = i < nused_ref[0]

    def weight_copies(e, s):
        return (pltpu.make_async_copy(wu_hbm.at[e], wu_buf.at[s], sems.at[0, s]),
                pltpu.make_async_copy(wd_hbm.at[e], wd_buf.at[s], sems.at[1, s]))

    @pl.when(i == 0)
    def _():
        for cp in weight_copies(bexp_ref[0], 0):
            cp.start()

    @pl.when(used & (first_ref[i] == 1))
    def _():
        s = slot_ref[i]
        for cp in weight_copies(bexp_ref[i], s):
            cp.wait()
        wu_bf[...] = wu_buf[s].astype(BF16)
        wd_bf[...] = wd_buf[s].astype(BF16)

        @pl.when(next_ref[i] >= 0)
        def _():
            for cp in weight_copies(next_ref[i], 1 - s):
                cp.start()

    @pl.when(used)
    def _():
        hc = jnp.dot(x_ref[...].astype(BF16), wu_bf[...], preferred_element_type=F32) + bu_ref[0]
        g = jnp.minimum(hc[:, :dff], SWIGLU_LIMIT)
        lin = jnp.clip(hc[:, dff:], -SWIGLU_LIMIT, SWIGLU_LIMIT)
        act = (lin + 1.0) * g * jax.nn.sigmoid(SWIGLU_ALPHA * g)
        o_ref[...] = jnp.dot(act.astype(BF16), wd_bf[...], preferred_element_type=F32) + bd_ref[0]

    @pl.when(i >= nused_ref[0])
    def _():
        o_ref[...] = jnp.zeros_like(o_ref)


def _experts(x_rows, block_exp, n_used, w_up, b_up, w_down, b_down, n_blocks):
    R, D = x_rows.shape
    E, _, F2 = w_up.shape
    dff = w_down.shape[1]
    bidx = jnp.arange(n_blocks, dtype=jnp.int32)
    first = ((bidx == 0) | (block_exp != jnp.roll(block_exp, 1))) & (bidx < n_used[0])
    slot = (jnp.cumsum(first) - 1) % 2
    first_pos = jnp.where(first, bidx, n_blocks)
    next_pos = jnp.flip(lax.cummin(jnp.flip(jnp.roll(first_pos, -1).at[-1].set(n_blocks))))
    next_exp = jnp.where(next_pos < n_blocks, block_exp[jnp.minimum(next_pos, n_blocks - 1)], -1)
    blk = lambda i, *_: (i, 0)
    bsel = lambda i, be, *_: (be[i], 0, 0)
    return pl.pallas_call(
        _experts_body,
        grid_spec=pltpu.PrefetchScalarGridSpec(
            num_scalar_prefetch=5,
            grid=(n_blocks,),
            in_specs=[pl.BlockSpec((EXPERT_ROWS, D), blk),
                      pl.BlockSpec(memory_space=pl.ANY),
                      pl.BlockSpec((1, 1, F2), bsel),
                      pl.BlockSpec(memory_space=pl.ANY),
                      pl.BlockSpec((1, 1, D), bsel)],
            out_specs=pl.BlockSpec((EXPERT_ROWS, D), blk),
            scratch_shapes=[pltpu.VMEM((2, D, F2), F32), pltpu.VMEM((2, dff, D), F32),
                            pltpu.VMEM((D, F2), BF16), pltpu.VMEM((dff, D), BF16),
                            pltpu.SemaphoreType.DMA((2, 2))],
        ),
        out_shape=jax.ShapeDtypeStruct((R, D), F32),
        compiler_params=_params("arbitrary"),
        name="experts",
    )(block_exp, n_used, first.astype(jnp.int32), slot.astype(jnp.int32), next_exp.astype(jnp.int32),
      x_rows, w_up, b_up.reshape(E, 1, F2), w_down, b_down.reshape(E, 1, D))


COMBINE_ROWS = 128


def _combine_body(dest_ref, y_hbm, x1_ref, gate_ref, mod_ref, o_ref, buf, sem):
    i = pl.program_id(0)
    t = COMBINE_ROWS

    def issue(g, carry):
        for u in range(ROW_DMA_UNROLL):
            r = g * ROW_DMA_UNROLL + u
            for k in range(TOP_K):
                d = dest_ref[(i * t + r) * TOP_K + k]
                pltpu.make_async_copy(y_hbm.at[pl.ds(d, 1), :], buf.at[pl.ds(r, 1), pl.ds(k * D, D)], sem).start()
        return carry

    D = o_ref.shape[1]
    lax.fori_loop(0, t // ROW_DMA_UNROLL, issue, 0)
    for k in range(TOP_K):
        pltpu.make_async_copy(y_hbm.at[pl.ds(0, t), :], buf.at[:, pl.ds(k * D, D)], sem).wait()
    gates = gate_ref[...]
    y = gates[:, 0:1] * buf[:, 0:D]
    for k in range(1, TOP_K):
        y = y + gates[:, k:k + 1] * buf[:, k * D:(k + 1) * D]
    o_ref[...] = x1_ref[...] + mod_ref[0, 5:6, :] * y


def _combine(y_rows, dest, x1, gates, mod3, rows_per_batch):
    n, D = x1.shape
    t = COMBINE_ROWS
    tpb = rows_per_batch // t
    return pl.pallas_call(
        _combine_body,
        grid_spec=pltpu.PrefetchScalarGridSpec(
            num_scalar_prefetch=1,
            grid=(n // t,),
            in_specs=[pl.BlockSpec(memory_space=pl.ANY),
                      pl.BlockSpec((t, D), lambda i, d: (i, 0)),
                      pl.BlockSpec((t, LANES), lambda i, d: (i, 0)),
                      pl.BlockSpec((1, 6, D), lambda i, d: (i // tpb, 0, 0))],
            out_specs=pl.BlockSpec((t, D), lambda i, d: (i, 0)),
            scratch_shapes=[pltpu.VMEM((t, TOP_K * D), F32), pltpu.SemaphoreType.DMA],
        ),
        out_shape=jax.ShapeDtypeStruct((n, D), F32),
        compiler_params=_params("arbitrary"),
        name="combine",
    )(dest, y_rows, x1, gates, mod3)


def kernel(x, c, positions, ada_w, ada_b, norm1_g, norm2_g, w_in, conv_w, conv_b, w_rg_a, b_rg_a, w_rg_x, b_rg_x, lru_lambda, q_norm_g, k_norm_g, b_gate, w_branch, w_out, w_router, b_router, w_up, b_up, w_down, b_down):
    B, S, D = x.shape
    n = B * S
    depth = ada_w.shape[0]
    x2 = x.reshape(n, D)
    cos_t, sin_t = _rope_tables(positions)
    perm = _rope_head_perm()
    head_cols = (jnp.arange(2 * N_HEADS)[:, None] * HEAD_DIM + perm[None, :]).reshape(-1)
    for l in range(depth):
        mod3 = _adaln(c, ada_w[l], ada_b[l]).reshape(B, 6, D)
        h1 = _norm_mod_call(x2, mod3, norm1_g[l], S)
        w = w_in[l].astype(BF16)
        xr = _proj("plain", h1, w[:, 0:D], [], F32, "proj_xr")
        gg = _proj("gelu", h1, w[:, D:2 * D], [], F32, "proj_gr")
        qk_g = jnp.concatenate([jnp.tile(q_norm_g[l][perm], N_HEADS),
                                jnp.tile(k_norm_g[l][perm], N_HEADS)]).reshape(1, 2 * D)
        qk = _proj("qk", h1, w[:, 2 * D:4 * D][:, head_cols], [qk_g, cos_t, sin_t], F32, "proj_qk")
        v = _proj("plain", h1, w[:, 4 * D:5 * D], [], BF16, "proj_v")
        gl = _proj("gate", h1, w[:, 5 * D:7 * D], [b_gate[l].reshape(1, 2 * D)], F32, "proj_gl")

        y_rnn = _rglru(xr, gg, conv_w[l], conv_b[l], w_rg_a[l], b_rg_a[l], w_rg_x[l], b_rg_x[l],
                       lru_lambda[l], B, S)
        y_att = _moba(qk, v, B, S)

        x1, h2, topi, gates, rank, cnt = _merge(y_rnn, y_att, gl, x2, mod3, w_branch[l], w_out[l],
                                                norm2_g[l], w_router[l], b_router[l], S)

        counts = cnt[0, :N_EXPERTS]
        padded = (counts + EXPERT_ROWS - 1) // EXPERT_ROWS * EXPERT_ROWS
        pad_end = jnp.cumsum(padded)
        pad_start = pad_end - padded
        top_idx = topi[:, :TOP_K]
        dest = (pad_start[top_idx] + rank[:, :TOP_K]).reshape(n * TOP_K).astype(jnp.int32)
        n_blocks = (n * TOP_K) // EXPERT_ROWS + N_EXPERTS
        block_start = jnp.arange(n_blocks, dtype=jnp.int32) * EXPERT_ROWS
        block_exp = jnp.minimum(jnp.sum(block_start[:, None] >= pad_end[None, :], axis=1),
                                N_EXPERTS - 1).astype(jnp.int32)
        n_used = (pad_end[-1] // EXPERT_ROWS).astype(jnp.int32)
        last_blk = jnp.where(padded > 0, pad_end // EXPERT_ROWS - 1, -1)
        spare = n_used + jnp.arange(N_EXPERTS, dtype=jnp.int32)
        spare = jnp.where(spare < n_blocks, spare, -1)
        zero_blocks = jnp.concatenate([last_blk, spare]).astype(jnp.int32)

        x_rows = _dispatch(h2, dest, zero_blocks, n_blocks * EXPERT_ROWS)
        y_rows = _experts(x_rows, block_exp, n_used.reshape(1), w_up[l], b_up[l], w_down[l], b_down[l], n_blocks)
        x2 = _combine(y_rows, dest, x1, gates, mod3, S)
    return x2.reshape(B, S, D)
```

```python
import functools

import jax
import jax.numpy as jnp
from jax import lax
from jax.experimental import pallas as pl
from jax.experimental.pallas import tpu as pltpu

F32 = jnp.float32
BF16 = jnp.bfloat16
HIGHEST = lax.Precision.HIGHEST

EPS = 1e-6
LANES = 128
N_HEADS = 8
HEAD_DIM = 128
RNN_BLOCKS = 8
RNN_BW = 128
CONV_W = 4
LRU_C = 8.0
MOBA_BLOCK = 256
MOBA_TOPK = 3
MOBA_CHUNK = 4
MOBA_HEADS = 4
MERGE_SUB = 256
LOG2_E = 1.4426950408889634
ROPE_DIMS = HEAD_DIM // 4
ROPE_THETA = 500000.0
N_EXPERTS = 32
TOP_K = 4
SWIGLU_LIMIT = 7.0
SWIGLU_ALPHA = 1.702
EXPERT_ROWS = 256
NEG_BIG = -1e30

VMEM_LIMIT = 56 * 1024 * 1024


def _params(*sem):
    return pltpu.CompilerParams(dimension_semantics=sem, vmem_limit_bytes=VMEM_LIMIT)


def _adaln_body(c_ref, w_ref, b_ref, o_ref):
    cs = c_ref[...]
    cs = cs * jax.nn.sigmoid(cs)
    o_ref[...] = jnp.dot(cs, w_ref[...], preferred_element_type=F32, precision=HIGHEST) + b_ref[...]


def _adaln(c, ada_w, ada_b):
    B, D = c.shape
    W = ada_w.shape[1]
    cpad = jnp.zeros((8, D), F32).at[:B].set(c)
    tn = 1024
    mod = pl.pallas_call(
        _adaln_body,
        grid=(W // tn,),
        in_specs=[pl.BlockSpec((8, D), lambda j: (0, 0)),
                  pl.BlockSpec((D, tn), lambda j: (0, j)),
                  pl.BlockSpec((1, tn), lambda j: (0, j))],
        out_specs=pl.BlockSpec((8, tn), lambda j: (0, j)),
        out_shape=jax.ShapeDtypeStruct((8, W), F32),
        compiler_params=_params("parallel"),
        name="adaln",
    )(cpad, ada_w, ada_b.reshape(1, W))
    return mod[:B]


def _rope_head_perm():
    half = ROPE_DIMS // 2
    mid = HEAD_DIM // 2
    return jnp.concatenate([jnp.arange(0, half), jnp.arange(ROPE_DIMS, mid + half),
                            jnp.arange(half, ROPE_DIMS), jnp.arange(mid + half, HEAD_DIM)])


def _rope_body(pos_ref, freq_ref, c_ref, s_ref):
    ang = pos_ref[...].astype(F32) * freq_ref[...]
    lane = lax.broadcasted_iota(jnp.int32, ang.shape, 1)
    half = ROPE_DIMS // 2
    mid = HEAD_DIM // 2
    s = jnp.sin(ang)
    c_ref[...] = jnp.cos(ang)
    s_ref[...] = jnp.where(lane < half, -s, jnp.where((lane >= mid) & (lane < mid + half), s, 0.0))


def _rope_tables(positions):
    n = positions.size
    half = ROPE_DIMS // 2
    mid = HEAD_DIM // 2
    freqs = ROPE_THETA ** (-jnp.arange(half, dtype=F32) / half)
    freq_lane = jnp.zeros((1, LANES), F32).at[0, :half].set(freqs).at[0, mid:mid + half].set(freqs)
    tm = 1024
    tab = jax.ShapeDtypeStruct((n, LANES), F32)
    return pl.pallas_call(
        _rope_body,
        grid=(n // tm,),
        in_specs=[pl.BlockSpec((tm, 1), lambda i: (i, 0)),
                  pl.BlockSpec((1, LANES), lambda i: (0, 0))],
        out_specs=[pl.BlockSpec((tm, LANES), lambda i: (i, 0))] * 2,
        out_shape=[tab, tab],
        compiler_params=_params("parallel"),
        name="rope_tab",
    )(positions.reshape(n, 1), freq_lane)


def _norm_mod(x, g, sh, sc):
    ms = jnp.mean(x * x, axis=-1, keepdims=True)
    y = x * lax.rsqrt(ms + EPS)
    return (y * g) * (1.0 + sc) + sh


def _norm_mod_body(x_ref, mod_ref, g_ref, o_ref):
    o_ref[...] = _norm_mod(x_ref[...], g_ref[...], mod_ref[0, 0:1, :], mod_ref[0, 1:2, :]).astype(o_ref.dtype)


def _norm_mod_call(x2, mod3, g, rows_per_batch):
    n, D = x2.shape
    tm = 512
    tpb = rows_per_batch // tm
    return pl.pallas_call(
        _norm_mod_body,
        grid=(n // tm,),
        in_specs=[pl.BlockSpec((tm, D), lambda i: (i, 0)),
                  pl.BlockSpec((1, 6, D), lambda i: (i // tpb, 0, 0)),
                  pl.BlockSpec((1, D), lambda i: (0, 0))],
        out_specs=pl.BlockSpec((tm, D), lambda i: (i, 0)),
        out_shape=jax.ShapeDtypeStruct((n, D), BF16),
        compiler_params=_params("parallel"),
        name="norm_mod",
    )(x2, mod3, g.reshape(1, D))


def _gelu_tanh(x):
    return 0.5 * x * (1.0 + jnp.tanh(0.7978845608028654 * (x + 0.044715 * (x * x * x))))


PROJ_SUB = 256
PROJ_PIECE = 256


def _qk_head(seg, g, cos, sin):
    ms = jnp.mean(seg * seg, axis=-1, keepdims=True)
    y = seg * lax.rsqrt(ms + EPS) * g
    return y * cos + pltpu.roll(y, HEAD_DIM // 2, axis=1) * sin


def _proj_stages(kind, rows, h_ref, w_ref, extras, o_ref):
    n_pieces = w_ref.shape[1] // PROJ_PIECE
    pieces = []
    for j in range(n_pieces):
        pieces.append(jnp.dot(h_ref[rows, :], w_ref[:, j * PROJ_PIECE:(j + 1) * PROJ_PIECE],
                              preferred_element_type=F32))
        yield
    for j in range(n_pieces):
        cols = slice(j * PROJ_PIECE, (j + 1) * PROJ_PIECE)
        acc = pieces[j]
        if kind == "gelu":
            acc = _gelu_tanh(acc)
        elif kind == "gate":
            (b_ref,) = extras
            acc = jax.nn.sigmoid(acc + b_ref[:, cols])
        elif kind == "qk":
            g_ref, c_ref, s_ref = extras
            cos, sin = c_ref[rows, :], s_ref[rows, :]
            acc = jnp.concatenate(
                [_qk_head(acc[:, o:o + HEAD_DIM], g_ref[:, cols.start + o:cols.start + o + HEAD_DIM], cos, sin)
                 for o in range(0, PROJ_PIECE, HEAD_DIM)], axis=1)
        o_ref[rows, cols] = acc.astype(o_ref.dtype)
        yield


def _proj_body(kind, h_ref, w_ref, *rest):
    *extras, o_ref = rest
    first, second = [_proj_stages(kind, slice(t * PROJ_SUB, (t + 1) * PROJ_SUB), h_ref, w_ref, extras, o_ref)
                     for t in range(2)]
    n_pieces = w_ref.shape[1] // PROJ_PIECE
    for _ in range(n_pieces):
        next(first)
    for _ in range(n_pieces):
        next(second)
        next(first)
    for stage in second:
        pass


def _proj(kind, h, w, extras, out_dtype, name):
    n, D = h.shape
    W = w.shape[1]
    tm, tn = 512, 1024
    in_specs = [pl.BlockSpec((tm, D), lambda j, i: (i, 0)),
                pl.BlockSpec((D, tn), lambda j, i: (0, j))]
    args = [h, w]
    for arr in extras:
        if arr.shape[0] == 1:
            in_specs.append(pl.BlockSpec((1, tn), lambda j, i: (0, j)))
        else:
            in_specs.append(pl.BlockSpec((tm, LANES), lambda j, i: (i, 0)))
        args.append(arr)
    return pl.pallas_call(
        functools.partial(_proj_body, kind),
        grid=(W // tn, n // tm),
        in_specs=in_specs,
        out_specs=pl.BlockSpec((tm, tn), lambda j, i: (i, j)),
        out_shape=jax.ShapeDtypeStruct((n, W), out_dtype),
        compiler_params=_params("parallel", "parallel"),
        name=name,
    )(*args)


def _rglru_body(xr_ref, gg_ref, cw_ref, cb_ref, wa_ref, ba_ref, wx_ref, bx_ref, lam_ref,
                o_ref, xbuf, hcar, a_s, u_s):
    s = pl.program_id(1)
    ts, D = xr_ref.shape

    @pl.when(s == 0)
    def _():
        xbuf[0:8, :] = jnp.zeros((8, D), F32)
        hcar[...] = jnp.zeros_like(hcar)

    @pl.when(s > 0)
    def _():
        xbuf[0:8, :] = xbuf[ts:ts + 8, :]

    xbuf[8:8 + ts, :] = xr_ref[...]
    xc = cb_ref[...] + cw_ref[0:1, :] * xbuf[8:8 + ts, :]
    for i in range(1, CONV_W):
        xc = xc + cw_ref[i:i + 1, :] * xbuf[8 - i:8 - i + ts, :]

    ra, rx = [], []
    for n in range(RNN_BLOCKS):
        xb = xc[:, n * RNN_BW:(n + 1) * RNN_BW].astype(BF16)
        ra.append(jnp.dot(xb, wa_ref[n], preferred_element_type=F32))
        rx.append(jnp.dot(xb, wx_ref[n], preferred_element_type=F32))
    r = jax.nn.sigmoid(jnp.concatenate(ra, axis=1) + ba_ref[...])
    ig = jax.nn.sigmoid(jnp.concatenate(rx, axis=1) + bx_ref[...])

    z = -lam_ref[...]
    softplus = jnp.maximum(z, 0.0) + jnp.log1p(jnp.exp(-jnp.abs(z)))
    log_a = -LRU_C * r * softplus
    a = jnp.exp(log_a)
    mult = jnp.sqrt(1.0 - a * a)
    row = lax.broadcasted_iota(jnp.int32, (ts, D), 0)
    mult = jnp.where((row == 0) & (s == 0), 1.0, mult)
    u = mult * (ig * xc)

    rm = row & 7
    for d in (1, 2, 4):
        keep = rm >= d
        a_sh = pltpu.roll(a, d, axis=0)
        u_sh = pltpu.roll(u, d, axis=0)
        u = jnp.where(keep, a * u_sh + u, u)
        a = jnp.where(keep, a * a_sh, a)
    a_s[...] = a
    u_s[...] = u

    def group(g, h):
        r0 = pl.multiple_of(g * 8, 8)
        hg = u_s[pl.ds(r0, 8), :] + a_s[pl.ds(r0, 8), :] * h
        u_s[pl.ds(r0, 8), :] = hg
        return hg[7:8, :]

    hcar[...] = lax.fori_loop(0, ts // 8, group, hcar[...])
    o_ref[...] = (u_s[...] * gg_ref[...]).astype(o_ref.dtype)


def _rglru(xr, gg, conv_w, conv_b, w_a, b_a, w_x, b_x, lam, B, S):
    n, D = xr.shape
    ts = 256
    spb = S // ts
    row = lambda b, s: (b * spb + s, 0)
    vec = lambda b, s: (0, 0)
    return pl.pallas_call(
        _rglru_body,
        grid=(B, spb),
        in_specs=[pl.BlockSpec((ts, D), row),
                  pl.BlockSpec((ts, D), row),
                  pl.BlockSpec((CONV_W, D), vec),
                  pl.BlockSpec((1, D), vec),
                  pl.BlockSpec((RNN_BLOCKS, RNN_BW, RNN_BW), lambda b, s: (0, 0, 0)),
                  pl.BlockSpec((1, D), vec),
                  pl.BlockSpec((RNN_BLOCKS, RNN_BW, RNN_BW), lambda b, s: (0, 0, 0)),
                  pl.BlockSpec((1, D), vec),
                  pl.BlockSpec((1, D), vec)],
        out_specs=pl.BlockSpec((ts, D), row),
        out_shape=jax.ShapeDtypeStruct((n, D), BF16),
        scratch_shapes=[pltpu.VMEM((ts + 8, D), F32), pltpu.VMEM((1, D), F32),
                        pltpu.VMEM((ts, D), F32), pltpu.VMEM((ts, D), F32)],
        compiler_params=_params("arbitrary", "arbitrary"),
        name="rglru",
    )(xr, gg, conv_w, conv_b.reshape(1, D), w_a.astype(BF16), b_a.reshape(1, D),
      w_x.astype(BF16), b_x.reshape(1, D), lam.reshape(1, D))


def _moba_body(q_ref, k_ref, v_ref, o_ref, kaug, vaug, kmean, m_s, acc_s):
    qb = pl.program_id(2)
    S = k_ref.shape[0]
    nb = S // MOBA_BLOCK
    bs = MOBA_BLOCK
    dh = HEAD_DIM
    heads = range(MOBA_HEADS)
    hcol = lambda hh: slice(hh * dh, (hh + 1) * dh)

    @pl.when(qb == 0)
    def _():
        blk = lax.broadcasted_iota(jnp.int32, (S, dh), 0) // bs
        col = lax.broadcasted_iota(jnp.int32, (S, dh), 1)
        onehot = jnp.where(col == blk, 1.0, 0.0).astype(BF16)
        for hh in heads:
            kaug[hh, :, 0:dh] = k_ref[:, hcol(hh)].astype(BF16)
            kaug[hh, :, dh:2 * dh] = onehot
            vaug[hh, :, 0:dh] = v_ref[:, hcol(hh)]
            vaug[hh, :, dh:2 * dh] = jnp.ones((S, dh), BF16)
            kmean[hh] = jnp.zeros((LANES, dh), F32)
            for j in range(nb):
                kmean[hh, j:j + 1, :] = jnp.mean(k_ref[j * bs:(j + 1) * bs, hcol(hh)], axis=0, keepdims=True)

    scale = HEAD_DIM ** -0.5
    nt = (((1,), (1,)), ((), ()))
    c2 = scale * LOG2_E
    r0 = pl.multiple_of(qb * bs, bs)
    nb_pad = -(-nb // 8) * 8
    rr = lax.broadcasted_iota(jnp.int32, (bs, bs), 0)
    cc = lax.broadcasted_iota(jnp.int32, (bs, bs), 1)

    qs = [q_ref[:, hcol(hh)] for hh in heads]
    qbfs = [q.astype(BF16) for q in qs]

    blk_i = lax.broadcasted_iota(jnp.int32, (nb_pad, bs), 0)
    blk_f = blk_i.astype(F32)
    past = blk_i < qb
    gs = [jnp.where(past, lax.dot_general(kmean[hh, 0:nb_pad, :], qs[hh], nt, preferred_element_type=F32,
                                          precision=HIGHEST), -jnp.inf) for hh in heads]
    ss = [jnp.where(cc <= rr, lax.dot_general(qbfs[hh], kaug[hh, pl.ds(r0, bs), 0:dh], nt,
                                              preferred_element_type=F32), NEG_BIG) for hh in heads]
    biases = [jnp.full((nb_pad, bs), NEG_BIG, F32) for _ in heads]
    for _ in range(MOBA_TOPK):
        for hh in heads:
            m = jnp.max(gs[hh], axis=0, keepdims=True)
            idx = jnp.min(jnp.where(gs[hh] == m, blk_f, float(LANES)), axis=0, keepdims=True)
            pick = (blk_f == idx) & past
            biases[hh] = jnp.where(pick, 0.0, biases[hh])
            gs[hh] = jnp.where(pick, -jnp.inf, gs[hh])
    fill = jnp.full((bs, dh - nb_pad), NEG_BIG, BF16)
    qaugs = [jnp.concatenate([qbfs[hh], biases[hh].T.astype(BF16), fill], axis=1) for hh in heads]

    ps = []
    for hh in heads:
        m0 = jnp.max(ss[hh], axis=1, keepdims=True)
        ps.append(jnp.exp2((ss[hh] - m0) * c2).astype(BF16))
        m_s[hh] = m0
    for hh in heads:
        acc_s[hh] = jnp.dot(ps[hh], vaug[hh, pl.ds(r0, bs), :], preferred_element_type=F32)

    cw = MOBA_CHUNK * bs
    for c in range(nb // MOBA_CHUNK):
        @pl.when(c * MOBA_CHUNK < qb)
        def _(c=c):
            scs = [lax.dot_general(qaugs[hh], kaug[hh, c * cw:(c + 1) * cw, :], nt, preferred_element_type=F32)
                   for hh in heads]
            pcs, alphas = [], []
            for hh in heads:
                m_old = m_s[hh]
                m_new = jnp.maximum(m_old, jnp.max(scs[hh], axis=1, keepdims=True))
                alphas.append(jnp.exp2((m_old - m_new) * c2))
                pcs.append(jnp.exp2((scs[hh] - m_new) * c2).astype(BF16))
                m_s[hh] = m_new
            for hh in heads:
                acc_s[hh] = alphas[hh] * acc_s[hh] + jnp.dot(pcs[hh], vaug[hh, c * cw:(c + 1) * cw, :],
                                                             preferred_element_type=F32)

    for hh in heads:
        o_ref[:, hcol(hh)] = (acc_s[hh, :, 0:dh] / acc_s[hh, :, dh:dh + 1]).astype(o_ref.dtype)


def _moba(qk, v, B, S):
    n, D = v.shape
    nq = S // MOBA_BLOCK
    hs = MOBA_HEADS
    w = hs * HEAD_DIM
    return pl.pallas_call(
        _moba_body,
        grid=(B, N_HEADS // hs, nq),
        in_specs=[pl.BlockSpec((MOBA_BLOCK, w), lambda b, h, i: (b * nq + i, h)),
                  pl.BlockSpec((S, w), lambda b, h, i: (b, N_HEADS // hs + h)),
                  pl.BlockSpec((S, w), lambda b, h, i: (b, h))],
        out_specs=pl.BlockSpec((MOBA_BLOCK, w), lambda b, h, i: (b * nq + i, h)),
        out_shape=jax.ShapeDtypeStruct((n, D), BF16),
        scratch_shapes=[pltpu.VMEM((hs, S, 2 * HEAD_DIM), BF16), pltpu.VMEM((hs, S, 2 * HEAD_DIM), BF16),
                        pltpu.VMEM((hs, LANES, HEAD_DIM), F32), pltpu.VMEM((hs, MOBA_BLOCK, 1), F32),
                        pltpu.VMEM((hs, MOBA_BLOCK, 2 * HEAD_DIM), F32)],
        compiler_params=_params("parallel", "parallel", "arbitrary"),
        name="moba",
    )(qk, qk, v)


def _merge_body(yr_ref, ya_ref, gr_ref, ga_ref, x_ref, mod_ref, wb_ref, wo_ref, g2_ref, wr_ref, br_ref,
                x1_ref, h2_ref, topi_ref, gate_ref, rank_ref, cnt_ref, carry):
    i = pl.program_id(0)

    @pl.when(i == 0)
    def _():
        carry[...] = jnp.zeros_like(carry)

    tm = MERGE_SUB
    counts = {"total": carry[...]}
    first, second = [
        _merge_stages(slice(h * tm, (h + 1) * tm), counts, yr_ref, ya_ref, gr_ref, ga_ref, x_ref, mod_ref,
                      wb_ref, wo_ref, g2_ref, wr_ref, br_ref, x1_ref, h2_ref, topi_ref, gate_ref, rank_ref)
        for h in range(2)]
    for _ in range(MERGE_MATMUL_STAGES):
        next(first)
    for stage in first:
        next(second)
    for stage in second:
        pass
    carry[...] = counts["total"]
    cnt_ref[...] = jnp.broadcast_to(counts["total"], cnt_ref.shape).astype(jnp.int32)


MERGE_MATMUL_STAGES = 4


def _merge_stages(rows, counts, yr_ref, ya_ref, gr_ref, ga_ref, x_ref, mod_ref, wb_ref, wo_ref, g2_ref, wr_ref,
                  br_ref, x1_ref, h2_ref, topi_ref, gate_ref, rank_ref):
    tm = rows.stop - rows.start
    zr = jnp.dot(yr_ref[rows, :], wb_ref[0], preferred_element_type=F32)
    yield
    za = jnp.dot(ya_ref[rows, :], wb_ref[1], preferred_element_type=F32)
    mix = (gr_ref[rows, :] * zr + ga_ref[rows, :] * za).astype(BF16)
    yield
    mixed = jnp.dot(mix, wo_ref[...], preferred_element_type=F32)
    x1 = x_ref[rows, :] + mod_ref[0, 2:3, :] * mixed
    x1_ref[rows, :] = x1
    h2 = _norm_mod(x1, g2_ref[...], mod_ref[0, 3:4, :], mod_ref[0, 4:5, :])
    h2_ref[rows, :] = h2
    yield

    hi = h2.astype(BF16)
    lo = (h2 - hi.astype(F32)).astype(BF16)
    both = jnp.dot(hi, wr_ref[...], preferred_element_type=F32)
    logits = (both[:, :LANES] + both[:, LANES:]
              + jnp.dot(lo, wr_ref[:, :LANES], preferred_element_type=F32) + br_ref[...])
    yield

    lane = lax.broadcasted_iota(jnp.int32, logits.shape, 1)
    lane_f = lane.astype(F32)
    lg = jnp.where(lane < N_EXPERTS, logits, -jnp.inf)
    vals, idxs = [], []
    onehot = jnp.zeros(logits.shape, F32)
    for _ in range(TOP_K):
        m = jnp.max(lg, axis=1, keepdims=True)
        idx = jnp.min(jnp.where(lg == m, lane_f, float(LANES)), axis=1, keepdims=True)
        pick = lane_f == idx
        onehot = jnp.where(pick, 1.0, onehot)
        lg = jnp.where(pick, -jnp.inf, lg)
        vals.append(m)
        idxs.append(idx)
        yield
    es = [jnp.exp(v - vals[0]) for v in vals]
    den = es[0] + es[1] + es[2] + es[3]
    before0 = counts["total"]

    rr = lax.broadcasted_iota(jnp.int32, (tm, tm), 0)
    cc = lax.broadcasted_iota(jnp.int32, (tm, tm), 1)
    tri = jnp.where(cc < rr, 1.0, 0.0).astype(BF16)
    before = jnp.dot(tri, onehot.astype(BF16), preferred_element_type=F32) + before0

    topi = jnp.zeros(logits.shape, F32)
    gates = jnp.zeros(logits.shape, F32)
    rank = jnp.zeros(logits.shape, F32)
    for k in range(TOP_K):
        rk = jnp.sum(jnp.where(lane_f == idxs[k], before, 0.0), axis=1, keepdims=True)
        topi = jnp.where(lane == k, idxs[k], topi)
        gates = jnp.where(lane == k, es[k] / den, gates)
        rank = jnp.where(lane == k, rk, rank)
    topi_ref[rows, :] = topi.astype(jnp.int32)
    gate_ref[rows, :] = gates
    rank_ref[rows, :] = rank.astype(jnp.int32)
    counts["total"] = before0 + jnp.sum(onehot, axis=0, keepdims=True)
    yield


def _merge(y_rnn, y_att, gl, x2, mod3, w_branch, w_out, norm2_g, w_router, b_router, rows_per_batch):
    n, D = x2.shape
    tm = 2 * MERGE_SUB
    tpb = rows_per_batch // tm
    wr32 = jnp.zeros((D, LANES), F32).at[:, :N_EXPERTS].set(w_router)
    wr_hi = wr32.astype(BF16)
    wr = jnp.concatenate([wr_hi, (wr32 - wr_hi.astype(F32)).astype(BF16)], axis=1)
    br = jnp.zeros((1, LANES), F32).at[0, :N_EXPERTS].set(b_router)
    row = lambda i: (i, 0)
    fixed = lambda i: (0, 0)
    once = pl.Buffered(1)
    wide = jax.ShapeDtypeStruct((n, D), F32)
    narrow_i = jax.ShapeDtypeStruct((n, LANES), jnp.int32)
    return pl.pallas_call(
        _merge_body,
        grid=(n // tm,),
        in_specs=[pl.BlockSpec((tm, D), row),
                  pl.BlockSpec((tm, D), row),
                  pl.BlockSpec((tm, D), lambda i: (i, 0)),
                  pl.BlockSpec((tm, D), lambda i: (i, 1)),
                  pl.BlockSpec((tm, D), row),
                  pl.BlockSpec((1, 6, D), lambda i: (i // tpb, 0, 0)),
                  pl.BlockSpec((2, D, D), lambda i: (0, 0, 0), pipeline_mode=once),
                  pl.BlockSpec((D, D), fixed, pipeline_mode=once),
                  pl.BlockSpec((1, D), fixed),
                  pl.BlockSpec((D, 2 * LANES), fixed, pipeline_mode=once),
                  pl.BlockSpec((1, LANES), fixed)],
        out_specs=[pl.BlockSpec((tm, D), row),
                   pl.BlockSpec((tm, D), row),
                   pl.BlockSpec((tm, LANES), row),
                   pl.BlockSpec((tm, LANES), row),
                   pl.BlockSpec((tm, LANES), row),
                   pl.BlockSpec((8, LANES), fixed)],
        out_shape=[wide, wide, narrow_i, jax.ShapeDtypeStruct((n, LANES), F32), narrow_i,
                   jax.ShapeDtypeStruct((8, LANES), jnp.int32)],
        scratch_shapes=[pltpu.VMEM((1, LANES), F32)],
        compiler_params=_params("arbitrary"),
        name="merge",
    )(y_rnn, y_att, gl, gl, x2, mod3, w_branch.astype(BF16), w_out.astype(BF16),
      norm2_g.reshape(1, D), wr, br)


DISPATCH_TOKENS = 256
ROW_DMA_UNROLL = 8
DMA_QUEUES = 2


def _dispatch_body(dest_ref, zblk_ref, src_ref, dst_hbm, zbuf, sem, zsem):
    t = DISPATCH_TOKENS
    c = pl.program_id(0)

    @pl.when(c == 0)
    def _():
        zbuf[...] = jnp.zeros_like(zbuf)

        def zero_copy(b):
            r0 = pl.multiple_of(b * EXPERT_ROWS, EXPERT_ROWS)
            return pltpu.make_async_copy(zbuf, dst_hbm.at[pl.ds(r0, EXPERT_ROWS), :], zsem)

        def zissue(j, carry):
            @pl.when(zblk_ref[j] >= 0)
            def _():
                zero_copy(zblk_ref[j]).start()
            return carry

        def zwait(j, carry):
            @pl.when(zblk_ref[j] >= 0)
            def _():
                zero_copy(zblk_ref[j]).wait()
            return carry

        lax.fori_loop(0, zblk_ref.shape[0], zissue, 0)
        lax.fori_loop(0, zblk_ref.shape[0], zwait, 0)

    def issue(g, carry):
        for u in range(ROW_DMA_UNROLL):
            r = g * ROW_DMA_UNROLL + u
            for k in range(TOP_K):
                d = dest_ref[(c * t + r) * TOP_K + k]
                pltpu.async_copy(src_ref.at[pl.ds(r, 1), :], dst_hbm.at[pl.ds(d, 1), :], sem,
                                 priority=k % DMA_QUEUES)
        return carry

    lax.fori_loop(0, t // ROW_DMA_UNROLL, issue, 0)
    for k in range(TOP_K):
        pltpu.make_async_copy(src_ref, dst_hbm.at[pl.ds(0, t), :], sem).wait()


def _dispatch(h2, dest, zero_blocks, n_rows):
    n, D = h2.shape
    return pl.pallas_call(
        _dispatch_body,
        grid_spec=pltpu.PrefetchScalarGridSpec(
            num_scalar_prefetch=2,
            grid=(n // DISPATCH_TOKENS,),
            in_specs=[pl.BlockSpec((DISPATCH_TOKENS, D), lambda i, d, z: (i, 0))],
            out_specs=pl.BlockSpec(memory_space=pl.ANY),
            scratch_shapes=[pltpu.VMEM((EXPERT_ROWS, D), F32), pltpu.SemaphoreType.DMA,
                            pltpu.SemaphoreType.DMA],
        ),
        out_shape=jax.ShapeDtypeStruct((n_rows, D), F32),
        compiler_params=_params("arbitrary"),
        name="dispatch",
    )(dest, zero_blocks, h2)


def _experts_body(bexp_ref, nused_ref, first_ref, slot_ref, next_ref, x_ref, wu_hbm, bu_ref, wd_hbm, bd_ref,
                  o_ref, wu_buf, wd_buf, wu_bf, wd_bf, sems):
    i = pl.program_id(0)
    dff = wd_hbm.shape[1]
    used = i < nused_ref[0]

    def weight_copies(e, s):
        return (pltpu.make_async_copy(wu_hbm.at[e], wu_buf.at[s], sems.at[0, s]),
                pltpu.make_async_copy(wd_hbm.at[e], wd_buf.at[s], sems.at[1, s]))

    @pl.when(i == 0)
    def _():
        for cp in weight_copies(bexp_ref[0], 0):
            cp.start()

    @pl.when(used & (first_ref[i] == 1))
    def _():
        s = slot_ref[i]
        for cp in weight_copies(bexp_ref[i], s):
            cp.wait()
        wu_bf[...] = wu_buf[s].astype(BF16)
        wd_bf[...] = wd_buf[s].astype(BF16)

        @pl.when(next_ref[i] >= 0)
        def _():
            for cp in weight_copies(next_ref[i], 1 - s):
                cp.start()

    @pl.when(used)
    def _():
        hc = jnp.dot(x_ref[...].astype(BF16), wu_bf[...], preferred_element_type=F32) + bu_ref[0]
        g = jnp.minimum(hc[:, :dff], SWIGLU_LIMIT)
        lin = jnp.clip(hc[:, dff:], -SWIGLU_LIMIT, SWIGLU_LIMIT)
        act = (lin + 1.0) * g * jax.nn.sigmoid(SWIGLU_ALPHA * g)
        o_ref[...] = jnp.dot(act.astype(BF16), wd_bf[...], preferred_element_type=F32) + bd_ref[0]

    @pl.when(i >= nused_ref[0])
    def _():
        o_ref[...] = jnp.zeros_like(o_ref)


def _experts(x_rows, block_exp, n_used, w_up, b_up, w_down, b_down, n_blocks):
    R, D = x_rows.shape
    E, _, F2 = w_up.shape
    dff = w_down.shape[1]
    bidx = jnp.arange(n_blocks, dtype=jnp.int32)
    first = ((bidx == 0) | (block_exp != jnp.roll(block_exp, 1))) & (bidx < n_used[0])
    slot = (jnp.cumsum(first) - 1) % 2
    first_pos = jnp.where(first, bidx, n_blocks)
    next_pos = jnp.flip(lax.cummin(jnp.flip(jnp.roll(first_pos, -1).at[-1].set(n_blocks))))
    next_exp = jnp.where(next_pos < n_blocks, block_exp[jnp.minimum(next_pos, n_blocks - 1)], -1)
    blk = lambda i, *_: (i, 0)
    bsel = lambda i, be, *_: (be[i], 0, 0)
    return pl.pallas_call(
        _experts_body,
        grid_spec=pltpu.PrefetchScalarGridSpec(
            num_scalar_prefetch=5,
            grid=(n_blocks,),
            in_specs=[pl.BlockSpec((EXPERT_ROWS, D), blk),
                      pl.BlockSpec(memory_space=pl.ANY),
                      pl.BlockSpec((1, 1, F2), bsel),
                      pl.BlockSpec(memory_space=pl.ANY),
                      pl.BlockSpec((1, 1, D), bsel)],
            out_specs=pl.BlockSpec((EXPERT_ROWS, D), blk),
            scratch_shapes=[pltpu.VMEM((2, D, F2), F32), pltpu.VMEM((2, dff, D), F32),
                            pltpu.VMEM((D, F2), BF16), pltpu.VMEM((dff, D), BF16),
                            pltpu.SemaphoreType.DMA((2, 2))],
        ),
        out_shape=jax.ShapeDtypeStruct((R, D), F32),
        compiler_params=_params("arbitrary"),
        name="experts",
    )(block_exp, n_used, first.astype(jnp.int32), slot.astype(jnp.int32), next_exp.astype(jnp.int32),
      x_rows, w_up, b_up.reshape(E, 1, F2), w_down, b_down.reshape(E, 1, D))


COMBINE_ROWS = 128


def _combine_body(dest_ref, y_hbm, x1_ref, gate_ref, mod_ref, o_ref, buf, sem):
    i = pl.program_id(0)
    t = COMBINE_ROWS

    def issue(g, carry):
        for u in range(ROW_DMA_UNROLL):
            r = g * ROW_DMA_UNROLL + u
            for k in range(TOP_K):
                d = dest_ref[(i * t + r) * TOP_K + k]
                pltpu.async_copy(y_hbm.at[pl.ds(d, 1), :], buf.at[pl.ds(r, 1), pl.ds(k * D, D)], sem,
                                 priority=k % DMA_QUEUES)
        return carry

    D = o_ref.shape[1]
    lax.fori_loop(0, t // ROW_DMA_UNROLL, issue, 0)
    for k in range(TOP_K):
        pltpu.make_async_copy(y_hbm.at[pl.ds(0, t), :], buf.at[:, pl.ds(k * D, D)], sem).wait()
    gates = gate_ref[...]
    y = gates[:, 0:1] * buf[:, 0:D]
    for k in range(1, TOP_K):
        y = y + gates[:, k:k + 1] * buf[:, k * D:(k + 1) * D]
    o_ref[...] = x1_ref[...] + mod_ref[0, 5:6, :] * y


def _combine(y_rows, dest, x1, gates, mod3, rows_per_batch):
    n, D = x1.shape
    t = COMBINE_ROWS
    tpb = rows_per_batch // t
    return pl.pallas_call(
        _combine_body,
        grid_spec=pltpu.PrefetchScalarGridSpec(
            num_scalar_prefetch=1,
            grid=(n // t,),
            in_specs=[pl.BlockSpec(memory_space=pl.ANY),
                      pl.BlockSpec((t, D), lambda i, d: (i, 0)),
                      pl.BlockSpec((t, LANES), lambda i, d: (i, 0)),
                      pl.BlockSpec((1, 6, D), lambda i, d: (i // tpb, 0, 0))],
            out_specs=pl.BlockSpec((t, D), lambda i, d: (i, 0)),
            scratch_shapes=[pltpu.VMEM((t, TOP_K * D), F32), pltpu.SemaphoreType.DMA],
        ),
        out_shape=jax.ShapeDtypeStruct((n, D), F32),
        compiler_params=_params("arbitrary"),
        name="combine",
    )(dest, y_rows, x1, gates, mod3)


def kernel(x, c, positions, ada_w, ada_b, norm1_g, norm2_g, w_in, conv_w, conv_b, w_rg_a, b_rg_a, w_rg_x, b_rg_x, lru_lambda, q_norm_g, k_norm_g, b_gate, w_branch, w_out, w_router, b_router, w_up, b_up, w_down, b_down):
    B, S, D = x.shape
    n = B * S
    depth = ada_w.shape[0]
    x2 = x.reshape(n, D)
    cos_t, sin_t = _rope_tables(positions)
    perm = _rope_head_perm()
    head_cols = (jnp.arange(2 * N_HEADS)[:, None] * HEAD_DIM + perm[None, :]).reshape(-1)
    for l in range(depth):
        mod3 = _adaln(c, ada_w[l], ada_b[l]).reshape(B, 6, D)
        h1 = _norm_mod_call(x2, mod3, norm1_g[l], S)
        w = w_in[l].astype(BF16)
        xr = _proj("plain", h1, w[:, 0:D], [], F32, "proj_xr")
        gg = _proj("gelu", h1, w[:, D:2 * D], [], F32, "proj_gr")
        qk_g = jnp.concatenate([jnp.tile(q_norm_g[l][perm], N_HEADS),
                                jnp.tile(k_norm_g[l][perm], N_HEADS)]).reshape(1, 2 * D)
        qk = _proj("qk", h1, w[:, 2 * D:4 * D][:, head_cols], [qk_g, cos_t, sin_t], F32, "proj_qk")
        v = _proj("plain", h1, w[:, 4 * D:5 * D], [], BF16, "proj_v")
        gl = _proj("gate", h1, w[:, 5 * D:7 * D], [b_gate[l].reshape(1, 2 * D)], F32, "proj_gl")

        y_rnn = _rglru(xr, gg, conv_w[l], conv_b[l], w_rg_a[l], b_rg_a[l], w_rg_x[l], b_rg_x[l],
                       lru_lambda[l], B, S)
        y_att = _moba(qk, v, B, S)

        x1, h2, topi, gates, rank, cnt = _merge(y_rnn, y_att, gl, x2, mod3, w_branch[l], w_out[l],
                                                norm2_g[l], w_router[l], b_router[l], S)

        counts = cnt[0, :N_EXPERTS]
        padded = (counts + EXPERT_ROWS - 1) // EXPERT_ROWS * EXPERT_ROWS
        pad_end = jnp.cumsum(padded)
        pad_start = pad_end - padded
        top_idx = topi[:, :TOP_K]
        dest = (pad_start[top_idx] + rank[:, :TOP_K]).reshape(n * TOP_K).astype(jnp.int32)
        n_blocks = (n * TOP_K) // EXPERT_ROWS + N_EXPERTS
        block_start = jnp.arange(n_blocks, dtype=jnp.int32) * EXPERT_ROWS
        block_exp = jnp.minimum(jnp.sum(block_start[:, None] >= pad_end[None, :], axis=1),
                                N_EXPERTS - 1).astype(jnp.int32)
        n_used = (pad_end[-1] // EXPERT_ROWS).astype(jnp.int32)
        last_blk = jnp.where(padded > 0, pad_end // EXPERT_ROWS - 1, -1)
        spare = n_used + jnp.arange(N_EXPERTS, dtype=jnp.int32)
        spare = jnp.where(spare < n_blocks, spare, -1)
        zero_blocks = jnp.concatenate([last_blk, spare]).astype(jnp.int32)

        x_rows = _dispatch(h2, dest, zero_blocks, n_blocks * EXPERT_ROWS)
        y_rows = _experts(x_rows, block_exp, n_used.reshape(1), w_up[l], b_up[l], w_down[l], b_down[l], n_blocks)
        x2 = _combine(y_rows, dest, x1, gates, mod3, S)
    return x2.reshape(B, S, D)
```

```python
import functools

import jax
import jax.numpy as jnp
from jax import lax
from jax.experimental import pallas as pl
from jax.experimental.pallas import tpu as pltpu

F32 = jnp.float32
BF16 = jnp.bfloat16
HIGHEST = lax.Precision.HIGHEST

EPS = 1e-6
LANES = 128
N_HEADS = 8
HEAD_DIM = 128
RNN_BLOCKS = 8
RNN_BW = 128
CONV_W = 4
LRU_C = 8.0
MOBA_BLOCK = 256
MOBA_TOPK = 3
MOBA_CHUNK = 4
MOBA_HEADS = 4
MERGE_SUB = 256
LOG2_E = 1.4426950408889634
ROPE_DIMS = HEAD_DIM // 4
ROPE_THETA = 500000.0
N_EXPERTS = 32
TOP_K = 4
SWIGLU_LIMIT = 7.0
SWIGLU_ALPHA = 1.702
EXPERT_ROWS = 256
NEG_BIG = -1e30

VMEM_LIMIT = 56 * 1024 * 1024


def _params(*sem):
    return pltpu.CompilerParams(dimension_semantics=sem, vmem_limit_bytes=VMEM_LIMIT)


def _adaln_body(c_ref, w_ref, b_ref, o_ref):
    cs = c_ref[...]
    cs = cs * jax.nn.sigmoid(cs)
    o_ref[...] = jnp.dot(cs, w_ref[...], preferred_element_type=F32, precision=HIGHEST) + b_ref[...]


def _adaln(c, ada_w, ada_b):
    B, D = c.shape
    W = ada_w.shape[1]
    cpad = jnp.zeros((8, D), F32).at[:B].set(c)
    tn = 1024
    mod = pl.pallas_call(
        _adaln_body,
        grid=(W // tn,),
        in_specs=[pl.BlockSpec((8, D), lambda j: (0, 0)),
                  pl.BlockSpec((D, tn), lambda j: (0, j)),
                  pl.BlockSpec((1, tn), lambda j: (0, j))],
        out_specs=pl.BlockSpec((8, tn), lambda j: (0, j)),
        out_shape=jax.ShapeDtypeStruct((8, W), F32),
        compiler_params=_params("parallel"),
        name="adaln",
    )(cpad, ada_w, ada_b.reshape(1, W))
    return mod[:B]


def _rope_head_perm():
    half = ROPE_DIMS // 2
    mid = HEAD_DIM // 2
    return jnp.concatenate([jnp.arange(0, half), jnp.arange(ROPE_DIMS, mid + half),
                            jnp.arange(half, ROPE_DIMS), jnp.arange(mid + half, HEAD_DIM)])


def _rope_body(pos_ref, freq_ref, c_ref, s_ref):
    ang = pos_ref[...].astype(F32) * freq_ref[...]
    lane = lax.broadcasted_iota(jnp.int32, ang.shape, 1)
    half = ROPE_DIMS // 2
    mid = HEAD_DIM // 2
    s = jnp.sin(ang)
    c_ref[...] = jnp.cos(ang)
    s_ref[...] = jnp.where(lane < half, -s, jnp.where((lane >= mid) & (lane < mid + half), s, 0.0))


def _rope_tables(positions):
    n = positions.size
    half = ROPE_DIMS // 2
    mid = HEAD_DIM // 2
    freqs = ROPE_THETA ** (-jnp.arange(half, dtype=F32) / half)
    freq_lane = jnp.zeros((1, LANES), F32).at[0, :half].set(freqs).at[0, mid:mid + half].set(freqs)
    tm = 1024
    tab = jax.ShapeDtypeStruct((n, LANES), F32)
    return pl.pallas_call(
        _rope_body,
        grid=(n // tm,),
        in_specs=[pl.BlockSpec((tm, 1), lambda i: (i, 0)),
                  pl.BlockSpec((1, LANES), lambda i: (0, 0))],
        out_specs=[pl.BlockSpec((tm, LANES), lambda i: (i, 0))] * 2,
        out_shape=[tab, tab],
        compiler_params=_params("parallel"),
        name="rope_tab",
    )(positions.reshape(n, 1), freq_lane)


def _norm_mod(x, g, sh, sc):
    ms = jnp.mean(x * x, axis=-1, keepdims=True)
    y = x * lax.rsqrt(ms + EPS)
    return (y * g) * (1.0 + sc) + sh


def _norm_mod_body(x_ref, mod_ref, g_ref, o_ref):
    o_ref[...] = _norm_mod(x_ref[...], g_ref[...], mod_ref[0, 0:1, :], mod_ref[0, 1:2, :]).astype(o_ref.dtype)


def _norm_mod_call(x2, mod3, g, rows_per_batch):
    n, D = x2.shape
    tm = 512
    tpb = rows_per_batch // tm
    return pl.pallas_call(
        _norm_mod_body,
        grid=(n // tm,),
        in_specs=[pl.BlockSpec((tm, D), lambda i: (i, 0)),
                  pl.BlockSpec((1, 6, D), lambda i: (i // tpb, 0, 0)),
                  pl.BlockSpec((1, D), lambda i: (0, 0))],
        out_specs=pl.BlockSpec((tm, D), lambda i: (i, 0)),
        out_shape=jax.ShapeDtypeStruct((n, D), BF16),
        compiler_params=_params("parallel"),
        name="norm_mod",
    )(x2, mod3, g.reshape(1, D))


def _gelu_tanh(x):
    return 0.5 * x * (1.0 + jnp.tanh(0.7978845608028654 * (x + 0.044715 * (x * x * x))))


PROJ_SUB = 256
PROJ_PIECE = 256


def _qk_head(seg, g, cos, sin):
    ms = jnp.mean(seg * seg, axis=-1, keepdims=True)
    y = seg * lax.rsqrt(ms + EPS) * g
    return y * cos + pltpu.roll(y, HEAD_DIM // 2, axis=1) * sin


def _proj_stages(kind, rows, h_ref, w_ref, extras, o_ref):
    n_pieces = w_ref.shape[1] // PROJ_PIECE
    pieces = []
    for j in range(n_pieces):
        pieces.append(jnp.dot(h_ref[rows, :], w_ref[:, j * PROJ_PIECE:(j + 1) * PROJ_PIECE],
                              preferred_element_type=F32))
        yield
    for j in range(n_pieces):
        cols = slice(j * PROJ_PIECE, (j + 1) * PROJ_PIECE)
        acc = pieces[j]
        if kind == "gelu":
            acc = _gelu_tanh(acc)
        elif kind == "gate":
            (b_ref,) = extras
            acc = jax.nn.sigmoid(acc + b_ref[:, cols])
        elif kind == "qk":
            g_ref, c_ref, s_ref = extras
            cos, sin = c_ref[rows, :], s_ref[rows, :]
            acc = jnp.concatenate(
                [_qk_head(acc[:, o:o + HEAD_DIM], g_ref[:, cols.start + o:cols.start + o + HEAD_DIM], cos, sin)
                 for o in range(0, PROJ_PIECE, HEAD_DIM)], axis=1)
        o_ref[rows, cols] = acc.astype(o_ref.dtype)
        yield


def _proj_body(kind, h_ref, w_ref, *rest):
    *extras, o_ref = rest
    first, second = [_proj_stages(kind, slice(t * PROJ_SUB, (t + 1) * PROJ_SUB), h_ref, w_ref, extras, o_ref)
                     for t in range(2)]
    n_pieces = w_ref.shape[1] // PROJ_PIECE
    for _ in range(n_pieces):
        next(first)
    for _ in range(n_pieces):
        next(second)
        next(first)
    for stage in second:
        pass


def _proj(kind, h, w, extras, out_dtype, name):
    n, D = h.shape
    W = w.shape[1]
    tm, tn = 512, 1024
    in_specs = [pl.BlockSpec((tm, D), lambda j, i: (i, 0)),
                pl.BlockSpec((D, tn), lambda j, i: (0, j))]
    args = [h, w]
    for arr in extras:
        if arr.shape[0] == 1:
            in_specs.append(pl.BlockSpec((1, tn), lambda j, i: (0, j)))
        else:
            in_specs.append(pl.BlockSpec((tm, LANES), lambda j, i: (i, 0)))
        args.append(arr)
    return pl.pallas_call(
        functools.partial(_proj_body, kind),
        grid=(W // tn, n // tm),
        in_specs=in_specs,
        out_specs=pl.BlockSpec((tm, tn), lambda j, i: (i, j)),
        out_shape=jax.ShapeDtypeStruct((n, W), out_dtype),
        compiler_params=_params("parallel", "parallel"),
        name=name,
    )(*args)


def _rglru_body(xr_ref, gg_ref, cw_ref, cb_ref, wa_ref, ba_ref, wx_ref, bx_ref, lam_ref,
                o_ref, xbuf, hcar, a_s, u_s):
    s = pl.program_id(1)
    ts, D = xr_ref.shape

    @pl.when(s == 0)
    def _():
        xbuf[0:8, :] = jnp.zeros((8, D), F32)
        hcar[...] = jnp.zeros_like(hcar)

    @pl.when(s > 0)
    def _():
        xbuf[0:8, :] = xbuf[ts:ts + 8, :]

    xbuf[8:8 + ts, :] = xr_ref[...]
    xc = cb_ref[...] + cw_ref[0:1, :] * xbuf[8:8 + ts, :]
    for i in range(1, CONV_W):
        xc = xc + cw_ref[i:i + 1, :] * xbuf[8 - i:8 - i + ts, :]

    ra, rx = [], []
    for n in range(RNN_BLOCKS):
        xb = xc[:, n * RNN_BW:(n + 1) * RNN_BW].astype(BF16)
        ra.append(jnp.dot(xb, wa_ref[n], preferred_element_type=F32))
        rx.append(jnp.dot(xb, wx_ref[n], preferred_element_type=F32))
    r = jax.nn.sigmoid(jnp.concatenate(ra, axis=1) + ba_ref[...])
    ig = jax.nn.sigmoid(jnp.concatenate(rx, axis=1) + bx_ref[...])

    z = -lam_ref[...]
    softplus = jnp.maximum(z, 0.0) + jnp.log1p(jnp.exp(-jnp.abs(z)))
    log_a = -LRU_C * r * softplus
    a = jnp.exp(log_a)
    mult = jnp.sqrt(1.0 - a * a)
    row = lax.broadcasted_iota(jnp.int32, (ts, D), 0)
    mult = jnp.where((row == 0) & (s == 0), 1.0, mult)
    u = mult * (ig * xc)

    rm = row & 7
    for d in (1, 2, 4):
        keep = rm >= d
        a_sh = pltpu.roll(a, d, axis=0)
        u_sh = pltpu.roll(u, d, axis=0)
        u = jnp.where(keep, a * u_sh + u, u)
        a = jnp.where(keep, a * a_sh, a)
    a_s[...] = a
    u_s[...] = u

    def group(g, h):
        r0 = pl.multiple_of(g * 8, 8)
        hg = u_s[pl.ds(r0, 8), :] + a_s[pl.ds(r0, 8), :] * h
        u_s[pl.ds(r0, 8), :] = hg
        return hg[7:8, :]

    hcar[...] = lax.fori_loop(0, ts // 8, group, hcar[...])
    o_ref[...] = (u_s[...] * gg_ref[...]).astype(o_ref.dtype)


def _rglru(xr, gg, conv_w, conv_b, w_a, b_a, w_x, b_x, lam, B, S):
    n, D = xr.shape
    ts = 256
    spb = S // ts
    row = lambda b, s: (b * spb + s, 0)
    vec = lambda b, s: (0, 0)
    return pl.pallas_call(
        _rglru_body,
        grid=(B, spb),
        in_specs=[pl.BlockSpec((ts, D), row),
                  pl.BlockSpec((ts, D), row),
                  pl.BlockSpec((CONV_W, D), vec),
                  pl.BlockSpec((1, D), vec),
                  pl.BlockSpec((RNN_BLOCKS, RNN_BW, RNN_BW), lambda b, s: (0, 0, 0)),
                  pl.BlockSpec((1, D), vec),
                  pl.BlockSpec((RNN_BLOCKS, RNN_BW, RNN_BW), lambda b, s: (0, 0, 0)),
                  pl.BlockSpec((1, D), vec),
                  pl.BlockSpec((1, D), vec)],
        out_specs=pl.BlockSpec((ts, D), row),
        out_shape=jax.ShapeDtypeStruct((n, D), BF16),
        scratch_shapes=[pltpu.VMEM((ts + 8, D), F32), pltpu.VMEM((1, D), F32),
                        pltpu.VMEM((ts, D), F32), pltpu.VMEM((ts, D), F32)],
        compiler_params=_params("arbitrary", "arbitrary"),
        name="rglru",
    )(xr, gg, conv_w, conv_b.reshape(1, D), w_a.astype(BF16), b_a.reshape(1, D),
      w_x.astype(BF16), b_x.reshape(1, D), lam.reshape(1, D))


def _moba_body(q_ref, k_ref, v_ref, o_ref, kaug, vaug, kmean, m_s, acc_s):
    qb = pl.program_id(2)
    S = k_ref.shape[0]
    nb = S // MOBA_BLOCK
    bs = MOBA_BLOCK
    dh = HEAD_DIM
    heads = range(MOBA_HEADS)
    hcol = lambda hh: slice(hh * dh, (hh + 1) * dh)

    @pl.when(qb == 0)
    def _():
        blk = lax.broadcasted_iota(jnp.int32, (S, dh), 0) // bs
        col = lax.broadcasted_iota(jnp.int32, (S, dh), 1)
        onehot = jnp.where(col == blk, 1.0, 0.0).astype(BF16)
        for hh in heads:
            kaug[hh, :, 0:dh] = k_ref[:, hcol(hh)].astype(BF16)
            kaug[hh, :, dh:2 * dh] = onehot
            vaug[hh, :, 0:dh] = v_ref[:, hcol(hh)]
            vaug[hh, :, dh:2 * dh] = jnp.ones((S, dh), BF16)
            kmean[hh] = jnp.zeros((LANES, dh), F32)
            for j in range(nb):
                kmean[hh, j:j + 1, :] = jnp.mean(k_ref[j * bs:(j + 1) * bs, hcol(hh)], axis=0, keepdims=True)

    scale = HEAD_DIM ** -0.5
    nt = (((1,), (1,)), ((), ()))
    c2 = scale * LOG2_E
    r0 = pl.multiple_of(qb * bs, bs)
    nb_pad = -(-nb // 8) * 8
    rr = lax.broadcasted_iota(jnp.int32, (bs, bs), 0)
    cc = lax.broadcasted_iota(jnp.int32, (bs, bs), 1)

    qs = [q_ref[:, hcol(hh)] for hh in heads]
    qbfs = [q.astype(BF16) for q in qs]

    blk_i = lax.broadcasted_iota(jnp.int32, (nb_pad, bs), 0)
    blk_f = blk_i.astype(F32)
    past = blk_i < qb
    gs = [jnp.where(past, lax.dot_general(kmean[hh, 0:nb_pad, :], qs[hh], nt, preferred_element_type=F32,
                                          precision=HIGHEST), -jnp.inf) for hh in heads]
    ss = [jnp.where(cc <= rr, lax.dot_general(qbfs[hh], kaug[hh, pl.ds(r0, bs), 0:dh], nt,
                                              preferred_element_type=F32), NEG_BIG) for hh in heads]
    biases = [jnp.full((nb_pad, bs), NEG_BIG, F32) for _ in heads]
    for _ in range(MOBA_TOPK):
        for hh in heads:
            m = jnp.max(gs[hh], axis=0, keepdims=True)
            idx = jnp.min(jnp.where(gs[hh] == m, blk_f, float(LANES)), axis=0, keepdims=True)
            pick = (blk_f == idx) & past
            biases[hh] = jnp.where(pick, 0.0, biases[hh])
            gs[hh] = jnp.where(pick, -jnp.inf, gs[hh])
    fill = jnp.full((bs, dh - nb_pad), NEG_BIG, BF16)
    qaugs = [jnp.concatenate([qbfs[hh], biases[hh].T.astype(BF16), fill], axis=1) for hh in heads]

    ps = []
    for hh in heads:
        m0 = jnp.max(ss[hh], axis=1, keepdims=True)
        ps.append(jnp.exp2((ss[hh] - m0) * c2).astype(BF16))
        m_s[hh] = m0
    for hh in heads:
        acc_s[hh] = jnp.dot(ps[hh], vaug[hh, pl.ds(r0, bs), :], preferred_element_type=F32)

    cw = MOBA_CHUNK * bs
    for c in range(nb // MOBA_CHUNK):
        @pl.when(c * MOBA_CHUNK < qb)
        def _(c=c):
            scs = [lax.dot_general(qaugs[hh], kaug[hh, c * cw:(c + 1) * cw, :], nt, preferred_element_type=F32)
                   for hh in heads]
            pcs, alphas = [], []
            for hh in heads:
                m_old = m_s[hh]
                m_new = jnp.maximum(m_old, jnp.max(scs[hh], axis=1, keepdims=True))
                alphas.append(jnp.exp2((m_old - m_new) * c2))
                pcs.append(jnp.exp2((scs[hh] - m_new) * c2).astype(BF16))
                m_s[hh] = m_new
            for hh in heads:
                acc_s[hh] = alphas[hh] * acc_s[hh] + jnp.dot(pcs[hh], vaug[hh, c * cw:(c + 1) * cw, :],
                                                             preferred_element_type=F32)

    for hh in heads:
        o_ref[:, hcol(hh)] = (acc_s[hh, :, 0:dh] / acc_s[hh, :, dh:dh + 1]).astype(o_ref.dtype)


def _moba(qk, v, B, S):
    n, D = v.shape
    nq = S // MOBA_BLOCK
    hs = MOBA_HEADS
    w = hs * HEAD_DIM
    return pl.pallas_call(
        _moba_body,
        grid=(B, N_HEADS // hs, nq),
        in_specs=[pl.BlockSpec((MOBA_BLOCK, w), lambda b, h, i: (b * nq + i, h)),
                  pl.BlockSpec((S, w), lambda b, h, i: (b, N_HEADS // hs + h)),
                  pl.BlockSpec((S, w), lambda b, h, i: (b, h))],
        out_specs=pl.BlockSpec((MOBA_BLOCK, w), lambda b, h, i: (b * nq + i, h)),
        out_shape=jax.ShapeDtypeStruct((n, D), BF16),
        scratch_shapes=[pltpu.VMEM((hs, S, 2 * HEAD_DIM), BF16), pltpu.VMEM((hs, S, 2 * HEAD_DIM), BF16),
                        pltpu.VMEM((hs, LANES, HEAD_DIM), F32), pltpu.VMEM((hs, MOBA_BLOCK, 1), F32),
                        pltpu.VMEM((hs, MOBA_BLOCK, 2 * HEAD_DIM), F32)],
        compiler_params=_params("parallel", "parallel", "arbitrary"),
        name="moba",
    )(qk, qk, v)


ROUTE_COLS = 16


def _merge_body(yr_ref, ya_ref, gr_ref, ga_ref, x_ref, mod_ref, wb_ref, wo_ref, g2_ref, wr_ref, br_ref,
                x1_ref, h2_ref, route_ref, cnt_ref, carry):
    i = pl.program_id(0)

    @pl.when(i == 0)
    def _():
        carry[...] = jnp.zeros_like(carry)

    tm = MERGE_SUB
    counts = {"total": carry[...]}
    first, second = [
        _merge_stages(slice(h * tm, (h + 1) * tm), counts, yr_ref, ya_ref, gr_ref, ga_ref, x_ref, mod_ref,
                      wb_ref, wo_ref, g2_ref, wr_ref, br_ref, x1_ref, h2_ref, route_ref)
        for h in range(2)]
    for _ in range(MERGE_MATMUL_STAGES):
        next(first)
    for stage in first:
        next(second)
    for stage in second:
        pass
    carry[...] = counts["total"]
    cnt_ref[...] = jnp.broadcast_to(counts["total"], cnt_ref.shape).astype(jnp.int32)


MERGE_MATMUL_STAGES = 4


def _merge_stages(rows, counts, yr_ref, ya_ref, gr_ref, ga_ref, x_ref, mod_ref, wb_ref, wo_ref, g2_ref, wr_ref,
                  br_ref, x1_ref, h2_ref, route_ref):
    tm = rows.stop - rows.start
    zr = jnp.dot(yr_ref[rows, :], wb_ref[0], preferred_element_type=F32)
    yield
    za = jnp.dot(ya_ref[rows, :], wb_ref[1], preferred_element_type=F32)
    mix = (gr_ref[rows, :] * zr + ga_ref[rows, :] * za).astype(BF16)
    yield
    mixed = jnp.dot(mix, wo_ref[...], preferred_element_type=F32)
    x1 = x_ref[rows, :] + mod_ref[0, 2:3, :] * mixed
    x1_ref[rows, :] = x1
    h2 = _norm_mod(x1, g2_ref[...], mod_ref[0, 3:4, :], mod_ref[0, 4:5, :])
    h2_ref[rows, :] = h2
    yield

    nt = (((1,), (1,)), ((), ()))
    ne = N_EXPERTS
    hi = h2.astype(BF16)
    lo = (h2 - hi.astype(F32)).astype(BF16)
    both = lax.dot_general(wr_ref[...], hi, nt, preferred_element_type=F32)
    lg = (both[:ne] + both[ne:]
          + lax.dot_general(wr_ref[0:ne, :], lo, nt, preferred_element_type=F32) + br_ref[...])
    yield

    eidx = lax.broadcasted_iota(jnp.int32, lg.shape, 0).astype(F32)
    vals, idxs = [], []
    onehot = jnp.zeros(lg.shape, F32)
    for _ in range(TOP_K):
        m = jnp.max(lg, axis=0, keepdims=True)
        idx = jnp.min(jnp.where(lg == m, eidx, float(ne)), axis=0, keepdims=True)
        pick = eidx == idx
        onehot = jnp.where(pick, 1.0, onehot)
        lg = jnp.where(pick, -jnp.inf, lg)
        vals.append(m)
        idxs.append(idx)
        yield
    es = [jnp.exp(v - vals[0]) for v in vals]
    den = es[0] + es[1] + es[2] + es[3]
    before0 = counts["total"]

    rr = lax.broadcasted_iota(jnp.int32, (tm, tm), 0)
    cc = lax.broadcasted_iota(jnp.int32, (tm, tm), 1)
    earlier = jnp.where(rr < cc, 1.0, 0.0).astype(BF16)
    before = jnp.dot(onehot.astype(BF16), earlier, preferred_element_type=F32) + before0
    ranks = [jnp.sum(jnp.where(eidx == idxs[k], before, 0.0), axis=0, keepdims=True) for k in range(TOP_K)]

    rows_t = jnp.concatenate(idxs + [e / den for e in es] + ranks
                             + [jnp.zeros((ROUTE_COLS - 3 * TOP_K, tm), F32)], axis=0)
    route_ref[rows, :] = rows_t.T
    counts["total"] = before0 + jnp.sum(onehot, axis=1, keepdims=True)
    yield


def _merge(y_rnn, y_att, gl, x2, mod3, w_branch, w_out, norm2_g, w_router, b_router, rows_per_batch):
    n, D = x2.shape
    tm = 2 * MERGE_SUB
    tpb = rows_per_batch // tm
    wr_t = w_router.T
    wr_hi = wr_t.astype(BF16)
    wr = jnp.concatenate([wr_hi, (wr_t - wr_hi.astype(F32)).astype(BF16)], axis=0)
    br = b_router.reshape(N_EXPERTS, 1)
    row = lambda i: (i, 0)
    fixed = lambda i: (0, 0)
    once = pl.Buffered(1)
    wide = jax.ShapeDtypeStruct((n, D), F32)
    return pl.pallas_call(
        _merge_body,
        grid=(n // tm,),
        in_specs=[pl.BlockSpec((tm, D), row),
                  pl.BlockSpec((tm, D), row),
                  pl.BlockSpec((tm, D), lambda i: (i, 0)),
                  pl.BlockSpec((tm, D), lambda i: (i, 1)),
                  pl.BlockSpec((tm, D), row),
                  pl.BlockSpec((1, 6, D), lambda i: (i // tpb, 0, 0)),
                  pl.BlockSpec((2, D, D), lambda i: (0, 0, 0), pipeline_mode=once),
                  pl.BlockSpec((D, D), fixed, pipeline_mode=once),
                  pl.BlockSpec((1, D), fixed),
                  pl.BlockSpec((2 * N_EXPERTS, D), fixed, pipeline_mode=once),
                  pl.BlockSpec((N_EXPERTS, 1), fixed)],
        out_specs=[pl.BlockSpec((tm, D), row),
                   pl.BlockSpec((tm, D), row),
                   pl.BlockSpec((tm, ROUTE_COLS), row),
                   pl.BlockSpec((N_EXPERTS, LANES), fixed)],
        out_shape=[wide, wide, jax.ShapeDtypeStruct((n, ROUTE_COLS), F32),
                   jax.ShapeDtypeStruct((N_EXPERTS, LANES), jnp.int32)],
        scratch_shapes=[pltpu.VMEM((N_EXPERTS, 1), F32)],
        compiler_params=_params("arbitrary"),
        name="merge",
    )(y_rnn, y_att, gl, gl, x2, mod3, w_branch.astype(BF16), w_out.astype(BF16),
      norm2_g.reshape(1, D), wr, br)


DISPATCH_TOKENS = 256
ROW_DMA_UNROLL = 8


def _dispatch_body(dest_ref, zblk_ref, src_ref, dst_hbm, zbuf, sem, zsem):
    t = DISPATCH_TOKENS
    c = pl.program_id(0)

    @pl.when(c == 0)
    def _():
        zbuf[...] = jnp.zeros_like(zbuf)

        def zero_copy(b):
            r0 = pl.multiple_of(b * EXPERT_ROWS, EXPERT_ROWS)
            return pltpu.make_async_copy(zbuf, dst_hbm.at[pl.ds(r0, EXPERT_ROWS), :], zsem)

        def zissue(j, carry):
            @pl.when(zblk_ref[j] >= 0)
            def _():
                zero_copy(zblk_ref[j]).start()
            return carry

        def zwait(j, carry):
            @pl.when(zblk_ref[j] >= 0)
            def _():
                zero_copy(zblk_ref[j]).wait()
            return carry

        lax.fori_loop(0, zblk_ref.shape[0], zissue, 0)
        lax.fori_loop(0, zblk_ref.shape[0], zwait, 0)

    def issue(g, carry):
        for u in range(ROW_DMA_UNROLL):
            r = g * ROW_DMA_UNROLL + u
            for k in range(TOP_K):
                d = dest_ref[(c * t + r) * TOP_K + k]
                pltpu.make_async_copy(src_ref.at[pl.ds(r, 1), :], dst_hbm.at[pl.ds(d, 1), :], sem).start()
        return carry

    lax.fori_loop(0, t // ROW_DMA_UNROLL, issue, 0)
    for k in range(TOP_K):
        pltpu.make_async_copy(src_ref, dst_hbm.at[pl.ds(0, t), :], sem).wait()


def _dispatch(h2, dest, zero_blocks, n_rows):
    n, D = h2.shape
    return pl.pallas_call(
        _dispatch_body,
        grid_spec=pltpu.PrefetchScalarGridSpec(
            num_scalar_prefetch=2,
            grid=(n // DISPATCH_TOKENS,),
            in_specs=[pl.BlockSpec((DISPATCH_TOKENS, D), lambda i, d, z: (i, 0))],
            out_specs=pl.BlockSpec(memory_space=pl.ANY),
            scratch_shapes=[pltpu.VMEM((EXPERT_ROWS, D), F32), pltpu.SemaphoreType.DMA,
                            pltpu.SemaphoreType.DMA],
        ),
        out_shape=jax.ShapeDtypeStruct((n_rows, D), F32),
        compiler_params=_params("arbitrary"),
        name="dispatch",
    )(dest, zero_blocks, h2)


def _experts_body(bexp_ref, nused_ref, first_ref, slot_ref, next_ref, x_ref, wu_hbm, bu_ref, wd_hbm, bd_ref,
                  o_ref, wu_buf, wd_buf, wu_bf, wd_bf, sems):
    i = pl.program_id(0)
    dff = wd_hbm.shape[1]
    used = i < nused_ref[0]

    def weight_copies(e, s):
        return (pltpu.make_async_copy(wu_hbm.at[e], wu_buf.at[s], sems.at[0, s]),
                pltpu.make_async_copy(wd_hbm.at[e], wd_buf.at[s], sems.at[1, s]))

    @pl.when(i == 0)
    def _():
        for cp in weight_copies(bexp_ref[0], 0):
            cp.start()

    @pl.when(used & (first_ref[i] == 1))
    def _():
        s = slot_ref[i]
        for cp in weight_copies(bexp_ref[i], s):
            cp.wait()
        wu_bf[...] = wu_buf[s].astype(BF16)
        wd_bf[...] = wd_buf[s].astype(BF16)

        @pl.when(next_ref[i] >= 0)
        def _():
            for cp in weight_copies(next_ref[i], 1 - s):
                cp.start()

    @pl.when(used)
    def _():
        hc = jnp.dot(x_ref[...].astype(BF16), wu_bf[...], preferred_element_type=F32) + bu_ref[0]
        g = jnp.minimum(hc[:, :dff], SWIGLU_LIMIT)
        lin = jnp.clip(hc[:, dff:], -SWIGLU_LIMIT, SWIGLU_LIMIT)
        act = (lin + 1.0) * g * jax.nn.sigmoid(SWIGLU_ALPHA * g)
        o_ref[...] = jnp.dot(act.astype(BF16), wd_bf[...], preferred_element_type=F32) + bd_ref[0]

    @pl.when(i >= nused_ref[0])
    def _():
        o_ref[...] = jnp.zeros_like(o_ref)


def _experts(x_rows, block_exp, n_used, w_up, b_up, w_down, b_down, n_blocks):
    R, D = x_rows.shape
    E, _, F2 = w_up.shape
    dff = w_down.shape[1]
    bidx = jnp.arange(n_blocks, dtype=jnp.int32)
    first = ((bidx == 0) | (block_exp != jnp.roll(block_exp, 1))) & (bidx < n_used[0])
    slot = (jnp.cumsum(first) - 1) % 2
    first_pos = jnp.where(first, bidx, n_blocks)
    next_pos = jnp.flip(lax.cummin(jnp.flip(jnp.roll(first_pos, -1).at[-1].set(n_blocks))))
    next_exp = jnp.where(next_pos < n_blocks, block_exp[jnp.minimum(next_pos, n_blocks - 1)], -1)
    blk = lambda i, *_: (i, 0)
    bsel = lambda i, be, *_: (be[i], 0, 0)
    return pl.pallas_call(
        _experts_body,
        grid_spec=pltpu.PrefetchScalarGridSpec(
            num_scalar_prefetch=5,
            grid=(n_blocks,),
            in_specs=[pl.BlockSpec((EXPERT_ROWS, D), blk),
                      pl.BlockSpec(memory_space=pl.ANY),
                      pl.BlockSpec((1, 1, F2), bsel),
                      pl.BlockSpec(memory_space=pl.ANY),
                      pl.BlockSpec((1, 1, D), bsel)],
            out_specs=pl.BlockSpec((EXPERT_ROWS, D), blk),
            scratch_shapes=[pltpu.VMEM((2, D, F2), F32), pltpu.VMEM((2, dff, D), F32),
                            pltpu.VMEM((D, F2), BF16), pltpu.VMEM((dff, D), BF16),
                            pltpu.SemaphoreType.DMA((2, 2))],
        ),
        out_shape=jax.ShapeDtypeStruct((R, D), F32),
        compiler_params=_params("arbitrary"),
        name="experts",
    )(block_exp, n_used, first.astype(jnp.int32), slot.astype(jnp.int32), next_exp.astype(jnp.int32),
      x_rows, w_up, b_up.reshape(E, 1, F2), w_down, b_down.reshape(E, 1, D))


COMBINE_ROWS = 128


def _combine_body(dest_ref, y_hbm, x1_ref, gate_ref, mod_ref, o_ref, buf, sem):
    i = pl.program_id(0)
    t = COMBINE_ROWS

    def issue(g, carry):
        for u in range(ROW_DMA_UNROLL):
            r = g * ROW_DMA_UNROLL + u
            for k in range(TOP_K):
                d = dest_ref[(i * t + r) * TOP_K + k]
                pltpu.make_async_copy(y_hbm.at[pl.ds(d, 1), :], buf.at[pl.ds(r, 1), pl.ds(k * D, D)], sem).start()
        return carry

    D = o_ref.shape[1]
    lax.fori_loop(0, t // ROW_DMA_UNROLL, issue, 0)
    for k in range(TOP_K):
        pltpu.make_async_copy(y_hbm.at[pl.ds(0, t), :], buf.at[:, pl.ds(k * D, D)], sem).wait()
    gates = gate_ref[:, TOP_K:2 * TOP_K]
    y = gates[:, 0:1] * buf[:, 0:D]
    for k in range(1, TOP_K):
        y = y + gates[:, k:k + 1] * buf[:, k * D:(k + 1) * D]
    o_ref[...] = x1_ref[...] + mod_ref[0, 5:6, :] * y


def _combine(y_rows, dest, x1, gates, mod3, rows_per_batch):
    n, D = x1.shape
    t = COMBINE_ROWS
    tpb = rows_per_batch // t
    return pl.pallas_call(
        _combine_body,
        grid_spec=pltpu.PrefetchScalarGridSpec(
            num_scalar_prefetch=1,
            grid=(n // t,),
            in_specs=[pl.BlockSpec(memory_space=pl.ANY),
                      pl.BlockSpec((t, D), lambda i, d: (i, 0)),
                      pl.BlockSpec((t, ROUTE_COLS), lambda i, d: (i, 0)),
                      pl.BlockSpec((1, 6, D), lambda i, d: (i // tpb, 0, 0))],
            out_specs=pl.BlockSpec((t, D), lambda i, d: (i, 0)),
            scratch_shapes=[pltpu.VMEM((t, TOP_K * D), F32), pltpu.SemaphoreType.DMA],
        ),
        out_shape=jax.ShapeDtypeStruct((n, D), F32),
        compiler_params=_params("arbitrary"),
        name="combine",
    )(dest, y_rows, x1, gates, mod3)


def kernel(x, c, positions, ada_w, ada_b, norm1_g, norm2_g, w_in, conv_w, conv_b, w_rg_a, b_rg_a, w_rg_x, b_rg_x, lru_lambda, q_norm_g, k_norm_g, b_gate, w_branch, w_out, w_router, b_router, w_up, b_up, w_down, b_down):
    B, S, D = x.shape
    n = B * S
    depth = ada_w.shape[0]
    x2 = x.reshape(n, D)
    cos_t, sin_t = _rope_tables(positions)
    perm = _rope_head_perm()
    head_cols = (jnp.arange(2 * N_HEADS)[:, None] * HEAD_DIM + perm[None, :]).reshape(-1)
    for l in range(depth):
        mod3 = _adaln(c, ada_w[l], ada_b[l]).reshape(B, 6, D)
        h1 = _norm_mod_call(x2, mod3, norm1_g[l], S)
        w = w_in[l].astype(BF16)
        xr = _proj("plain", h1, w[:, 0:D], [], F32, "proj_xr")
        gg = _proj("gelu", h1, w[:, D:2 * D], [], F32, "proj_gr")
        qk_g = jnp.concatenate([jnp.tile(q_norm_g[l][perm], N_HEADS),
                                jnp.tile(k_norm_g[l][perm], N_HEADS)]).reshape(1, 2 * D)
        qk = _proj("qk", h1, w[:, 2 * D:4 * D][:, head_cols], [qk_g, cos_t, sin_t], F32, "proj_qk")
        v = _proj("plain", h1, w[:, 4 * D:5 * D], [], BF16, "proj_v")
        gl = _proj("gate", h1, w[:, 5 * D:7 * D], [b_gate[l].reshape(1, 2 * D)], F32, "proj_gl")

        y_rnn = _rglru(xr, gg, conv_w[l], conv_b[l], w_rg_a[l], b_rg_a[l], w_rg_x[l], b_rg_x[l],
                       lru_lambda[l], B, S)
        y_att = _moba(qk, v, B, S)

        x1, h2, route, cnt = _merge(y_rnn, y_att, gl, x2, mod3, w_branch[l], w_out[l],
                                    norm2_g[l], w_router[l], b_router[l], S)

        counts = cnt[:, 0]
        padded = (counts + EXPERT_ROWS - 1) // EXPERT_ROWS * EXPERT_ROWS
        pad_end = jnp.cumsum(padded)
        pad_start = pad_end - padded
        top_idx = route[:, 0:TOP_K].astype(jnp.int32)
        rank = route[:, 2 * TOP_K:3 * TOP_K].astype(jnp.int32)
        dest = (pad_start[top_idx] + rank).reshape(n * TOP_K).astype(jnp.int32)
        n_blocks = (n * TOP_K) // EXPERT_ROWS + N_EXPERTS
        block_start = jnp.arange(n_blocks, dtype=jnp.int32) * EXPERT_ROWS
        block_exp = jnp.minimum(jnp.sum(block_start[:, None] >= pad_end[None, :], axis=1),
                                N_EXPERTS - 1).astype(jnp.int32)
        n_used = (pad_end[-1] // EXPERT_ROWS).astype(jnp.int32)
        last_blk = jnp.where(padded > 0, pad_end // EXPERT_ROWS - 1, -1)
        spare = n_used + jnp.arange(N_EXPERTS, dtype=jnp.int32)
        spare = jnp.where(spare < n_blocks, spare, -1)
        zero_blocks = jnp.concatenate([last_blk, spare]).astype(jnp.int32)

        x_rows = _dispatch(h2, dest, zero_blocks, n_blocks * EXPERT_ROWS)
        y_rows = _experts(x_rows, block_exp, n_used.reshape(1), w_up[l], b_up[l], w_down[l], b_down[l], n_blocks)
        x2 = _combine(y_rows, dest, x1, route, mod3, S)
    return x2.reshape(B, S, D)
```

```python
import functools

import jax
import jax.numpy as jnp
from jax import lax
from jax.experimental import pallas as pl
from jax.experimental.pallas import tpu as pltpu

F32 = jnp.float32
BF16 = jnp.bfloat16
HIGHEST = lax.Precision.HIGHEST

EPS = 1e-6
LANES = 128
N_HEADS = 8
HEAD_DIM = 128
RNN_BLOCKS = 8
RNN_BW = 128
CONV_W = 4
LRU_C = 8.0
MOBA_BLOCK = 256
MOBA_TOPK = 3
MOBA_CHUNK = 4
MOBA_HEADS = 4
MERGE_SUB = 256
LOG2_E = 1.4426950408889634
ROPE_DIMS = HEAD_DIM // 4
ROPE_THETA = 500000.0
N_EXPERTS = 32
TOP_K = 4
SWIGLU_LIMIT = 7.0
SWIGLU_ALPHA = 1.702
EXPERT_ROWS = 256
NEG_BIG = -1e30

VMEM_LIMIT = 56 * 1024 * 1024


def _params(*sem):
    return pltpu.CompilerParams(dimension_semantics=sem, vmem_limit_bytes=VMEM_LIMIT)


def _adaln_body(c_ref, w_ref, b_ref, o_ref):
    cs = c_ref[...]
    cs = cs * jax.nn.sigmoid(cs)
    o_ref[...] = jnp.dot(cs, w_ref[...], preferred_element_type=F32, precision=HIGHEST) + b_ref[...]


def _adaln(c, ada_w, ada_b):
    B, D = c.shape
    W = ada_w.shape[1]
    cpad = jnp.zeros((8, D), F32).at[:B].set(c)
    tn = 1024
    mod = pl.pallas_call(
        _adaln_body,
        grid=(W // tn,),
        in_specs=[pl.BlockSpec((8, D), lambda j: (0, 0)),
                  pl.BlockSpec((D, tn), lambda j: (0, j)),
                  pl.BlockSpec((1, tn), lambda j: (0, j))],
        out_specs=pl.BlockSpec((8, tn), lambda j: (0, j)),
        out_shape=jax.ShapeDtypeStruct((8, W), F32),
        compiler_params=_params("parallel"),
        name="adaln",
    )(cpad, ada_w, ada_b.reshape(1, W))
    return mod[:B]


def _rope_head_perm():
    half = ROPE_DIMS // 2
    mid = HEAD_DIM // 2
    return jnp.concatenate([jnp.arange(0, half), jnp.arange(ROPE_DIMS, mid + half),
                            jnp.arange(half, ROPE_DIMS), jnp.arange(mid + half, HEAD_DIM)])


def _rope_body(pos_ref, freq_ref, c_ref, s_ref):
    ang = pos_ref[...].astype(F32) * freq_ref[...]
    lane = lax.broadcasted_iota(jnp.int32, ang.shape, 1)
    half = ROPE_DIMS // 2
    mid = HEAD_DIM // 2
    s = jnp.sin(ang)
    c_ref[...] = jnp.cos(ang)
    s_ref[...] = jnp.where(lane < half, -s, jnp.where((lane >= mid) & (lane < mid + half), s, 0.0))


def _rope_tables(positions):
    n = positions.size
    half = ROPE_DIMS // 2
    mid = HEAD_DIM // 2
    freqs = ROPE_THETA ** (-jnp.arange(half, dtype=F32) / half)
    freq_lane = jnp.zeros((1, LANES), F32).at[0, :half].set(freqs).at[0, mid:mid + half].set(freqs)
    tm = 1024
    tab = jax.ShapeDtypeStruct((n, LANES), F32)
    return pl.pallas_call(
        _rope_body,
        grid=(n // tm,),
        in_specs=[pl.BlockSpec((tm, 1), lambda i: (i, 0)),
                  pl.BlockSpec((1, LANES), lambda i: (0, 0))],
        out_specs=[pl.BlockSpec((tm, LANES), lambda i: (i, 0))] * 2,
        out_shape=[tab, tab],
        compiler_params=_params("parallel"),
        name="rope_tab",
    )(positions.reshape(n, 1), freq_lane)


def _norm_mod(x, g, sh, sc):
    ms = jnp.mean(x * x, axis=-1, keepdims=True)
    y = x * lax.rsqrt(ms + EPS)
    return (y * g) * (1.0 + sc) + sh


def _norm_mod_body(x_ref, mod_ref, g_ref, o_ref):
    o_ref[...] = _norm_mod(x_ref[...], g_ref[...], mod_ref[0, 0:1, :], mod_ref[0, 1:2, :]).astype(o_ref.dtype)


def _norm_mod_call(x2, mod3, g, rows_per_batch):
    n, D = x2.shape
    tm = 512
    tpb = rows_per_batch // tm
    return pl.pallas_call(
        _norm_mod_body,
        grid=(n // tm,),
        in_specs=[pl.BlockSpec((tm, D), lambda i: (i, 0)),
                  pl.BlockSpec((1, 6, D), lambda i: (i // tpb, 0, 0)),
                  pl.BlockSpec((1, D), lambda i: (0, 0))],
        out_specs=pl.BlockSpec((tm, D), lambda i: (i, 0)),
        out_shape=jax.ShapeDtypeStruct((n, D), BF16),
        compiler_params=_params("parallel"),
        name="norm_mod",
    )(x2, mod3, g.reshape(1, D))


def _gelu_tanh(x):
    return 0.5 * x * (1.0 + jnp.tanh(0.7978845608028654 * (x + 0.044715 * (x * x * x))))


PROJ_SUB = 256
PROJ_PIECE = 256


def _qk_head(seg, g, cos, sin):
    ms = jnp.mean(seg * seg, axis=-1, keepdims=True)
    y = seg * lax.rsqrt(ms + EPS) * g
    return y * cos + pltpu.roll(y, HEAD_DIM // 2, axis=1) * sin


def _proj_stages(kind, rows, h_ref, w_ref, extras, o_ref):
    n_pieces = w_ref.shape[1] // PROJ_PIECE
    pieces = []
    for j in range(n_pieces):
        pieces.append(jnp.dot(h_ref[rows, :], w_ref[:, j * PROJ_PIECE:(j + 1) * PROJ_PIECE],
                              preferred_element_type=F32))
        yield
    for j in range(n_pieces):
        cols = slice(j * PROJ_PIECE, (j + 1) * PROJ_PIECE)
        acc = pieces[j]
        if kind == "gelu":
            acc = _gelu_tanh(acc)
        elif kind == "gate":
            (b_ref,) = extras
            acc = jax.nn.sigmoid(acc + b_ref[:, cols])
        elif kind == "qk":
            g_ref, c_ref, s_ref = extras
            cos, sin = c_ref[rows, :], s_ref[rows, :]
            acc = jnp.concatenate(
                [_qk_head(acc[:, o:o + HEAD_DIM], g_ref[:, cols.start + o:cols.start + o + HEAD_DIM], cos, sin)
                 for o in range(0, PROJ_PIECE, HEAD_DIM)], axis=1)
        o_ref[rows, cols] = acc.astype(o_ref.dtype)
        yield


def _proj_body(kind, h_ref, w_ref, *rest):
    *extras, o_ref = rest
    first, second = [_proj_stages(kind, slice(t * PROJ_SUB, (t + 1) * PROJ_SUB), h_ref, w_ref, extras, o_ref)
                     for t in range(2)]
    n_pieces = w_ref.shape[1] // PROJ_PIECE
    for _ in range(n_pieces):
        next(first)
    for _ in range(n_pieces):
        next(second)
        next(first)
    for stage in second:
        pass


def _proj(kind, h, w, extras, out_dtype, name):
    n, D = h.shape
    W = w.shape[1]
    tm, tn = 512, 1024
    in_specs = [pl.BlockSpec((tm, D), lambda j, i: (i, 0)),
                pl.BlockSpec((D, tn), lambda j, i: (0, j))]
    args = [h, w]
    for arr in extras:
        if arr.shape[0] == 1:
            in_specs.append(pl.BlockSpec((1, tn), lambda j, i: (0, j)))
        else:
            in_specs.append(pl.BlockSpec((tm, LANES), lambda j, i: (i, 0)))
        args.append(arr)
    return pl.pallas_call(
        functools.partial(_proj_body, kind),
        grid=(W // tn, n // tm),
        in_specs=in_specs,
        out_specs=pl.BlockSpec((tm, tn), lambda j, i: (i, j)),
        out_shape=jax.ShapeDtypeStruct((n, W), out_dtype),
        compiler_params=_params("parallel", "parallel"),
        name=name,
    )(*args)


def _rglru_body(xr_ref, gg_ref, cw_ref, cb_ref, wa_ref, ba_ref, wx_ref, bx_ref, lam_ref,
                o_ref, xbuf, hcar, a_s, u_s):
    s = pl.program_id(1)
    ts, D = xr_ref.shape

    @pl.when(s == 0)
    def _():
        xbuf[0:8, :] = jnp.zeros((8, D), F32)
        hcar[...] = jnp.zeros_like(hcar)

    @pl.when(s > 0)
    def _():
        xbuf[0:8, :] = xbuf[ts:ts + 8, :]

    xbuf[8:8 + ts, :] = xr_ref[...]
    xc = cb_ref[...] + cw_ref[0:1, :] * xbuf[8:8 + ts, :]
    for i in range(1, CONV_W):
        xc = xc + cw_ref[i:i + 1, :] * xbuf[8 - i:8 - i + ts, :]

    ra, rx = [], []
    for n in range(RNN_BLOCKS):
        xb = xc[:, n * RNN_BW:(n + 1) * RNN_BW].astype(BF16)
        ra.append(jnp.dot(xb, wa_ref[n], preferred_element_type=F32))
        rx.append(jnp.dot(xb, wx_ref[n], preferred_element_type=F32))
    r = jax.nn.sigmoid(jnp.concatenate(ra, axis=1) + ba_ref[...])
    ig = jax.nn.sigmoid(jnp.concatenate(rx, axis=1) + bx_ref[...])

    z = -lam_ref[...]
    softplus = jnp.maximum(z, 0.0) + jnp.log1p(jnp.exp(-jnp.abs(z)))
    log_a = -LRU_C * r * softplus
    a = jnp.exp(log_a)
    mult = jnp.sqrt(1.0 - a * a)
    row = lax.broadcasted_iota(jnp.int32, (ts, D), 0)
    mult = jnp.where((row == 0) & (s == 0), 1.0, mult)
    u = mult * (ig * xc)

    rm = row & 7
    for d in (1, 2, 4):
        keep = rm >= d
        a_sh = pltpu.roll(a, d, axis=0)
        u_sh = pltpu.roll(u, d, axis=0)
        u = jnp.where(keep, a * u_sh + u, u)
        a = jnp.where(keep, a * a_sh, a)
    a_s[...] = a
    u_s[...] = u

    def group(g, h):
        r0 = pl.multiple_of(g * 8, 8)
        hg = u_s[pl.ds(r0, 8), :] + a_s[pl.ds(r0, 8), :] * h
        u_s[pl.ds(r0, 8), :] = hg
        return hg[7:8, :]

    hcar[...] = lax.fori_loop(0, ts // 8, group, hcar[...])
    o_ref[...] = (u_s[...] * gg_ref[...]).astype(o_ref.dtype)


def _rglru(xr, gg, conv_w, conv_b, w_a, b_a, w_x, b_x, lam, B, S):
    n, D = xr.shape
    ts = 256
    spb = S // ts
    row = lambda b, s: (b * spb + s, 0)
    vec = lambda b, s: (0, 0)
    return pl.pallas_call(
        _rglru_body,
        grid=(B, spb),
        in_specs=[pl.BlockSpec((ts, D), row),
                  pl.BlockSpec((ts, D), row),
                  pl.BlockSpec((CONV_W, D), vec),
                  pl.BlockSpec((1, D), vec),
                  pl.BlockSpec((RNN_BLOCKS, RNN_BW, RNN_BW), lambda b, s: (0, 0, 0)),
                  pl.BlockSpec((1, D), vec),
                  pl.BlockSpec((RNN_BLOCKS, RNN_BW, RNN_BW), lambda b, s: (0, 0, 0)),
                  pl.BlockSpec((1, D), vec),
                  pl.BlockSpec((1, D), vec)],
        out_specs=pl.BlockSpec((ts, D), row),
        out_shape=jax.ShapeDtypeStruct((n, D), BF16),
        scratch_shapes=[pltpu.VMEM((ts + 8, D), F32), pltpu.VMEM((1, D), F32),
                        pltpu.VMEM((ts, D), F32), pltpu.VMEM((ts, D), F32)],
        compiler_params=_params("arbitrary", "arbitrary"),
        name="rglru",
    )(xr, gg, conv_w, conv_b.reshape(1, D), w_a.astype(BF16), b_a.reshape(1, D),
      w_x.astype(BF16), b_x.reshape(1, D), lam.reshape(1, D))


def _moba_body(q_ref, k_ref, v_ref, o_ref, kaug, vaug, kmean, m_s, acc_s):
    qb = pl.program_id(2)
    S = k_ref.shape[0]
    nb = S // MOBA_BLOCK
    bs = MOBA_BLOCK
    dh = HEAD_DIM
    heads = range(MOBA_HEADS)
    hcol = lambda hh: slice(hh * dh, (hh + 1) * dh)

    @pl.when(qb == 0)
    def _():
        blk = lax.broadcasted_iota(jnp.int32, (S, dh), 0) // bs
        col = lax.broadcasted_iota(jnp.int32, (S, dh), 1)
        onehot = jnp.where(col == blk, 1.0, 0.0).astype(BF16)
        for hh in heads:
            kaug[hh, :, 0:dh] = k_ref[:, hcol(hh)].astype(BF16)
            kaug[hh, :, dh:2 * dh] = onehot
            vaug[hh, :, 0:dh] = v_ref[:, hcol(hh)]
            vaug[hh, :, dh:2 * dh] = jnp.ones((S, dh), BF16)
            kmean[hh] = jnp.zeros((LANES, dh), F32)
            for j in range(nb):
                kmean[hh, j:j + 1, :] = jnp.mean(k_ref[j * bs:(j + 1) * bs, hcol(hh)], axis=0, keepdims=True)

    scale = HEAD_DIM ** -0.5
    nt = (((1,), (1,)), ((), ()))
    c2 = scale * LOG2_E
    r0 = pl.multiple_of(qb * bs, bs)
    nb_pad = -(-nb // 8) * 8
    rr = lax.broadcasted_iota(jnp.int32, (bs, bs), 0)
    cc = lax.broadcasted_iota(jnp.int32, (bs, bs), 1)

    qs = [q_ref[:, hcol(hh)] for hh in heads]
    qbfs = [q.astype(BF16) for q in qs]

    blk_i = lax.broadcasted_iota(jnp.int32, (nb_pad, bs), 0)
    blk_f = blk_i.astype(F32)
    past = blk_i < qb
    gs = [jnp.where(past, lax.dot_general(kmean[hh, 0:nb_pad, :], qs[hh], nt, preferred_element_type=F32,
                                          precision=HIGHEST), -jnp.inf) for hh in heads]
    ss = [jnp.where(cc <= rr, lax.dot_general(qbfs[hh], kaug[hh, pl.ds(r0, bs), 0:dh], nt,
                                              preferred_element_type=F32), NEG_BIG) for hh in heads]
    biases = [jnp.full((nb_pad, bs), NEG_BIG, F32) for _ in heads]
    for _ in range(MOBA_TOPK):
        for hh in heads:
            m = jnp.max(gs[hh], axis=0, keepdims=True)
            idx = jnp.min(jnp.where(gs[hh] == m, blk_f, float(LANES)), axis=0, keepdims=True)
            pick = (blk_f == idx) & past
            biases[hh] = jnp.where(pick, 0.0, biases[hh])
            gs[hh] = jnp.where(pick, -jnp.inf, gs[hh])
    fill = jnp.full((bs, dh - nb_pad), NEG_BIG, BF16)
    qaugs = [jnp.concatenate([qbfs[hh], biases[hh].T.astype(BF16), fill], axis=1) for hh in heads]

    ps = []
    for hh in heads:
        m0 = jnp.max(ss[hh], axis=1, keepdims=True)
        ps.append(jnp.exp2((ss[hh] - m0) * c2).astype(BF16))
        m_s[hh] = m0
    for hh in heads:
        acc_s[hh] = jnp.dot(ps[hh], vaug[hh, pl.ds(r0, bs), :], preferred_element_type=F32)

    cw = MOBA_CHUNK * bs
    for c in range(nb // MOBA_CHUNK):
        @pl.when(c * MOBA_CHUNK < qb)
        def _(c=c):
            scs = [lax.dot_general(qaugs[hh], kaug[hh, c * cw:(c + 1) * cw, :], nt, preferred_element_type=F32)
                   for hh in heads]
            pcs, alphas = [], []
            for hh in heads:
                m_old = m_s[hh]
                m_new = jnp.maximum(m_old, jnp.max(scs[hh], axis=1, keepdims=True))
                alphas.append(jnp.exp2((m_old - m_new) * c2))
                pcs.append(jnp.exp2((scs[hh] - m_new) * c2).astype(BF16))
                m_s[hh] = m_new
            for hh in heads:
                acc_s[hh] = alphas[hh] * acc_s[hh] + jnp.dot(pcs[hh], vaug[hh, c * cw:(c + 1) * cw, :],
                                                             preferred_element_type=F32)

    for hh in heads:
        o_ref[:, hcol(hh)] = (acc_s[hh, :, 0:dh] / acc_s[hh, :, dh:dh + 1]).astype(o_ref.dtype)


def _moba(qk, v, B, S):
    n, D = v.shape
    nq = S // MOBA_BLOCK
    hs = MOBA_HEADS
    w = hs * HEAD_DIM
    return pl.pallas_call(
        _moba_body,
        grid=(B, N_HEADS // hs, nq),
        in_specs=[pl.BlockSpec((MOBA_BLOCK, w), lambda b, h, i: (b * nq + i, h)),
                  pl.BlockSpec((S, w), lambda b, h, i: (b, N_HEADS // hs + h)),
                  pl.BlockSpec((S, w), lambda b, h, i: (b, h))],
        out_specs=pl.BlockSpec((MOBA_BLOCK, w), lambda b, h, i: (b * nq + i, h)),
        out_shape=jax.ShapeDtypeStruct((n, D), BF16),
        scratch_shapes=[pltpu.VMEM((hs, S, 2 * HEAD_DIM), BF16), pltpu.VMEM((hs, S, 2 * HEAD_DIM), BF16),
                        pltpu.VMEM((hs, LANES, HEAD_DIM), F32), pltpu.VMEM((hs, MOBA_BLOCK, 1), F32),
                        pltpu.VMEM((hs, MOBA_BLOCK, 2 * HEAD_DIM), F32)],
        compiler_params=_params("parallel", "parallel", "arbitrary"),
        name="moba",
    )(qk, qk, v)


ROUTE_COLS = 16


def _merge_body(yr_ref, ya_ref, gr_ref, ga_ref, x_ref, mod_ref, wb_ref, wo_ref, g2_ref, wr_ref, br_ref,
                x1_ref, h2_ref, route_ref, cnt_ref, carry):
    i = pl.program_id(0)

    @pl.when(i == 0)
    def _():
        carry[...] = jnp.zeros_like(carry)

    tm = MERGE_SUB
    counts = {"total": carry[...]}
    first, second = [
        _merge_stages(slice(h * tm, (h + 1) * tm), counts, yr_ref, ya_ref, gr_ref, ga_ref, x_ref, mod_ref,
                      wb_ref, wo_ref, g2_ref, wr_ref, br_ref, x1_ref, h2_ref, route_ref)
        for h in range(2)]
    for _ in range(MERGE_MATMUL_STAGES):
        next(first)
    for stage in first:
        next(second)
    for stage in second:
        pass
    carry[...] = counts["total"]
    cnt_ref[...] = jnp.broadcast_to(counts["total"], cnt_ref.shape).astype(jnp.int32)


MERGE_MATMUL_STAGES = 4


def _merge_stages(rows, counts, yr_ref, ya_ref, gr_ref, ga_ref, x_ref, mod_ref, wb_ref, wo_ref, g2_ref, wr_ref,
                  br_ref, x1_ref, h2_ref, route_ref):
    tm = rows.stop - rows.start
    zr = jnp.dot(yr_ref[rows, :], wb_ref[0], preferred_element_type=F32)
    yield
    za = jnp.dot(ya_ref[rows, :], wb_ref[1], preferred_element_type=F32)
    mix = (gr_ref[rows, :] * zr + ga_ref[rows, :] * za).astype(BF16)
    yield
    mixed = jnp.dot(mix, wo_ref[...], preferred_element_type=F32)
    x1 = x_ref[rows, :] + mod_ref[0, 2:3, :] * mixed
    x1_ref[rows, :] = x1
    h2 = _norm_mod(x1, g2_ref[...], mod_ref[0, 3:4, :], mod_ref[0, 4:5, :])
    h2_ref[rows, :] = h2
    yield

    nt = (((1,), (1,)), ((), ()))
    ne = N_EXPERTS
    hi = h2.astype(BF16)
    lo = (h2 - hi.astype(F32)).astype(BF16)
    both = lax.dot_general(wr_ref[...], hi, nt, preferred_element_type=F32)
    lg = (both[:ne] + both[ne:]
          + lax.dot_general(wr_ref[0:ne, :], lo, nt, preferred_element_type=F32) + br_ref[...])
    yield

    eidx = lax.broadcasted_iota(jnp.int32, lg.shape, 0).astype(F32)
    vals, idxs = [], []
    onehot = jnp.zeros(lg.shape, F32)
    for _ in range(TOP_K):
        m = jnp.max(lg, axis=0, keepdims=True)
        idx = jnp.min(jnp.where(lg == m, eidx, float(ne)), axis=0, keepdims=True)
        pick = eidx == idx
        onehot = jnp.where(pick, 1.0, onehot)
        lg = jnp.where(pick, -jnp.inf, lg)
        vals.append(m)
        idxs.append(idx)
        yield
    es = [jnp.exp(v - vals[0]) for v in vals]
    den = es[0] + es[1] + es[2] + es[3]
    before0 = counts["total"]

    rr = lax.broadcasted_iota(jnp.int32, (tm, tm), 0)
    cc = lax.broadcasted_iota(jnp.int32, (tm, tm), 1)
    earlier = jnp.where(rr < cc, 1.0, 0.0).astype(BF16)
    before = jnp.dot(onehot.astype(BF16), earlier, preferred_element_type=F32) + before0
    ranks = [jnp.sum(jnp.where(eidx == idxs[k], before, 0.0), axis=0, keepdims=True) for k in range(TOP_K)]

    rows_t = jnp.concatenate(idxs + [e / den for e in es] + ranks
                             + [jnp.zeros((ROUTE_COLS - 3 * TOP_K, tm), F32)], axis=0)
    route_ref[rows, :] = rows_t.T
    counts["total"] = before0 + jnp.sum(onehot, axis=1, keepdims=True)
    yield


def _merge(y_rnn, y_att, gl, x2, mod3, w_branch, w_out, norm2_g, w_router, b_router, rows_per_batch):
    n, D = x2.shape
    tm = 2 * MERGE_SUB
    tpb = rows_per_batch // tm
    wr_t = w_router.T
    wr_hi = wr_t.astype(BF16)
    wr = jnp.concatenate([wr_hi, (wr_t - wr_hi.astype(F32)).astype(BF16)], axis=0)
    br = b_router.reshape(N_EXPERTS, 1)
    row = lambda i: (i, 0)
    fixed = lambda i: (0, 0)
    once = pl.Buffered(1)
    wide = jax.ShapeDtypeStruct((n, D), F32)
    return pl.pallas_call(
        _merge_body,
        grid=(n // tm,),
        in_specs=[pl.BlockSpec((tm, D), row),
                  pl.BlockSpec((tm, D), row),
                  pl.BlockSpec((tm, D), lambda i: (i, 0)),
                  pl.BlockSpec((tm, D), lambda i: (i, 1)),
                  pl.BlockSpec((tm, D), row),
                  pl.BlockSpec((1, 6, D), lambda i: (i // tpb, 0, 0)),
                  pl.BlockSpec((2, D, D), lambda i: (0, 0, 0), pipeline_mode=once),
                  pl.BlockSpec((D, D), fixed, pipeline_mode=once),
                  pl.BlockSpec((1, D), fixed),
                  pl.BlockSpec((2 * N_EXPERTS, D), fixed, pipeline_mode=once),
                  pl.BlockSpec((N_EXPERTS, 1), fixed)],
        out_specs=[pl.BlockSpec((tm, D), row),
                   pl.BlockSpec((tm, D), row),
                   pl.BlockSpec((tm, ROUTE_COLS), row),
                   pl.BlockSpec((N_EXPERTS, LANES), fixed)],
        out_shape=[wide, wide, jax.ShapeDtypeStruct((n, ROUTE_COLS), F32),
                   jax.ShapeDtypeStruct((N_EXPERTS, LANES), jnp.int32)],
        scratch_shapes=[pltpu.VMEM((N_EXPERTS, 1), F32)],
        compiler_params=_params("arbitrary"),
        name="merge",
    )(y_rnn, y_att, gl, gl, x2, mod3, w_branch.astype(BF16), w_out.astype(BF16),
      norm2_g.reshape(1, D), wr, br)


DISPATCH_TOKENS = 256
ROW_DMA_UNROLL = 8


def _dispatch_body(exp_ref, rank_ref, start_ref, zblk_ref, src_ref, dst_hbm, zbuf, sem, zsem):
    t = DISPATCH_TOKENS
    c = pl.program_id(0)

    @pl.when(c == 0)
    def _():
        zbuf[...] = jnp.zeros_like(zbuf)

        def zero_copy(b):
            r0 = pl.multiple_of(b * EXPERT_ROWS, EXPERT_ROWS)
            return pltpu.make_async_copy(zbuf, dst_hbm.at[pl.ds(r0, EXPERT_ROWS), :], zsem)

        def zissue(j, carry):
            @pl.when(zblk_ref[j] >= 0)
            def _():
                zero_copy(zblk_ref[j]).start()
            return carry

        def zwait(j, carry):
            @pl.when(zblk_ref[j] >= 0)
            def _():
                zero_copy(zblk_ref[j]).wait()
            return carry

        lax.fori_loop(0, zblk_ref.shape[0], zissue, 0)
        lax.fori_loop(0, zblk_ref.shape[0], zwait, 0)

    def issue(g, carry):
        for u in range(ROW_DMA_UNROLL):
            r = g * ROW_DMA_UNROLL + u
            for k in range(TOP_K):
                a = (c * t + r) * TOP_K + k
                d = start_ref[exp_ref[a]] + rank_ref[a]
                pltpu.make_async_copy(src_ref.at[pl.ds(r, 1), :], dst_hbm.at[pl.ds(d, 1), :], sem).start()
        return carry

    lax.fori_loop(0, t // ROW_DMA_UNROLL, issue, 0)
    for k in range(TOP_K):
        pltpu.make_async_copy(src_ref, dst_hbm.at[pl.ds(0, t), :], sem).wait()


def _dispatch(h2, choice_exp, choice_rank, pad_start, zero_blocks, n_rows):
    n, D = h2.shape
    return pl.pallas_call(
        _dispatch_body,
        grid_spec=pltpu.PrefetchScalarGridSpec(
            num_scalar_prefetch=4,
            grid=(n // DISPATCH_TOKENS,),
            in_specs=[pl.BlockSpec((DISPATCH_TOKENS, D), lambda i, *_: (i, 0))],
            out_specs=pl.BlockSpec(memory_space=pl.ANY),
            scratch_shapes=[pltpu.VMEM((EXPERT_ROWS, D), F32), pltpu.SemaphoreType.DMA,
                            pltpu.SemaphoreType.DMA],
        ),
        out_shape=jax.ShapeDtypeStruct((n_rows, D), F32),
        compiler_params=_params("arbitrary"),
        name="dispatch",
    )(choice_exp, choice_rank, pad_start, zero_blocks, h2)


def _experts_body(bexp_ref, nused_ref, first_ref, slot_ref, next_ref, x_ref, wu_hbm, bu_ref, wd_hbm, bd_ref,
                  o_ref, wu_buf, wd_buf, wu_bf, wd_bf, sems):
    i = pl.program_id(0)
    dff = wd_hbm.shape[1]
    used = i < nused_ref[0]

    def weight_copies(e, s):
        return (pltpu.make_async_copy(wu_hbm.at[e], wu_buf.at[s], sems.at[0, s]),
                pltpu.make_async_copy(wd_hbm.at[e], wd_buf.at[s], sems.at[1, s]))

    @pl.when(i == 0)
    def _():
        for cp in weight_copies(bexp_ref[0], 0):
            cp.start()

    @pl.when(used & (first_ref[i] == 1))
    def _():
        s = slot_ref[i]
        for cp in weight_copies(bexp_ref[i], s):
            cp.wait()
        wu_bf[...] = wu_buf[s].astype(BF16)
        wd_bf[...] = wd_buf[s].astype(BF16)

        @pl.when(next_ref[i] >= 0)
        def _():
            for cp in weight_copies(next_ref[i], 1 - s):
                cp.start()

    @pl.when(used)
    def _():
        hc = jnp.dot(x_ref[...].astype(BF16), wu_bf[...], preferred_element_type=F32) + bu_ref[0]
        g = jnp.minimum(hc[:, :dff], SWIGLU_LIMIT)
        lin = jnp.clip(hc[:, dff:], -SWIGLU_LIMIT, SWIGLU_LIMIT)
        act = (lin + 1.0) * g * jax.nn.sigmoid(SWIGLU_ALPHA * g)
        o_ref[...] = jnp.dot(act.astype(BF16), wd_bf[...], preferred_element_type=F32) + bd_ref[0]

    @pl.when(i >= nused_ref[0])
    def _():
        o_ref[...] = jnp.zeros_like(o_ref)


def _experts(x_rows, block_exp, n_used, w_up, b_up, w_down, b_down, n_blocks):
    R, D = x_rows.shape
    E, _, F2 = w_up.shape
    dff = w_down.shape[1]
    bidx = jnp.arange(n_blocks, dtype=jnp.int32)
    first = ((bidx == 0) | (block_exp != jnp.roll(block_exp, 1))) & (bidx < n_used[0])
    slot = (jnp.cumsum(first) - 1) % 2
    first_pos = jnp.where(first, bidx, n_blocks)
    next_pos = jnp.flip(lax.cummin(jnp.flip(jnp.roll(first_pos, -1).at[-1].set(n_blocks))))
    next_exp = jnp.where(next_pos < n_blocks, block_exp[jnp.minimum(next_pos, n_blocks - 1)], -1)
    blk = lambda i, *_: (i, 0)
    bsel = lambda i, be, *_: (be[i], 0, 0)
    return pl.pallas_call(
        _experts_body,
        grid_spec=pltpu.PrefetchScalarGridSpec(
            num_scalar_prefetch=5,
            grid=(n_blocks,),
            in_specs=[pl.BlockSpec((EXPERT_ROWS, D), blk),
                      pl.BlockSpec(memory_space=pl.ANY),
                      pl.BlockSpec((1, 1, F2), bsel),
                      pl.BlockSpec(memory_space=pl.ANY),
                      pl.BlockSpec((1, 1, D), bsel)],
            out_specs=pl.BlockSpec((EXPERT_ROWS, D), blk),
            scratch_shapes=[pltpu.VMEM((2, D, F2), F32), pltpu.VMEM((2, dff, D), F32),
                            pltpu.VMEM((D, F2), BF16), pltpu.VMEM((dff, D), BF16),
                            pltpu.SemaphoreType.DMA((2, 2))],
        ),
        out_shape=jax.ShapeDtypeStruct((R, D), F32),
        compiler_params=_params("arbitrary"),
        name="experts",
    )(block_exp, n_used, first.astype(jnp.int32), slot.astype(jnp.int32), next_exp.astype(jnp.int32),
      x_rows, w_up, b_up.reshape(E, 1, F2), w_down, b_down.reshape(E, 1, D))


COMBINE_ROWS = 128


def _combine_body(exp_ref, rank_ref, start_ref, y_hbm, x1_ref, gate_ref, mod_ref, o_ref, buf, sem):
    i = pl.program_id(0)
    t = COMBINE_ROWS

    def issue(g, carry):
        for u in range(ROW_DMA_UNROLL):
            r = g * ROW_DMA_UNROLL + u
            for k in range(TOP_K):
                a = (i * t + r) * TOP_K + k
                d = start_ref[exp_ref[a]] + rank_ref[a]
                pltpu.make_async_copy(y_hbm.at[pl.ds(d, 1), :], buf.at[pl.ds(r, 1), pl.ds(k * D, D)], sem).start()
        return carry

    D = o_ref.shape[1]
    lax.fori_loop(0, t // ROW_DMA_UNROLL, issue, 0)
    for k in range(TOP_K):
        pltpu.make_async_copy(y_hbm.at[pl.ds(0, t), :], buf.at[:, pl.ds(k * D, D)], sem).wait()
    gates = gate_ref[:, TOP_K:2 * TOP_K]
    y = gates[:, 0:1] * buf[:, 0:D]
    for k in range(1, TOP_K):
        y = y + gates[:, k:k + 1] * buf[:, k * D:(k + 1) * D]
    o_ref[...] = x1_ref[...] + mod_ref[0, 5:6, :] * y


def _combine(y_rows, choice_exp, choice_rank, pad_start, x1, gates, mod3, rows_per_batch):
    n, D = x1.shape
    t = COMBINE_ROWS
    tpb = rows_per_batch // t
    return pl.pallas_call(
        _combine_body,
        grid_spec=pltpu.PrefetchScalarGridSpec(
            num_scalar_prefetch=3,
            grid=(n // t,),
            in_specs=[pl.BlockSpec(memory_space=pl.ANY),
                      pl.BlockSpec((t, D), lambda i, *_: (i, 0)),
                      pl.BlockSpec((t, ROUTE_COLS), lambda i, *_: (i, 0)),
                      pl.BlockSpec((1, 6, D), lambda i, *_: (i // tpb, 0, 0))],
            out_specs=pl.BlockSpec((t, D), lambda i, *_: (i, 0)),
            scratch_shapes=[pltpu.VMEM((t, TOP_K * D), F32), pltpu.SemaphoreType.DMA],
        ),
        out_shape=jax.ShapeDtypeStruct((n, D), F32),
        compiler_params=_params("arbitrary"),
        name="combine",
    )(choice_exp, choice_rank, pad_start, y_rows, x1, gates, mod3)


def kernel(x, c, positions, ada_w, ada_b, norm1_g, norm2_g, w_in, conv_w, conv_b, w_rg_a, b_rg_a, w_rg_x, b_rg_x, lru_lambda, q_norm_g, k_norm_g, b_gate, w_branch, w_out, w_router, b_router, w_up, b_up, w_down, b_down):
    B, S, D = x.shape
    n = B * S
    depth = ada_w.shape[0]
    x2 = x.reshape(n, D)
    cos_t, sin_t = _rope_tables(positions)
    perm = _rope_head_perm()
    head_cols = (jnp.arange(2 * N_HEADS)[:, None] * HEAD_DIM + perm[None, :]).reshape(-1)
    for l in range(depth):
        mod3 = _adaln(c, ada_w[l], ada_b[l]).reshape(B, 6, D)
        h1 = _norm_mod_call(x2, mod3, norm1_g[l], S)
        w = w_in[l].astype(BF16)
        xr = _proj("plain", h1, w[:, 0:D], [], F32, "proj_xr")
        gg = _proj("gelu", h1, w[:, D:2 * D], [], F32, "proj_gr")
        qk_g = jnp.concatenate([jnp.tile(q_norm_g[l][perm], N_HEADS),
                                jnp.tile(k_norm_g[l][perm], N_HEADS)]).reshape(1, 2 * D)
        qk = _proj("qk", h1, w[:, 2 * D:4 * D][:, head_cols], [qk_g, cos_t, sin_t], F32, "proj_qk")
        v = _proj("plain", h1, w[:, 4 * D:5 * D], [], BF16, "proj_v")
        gl = _proj("gate", h1, w[:, 5 * D:7 * D], [b_gate[l].reshape(1, 2 * D)], F32, "proj_gl")

        y_rnn = _rglru(xr, gg, conv_w[l], conv_b[l], w_rg_a[l], b_rg_a[l], w_rg_x[l], b_rg_x[l],
                       lru_lambda[l], B, S)
        y_att = _moba(qk, v, B, S)

        x1, h2, route, cnt = _merge(y_rnn, y_att, gl, x2, mod3, w_branch[l], w_out[l],
                                    norm2_g[l], w_router[l], b_router[l], S)

        counts = cnt[:, 0]
        padded = (counts + EXPERT_ROWS - 1) // EXPERT_ROWS * EXPERT_ROWS
        pad_end = jnp.cumsum(padded)
        pad_start = pad_end - padded
        pad_start = pad_start.astype(jnp.int32)
        choice_exp = route[:, 0:TOP_K].astype(jnp.int32).reshape(n * TOP_K)
        choice_rank = route[:, 2 * TOP_K:3 * TOP_K].astype(jnp.int32).reshape(n * TOP_K)
        n_blocks = (n * TOP_K) // EXPERT_ROWS + N_EXPERTS
        block_start = jnp.arange(n_blocks, dtype=jnp.int32) * EXPERT_ROWS
        block_exp = jnp.minimum(jnp.sum(block_start[:, None] >= pad_end[None, :], axis=1),
                                N_EXPERTS - 1).astype(jnp.int32)
        n_used = (pad_end[-1] // EXPERT_ROWS).astype(jnp.int32)
        last_blk = jnp.where(padded > 0, pad_end // EXPERT_ROWS - 1, -1)
        spare = n_used + jnp.arange(N_EXPERTS, dtype=jnp.int32)
        spare = jnp.where(spare < n_blocks, spare, -1)
        zero_blocks = jnp.concatenate([last_blk, spare]).astype(jnp.int32)

        x_rows = _dispatch(h2, choice_exp, choice_rank, pad_start, zero_blocks, n_blocks * EXPERT_ROWS)
        y_rows = _experts(x_rows, block_exp, n_used.reshape(1), w_up[l], b_up[l], w_down[l], b_down[l], n_blocks)
        x2 = _combine(y_rows, choice_exp, choice_rank, pad_start, x1, route, mod3, S)
    return x2.reshape(B, S, D)
```

```python
import functools

import jax
import jax.numpy as jnp
from jax import lax
from jax.experimental import pallas as pl
from jax.experimental.pallas import tpu as pltpu

F32 = jnp.float32
BF16 = jnp.bfloat16
HIGHEST = lax.Precision.HIGHEST

EPS = 1e-6
LANES = 128
N_HEADS = 8
HEAD_DIM = 128
RNN_BLOCKS = 8
RNN_BW = 128
CONV_W = 4
LRU_C = 8.0
MOBA_BLOCK = 256
MOBA_TOPK = 3
MOBA_CHUNK = 4
MOBA_HEADS = 4
MERGE_SUB = 256
LOG2_E = 1.4426950408889634
ROPE_DIMS = HEAD_DIM // 4
ROPE_THETA = 500000.0
N_EXPERTS = 32
TOP_K = 4
SWIGLU_LIMIT = 7.0
SWIGLU_ALPHA = 1.702
EXPERT_ROWS = 256
NEG_BIG = -1e30

VMEM_LIMIT = 56 * 1024 * 1024


def _params(*sem):
    return pltpu.CompilerParams(dimension_semantics=sem, vmem_limit_bytes=VMEM_LIMIT)


def _adaln_body(c_ref, w_ref, b_ref, o_ref):
    cs = c_ref[...]
    cs = cs * jax.nn.sigmoid(cs)
    o_ref[...] = jnp.dot(cs, w_ref[...], preferred_element_type=F32, precision=HIGHEST) + b_ref[...]


def _adaln(c, ada_w, ada_b):
    B, D = c.shape
    W = ada_w.shape[1]
    cpad = jnp.zeros((8, D), F32).at[:B].set(c)
    tn = 1024
    mod = pl.pallas_call(
        _adaln_body,
        grid=(W // tn,),
        in_specs=[pl.BlockSpec((8, D), lambda j: (0, 0)),
                  pl.BlockSpec((D, tn), lambda j: (0, j)),
                  pl.BlockSpec((1, tn), lambda j: (0, j))],
        out_specs=pl.BlockSpec((8, tn), lambda j: (0, j)),
        out_shape=jax.ShapeDtypeStruct((8, W), F32),
        compiler_params=_params("parallel"),
        name="adaln",
    )(cpad, ada_w, ada_b.reshape(1, W))
    return mod[:B]


def _rope_head_perm():
    half = ROPE_DIMS // 2
    mid = HEAD_DIM // 2
    return jnp.concatenate([jnp.arange(0, half), jnp.arange(ROPE_DIMS, mid + half),
                            jnp.arange(half, ROPE_DIMS), jnp.arange(mid + half, HEAD_DIM)])


def _rope_body(pos_ref, freq_ref, c_ref, s_ref):
    ang = pos_ref[...].astype(F32) * freq_ref[...]
    lane = lax.broadcasted_iota(jnp.int32, ang.shape, 1)
    half = ROPE_DIMS // 2
    mid = HEAD_DIM // 2
    s = jnp.sin(ang)
    c_ref[...] = jnp.cos(ang)
    s_ref[...] = jnp.where(lane < half, -s, jnp.where((lane >= mid) & (lane < mid + half), s, 0.0))


def _rope_tables(positions):
    n = positions.size
    half = ROPE_DIMS // 2
    mid = HEAD_DIM // 2
    freqs = ROPE_THETA ** (-jnp.arange(half, dtype=F32) / half)
    freq_lane = jnp.zeros((1, LANES), F32).at[0, :half].set(freqs).at[0, mid:mid + half].set(freqs)
    tm = 1024
    tab = jax.ShapeDtypeStruct((n, LANES), F32)
    return pl.pallas_call(
        _rope_body,
        grid=(n // tm,),
        in_specs=[pl.BlockSpec((tm, 1), lambda i: (i, 0)),
                  pl.BlockSpec((1, LANES), lambda i: (0, 0))],
        out_specs=[pl.BlockSpec((tm, LANES), lambda i: (i, 0))] * 2,
        out_shape=[tab, tab],
        compiler_params=_params("parallel"),
        name="rope_tab",
    )(positions.reshape(n, 1), freq_lane)


def _norm_mod(x, g, sh, sc):
    ms = jnp.mean(x * x, axis=-1, keepdims=True)
    y = x * lax.rsqrt(ms + EPS)
    return (y * g) * (1.0 + sc) + sh


def _norm_mod_body(x_ref, mod_ref, g_ref, o_ref):
    o_ref[...] = _norm_mod(x_ref[...], g_ref[...], mod_ref[0, 0:1, :], mod_ref[0, 1:2, :]).astype(o_ref.dtype)


def _norm_mod_call(x2, mod3, g, rows_per_batch):
    n, D = x2.shape
    tm = 512
    tpb = rows_per_batch // tm
    return pl.pallas_call(
        _norm_mod_body,
        grid=(n // tm,),
        in_specs=[pl.BlockSpec((tm, D), lambda i: (i, 0)),
                  pl.BlockSpec((1, 6, D), lambda i: (i // tpb, 0, 0)),
                  pl.BlockSpec((1, D), lambda i: (0, 0))],
        out_specs=pl.BlockSpec((tm, D), lambda i: (i, 0)),
        out_shape=jax.ShapeDtypeStruct((n, D), BF16),
        compiler_params=_params("parallel"),
        name="norm_mod",
    )(x2, mod3, g.reshape(1, D))


def _gelu_tanh(x):
    return 0.5 * x * (1.0 + jnp.tanh(0.7978845608028654 * (x + 0.044715 * (x * x * x))))


PROJ_SUB = 256
PROJ_PIECE = 256


def _qk_head(seg, g, cos, sin):
    ms = jnp.mean(seg * seg, axis=-1, keepdims=True)
    y = seg * lax.rsqrt(ms + EPS) * g
    return y * cos + pltpu.roll(y, HEAD_DIM // 2, axis=1) * sin


def _proj_stages(kind, rows, h_ref, w_ref, extras, o_ref):
    n_pieces = w_ref.shape[1] // PROJ_PIECE
    pieces = []
    for j in range(n_pieces):
        pieces.append(jnp.dot(h_ref[rows, :], w_ref[:, j * PROJ_PIECE:(j + 1) * PROJ_PIECE],
                              preferred_element_type=F32))
        yield
    for j in range(n_pieces):
        cols = slice(j * PROJ_PIECE, (j + 1) * PROJ_PIECE)
        acc = pieces[j]
        if kind == "gelu":
            acc = _gelu_tanh(acc)
        elif kind == "gate":
            (b_ref,) = extras
            acc = jax.nn.sigmoid(acc + b_ref[:, cols])
        elif kind == "qk":
            g_ref, c_ref, s_ref = extras
            cos, sin = c_ref[rows, :], s_ref[rows, :]
            acc = jnp.concatenate(
                [_qk_head(acc[:, o:o + HEAD_DIM], g_ref[:, cols.start + o:cols.start + o + HEAD_DIM], cos, sin)
                 for o in range(0, PROJ_PIECE, HEAD_DIM)], axis=1)
        o_ref[rows, cols] = acc.astype(o_ref.dtype)
        yield


def _proj_body(kind, h_ref, w_ref, *rest):
    *extras, o_ref = rest
    first, second = [_proj_stages(kind, slice(t * PROJ_SUB, (t + 1) * PROJ_SUB), h_ref, w_ref, extras, o_ref)
                     for t in range(2)]
    n_pieces = w_ref.shape[1] // PROJ_PIECE
    for _ in range(n_pieces):
        next(first)
    for _ in range(n_pieces):
        next(second)
        next(first)
    for stage in second:
        pass


def _proj(kind, h, w, extras, out_dtype, name):
    n, D = h.shape
    W = w.shape[1]
    tm, tn = 512, 1024
    in_specs = [pl.BlockSpec((tm, D), lambda j, i: (i, 0)),
                pl.BlockSpec((D, tn), lambda j, i: (0, j))]
    args = [h, w]
    for arr in extras:
        if arr.shape[0] == 1:
            in_specs.append(pl.BlockSpec((1, tn), lambda j, i: (0, j)))
        else:
            in_specs.append(pl.BlockSpec((tm, LANES), lambda j, i: (i, 0)))
        args.append(arr)
    return pl.pallas_call(
        functools.partial(_proj_body, kind),
        grid=(W // tn, n // tm),
        in_specs=in_specs,
        out_specs=pl.BlockSpec((tm, tn), lambda j, i: (i, j)),
        out_shape=jax.ShapeDtypeStruct((n, W), out_dtype),
        compiler_params=_params("parallel", "parallel"),
        name=name,
    )(*args)


def _rglru_body(xr_ref, gg_ref, cw_ref, cb_ref, wa_ref, ba_ref, wx_ref, bx_ref, lam_ref,
                o_ref, xbuf, hcar, a_s, u_s):
    s = pl.program_id(1)
    ts, D = xr_ref.shape

    @pl.when(s == 0)
    def _():
        xbuf[0:8, :] = jnp.zeros((8, D), F32)
        hcar[...] = jnp.zeros_like(hcar)

    @pl.when(s > 0)
    def _():
        xbuf[0:8, :] = xbuf[ts:ts + 8, :]

    xbuf[8:8 + ts, :] = xr_ref[...]
    xc = cb_ref[...] + cw_ref[0:1, :] * xbuf[8:8 + ts, :]
    for i in range(1, CONV_W):
        xc = xc + cw_ref[i:i + 1, :] * xbuf[8 - i:8 - i + ts, :]

    ra, rx = [], []
    for n in range(RNN_BLOCKS):
        xb = xc[:, n * RNN_BW:(n + 1) * RNN_BW].astype(BF16)
        ra.append(jnp.dot(xb, wa_ref[n], preferred_element_type=F32))
        rx.append(jnp.dot(xb, wx_ref[n], preferred_element_type=F32))
    r = jax.nn.sigmoid(jnp.concatenate(ra, axis=1) + ba_ref[...])
    ig = jax.nn.sigmoid(jnp.concatenate(rx, axis=1) + bx_ref[...])

    z = -lam_ref[...]
    softplus = jnp.maximum(z, 0.0) + jnp.log1p(jnp.exp(-jnp.abs(z)))
    log_a = -LRU_C * r * softplus
    a = jnp.exp(log_a)
    mult = jnp.sqrt(1.0 - a * a)
    row = lax.broadcasted_iota(jnp.int32, (ts, D), 0)
    mult = jnp.where((row == 0) & (s == 0), 1.0, mult)
    u = mult * (ig * xc)

    rm = row & 7
    for d in (1, 2, 4):
        keep = rm >= d
        a_sh = pltpu.roll(a, d, axis=0)
        u_sh = pltpu.roll(u, d, axis=0)
        u = jnp.where(keep, a * u_sh + u, u)
        a = jnp.where(keep, a * a_sh, a)
    a_s[...] = a
    u_s[...] = u

    def group(g, h):
        r0 = pl.multiple_of(g * 8, 8)
        hg = u_s[pl.ds(r0, 8), :] + a_s[pl.ds(r0, 8), :] * h
        u_s[pl.ds(r0, 8), :] = hg
        return hg[7:8, :]

    hcar[...] = lax.fori_loop(0, ts // 8, group, hcar[...])
    o_ref[...] = (u_s[...] * gg_ref[...]).astype(o_ref.dtype)


def _rglru(xr, gg, conv_w, conv_b, w_a, b_a, w_x, b_x, lam, B, S):
    n, D = xr.shape
    ts = 256
    spb = S // ts
    row = lambda b, s: (b * spb + s, 0)
    vec = lambda b, s: (0, 0)
    return pl.pallas_call(
        _rglru_body,
        grid=(B, spb),
        in_specs=[pl.BlockSpec((ts, D), row),
                  pl.BlockSpec((ts, D), row),
                  pl.BlockSpec((CONV_W, D), vec),
                  pl.BlockSpec((1, D), vec),
                  pl.BlockSpec((RNN_BLOCKS, RNN_BW, RNN_BW), lambda b, s: (0, 0, 0)),
                  pl.BlockSpec((1, D), vec),
                  pl.BlockSpec((RNN_BLOCKS, RNN_BW, RNN_BW), lambda b, s: (0, 0, 0)),
                  pl.BlockSpec((1, D), vec),
                  pl.BlockSpec((1, D), vec)],
        out_specs=pl.BlockSpec((ts, D), row),
        out_shape=jax.ShapeDtypeStruct((n, D), BF16),
        scratch_shapes=[pltpu.VMEM((ts + 8, D), F32), pltpu.VMEM((1, D), F32),
                        pltpu.VMEM((ts, D), F32), pltpu.VMEM((ts, D), F32)],
        compiler_params=_params("arbitrary", "arbitrary"),
        name="rglru",
    )(xr, gg, conv_w, conv_b.reshape(1, D), w_a.astype(BF16), b_a.reshape(1, D),
      w_x.astype(BF16), b_x.reshape(1, D), lam.reshape(1, D))


def _moba_body(q_ref, k_ref, v_ref, o_ref, kaug, vaug, kmean, m_s, acc_s):
    qb = pl.program_id(2)
    S = k_ref.shape[0]
    nb = S // MOBA_BLOCK
    bs = MOBA_BLOCK
    dh = HEAD_DIM
    heads = range(MOBA_HEADS)
    hcol = lambda hh: slice(hh * dh, (hh + 1) * dh)

    @pl.when(qb == 0)
    def _():
        blk = lax.broadcasted_iota(jnp.int32, (S, dh), 0) // bs
        col = lax.broadcasted_iota(jnp.int32, (S, dh), 1)
        onehot = jnp.where(col == blk, 1.0, 0.0).astype(BF16)
        for hh in heads:
            kaug[hh, :, 0:dh] = k_ref[:, hcol(hh)].astype(BF16)
            kaug[hh, :, dh:2 * dh] = onehot
            vaug[hh, :, 0:dh] = v_ref[:, hcol(hh)]
            vaug[hh, :, dh:2 * dh] = jnp.ones((S, dh), BF16)
            kmean[hh] = jnp.zeros((LANES, dh), F32)
            for j in range(nb):
                kmean[hh, j:j + 1, :] = jnp.mean(k_ref[j * bs:(j + 1) * bs, hcol(hh)], axis=0, keepdims=True)

    scale = HEAD_DIM ** -0.5
    nt = (((1,), (1,)), ((), ()))
    c2 = scale * LOG2_E
    r0 = pl.multiple_of(qb * bs, bs)
    nb_pad = -(-nb // 8) * 8
    rr = lax.broadcasted_iota(jnp.int32, (bs, bs), 0)
    cc = lax.broadcasted_iota(jnp.int32, (bs, bs), 1)

    qs = [q_ref[:, hcol(hh)] for hh in heads]
    qbfs = [q.astype(BF16) for q in qs]

    blk_i = lax.broadcasted_iota(jnp.int32, (nb_pad, bs), 0)
    blk_f = blk_i.astype(F32)
    past = blk_i < qb
    gs = [jnp.where(past, lax.dot_general(kmean[hh, 0:nb_pad, :], qs[hh], nt, preferred_element_type=F32,
                                          precision=HIGHEST), -jnp.inf) for hh in heads]
    ss = [jnp.where(cc <= rr, lax.dot_general(qbfs[hh], kaug[hh, pl.ds(r0, bs), 0:dh], nt,
                                              preferred_element_type=F32), NEG_BIG) for hh in heads]
    biases = [jnp.full((nb_pad, bs), NEG_BIG, F32) for _ in heads]
    for _ in range(MOBA_TOPK):
        for hh in heads:
            m = jnp.max(gs[hh], axis=0, keepdims=True)
            idx = jnp.min(jnp.where(gs[hh] == m, blk_f, float(LANES)), axis=0, keepdims=True)
            pick = (blk_f == idx) & past
            biases[hh] = jnp.where(pick, 0.0, biases[hh])
            gs[hh] = jnp.where(pick, -jnp.inf, gs[hh])
    fill = jnp.full((bs, dh - nb_pad), NEG_BIG, BF16)
    qaugs = [jnp.concatenate([qbfs[hh], biases[hh].T.astype(BF16), fill], axis=1) for hh in heads]

    ps = []
    for hh in heads:
        m0 = jnp.max(ss[hh], axis=1, keepdims=True)
        ps.append(jnp.exp2((ss[hh] - m0) * c2).astype(BF16))
        m_s[hh] = m0
    for hh in heads:
        acc_s[hh] = jnp.dot(ps[hh], vaug[hh, pl.ds(r0, bs), :], preferred_element_type=F32)

    cw = MOBA_CHUNK * bs
    for c in range(nb // MOBA_CHUNK):
        @pl.when(c * MOBA_CHUNK < qb)
        def _(c=c):
            scs = [lax.dot_general(qaugs[hh], kaug[hh, c * cw:(c + 1) * cw, :], nt, preferred_element_type=F32)
                   for hh in heads]
            pcs, alphas = [], []
            for hh in heads:
                m_old = m_s[hh]
                m_new = jnp.maximum(m_old, jnp.max(scs[hh], axis=1, keepdims=True))
                alphas.append(jnp.exp2((m_old - m_new) * c2))
                pcs.append(jnp.exp2((scs[hh] - m_new) * c2).astype(BF16))
                m_s[hh] = m_new
            for hh in heads:
                acc_s[hh] = alphas[hh] * acc_s[hh] + jnp.dot(pcs[hh], vaug[hh, c * cw:(c + 1) * cw, :],
                                                             preferred_element_type=F32)

    for hh in heads:
        o_ref[:, hcol(hh)] = (acc_s[hh, :, 0:dh] / acc_s[hh, :, dh:dh + 1]).astype(o_ref.dtype)


def _moba(qk, v, B, S):
    n, D = v.shape
    nq = S // MOBA_BLOCK
    hs = MOBA_HEADS
    w = hs * HEAD_DIM
    return pl.pallas_call(
        _moba_body,
        grid=(B, N_HEADS // hs, nq),
        in_specs=[pl.BlockSpec((MOBA_BLOCK, w), lambda b, h, i: (b * nq + i, h)),
                  pl.BlockSpec((S, w), lambda b, h, i: (b, N_HEADS // hs + h)),
                  pl.BlockSpec((S, w), lambda b, h, i: (b, h))],
        out_specs=pl.BlockSpec((MOBA_BLOCK, w), lambda b, h, i: (b * nq + i, h)),
        out_shape=jax.ShapeDtypeStruct((n, D), BF16),
        scratch_shapes=[pltpu.VMEM((hs, S, 2 * HEAD_DIM), BF16), pltpu.VMEM((hs, S, 2 * HEAD_DIM), BF16),
                        pltpu.VMEM((hs, LANES, HEAD_DIM), F32), pltpu.VMEM((hs, MOBA_BLOCK, 1), F32),
                        pltpu.VMEM((hs, MOBA_BLOCK, 2 * HEAD_DIM), F32)],
        compiler_params=_params("parallel", "parallel", "arbitrary"),
        name="moba",
    )(qk, qk, v)


ROUTE_COLS = 16


def _merge_body(yr_ref, ya_ref, gr_ref, ga_ref, x_ref, mod_ref, wb_ref, wo_ref, g2_ref, wr_ref, br_ref,
                x1_ref, h2_ref, route_ref, cnt_ref, carry):
    i = pl.program_id(0)

    @pl.when(i == 0)
    def _():
        carry[...] = jnp.zeros_like(carry)

    tm = MERGE_SUB
    counts = {"total": carry[...]}
    first, second = [
        _merge_stages(slice(h * tm, (h + 1) * tm), counts, yr_ref, ya_ref, gr_ref, ga_ref, x_ref, mod_ref,
                      wb_ref, wo_ref, g2_ref, wr_ref, br_ref, x1_ref, h2_ref, route_ref)
        for h in range(2)]
    for _ in range(MERGE_MATMUL_STAGES):
        next(first)
    for stage in first:
        next(second)
    for stage in second:
        pass
    carry[...] = counts["total"]
    cnt_ref[...] = jnp.broadcast_to(counts["total"], cnt_ref.shape).astype(jnp.int32)


MERGE_MATMUL_STAGES = 4


def _merge_stages(rows, counts, yr_ref, ya_ref, gr_ref, ga_ref, x_ref, mod_ref, wb_ref, wo_ref, g2_ref, wr_ref,
                  br_ref, x1_ref, h2_ref, route_ref):
    tm = rows.stop - rows.start
    zr = jnp.dot(yr_ref[rows, :], wb_ref[0], preferred_element_type=F32)
    yield
    za = jnp.dot(ya_ref[rows, :], wb_ref[1], preferred_element_type=F32)
    mix = (gr_ref[rows, :] * zr + ga_ref[rows, :] * za).astype(BF16)
    yield
    mixed = jnp.dot(mix, wo_ref[...], preferred_element_type=F32)
    x1 = x_ref[rows, :] + mod_ref[0, 2:3, :] * mixed
    x1_ref[rows, :] = x1
    h2 = _norm_mod(x1, g2_ref[...], mod_ref[0, 3:4, :], mod_ref[0, 4:5, :])
    h2_ref[rows, :] = h2
    yield

    nt = (((1,), (1,)), ((), ()))
    ne = N_EXPERTS
    hi = h2.astype(BF16)
    lo = (h2 - hi.astype(F32)).astype(BF16)
    both = lax.dot_general(wr_ref[...], hi, nt, preferred_element_type=F32)
    lg = (both[:ne] + both[ne:]
          + lax.dot_general(wr_ref[0:ne, :], lo, nt, preferred_element_type=F32) + br_ref[...])
    yield

    eidx = lax.broadcasted_iota(jnp.int32, lg.shape, 0).astype(F32)
    vals, idxs = [], []
    onehot = jnp.zeros(lg.shape, F32)
    for _ in range(TOP_K):
        m = jnp.max(lg, axis=0, keepdims=True)
        idx = jnp.min(jnp.where(lg == m, eidx, float(ne)), axis=0, keepdims=True)
        pick = eidx == idx
        onehot = jnp.where(pick, 1.0, onehot)
        lg = jnp.where(pick, -jnp.inf, lg)
        vals.append(m)
        idxs.append(idx)
        yield
    es = [jnp.exp(v - vals[0]) for v in vals]
    den = es[0] + es[1] + es[2] + es[3]
    before0 = counts["total"]

    rr = lax.broadcasted_iota(jnp.int32, (tm, tm), 0)
    cc = lax.broadcasted_iota(jnp.int32, (tm, tm), 1)
    earlier = jnp.where(rr < cc, 1.0, 0.0).astype(BF16)
    before = jnp.dot(onehot.astype(BF16), earlier, preferred_element_type=F32) + before0
    ranks = [jnp.sum(jnp.where(eidx == idxs[k], before, 0.0), axis=0, keepdims=True) for k in range(TOP_K)]

    rows_t = jnp.concatenate(idxs + [e / den for e in es] + ranks
                             + [jnp.zeros((ROUTE_COLS - 3 * TOP_K, tm), F32)], axis=0)
    route_ref[rows, :] = rows_t.T
    counts["total"] = before0 + jnp.sum(onehot, axis=1, keepdims=True)
    yield


def _merge(y_rnn, y_att, gl, x2, mod3, w_branch, w_out, norm2_g, w_router, b_router, rows_per_batch):
    n, D = x2.shape
    tm = 2 * MERGE_SUB
    tpb = rows_per_batch // tm
    wr_t = w_router.T
    wr_hi = wr_t.astype(BF16)
    wr = jnp.concatenate([wr_hi, (wr_t - wr_hi.astype(F32)).astype(BF16)], axis=0)
    br = b_router.reshape(N_EXPERTS, 1)
    row = lambda i: (i, 0)
    fixed = lambda i: (0, 0)
    once = pl.Buffered(1)
    wide = jax.ShapeDtypeStruct((n, D), F32)
    return pl.pallas_call(
        _merge_body,
        grid=(n // tm,),
        in_specs=[pl.BlockSpec((tm, D), row),
                  pl.BlockSpec((tm, D), row),
                  pl.BlockSpec((tm, D), lambda i: (i, 0)),
                  pl.BlockSpec((tm, D), lambda i: (i, 1)),
                  pl.BlockSpec((tm, D), row),
                  pl.BlockSpec((1, 6, D), lambda i: (i // tpb, 0, 0)),
                  pl.BlockSpec((2, D, D), lambda i: (0, 0, 0), pipeline_mode=once),
                  pl.BlockSpec((D, D), fixed, pipeline_mode=once),
                  pl.BlockSpec((1, D), fixed),
                  pl.BlockSpec((2 * N_EXPERTS, D), fixed, pipeline_mode=once),
                  pl.BlockSpec((N_EXPERTS, 1), fixed)],
        out_specs=[pl.BlockSpec((tm, D), row),
                   pl.BlockSpec((tm, D), row),
                   pl.BlockSpec((tm, ROUTE_COLS), row),
                   pl.BlockSpec((N_EXPERTS, LANES), fixed)],
        out_shape=[wide, wide, jax.ShapeDtypeStruct((n, ROUTE_COLS), F32),
                   jax.ShapeDtypeStruct((N_EXPERTS, LANES), jnp.int32)],
        scratch_shapes=[pltpu.VMEM((N_EXPERTS, 1), F32)],
        compiler_params=_params("arbitrary"),
        name="merge",
    )(y_rnn, y_att, gl, gl, x2, mod3, w_branch.astype(BF16), w_out.astype(BF16),
      norm2_g.reshape(1, D), wr, br)


DISPATCH_TOKENS = 256
ROW_DMA_UNROLL = 8


def _dispatch_body(dest_ref, zblk_ref, src_ref, dst_hbm, zbuf, stage, sems, zsem):
    t = DISPATCH_TOKENS
    c = pl.program_id(0)

    @pl.when(c == 0)
    def _():
        zbuf[...] = jnp.zeros_like(zbuf)

        def zero_copy(b):
            r0 = pl.multiple_of(b * EXPERT_ROWS, EXPERT_ROWS)
            return pltpu.make_async_copy(zbuf, dst_hbm.at[pl.ds(r0, EXPERT_ROWS), :], zsem)

        def zissue(j, carry):
            @pl.when(zblk_ref[j] >= 0)
            def _():
                zero_copy(zblk_ref[j]).start()
            return carry

        def zwait(j, carry):
            @pl.when(zblk_ref[j] >= 0)
            def _():
                zero_copy(zblk_ref[j]).wait()
            return carry

        lax.fori_loop(0, zblk_ref.shape[0], zissue, 0)
        lax.fori_loop(0, zblk_ref.shape[0], zwait, 0)

    slot = c % 2

    def drain(s):
        for k in range(TOP_K):
            pltpu.make_async_copy(stage.at[s], dst_hbm.at[pl.ds(0, t), :], sems.at[s]).wait()

    @pl.when(c >= 2)
    def _():
        drain(slot)

    stage[slot] = src_ref[...]

    def issue(g, carry):
        for u in range(ROW_DMA_UNROLL):
            r = g * ROW_DMA_UNROLL + u
            for k in range(TOP_K):
                d = dest_ref[(c * t + r) * TOP_K + k]
                pltpu.make_async_copy(stage.at[slot, pl.ds(r, 1), :], dst_hbm.at[pl.ds(d, 1), :],
                                      sems.at[slot]).start()
        return carry

    lax.fori_loop(0, t // ROW_DMA_UNROLL, issue, 0)

    @pl.when(c == pl.num_programs(0) - 1)
    def _():
        drain(1 - slot)
        drain(slot)


def _dispatch(h2, dest, zero_blocks, n_rows):
    n, D = h2.shape
    return pl.pallas_call(
        _dispatch_body,
        grid_spec=pltpu.PrefetchScalarGridSpec(
            num_scalar_prefetch=2,
            grid=(n // DISPATCH_TOKENS,),
            in_specs=[pl.BlockSpec((DISPATCH_TOKENS, D), lambda i, d, z: (i, 0))],
            out_specs=pl.BlockSpec(memory_space=pl.ANY),
            scratch_shapes=[pltpu.VMEM((EXPERT_ROWS, D), F32), pltpu.VMEM((2, DISPATCH_TOKENS, D), F32),
                            pltpu.SemaphoreType.DMA((2,)), pltpu.SemaphoreType.DMA],
        ),
        out_shape=jax.ShapeDtypeStruct((n_rows, D), F32),
        compiler_params=_params("arbitrary"),
        name="dispatch",
    )(dest, zero_blocks, h2)


def _experts_body(bexp_ref, nused_ref, first_ref, slot_ref, next_ref, x_ref, wu_hbm, bu_ref, wd_hbm, bd_ref,
                  o_ref, wu_buf, wd_buf, wu_bf, wd_bf, sems):
    i = pl.program_id(0)
    dff = wd_hbm.shape[1]
    used = i < nused_ref[0]

    def weight_copies(e, s):
        return (pltpu.make_async_copy(wu_hbm.at[e], wu_buf.at[s], sems.at[0, s]),
                pltpu.make_async_copy(wd_hbm.at[e], wd_buf.at[s], sems.at[1, s]))

    @pl.when(i == 0)
    def _():
        for cp in weight_copies(bexp_ref[0], 0):
            cp.start()

    @pl.when(used & (first_ref[i] == 1))
    def _():
        s = slot_ref[i]
        for cp in weight_copies(bexp_ref[i], s):
            cp.wait()
        wu_bf[...] = wu_buf[s].astype(BF16)
        wd_bf[...] = wd_buf[s].astype(BF16)

        @pl.when(next_ref[i] >= 0)
        def _():
            for cp in weight_copies(next_ref[i], 1 - s):
                cp.start()

    @pl.when(used)
    def _():
        hc = jnp.dot(x_ref[...].astype(BF16), wu_bf[...], preferred_element_type=F32) + bu_ref[0]
        g = jnp.minimum(hc[:, :dff], SWIGLU_LIMIT)
        lin = jnp.clip(hc[:, dff:], -SWIGLU_LIMIT, SWIGLU_LIMIT)
        act = (lin + 1.0) * g * jax.nn.sigmoid(SWIGLU_ALPHA * g)
        o_ref[...] = jnp.dot(act.astype(BF16), wd_bf[...], preferred_element_type=F32) + bd_ref[0]

    @pl.when(i >= nused_ref[0])
    def _():
        o_ref[...] = jnp.zeros_like(o_ref)


def _experts(x_rows, block_exp, n_used, w_up, b_up, w_down, b_down, n_blocks):
    R, D = x_rows.shape
    E, _, F2 = w_up.shape
    dff = w_down.shape[1]
    bidx = jnp.arange(n_blocks, dtype=jnp.int32)
    first = ((bidx == 0) | (block_exp != jnp.roll(block_exp, 1))) & (bidx < n_used[0])
    slot = (jnp.cumsum(first) - 1) % 2
    first_pos = jnp.where(first, bidx, n_blocks)
    next_pos = jnp.flip(lax.cummin(jnp.flip(jnp.roll(first_pos, -1).at[-1].set(n_blocks))))
    next_exp = jnp.where(next_pos < n_blocks, block_exp[jnp.minimum(next_pos, n_blocks - 1)], -1)
    blk = lambda i, *_: (i, 0)
    bsel = lambda i, be, *_: (be[i], 0, 0)
    return pl.pallas_call(
        _experts_body,
        grid_spec=pltpu.PrefetchScalarGridSpec(
            num_scalar_prefetch=5,
            grid=(n_blocks,),
            in_specs=[pl.BlockSpec((EXPERT_ROWS, D), blk),
                      pl.BlockSpec(memory_space=pl.ANY),
                      pl.BlockSpec((1, 1, F2), bsel),
                      pl.BlockSpec(memory_space=pl.ANY),
                      pl.BlockSpec((1, 1, D), bsel)],
            out_specs=pl.BlockSpec((EXPERT_ROWS, D), blk),
            scratch_shapes=[pltpu.VMEM((2, D, F2), F32), pltpu.VMEM((2, dff, D), F32),
                            pltpu.VMEM((D, F2), BF16), pltpu.VMEM((dff, D), BF16),
                            pltpu.SemaphoreType.DMA((2, 2))],
        ),
        out_shape=jax.ShapeDtypeStruct((R, D), F32),
        compiler_params=_params("arbitrary"),
        name="experts",
    )(block_exp, n_used, first.astype(jnp.int32), slot.astype(jnp.int32), next_exp.astype(jnp.int32),
      x_rows, w_up, b_up.reshape(E, 1, F2), w_down, b_down.reshape(E, 1, D))


COMBINE_ROWS = 128


def _combine_body(dest_ref, y_hbm, x1_ref, gate_ref, mod_ref, o_ref, buf, sems):
    i = pl.program_id(0)
    t = COMBINE_ROWS
    D = o_ref.shape[1]
    cur = i % 2

    def request(tile, slot):
        def issue(g, carry):
            for u in range(ROW_DMA_UNROLL):
                r = g * ROW_DMA_UNROLL + u
                for k in range(TOP_K):
                    d = dest_ref[(tile * t + r) * TOP_K + k]
                    pltpu.make_async_copy(y_hbm.at[pl.ds(d, 1), :], buf.at[slot, pl.ds(r, 1), pl.ds(k * D, D)],
                                          sems.at[slot]).start()
            return carry
        lax.fori_loop(0, t // ROW_DMA_UNROLL, issue, 0)

    @pl.when(i == 0)
    def _():
        request(0, 0)

    @pl.when(i + 1 < pl.num_programs(0))
    def _():
        request(i + 1, 1 - cur)

    for k in range(TOP_K):
        pltpu.make_async_copy(y_hbm.at[pl.ds(0, t), :], buf.at[cur, :, pl.ds(k * D, D)], sems.at[cur]).wait()
    gates = gate_ref[:, TOP_K:2 * TOP_K]
    y = gates[:, 0:1] * buf[cur, :, 0:D]
    for k in range(1, TOP_K):
        y = y + gates[:, k:k + 1] * buf[cur, :, k * D:(k + 1) * D]
    o_ref[...] = x1_ref[...] + mod_ref[0, 5:6, :] * y


def _combine(y_rows, dest, x1, gates, mod3, rows_per_batch):
    n, D = x1.shape
    t = COMBINE_ROWS
    tpb = rows_per_batch // t
    return pl.pallas_call(
        _combine_body,
        grid_spec=pltpu.PrefetchScalarGridSpec(
            num_scalar_prefetch=1,
            grid=(n // t,),
            in_specs=[pl.BlockSpec(memory_space=pl.ANY),
                      pl.BlockSpec((t, D), lambda i, d: (i, 0)),
                      pl.BlockSpec((t, ROUTE_COLS), lambda i, d: (i, 0)),
                      pl.BlockSpec((1, 6, D), lambda i, d: (i // tpb, 0, 0))],
            out_specs=pl.BlockSpec((t, D), lambda i, d: (i, 0)),
            scratch_shapes=[pltpu.VMEM((2, t, TOP_K * D), F32), pltpu.SemaphoreType.DMA((2,))],
        ),
        out_shape=jax.ShapeDtypeStruct((n, D), F32),
        compiler_params=_params("arbitrary"),
        name="combine",
    )(dest, y_rows, x1, gates, mod3)


def kernel(x, c, positions, ada_w, ada_b, norm1_g, norm2_g, w_in, conv_w, conv_b, w_rg_a, b_rg_a, w_rg_x, b_rg_x, lru_lambda, q_norm_g, k_norm_g, b_gate, w_branch, w_out, w_router, b_router, w_up, b_up, w_down, b_down):
    B, S, D = x.shape
    n = B * S
    depth = ada_w.shape[0]
    x2 = x.reshape(n, D)
    cos_t, sin_t = _rope_tables(positions)
    perm = _rope_head_perm()
    head_cols = (jnp.arange(2 * N_HEADS)[:, None] * HEAD_DIM + perm[None, :]).reshape(-1)
    for l in range(depth):
        mod3 = _adaln(c, ada_w[l], ada_b[l]).reshape(B, 6, D)
        h1 = _norm_mod_call(x2, mod3, norm1_g[l], S)
        w = w_in[l].astype(BF16)
        xr = _proj("plain", h1, w[:, 0:D], [], F32, "proj_xr")
        gg = _proj("gelu", h1, w[:, D:2 * D], [], F32, "proj_gr")
        qk_g = jnp.concatenate([jnp.tile(q_norm_g[l][perm], N_HEADS),
                                jnp.tile(k_norm_g[l][perm], N_HEADS)]).reshape(1, 2 * D)
        qk = _proj("qk", h1, w[:, 2 * D:4 * D][:, head_cols], [qk_g, cos_t, sin_t], F32, "proj_qk")
        v = _proj("plain", h1, w[:, 4 * D:5 * D], [], BF16, "proj_v")
        gl = _proj("gate", h1, w[:, 5 * D:7 * D], [b_gate[l].reshape(1, 2 * D)], F32, "proj_gl")

        y_rnn = _rglru(xr, gg, conv_w[l], conv_b[l], w_rg_a[l], b_rg_a[l], w_rg_x[l], b_rg_x[l],
                       lru_lambda[l], B, S)
        y_att = _moba(qk, v, B, S)

        x1, h2, route, cnt = _merge(y_rnn, y_att, gl, x2, mod3, w_branch[l], w_out[l],
                                    norm2_g[l], w_router[l], b_router[l], S)

        counts = cnt[:, 0]
        padded = (counts + EXPERT_ROWS - 1) // EXPERT_ROWS * EXPERT_ROWS
        pad_end = jnp.cumsum(padded)
        pad_start = pad_end - padded
        top_idx = route[:, 0:TOP_K].astype(jnp.int32)
        rank = route[:, 2 * TOP_K:3 * TOP_K].astype(jnp.int32)
        dest = (pad_start[top_idx] + rank).reshape(n * TOP_K).astype(jnp.int32)
        n_blocks = (n * TOP_K) // EXPERT_ROWS + N_EXPERTS
        block_start = jnp.arange(n_blocks, dtype=jnp.int32) * EXPERT_ROWS
        block_exp = jnp.minimum(jnp.sum(block_start[:, None] >= pad_end[None, :], axis=1),
                                N_EXPERTS - 1).astype(jnp.int32)
        n_used = (pad_end[-1] // EXPERT_ROWS).astype(jnp.int32)
        last_blk = jnp.where(padded > 0, pad_end // EXPERT_ROWS - 1, -1)
        spare = n_used + jnp.arange(N_EXPERTS, dtype=jnp.int32)
        spare = jnp.where(spare < n_blocks, spare, -1)
        zero_blocks = jnp.concatenate([last_blk, spare]).astype(jnp.int32)

        x_rows = _dispatch(h2, dest, zero_blocks, n_blocks * EXPERT_ROWS)
        y_rows = _experts(x_rows, block_exp, n_used.reshape(1), w_up[l], b_up[l], w_down[l], b_down[l], n_blocks)
        x2 = _combine(y_rows, dest, x1, route, mod3, S)
    return x2.reshape(B, S, D)
```

```python
import functools

import jax
import jax.numpy as jnp
from jax import lax
from jax.experimental import pallas as pl
from jax.experimental.pallas import tpu as pltpu

F32 = jnp.float32
BF16 = jnp.bfloat16
HIGHEST = lax.Precision.HIGHEST

EPS = 1e-6
LANES = 128
N_HEADS = 8
HEAD_DIM = 128
RNN_BLOCKS = 8
RNN_BW = 128
CONV_W = 4
LRU_C = 8.0
MOBA_BLOCK = 256
MOBA_TOPK = 3
MOBA_CHUNK = 4
MOBA_HEADS = 4
MERGE_SUB = 256
LOG2_E = 1.4426950408889634
ROPE_DIMS = HEAD_DIM // 4
ROPE_THETA = 500000.0
N_EXPERTS = 32
TOP_K = 4
SWIGLU_LIMIT = 7.0
SWIGLU_ALPHA = 1.702
EXPERT_ROWS = 256
NEG_BIG = -1e30

VMEM_LIMIT = 56 * 1024 * 1024


def _params(*sem):
    return pltpu.CompilerParams(dimension_semantics=sem, vmem_limit_bytes=VMEM_LIMIT)


def _adaln_body(c_ref, w_ref, b_ref, o_ref):
    cs = c_ref[...]
    cs = cs * jax.nn.sigmoid(cs)
    o_ref[...] = jnp.dot(cs, w_ref[...], preferred_element_type=F32, precision=HIGHEST) + b_ref[...]


def _adaln(c, ada_w, ada_b):
    B, D = c.shape
    W = ada_w.shape[1]
    cpad = jnp.zeros((8, D), F32).at[:B].set(c)
    tn = 1024
    mod = pl.pallas_call(
        _adaln_body,
        grid=(W // tn,),
        in_specs=[pl.BlockSpec((8, D), lambda j: (0, 0)),
                  pl.BlockSpec((D, tn), lambda j: (0, j)),
                  pl.BlockSpec((1, tn), lambda j: (0, j))],
        out_specs=pl.BlockSpec((8, tn), lambda j: (0, j)),
        out_shape=jax.ShapeDtypeStruct((8, W), F32),
        compiler_params=_params("parallel"),
        name="adaln",
    )(cpad, ada_w, ada_b.reshape(1, W))
    return mod[:B]


def _rope_head_perm():
    half = ROPE_DIMS // 2
    mid = HEAD_DIM // 2
    return jnp.concatenate([jnp.arange(0, half), jnp.arange(ROPE_DIMS, mid + half),
                            jnp.arange(half, ROPE_DIMS), jnp.arange(mid + half, HEAD_DIM)])


def _rope_body(pos_ref, freq_ref, c_ref, s_ref):
    ang = pos_ref[...].astype(F32) * freq_ref[...]
    lane = lax.broadcasted_iota(jnp.int32, ang.shape, 1)
    half = ROPE_DIMS // 2
    mid = HEAD_DIM // 2
    s = jnp.sin(ang)
    c_ref[...] = jnp.cos(ang)
    s_ref[...] = jnp.where(lane < half, -s, jnp.where((lane >= mid) & (lane < mid + half), s, 0.0))


def _rope_tables(positions):
    n = positions.size
    half = ROPE_DIMS // 2
    mid = HEAD_DIM // 2
    freqs = ROPE_THETA ** (-jnp.arange(half, dtype=F32) / half)
    freq_lane = jnp.zeros((1, LANES), F32).at[0, :half].set(freqs).at[0, mid:mid + half].set(freqs)
    tm = 1024
    tab = jax.ShapeDtypeStruct((n, LANES), F32)
    return pl.pallas_call(
        _rope_body,
        grid=(n // tm,),
        in_specs=[pl.BlockSpec((tm, 1), lambda i: (i, 0)),
                  pl.BlockSpec((1, LANES), lambda i: (0, 0))],
        out_specs=[pl.BlockSpec((tm, LANES), lambda i: (i, 0))] * 2,
        out_shape=[tab, tab],
        compiler_params=_params("parallel"),
        name="rope_tab",
    )(positions.reshape(n, 1), freq_lane)


def _norm_mod(x, g, sh, sc):
    ms = jnp.mean(x * x, axis=-1, keepdims=True)
    y = x * lax.rsqrt(ms + EPS)
    return (y * g) * (1.0 + sc) + sh


def _norm_mod_body(x_ref, mod_ref, g_ref, o_ref):
    o_ref[...] = _norm_mod(x_ref[...], g_ref[...], mod_ref[0, 0:1, :], mod_ref[0, 1:2, :]).astype(o_ref.dtype)


def _norm_mod_call(x2, mod3, g, rows_per_batch):
    n, D = x2.shape
    tm = 512
    tpb = rows_per_batch // tm
    return pl.pallas_call(
        _norm_mod_body,
        grid=(n // tm,),
        in_specs=[pl.BlockSpec((tm, D), lambda i: (i, 0)),
                  pl.BlockSpec((1, 6, D), lambda i: (i // tpb, 0, 0)),
                  pl.BlockSpec((1, D), lambda i: (0, 0))],
        out_specs=pl.BlockSpec((tm, D), lambda i: (i, 0)),
        out_shape=jax.ShapeDtypeStruct((n, D), BF16),
        compiler_params=_params("parallel"),
        name="norm_mod",
    )(x2, mod3, g.reshape(1, D))


def _gelu_tanh(x):
    return 0.5 * x * (1.0 + jnp.tanh(0.7978845608028654 * (x + 0.044715 * (x * x * x))))


PROJ_SUB = 256
PROJ_PIECE = 256


def _qk_head(seg, g, cos, sin):
    ms = jnp.mean(seg * seg, axis=-1, keepdims=True)
    y = seg * lax.rsqrt(ms + EPS) * g
    return y * cos + pltpu.roll(y, HEAD_DIM // 2, axis=1) * sin


def _proj_stages(kind, rows, h_ref, w_ref, extras, o_ref):
    n_pieces = w_ref.shape[1] // PROJ_PIECE
    pieces = []
    for j in range(n_pieces):
        pieces.append(jnp.dot(h_ref[rows, :], w_ref[:, j * PROJ_PIECE:(j + 1) * PROJ_PIECE],
                              preferred_element_type=F32))
        yield
    for j in range(n_pieces):
        cols = slice(j * PROJ_PIECE, (j + 1) * PROJ_PIECE)
        acc = pieces[j]
        if kind == "gelu":
            acc = _gelu_tanh(acc)
        elif kind == "gate":
            (b_ref,) = extras
            acc = jax.nn.sigmoid(acc + b_ref[:, cols])
        elif kind == "qk":
            g_ref, c_ref, s_ref = extras
            cos, sin = c_ref[rows, :], s_ref[rows, :]
            acc = jnp.concatenate(
                [_qk_head(acc[:, o:o + HEAD_DIM], g_ref[:, cols.start + o:cols.start + o + HEAD_DIM], cos, sin)
                 for o in range(0, PROJ_PIECE, HEAD_DIM)], axis=1)
        o_ref[rows, cols] = acc.astype(o_ref.dtype)
        yield


def _proj_body(kind, h_ref, w_ref, *rest):
    *extras, o_ref = rest
    first, second = [_proj_stages(kind, slice(t * PROJ_SUB, (t + 1) * PROJ_SUB), h_ref, w_ref, extras, o_ref)
                     for t in range(2)]
    n_pieces = w_ref.shape[1] // PROJ_PIECE
    for _ in range(n_pieces):
        next(first)
    for _ in range(n_pieces):
        next(second)
        next(first)
    for stage in second:
        pass


def _proj(kind, h, w, extras, out_dtype, name):
    n, D = h.shape
    W = w.shape[1]
    tm, tn = 512, 1024
    in_specs = [pl.BlockSpec((tm, D), lambda j, i: (i, 0)),
                pl.BlockSpec((D, tn), lambda j, i: (0, j))]
    args = [h, w]
    for arr in extras:
        if arr.shape[0] == 1:
            in_specs.append(pl.BlockSpec((1, tn), lambda j, i: (0, j)))
        else:
            in_specs.append(pl.BlockSpec((tm, LANES), lambda j, i: (i, 0)))
        args.append(arr)
    return pl.pallas_call(
        functools.partial(_proj_body, kind),
        grid=(W // tn, n // tm),
        in_specs=in_specs,
        out_specs=pl.BlockSpec((tm, tn), lambda j, i: (i, j)),
        out_shape=jax.ShapeDtypeStruct((n, W), out_dtype),
        compiler_params=_params("parallel", "parallel"),
        name=name,
    )(*args)


def _rglru_body(xr_ref, gg_ref, cw_ref, cb_ref, wa_ref, ba_ref, wx_ref, bx_ref, lam_ref,
                o_ref, xbuf, hcar, a_s, u_s):
    s = pl.program_id(1)
    ts, D = xr_ref.shape

    @pl.when(s == 0)
    def _():
        xbuf[0:8, :] = jnp.zeros((8, D), F32)
        hcar[...] = jnp.zeros_like(hcar)

    @pl.when(s > 0)
    def _():
        xbuf[0:8, :] = xbuf[ts:ts + 8, :]

    xbuf[8:8 + ts, :] = xr_ref[...]
    xc = cb_ref[...] + cw_ref[0:1, :] * xbuf[8:8 + ts, :]
    for i in range(1, CONV_W):
        xc = xc + cw_ref[i:i + 1, :] * xbuf[8 - i:8 - i + ts, :]

    ra, rx = [], []
    for n in range(RNN_BLOCKS):
        xb = xc[:, n * RNN_BW:(n + 1) * RNN_BW].astype(BF16)
        ra.append(jnp.dot(xb, wa_ref[n], preferred_element_type=F32))
        rx.append(jnp.dot(xb, wx_ref[n], preferred_element_type=F32))
    r = jax.nn.sigmoid(jnp.concatenate(ra, axis=1) + ba_ref[...])
    ig = jax.nn.sigmoid(jnp.concatenate(rx, axis=1) + bx_ref[...])

    z = -lam_ref[...]
    softplus = jnp.maximum(z, 0.0) + jnp.log1p(jnp.exp(-jnp.abs(z)))
    log_a = -LRU_C * r * softplus
    a = jnp.exp(log_a)
    mult = jnp.sqrt(1.0 - a * a)
    row = lax.broadcasted_iota(jnp.int32, (ts, D), 0)
    mult = jnp.where((row == 0) & (s == 0), 1.0, mult)
    u = mult * (ig * xc)

    rm = row & 7
    for d in (1, 2, 4):
        keep = rm >= d
        a_sh = pltpu.roll(a, d, axis=0)
        u_sh = pltpu.roll(u, d, axis=0)
        u = jnp.where(keep, a * u_sh + u, u)
        a = jnp.where(keep, a * a_sh, a)
    a_s[...] = a
    u_s[...] = u

    def group(g, h):
        r0 = pl.multiple_of(g * 8, 8)
        hg = u_s[pl.ds(r0, 8), :] + a_s[pl.ds(r0, 8), :] * h
        u_s[pl.ds(r0, 8), :] = hg
        return hg[7:8, :]

    hcar[...] = lax.fori_loop(0, ts // 8, group, hcar[...])
    o_ref[...] = (u_s[...] * gg_ref[...]).astype(o_ref.dtype)


def _rglru(xr, gg, conv_w, conv_b, w_a, b_a, w_x, b_x, lam, B, S):
    n, D = xr.shape
    ts = 256
    spb = S // ts
    row = lambda b, s: (b * spb + s, 0)
    vec = lambda b, s: (0, 0)
    return pl.pallas_call(
        _rglru_body,
        grid=(B, spb),
        in_specs=[pl.BlockSpec((ts, D), row),
                  pl.BlockSpec((ts, D), row),
                  pl.BlockSpec((CONV_W, D), vec),
                  pl.BlockSpec((1, D), vec),
                  pl.BlockSpec((RNN_BLOCKS, RNN_BW, RNN_BW), lambda b, s: (0, 0, 0)),
                  pl.BlockSpec((1, D), vec),
                  pl.BlockSpec((RNN_BLOCKS, RNN_BW, RNN_BW), lambda b, s: (0, 0, 0)),
                  pl.BlockSpec((1, D), vec),
                  pl.BlockSpec((1, D), vec)],
        out_specs=pl.BlockSpec((ts, D), row),
        out_shape=jax.ShapeDtypeStruct((n, D), BF16),
        scratch_shapes=[pltpu.VMEM((ts + 8, D), F32), pltpu.VMEM((1, D), F32),
                        pltpu.VMEM((ts, D), F32), pltpu.VMEM((ts, D), F32)],
        compiler_params=_params("arbitrary", "arbitrary"),
        name="rglru",
    )(xr, gg, conv_w, conv_b.reshape(1, D), w_a.astype(BF16), b_a.reshape(1, D),
      w_x.astype(BF16), b_x.reshape(1, D), lam.reshape(1, D))


def _moba_body(q_ref, k_ref, v_ref, o_ref, kaug, vaug, kmean, m_s, acc_s):
    qb = pl.program_id(2)
    S = k_ref.shape[0]
    nb = S // MOBA_BLOCK
    bs = MOBA_BLOCK
    dh = HEAD_DIM
    heads = range(MOBA_HEADS)
    hcol = lambda hh: slice(hh * dh, (hh + 1) * dh)

    @pl.when(qb == 0)
    def _():
        blk = lax.broadcasted_iota(jnp.int32, (S, dh), 0) // bs
        col = lax.broadcasted_iota(jnp.int32, (S, dh), 1)
        onehot = jnp.where(col == blk, 1.0, 0.0).astype(BF16)
        for hh in heads:
            kaug[hh, :, 0:dh] = k_ref[:, hcol(hh)].astype(BF16)
            kaug[hh, :, dh:2 * dh] = onehot
            vaug[hh, :, 0:dh] = v_ref[:, hcol(hh)]
            vaug[hh, :, dh:2 * dh] = jnp.ones((S, dh), BF16)
            kmean[hh] = jnp.zeros((LANES, dh), F32)
            for j in range(nb):
                kmean[hh, j:j + 1, :] = jnp.mean(k_ref[j * bs:(j + 1) * bs, hcol(hh)], axis=0, keepdims=True)

    scale = HEAD_DIM ** -0.5
    nt = (((1,), (1,)), ((), ()))
    c2 = scale * LOG2_E
    r0 = pl.multiple_of(qb * bs, bs)
    nb_pad = -(-nb // 8) * 8
    rr = lax.broadcasted_iota(jnp.int32, (bs, bs), 0)
    cc = lax.broadcasted_iota(jnp.int32, (bs, bs), 1)

    qs = [q_ref[:, hcol(hh)] for hh in heads]
    qbfs = [q.astype(BF16) for q in qs]

    blk_i = lax.broadcasted_iota(jnp.int32, (nb_pad, bs), 0)
    blk_f = blk_i.astype(F32)
    past = blk_i < qb
    gs = [jnp.where(past, lax.dot_general(kmean[hh, 0:nb_pad, :], qs[hh], nt, preferred_element_type=F32,
                                          precision=HIGHEST), -jnp.inf) for hh in heads]
    ss = [jnp.where(cc <= rr, lax.dot_general(qbfs[hh], kaug[hh, pl.ds(r0, bs), 0:dh], nt,
                                              preferred_element_type=F32), NEG_BIG) for hh in heads]
    biases = [jnp.full((nb_pad, bs), NEG_BIG, F32) for _ in heads]
    for _ in range(MOBA_TOPK):
        for hh in heads:
            m = jnp.max(gs[hh], axis=0, keepdims=True)
            idx = jnp.min(jnp.where(gs[hh] == m, blk_f, float(LANES)), axis=0, keepdims=True)
            pick = (blk_f == idx) & past
            biases[hh] = jnp.where(pick, 0.0, biases[hh])
            gs[hh] = jnp.where(pick, -jnp.inf, gs[hh])
    fill = jnp.full((bs, dh - nb_pad), NEG_BIG, BF16)
    qaugs = [jnp.concatenate([qbfs[hh], biases[hh].T.astype(BF16), fill], axis=1) for hh in heads]

    ps = []
    for hh in heads:
        m0 = jnp.max(ss[hh], axis=1, keepdims=True)
        ps.append(jnp.exp2((ss[hh] - m0) * c2).astype(BF16))
        m_s[hh] = m0
    for hh in heads:
        acc_s[hh] = jnp.dot(ps[hh], vaug[hh, pl.ds(r0, bs), :], preferred_element_type=F32)

    cw = MOBA_CHUNK * bs
    for c in range(nb // MOBA_CHUNK):
        @pl.when(c * MOBA_CHUNK < qb)
        def _(c=c):
            scs = [lax.dot_general(qaugs[hh], kaug[hh, c * cw:(c + 1) * cw, :], nt, preferred_element_type=F32)
                   for hh in heads]
            pcs, alphas = [], []
            for hh in heads:
                m_old = m_s[hh]
                m_new = jnp.maximum(m_old, jnp.max(scs[hh], axis=1, keepdims=True))
                alphas.append(jnp.exp2((m_old - m_new) * c2))
                pcs.append(jnp.exp2((scs[hh] - m_new) * c2).astype(BF16))
                m_s[hh] = m_new
            for hh in heads:
                acc_s[hh] = alphas[hh] * acc_s[hh] + jnp.dot(pcs[hh], vaug[hh, c * cw:(c + 1) * cw, :],
                                                             preferred_element_type=F32)

    for hh in heads:
        o_ref[:, hcol(hh)] = (acc_s[hh, :, 0:dh] / acc_s[hh, :, dh:dh + 1]).astype(o_ref.dtype)


def _moba(qk, v, B, S):
    n, D = v.shape
    nq = S // MOBA_BLOCK
    hs = MOBA_HEADS
    w = hs * HEAD_DIM
    return pl.pallas_call(
        _moba_body,
        grid=(B, N_HEADS // hs, nq),
        in_specs=[pl.BlockSpec((MOBA_BLOCK, w), lambda b, h, i: (b * nq + i, h)),
                  pl.BlockSpec((S, w), lambda b, h, i: (b, N_HEADS // hs + h)),
                  pl.BlockSpec((S, w), lambda b, h, i: (b, h))],
        out_specs=pl.BlockSpec((MOBA_BLOCK, w), lambda b, h, i: (b * nq + i, h)),
        out_shape=jax.ShapeDtypeStruct((n, D), BF16),
        scratch_shapes=[pltpu.VMEM((hs, S, 2 * HEAD_DIM), BF16), pltpu.VMEM((hs, S, 2 * HEAD_DIM), BF16),
                        pltpu.VMEM((hs, LANES, HEAD_DIM), F32), pltpu.VMEM((hs, MOBA_BLOCK, 1), F32),
                        pltpu.VMEM((hs, MOBA_BLOCK, 2 * HEAD_DIM), F32)],
        compiler_params=_params("parallel", "parallel", "arbitrary"),
        name="moba",
    )(qk, qk, v)


ROUTE_COLS = 16


def _merge_body(yr_ref, ya_ref, gr_ref, ga_ref, x_ref, mod_ref, wb_ref, wo_ref, g2_ref, wr_ref, br_ref,
                x1_ref, h2_ref, route_ref, cnt_ref, carry):
    i = pl.program_id(0)

    @pl.when(i == 0)
    def _():
        carry[...] = jnp.zeros_like(carry)

    tm = MERGE_SUB
    counts = {"total": carry[...]}
    first, second = [
        _merge_stages(slice(h * tm, (h + 1) * tm), counts, yr_ref, ya_ref, gr_ref, ga_ref, x_ref, mod_ref,
                      wb_ref, wo_ref, g2_ref, wr_ref, br_ref, x1_ref, h2_ref, route_ref)
        for h in range(2)]
    for _ in range(MERGE_MATMUL_STAGES):
        next(first)
    for stage in first:
        next(second)
    for stage in second:
        pass
    carry[...] = counts["total"]
    cnt_ref[...] = jnp.broadcast_to(counts["total"], cnt_ref.shape).astype(jnp.int32)


MERGE_MATMUL_STAGES = 4


def _merge_stages(rows, counts, yr_ref, ya_ref, gr_ref, ga_ref, x_ref, mod_ref, wb_ref, wo_ref, g2_ref, wr_ref,
                  br_ref, x1_ref, h2_ref, route_ref):
    tm = rows.stop - rows.start
    zr = jnp.dot(yr_ref[rows, :], wb_ref[0], preferred_element_type=F32)
    yield
    za = jnp.dot(ya_ref[rows, :], wb_ref[1], preferred_element_type=F32)
    mix = (gr_ref[rows, :] * zr + ga_ref[rows, :] * za).astype(BF16)
    yield
    mixed = jnp.dot(mix, wo_ref[...], preferred_element_type=F32)
    x1 = x_ref[rows, :] + mod_ref[0, 2:3, :] * mixed
    x1_ref[rows, :] = x1
    h2 = _norm_mod(x1, g2_ref[...], mod_ref[0, 3:4, :], mod_ref[0, 4:5, :])
    h2_ref[rows, :] = h2
    yield

    nt = (((1,), (1,)), ((), ()))
    ne = N_EXPERTS
    hi = h2.astype(BF16)
    lo = (h2 - hi.astype(F32)).astype(BF16)
    both = lax.dot_general(wr_ref[...], hi, nt, preferred_element_type=F32)
    lg = (both[:ne] + both[ne:]
          + lax.dot_general(wr_ref[0:ne, :], lo, nt, preferred_element_type=F32) + br_ref[...])
    yield

    eidx = lax.broadcasted_iota(jnp.int32, lg.shape, 0).astype(F32)
    vals, idxs = [], []
    onehot = jnp.zeros(lg.shape, F32)
    for _ in range(TOP_K):
        m = jnp.max(lg, axis=0, keepdims=True)
        idx = jnp.min(jnp.where(lg == m, eidx, float(ne)), axis=0, keepdims=True)
        pick = eidx == idx
        onehot = jnp.where(pick, 1.0, onehot)
        lg = jnp.where(pick, -jnp.inf, lg)
        vals.append(m)
        idxs.append(idx)
        yield
    es = [jnp.exp(v - vals[0]) for v in vals]
    den = es[0] + es[1] + es[2] + es[3]
    before0 = counts["total"]

    rr = lax.broadcasted_iota(jnp.int32, (tm, tm), 0)
    cc = lax.broadcasted_iota(jnp.int32, (tm, tm), 1)
    earlier = jnp.where(rr < cc, 1.0, 0.0).astype(BF16)
    before = jnp.dot(onehot.astype(BF16), earlier, preferred_element_type=F32) + before0
    ranks = [jnp.sum(jnp.where(eidx == idxs[k], before, 0.0), axis=0, keepdims=True) for k in range(TOP_K)]

    rows_t = jnp.concatenate(idxs + [e / den for e in es] + ranks
                             + [jnp.zeros((ROUTE_COLS - 3 * TOP_K, tm), F32)], axis=0)
    route_ref[rows, :] = rows_t.T
    counts["total"] = before0 + jnp.sum(onehot, axis=1, keepdims=True)
    yield


def _merge(y_rnn, y_att, gl, x2, mod3, w_branch, w_out, norm2_g, w_router, b_router, rows_per_batch):
    n, D = x2.shape
    tm = 2 * MERGE_SUB
    tpb = rows_per_batch // tm
    wr_t = w_router.T
    wr_hi = wr_t.astype(BF16)
    wr = jnp.concatenate([wr_hi, (wr_t - wr_hi.astype(F32)).astype(BF16)], axis=0)
    br = b_router.reshape(N_EXPERTS, 1)
    row = lambda i: (i, 0)
    fixed = lambda i: (0, 0)
    once = pl.Buffered(1)
    wide = jax.ShapeDtypeStruct((n, D), F32)
    return pl.pallas_call(
        _merge_body,
        grid=(n // tm,),
        in_specs=[pl.BlockSpec((tm, D), row),
                  pl.BlockSpec((tm, D), row),
                  pl.BlockSpec((tm, D), lambda i: (i, 0)),
                  pl.BlockSpec((tm, D), lambda i: (i, 1)),
                  pl.BlockSpec((tm, D), row),
                  pl.BlockSpec((1, 6, D), lambda i: (i // tpb, 0, 0)),
                  pl.BlockSpec((2, D, D), lambda i: (0, 0, 0), pipeline_mode=once),
                  pl.BlockSpec((D, D), fixed, pipeline_mode=once),
                  pl.BlockSpec((1, D), fixed),
                  pl.BlockSpec((2 * N_EXPERTS, D), fixed, pipeline_mode=once),
                  pl.BlockSpec((N_EXPERTS, 1), fixed)],
        out_specs=[pl.BlockSpec((tm, D), row),
                   pl.BlockSpec((tm, D), row),
                   pl.BlockSpec((tm, ROUTE_COLS), row),
                   pl.BlockSpec((N_EXPERTS, LANES), fixed)],
        out_shape=[wide, wide, jax.ShapeDtypeStruct((n, ROUTE_COLS), F32),
                   jax.ShapeDtypeStruct((N_EXPERTS, LANES), jnp.int32)],
        scratch_shapes=[pltpu.VMEM((N_EXPERTS, 1), F32)],
        compiler_params=_params("arbitrary"),
        name="merge",
    )(y_rnn, y_att, gl, gl, x2, mod3, w_branch.astype(BF16), w_out.astype(BF16),
      norm2_g.reshape(1, D), wr, br)


DISPATCH_TOKENS = 256
ROW_DMA_UNROLL = 8


def _dispatch_body(dest_ref, zblk_ref, src_ref, dst_hbm, zbuf, stage, sems, zsem):
    t = DISPATCH_TOKENS
    c = pl.program_id(0)

    @pl.when(c == 0)
    def _():
        zbuf[...] = jnp.zeros_like(zbuf)

        def zero_copy(b):
            r0 = pl.multiple_of(b * EXPERT_ROWS, EXPERT_ROWS)
            return pltpu.make_async_copy(zbuf, dst_hbm.at[pl.ds(r0, EXPERT_ROWS), :], zsem)

        def zissue(j, carry):
            @pl.when(zblk_ref[j] >= 0)
            def _():
                zero_copy(zblk_ref[j]).start()
            return carry

        def zwait(j, carry):
            @pl.when(zblk_ref[j] >= 0)
            def _():
                zero_copy(zblk_ref[j]).wait()
            return carry

        lax.fori_loop(0, zblk_ref.shape[0], zissue, 0)
        lax.fori_loop(0, zblk_ref.shape[0], zwait, 0)

    slot = c % 2

    def drain(s):
        for k in range(TOP_K):
            pltpu.make_async_copy(stage.at[s], dst_hbm.at[pl.ds(0, t), :], sems.at[s]).wait()

    @pl.when(c >= 2)
    def _():
        drain(slot)

    stage[slot] = src_ref[...]

    def issue(g, carry):
        for u in range(ROW_DMA_UNROLL):
            r = g * ROW_DMA_UNROLL + u
            for k in range(TOP_K):
                d = dest_ref[(c * t + r) * TOP_K + k]
                pltpu.make_async_copy(stage.at[slot, pl.ds(r, 1), :], dst_hbm.at[pl.ds(d, 1), :],
                                      sems.at[slot]).start()
        return carry

    lax.fori_loop(0, t // ROW_DMA_UNROLL, issue, 0)

    @pl.when(c == pl.num_programs(0) - 1)
    def _():
        drain(1 - slot)
        drain(slot)


def _dispatch(h2, dest, zero_blocks, n_rows):
    n, D = h2.shape
    return pl.pallas_call(
        _dispatch_body,
        grid_spec=pltpu.PrefetchScalarGridSpec(
            num_scalar_prefetch=2,
            grid=(n // DISPATCH_TOKENS,),
            in_specs=[pl.BlockSpec((DISPATCH_TOKENS, D), lambda i, d, z: (i, 0))],
            out_specs=pl.BlockSpec(memory_space=pl.ANY),
            scratch_shapes=[pltpu.VMEM((EXPERT_ROWS, D), F32), pltpu.VMEM((2, DISPATCH_TOKENS, D), F32),
                            pltpu.SemaphoreType.DMA((2,)), pltpu.SemaphoreType.DMA],
        ),
        out_shape=jax.ShapeDtypeStruct((n_rows, D), F32),
        compiler_params=_params("arbitrary"),
        name="dispatch",
    )(dest, zero_blocks, h2)


def _experts_body(bexp_ref, nused_ref, first_ref, slot_ref, next_ref, x_ref, wu_hbm, bu_ref, wd_hbm, bd_ref,
                  o_ref, wu_buf, wd_buf, wu_bf, wd_bf, sems):
    i = pl.program_id(0)
    dff = wd_hbm.shape[1]
    used = i < nused_ref[0]

    def weight_copies(e, s):
        return (pltpu.make_async_copy(wu_hbm.at[e], wu_buf.at[s], sems.at[0, s]),
                pltpu.make_async_copy(wd_hbm.at[e], wd_buf.at[s], sems.at[1, s]))

    @pl.when(i == 0)
    def _():
        for cp in weight_copies(bexp_ref[0], 0):
            cp.start()

    @pl.when(used & (first_ref[i] == 1))
    def _():
        s = slot_ref[i]
        for cp in weight_copies(bexp_ref[i], s):
            cp.wait()
        wu_bf[...] = wu_buf[s].astype(BF16)
        wd_bf[...] = wd_buf[s].astype(BF16)

        @pl.when(next_ref[i] >= 0)
        def _():
            for cp in weight_copies(next_ref[i], 1 - s):
                cp.start()

    @pl.when(used)
    def _():
        hc = jnp.dot(x_ref[...].astype(BF16), wu_bf[...], preferred_element_type=F32) + bu_ref[0]
        g = jnp.minimum(hc[:, :dff], SWIGLU_LIMIT)
        lin = jnp.clip(hc[:, dff:], -SWIGLU_LIMIT, SWIGLU_LIMIT)
        act = (lin + 1.0) * g * jax.nn.sigmoid(SWIGLU_ALPHA * g)
        y = jnp.dot(act.astype(BF16), wd_bf[...], preferred_element_type=F32) + bd_ref[0]
        o_ref[...] = y.reshape(o_ref.shape)

    @pl.when(i >= nused_ref[0])
    def _():
        o_ref[...] = jnp.zeros_like(o_ref)


def _experts(x_rows, block_exp, n_used, w_up, b_up, w_down, b_down, n_blocks):
    R, D = x_rows.shape
    E, _, F2 = w_up.shape
    dff = w_down.shape[1]
    bidx = jnp.arange(n_blocks, dtype=jnp.int32)
    first = ((bidx == 0) | (block_exp != jnp.roll(block_exp, 1))) & (bidx < n_used[0])
    slot = (jnp.cumsum(first) - 1) % 2
    first_pos = jnp.where(first, bidx, n_blocks)
    next_pos = jnp.flip(lax.cummin(jnp.flip(jnp.roll(first_pos, -1).at[-1].set(n_blocks))))
    next_exp = jnp.where(next_pos < n_blocks, block_exp[jnp.minimum(next_pos, n_blocks - 1)], -1)
    blk = lambda i, *_: (i, 0)
    bsel = lambda i, be, *_: (be[i], 0, 0)
    return pl.pallas_call(
        _experts_body,
        grid_spec=pltpu.PrefetchScalarGridSpec(
            num_scalar_prefetch=5,
            grid=(n_blocks,),
            in_specs=[pl.BlockSpec((EXPERT_ROWS, D), blk),
                      pl.BlockSpec(memory_space=pl.ANY),
                      pl.BlockSpec((1, 1, F2), bsel),
                      pl.BlockSpec(memory_space=pl.ANY),
                      pl.BlockSpec((1, 1, D), bsel)],
            out_specs=pl.BlockSpec((EXPERT_ROWS, D // LANES, LANES), lambda i, *_: (i, 0, 0)),
            scratch_shapes=[pltpu.VMEM((2, D, F2), F32), pltpu.VMEM((2, dff, D), F32),
                            pltpu.VMEM((D, F2), BF16), pltpu.VMEM((dff, D), BF16),
                            pltpu.SemaphoreType.DMA((2, 2))],
        ),
        out_shape=jax.ShapeDtypeStruct((R, D // LANES, LANES), F32),
        compiler_params=_params("arbitrary"),
        name="experts",
    )(block_exp, n_used, first.astype(jnp.int32), slot.astype(jnp.int32), next_exp.astype(jnp.int32),
      x_rows, w_up, b_up.reshape(E, 1, F2), w_down, b_down.reshape(E, 1, D))


COMBINE_ROWS = 128


def _combine_body(dest_ref, y_hbm, x1_ref, gate_ref, mod_ref, o_ref, buf, sems):
    i = pl.program_id(0)
    t = COMBINE_ROWS
    cur = i % 2

    def request(tile, slot):
        def issue(g, carry):
            for u in range(ROW_DMA_UNROLL):
                r = g * ROW_DMA_UNROLL + u
                for k in range(TOP_K):
                    d = dest_ref[(tile * t + r) * TOP_K + k]
                    pltpu.make_async_copy(y_hbm.at[d], buf.at[slot, k, r], sems.at[slot]).start()
            return carry
        lax.fori_loop(0, t // ROW_DMA_UNROLL, issue, 0)

    @pl.when(i == 0)
    def _():
        request(0, 0)

    @pl.when(i + 1 < pl.num_programs(0))
    def _():
        request(i + 1, 1 - cur)

    for k in range(TOP_K):
        pltpu.make_async_copy(y_hbm.at[pl.ds(0, t)], buf.at[cur, k], sems.at[cur]).wait()
    gates = gate_ref[:, TOP_K:2 * TOP_K]
    y = gates[:, 0:1] * buf[cur, 0].reshape(t, -1)
    for k in range(1, TOP_K):
        y = y + gates[:, k:k + 1] * buf[cur, k].reshape(t, -1)
    o_ref[...] = x1_ref[...] + mod_ref[0, 5:6, :] * y


def _combine(y_rows, dest, x1, gates, mod3, rows_per_batch):
    n, D = x1.shape
    t = COMBINE_ROWS
    tpb = rows_per_batch // t
    return pl.pallas_call(
        _combine_body,
        grid_spec=pltpu.PrefetchScalarGridSpec(
            num_scalar_prefetch=1,
            grid=(n // t,),
            in_specs=[pl.BlockSpec(memory_space=pl.ANY),
                      pl.BlockSpec((t, D), lambda i, d: (i, 0)),
                      pl.BlockSpec((t, ROUTE_COLS), lambda i, d: (i, 0)),
                      pl.BlockSpec((1, 6, D), lambda i, d: (i // tpb, 0, 0))],
            out_specs=pl.BlockSpec((t, D), lambda i, d: (i, 0)),
            scratch_shapes=[pltpu.VMEM((2, TOP_K, t, D // LANES, LANES), F32), pltpu.SemaphoreType.DMA((2,))],
        ),
        out_shape=jax.ShapeDtypeStruct((n, D), F32),
        compiler_params=_params("arbitrary"),
        name="combine",
    )(dest, y_rows, x1, gates, mod3)


def kernel(x, c, positions, ada_w, ada_b, norm1_g, norm2_g, w_in, conv_w, conv_b, w_rg_a, b_rg_a, w_rg_x, b_rg_x, lru_lambda, q_norm_g, k_norm_g, b_gate, w_branch, w_out, w_router, b_router, w_up, b_up, w_down, b_down):
    B, S, D = x.shape
    n = B * S
    depth = ada_w.shape[0]
    x2 = x.reshape(n, D)
    cos_t, sin_t = _rope_tables(positions)
    perm = _rope_head_perm()
    head_cols = (jnp.arange(2 * N_HEADS)[:, None] * HEAD_DIM + perm[None, :]).reshape(-1)
    for l in range(depth):
        mod3 = _adaln(c, ada_w[l], ada_b[l]).reshape(B, 6, D)
        h1 = _norm_mod_call(x2, mod3, norm1_g[l], S)
        w = w_in[l].astype(BF16)
        xr = _proj("plain", h1, w[:, 0:D], [], F32, "proj_xr")
        gg = _proj("gelu", h1, w[:, D:2 * D], [], F32, "proj_gr")
        qk_g = jnp.concatenate([jnp.tile(q_norm_g[l][perm], N_HEADS),
                                jnp.tile(k_norm_g[l][perm], N_HEADS)]).reshape(1, 2 * D)
        qk = _proj("qk", h1, w[:, 2 * D:4 * D][:, head_cols], [qk_g, cos_t, sin_t], F32, "proj_qk")
        v = _proj("plain", h1, w[:, 4 * D:5 * D], [], BF16, "proj_v")
        gl = _proj("gate", h1, w[:, 5 * D:7 * D], [b_gate[l].reshape(1, 2 * D)], F32, "proj_gl")

        y_rnn = _rglru(xr, gg, conv_w[l], conv_b[l], w_rg_a[l], b_rg_a[l], w_rg_x[l], b_rg_x[l],
                       lru_lambda[l], B, S)
        y_att = _moba(qk, v, B, S)

        x1, h2, route, cnt = _merge(y_rnn, y_att, gl, x2, mod3, w_branch[l], w_out[l],
                                    norm2_g[l], w_router[l], b_router[l], S)

        counts = cnt[:, 0]
        padded = (counts + EXPERT_ROWS - 1) // EXPERT_ROWS * EXPERT_ROWS
        pad_end = jnp.cumsum(padded)
        pad_start = pad_end - padded
        top_idx = route[:, 0:TOP_K].astype(jnp.int32)
        rank = route[:, 2 * TOP_K:3 * TOP_K].astype(jnp.int32)
        dest = (pad_start[top_idx] + rank).reshape(n * TOP_K).astype(jnp.int32)
        n_blocks = (n * TOP_K) // EXPERT_ROWS + N_EXPERTS
        block_start = jnp.arange(n_blocks, dtype=jnp.int32) * EXPERT_ROWS
        block_exp = jnp.minimum(jnp.sum(block_start[:, None] >= pad_end[None, :], axis=1),
                                N_EXPERTS - 1).astype(jnp.int32)
        n_used = (pad_end[-1] // EXPERT_ROWS).astype(jnp.int32)
        last_blk = jnp.where(padded > 0, pad_end // EXPERT_ROWS - 1, -1)
        spare = n_used + jnp.arange(N_EXPERTS, dtype=jnp.int32)
        spare = jnp.where(spare < n_blocks, spare, -1)
        zero_blocks = jnp.concatenate([last_blk, spare]).astype(jnp.int32)

        x_rows = _dispatch(h2, dest, zero_blocks, n_blocks * EXPERT_ROWS)
        y_rows = _experts(x_rows, block_exp, n_used.reshape(1), w_up[l], b_up[l], w_down[l], b_down[l], n_blocks)
        x2 = _combine(y_rows, dest, x1, route, mod3, S)
    return x2.reshape(B, S, D)
```

```python
import functools

import jax
import jax.numpy as jnp
from jax import lax
from jax.experimental import pallas as pl
from jax.experimental.pallas import tpu as pltpu

F32 = jnp.float32
BF16 = jnp.bfloat16
HIGHEST = lax.Precision.HIGHEST

EPS = 1e-6
LANES = 128
N_HEADS = 8
HEAD_DIM = 128
RNN_BLOCKS = 8
RNN_BW = 128
CONV_W = 4
LRU_C = 8.0
MOBA_BLOCK = 256
MOBA_TOPK = 3
MOBA_CHUNK = 4
MOBA_HEADS = 4
MERGE_SUB = 256
LOG2_E = 1.4426950408889634
ROPE_DIMS = HEAD_DIM // 4
ROPE_THETA = 500000.0
N_EXPERTS = 32
TOP_K = 4
SWIGLU_LIMIT = 7.0
SWIGLU_ALPHA = 1.702
EXPERT_ROWS = 256
NEG_BIG = -1e30

VMEM_LIMIT = 56 * 1024 * 1024


def _params(*sem):
    return pltpu.CompilerParams(dimension_semantics=sem, vmem_limit_bytes=VMEM_LIMIT)


def _adaln_body(c_ref, w_ref, b_ref, o_ref):
    cs = c_ref[...]
    cs = cs * jax.nn.sigmoid(cs)
    o_ref[...] = jnp.dot(cs, w_ref[...], preferred_element_type=F32, precision=HIGHEST) + b_ref[...]


def _adaln(c, ada_w, ada_b):
    B, D = c.shape
    W = ada_w.shape[1]
    cpad = jnp.zeros((8, D), F32).at[:B].set(c)
    tn = 1024
    mod = pl.pallas_call(
        _adaln_body,
        grid=(W // tn,),
        in_specs=[pl.BlockSpec((8, D), lambda j: (0, 0)),
                  pl.BlockSpec((D, tn), lambda j: (0, j)),
                  pl.BlockSpec((1, tn), lambda j: (0, j))],
        out_specs=pl.BlockSpec((8, tn), lambda j: (0, j)),
        out_shape=jax.ShapeDtypeStruct((8, W), F32),
        compiler_params=_params("parallel"),
        name="adaln",
    )(cpad, ada_w, ada_b.reshape(1, W))
    return mod[:B]


def _rope_head_perm():
    half = ROPE_DIMS // 2
    mid = HEAD_DIM // 2
    return jnp.concatenate([jnp.arange(0, half), jnp.arange(ROPE_DIMS, mid + half),
                            jnp.arange(half, ROPE_DIMS), jnp.arange(mid + half, HEAD_DIM)])


def _rope_body(pos_ref, freq_ref, c_ref, s_ref):
    ang = pos_ref[...].astype(F32) * freq_ref[...]
    lane = lax.broadcasted_iota(jnp.int32, ang.shape, 1)
    half = ROPE_DIMS // 2
    mid = HEAD_DIM // 2
    s = jnp.sin(ang)
    c_ref[...] = jnp.cos(ang)
    s_ref[...] = jnp.where(lane < half, -s, jnp.where((lane >= mid) & (lane < mid + half), s, 0.0))


def _rope_tables(positions):
    n = positions.size
    half = ROPE_DIMS // 2
    mid = HEAD_DIM // 2
    freqs = ROPE_THETA ** (-jnp.arange(half, dtype=F32) / half)
    freq_lane = jnp.zeros((1, LANES), F32).at[0, :half].set(freqs).at[0, mid:mid + half].set(freqs)
    tm = 1024
    tab = jax.ShapeDtypeStruct((n, LANES), F32)
    return pl.pallas_call(
        _rope_body,
        grid=(n // tm,),
        in_specs=[pl.BlockSpec((tm, 1), lambda i: (i, 0)),
                  pl.BlockSpec((1, LANES), lambda i: (0, 0))],
        out_specs=[pl.BlockSpec((tm, LANES), lambda i: (i, 0))] * 2,
        out_shape=[tab, tab],
        compiler_params=_params("parallel"),
        name="rope_tab",
    )(positions.reshape(n, 1), freq_lane)


def _norm_mod(x, g, sh, sc):
    ms = jnp.mean(x * x, axis=-1, keepdims=True)
    y = x * lax.rsqrt(ms + EPS)
    return (y * g) * (1.0 + sc) + sh


def _norm_mod_body(x_ref, mod_ref, g_ref, o_ref):
    o_ref[...] = _norm_mod(x_ref[...], g_ref[...], mod_ref[0, 0:1, :], mod_ref[0, 1:2, :]).astype(o_ref.dtype)


def _norm_mod_call(x2, mod3, g, rows_per_batch):
    n, D = x2.shape
    tm = 512
    tpb = rows_per_batch // tm
    return pl.pallas_call(
        _norm_mod_body,
        grid=(n // tm,),
        in_specs=[pl.BlockSpec((tm, D), lambda i: (i, 0)),
                  pl.BlockSpec((1, 6, D), lambda i: (i // tpb, 0, 0)),
                  pl.BlockSpec((1, D), lambda i: (0, 0))],
        out_specs=pl.BlockSpec((tm, D), lambda i: (i, 0)),
        out_shape=jax.ShapeDtypeStruct((n, D), BF16),
        compiler_params=_params("parallel"),
        name="norm_mod",
    )(x2, mod3, g.reshape(1, D))


def _gelu_tanh(x):
    return 0.5 * x * (1.0 + jnp.tanh(0.7978845608028654 * (x + 0.044715 * (x * x * x))))


PROJ_SUB = 256
PROJ_PIECE = 256


def _qk_head(seg, g, cos, sin):
    ms = jnp.mean(seg * seg, axis=-1, keepdims=True)
    y = seg * lax.rsqrt(ms + EPS) * g
    return y * cos + pltpu.roll(y, HEAD_DIM // 2, axis=1) * sin


def _proj_stages(kind, rows, h_ref, w_ref, extras, o_ref):
    n_pieces = w_ref.shape[1] // PROJ_PIECE
    pieces = []
    for j in range(n_pieces):
        pieces.append(jnp.dot(h_ref[rows, :], w_ref[:, j * PROJ_PIECE:(j + 1) * PROJ_PIECE],
                              preferred_element_type=F32))
        yield
    for j in range(n_pieces):
        cols = slice(j * PROJ_PIECE, (j + 1) * PROJ_PIECE)
        acc = pieces[j]
        if kind == "gelu":
            acc = _gelu_tanh(acc)
        elif kind == "gate":
            (b_ref,) = extras
            acc = jax.nn.sigmoid(acc + b_ref[:, cols])
        elif kind == "qk":
            g_ref, c_ref, s_ref = extras
            cos, sin = c_ref[rows, :], s_ref[rows, :]
            acc = jnp.concatenate(
                [_qk_head(acc[:, o:o + HEAD_DIM], g_ref[:, cols.start + o:cols.start + o + HEAD_DIM], cos, sin)
                 for o in range(0, PROJ_PIECE, HEAD_DIM)], axis=1)
        o_ref[rows, cols] = acc.astype(o_ref.dtype)
        yield


def _proj_body(kind, h_ref, w_ref, *rest):
    *extras, o_ref = rest
    first, second = [_proj_stages(kind, slice(t * PROJ_SUB, (t + 1) * PROJ_SUB), h_ref, w_ref, extras, o_ref)
                     for t in range(2)]
    n_pieces = w_ref.shape[1] // PROJ_PIECE
    for _ in range(n_pieces):
        next(first)
    for _ in range(n_pieces):
        next(second)
        next(first)
    for stage in second:
        pass


def _proj(kind, h, w, extras, out_dtype, name):
    n, D = h.shape
    W = w.shape[1]
    tm, tn = 512, 1024
    in_specs = [pl.BlockSpec((tm, D), lambda j, i: (i, 0)),
                pl.BlockSpec((D, tn), lambda j, i: (0, j))]
    args = [h, w]
    for arr in extras:
        if arr.shape[0] == 1:
            in_specs.append(pl.BlockSpec((1, tn), lambda j, i: (0, j)))
        else:
            in_specs.append(pl.BlockSpec((tm, LANES), lambda j, i: (i, 0)))
        args.append(arr)
    return pl.pallas_call(
        functools.partial(_proj_body, kind),
        grid=(W // tn, n // tm),
        in_specs=in_specs,
        out_specs=pl.BlockSpec((tm, tn), lambda j, i: (i, j)),
        out_shape=jax.ShapeDtypeStruct((n, W), out_dtype),
        compiler_params=_params("parallel", "parallel"),
        name=name,
    )(*args)


def _rglru_body(xr_ref, gg_ref, cw_ref, cb_ref, wa_ref, ba_ref, wx_ref, bx_ref, lam_ref,
                o_ref, xbuf, hcar, a_s, u_s):
    s = pl.program_id(1)
    ts, D = xr_ref.shape

    @pl.when(s == 0)
    def _():
        xbuf[0:8, :] = jnp.zeros((8, D), F32)
        hcar[...] = jnp.zeros_like(hcar)

    @pl.when(s > 0)
    def _():
        xbuf[0:8, :] = xbuf[ts:ts + 8, :]

    xbuf[8:8 + ts, :] = xr_ref[...]
    xc = cb_ref[...] + cw_ref[0:1, :] * xbuf[8:8 + ts, :]
    for i in range(1, CONV_W):
        xc = xc + cw_ref[i:i + 1, :] * xbuf[8 - i:8 - i + ts, :]

    ra, rx = [], []
    for n in range(RNN_BLOCKS):
        xb = xc[:, n * RNN_BW:(n + 1) * RNN_BW].astype(BF16)
        ra.append(jnp.dot(xb, wa_ref[n], preferred_element_type=F32))
        rx.append(jnp.dot(xb, wx_ref[n], preferred_element_type=F32))
    r = jax.nn.sigmoid(jnp.concatenate(ra, axis=1) + ba_ref[...])
    ig = jax.nn.sigmoid(jnp.concatenate(rx, axis=1) + bx_ref[...])

    z = -lam_ref[...]
    softplus = jnp.maximum(z, 0.0) + jnp.log1p(jnp.exp(-jnp.abs(z)))
    log_a = -LRU_C * r * softplus
    a = jnp.exp(log_a)
    mult = jnp.sqrt(1.0 - a * a)
    row = lax.broadcasted_iota(jnp.int32, (ts, D), 0)
    mult = jnp.where((row == 0) & (s == 0), 1.0, mult)
    u = mult * (ig * xc)

    rm = row & 7
    for d in (1, 2, 4):
        keep = rm >= d
        a_sh = pltpu.roll(a, d, axis=0)
        u_sh = pltpu.roll(u, d, axis=0)
        u = jnp.where(keep, a * u_sh + u, u)
        a = jnp.where(keep, a * a_sh, a)
    a_s[...] = a
    u_s[...] = u

    def group(g, h):
        r0 = pl.multiple_of(g * 8, 8)
        hg = u_s[pl.ds(r0, 8), :] + a_s[pl.ds(r0, 8), :] * h
        u_s[pl.ds(r0, 8), :] = hg
        return hg[7:8, :]

    hcar[...] = lax.fori_loop(0, ts // 8, group, hcar[...])
    o_ref[...] = (u_s[...] * gg_ref[...]).astype(o_ref.dtype)


def _rglru(xr, gg, conv_w, conv_b, w_a, b_a, w_x, b_x, lam, B, S):
    n, D = xr.shape
    ts = 256
    spb = S // ts
    row = lambda b, s: (b * spb + s, 0)
    vec = lambda b, s: (0, 0)
    return pl.pallas_call(
        _rglru_body,
        grid=(B, spb),
        in_specs=[pl.BlockSpec((ts, D), row),
                  pl.BlockSpec((ts, D), row),
                  pl.BlockSpec((CONV_W, D), vec),
                  pl.BlockSpec((1, D), vec),
                  pl.BlockSpec((RNN_BLOCKS, RNN_BW, RNN_BW), lambda b, s: (0, 0, 0)),
                  pl.BlockSpec((1, D), vec),
                  pl.BlockSpec((RNN_BLOCKS, RNN_BW, RNN_BW), lambda b, s: (0, 0, 0)),
                  pl.BlockSpec((1, D), vec),
                  pl.BlockSpec((1, D), vec)],
        out_specs=pl.BlockSpec((ts, D), row),
        out_shape=jax.ShapeDtypeStruct((n, D), BF16),
        scratch_shapes=[pltpu.VMEM((ts + 8, D), F32), pltpu.VMEM((1, D), F32),
                        pltpu.VMEM((ts, D), F32), pltpu.VMEM((ts, D), F32)],
        compiler_params=_params("arbitrary", "arbitrary"),
        name="rglru",
    )(xr, gg, conv_w, conv_b.reshape(1, D), w_a.astype(BF16), b_a.reshape(1, D),
      w_x.astype(BF16), b_x.reshape(1, D), lam.reshape(1, D))


def _moba_body(q_ref, k_ref, v_ref, o_ref, kaug, vaug, kmean, m_s, acc_s):
    qb = pl.program_id(2)
    S = k_ref.shape[0]
    nb = S // MOBA_BLOCK
    bs = MOBA_BLOCK
    dh = HEAD_DIM
    heads = range(MOBA_HEADS)
    hcol = lambda hh: slice(hh * dh, (hh + 1) * dh)

    @pl.when(qb == 0)
    def _():
        blk = lax.broadcasted_iota(jnp.int32, (S, dh), 0) // bs
        col = lax.broadcasted_iota(jnp.int32, (S, dh), 1)
        onehot = jnp.where(col == blk, 1.0, 0.0).astype(BF16)
        for hh in heads:
            kaug[hh, :, 0:dh] = k_ref[:, hcol(hh)].astype(BF16)
            kaug[hh, :, dh:2 * dh] = onehot
            vaug[hh, :, 0:dh] = v_ref[:, hcol(hh)]
            vaug[hh, :, dh:2 * dh] = jnp.ones((S, dh), BF16)
            kmean[hh] = jnp.zeros((LANES, dh), F32)
            for j in range(nb):
                kmean[hh, j:j + 1, :] = jnp.mean(k_ref[j * bs:(j + 1) * bs, hcol(hh)], axis=0, keepdims=True)

    scale = HEAD_DIM ** -0.5
    nt = (((1,), (1,)), ((), ()))
    c2 = scale * LOG2_E
    r0 = pl.multiple_of(qb * bs, bs)
    nb_pad = -(-nb // 8) * 8
    rr = lax.broadcasted_iota(jnp.int32, (bs, bs), 0)
    cc = lax.broadcasted_iota(jnp.int32, (bs, bs), 1)

    qs = [q_ref[:, hcol(hh)] for hh in heads]
    qbfs = [q.astype(BF16) for q in qs]

    blk_i = lax.broadcasted_iota(jnp.int32, (nb_pad, bs), 0)
    blk_f = blk_i.astype(F32)
    past = blk_i < qb
    gs = [jnp.where(past, lax.dot_general(kmean[hh, 0:nb_pad, :], qs[hh], nt, preferred_element_type=F32,
                                          precision=HIGHEST), -jnp.inf) for hh in heads]
    ss = [jnp.where(cc <= rr, lax.dot_general(qbfs[hh], kaug[hh, pl.ds(r0, bs), 0:dh], nt,
                                              preferred_element_type=F32), NEG_BIG) for hh in heads]
    biases = [jnp.full((nb_pad, bs), NEG_BIG, F32) for _ in heads]
    for _ in range(MOBA_TOPK):
        for hh in heads:
            m = jnp.max(gs[hh], axis=0, keepdims=True)
            idx = jnp.min(jnp.where(gs[hh] == m, blk_f, float(LANES)), axis=0, keepdims=True)
            pick = (blk_f == idx) & past
            biases[hh] = jnp.where(pick, 0.0, biases[hh])
            gs[hh] = jnp.where(pick, -jnp.inf, gs[hh])
    fill = jnp.full((bs, dh - nb_pad), NEG_BIG, BF16)
    qaugs = [jnp.concatenate([qbfs[hh], biases[hh].T.astype(BF16), fill], axis=1) for hh in heads]

    ps = []
    for hh in heads:
        m0 = jnp.max(ss[hh], axis=1, keepdims=True)
        ps.append(jnp.exp2((ss[hh] - m0) * c2).astype(BF16))
        m_s[hh] = m0
    for hh in heads:
        acc_s[hh] = jnp.dot(ps[hh], vaug[hh, pl.ds(r0, bs), :], preferred_element_type=F32)

    cw = MOBA_CHUNK * bs
    for c in range(nb // MOBA_CHUNK):
        @pl.when(c * MOBA_CHUNK < qb)
        def _(c=c):
            scs = [lax.dot_general(qaugs[hh], kaug[hh, c * cw:(c + 1) * cw, :], nt, preferred_element_type=F32)
                   for hh in heads]
            pcs, alphas = [], []
            for hh in heads:
                m_old = m_s[hh]
                m_new = jnp.maximum(m_old, jnp.max(scs[hh], axis=1, keepdims=True))
                alphas.append(jnp.exp2((m_old - m_new) * c2))
                pcs.append(jnp.exp2((scs[hh] - m_new) * c2).astype(BF16))
                m_s[hh] = m_new
            for hh in heads:
                acc_s[hh] = alphas[hh] * acc_s[hh] + jnp.dot(pcs[hh], vaug[hh, c * cw:(c + 1) * cw, :],
                                                             preferred_element_type=F32)

    for hh in heads:
        o_ref[:, hcol(hh)] = (acc_s[hh, :, 0:dh] / acc_s[hh, :, dh:dh + 1]).astype(o_ref.dtype)


def _moba(qk, v, B, S):
    n, D = v.shape
    nq = S // MOBA_BLOCK
    hs = MOBA_HEADS
    w = hs * HEAD_DIM
    return pl.pallas_call(
        _moba_body,
        grid=(B, N_HEADS // hs, nq),
        in_specs=[pl.BlockSpec((MOBA_BLOCK, w), lambda b, h, i: (b * nq + i, h)),
                  pl.BlockSpec((S, w), lambda b, h, i: (b, N_HEADS // hs + h)),
                  pl.BlockSpec((S, w), lambda b, h, i: (b, h))],
        out_specs=pl.BlockSpec((MOBA_BLOCK, w), lambda b, h, i: (b * nq + i, h)),
        out_shape=jax.ShapeDtypeStruct((n, D), BF16),
        scratch_shapes=[pltpu.VMEM((hs, S, 2 * HEAD_DIM), BF16), pltpu.VMEM((hs, S, 2 * HEAD_DIM), BF16),
                        pltpu.VMEM((hs, LANES, HEAD_DIM), F32), pltpu.VMEM((hs, MOBA_BLOCK, 1), F32),
                        pltpu.VMEM((hs, MOBA_BLOCK, 2 * HEAD_DIM), F32)],
        compiler_params=_params("parallel", "parallel", "arbitrary"),
        name="moba",
    )(qk, qk, v)


ROUTE_COLS = 16


def _merge_body(yr_ref, ya_ref, gr_ref, ga_ref, x_ref, mod_ref, wb_ref, wo_ref, g2_ref, wr_ref, br_ref,
                x1_ref, h2_ref, route_ref, cnt_ref, carry):
    i = pl.program_id(0)

    @pl.when(i == 0)
    def _():
        carry[...] = jnp.zeros_like(carry)

    tm = MERGE_SUB
    counts = {"total": carry[...]}
    first, second = [
        _merge_stages(slice(h * tm, (h + 1) * tm), counts, yr_ref, ya_ref, gr_ref, ga_ref, x_ref, mod_ref,
                      wb_ref, wo_ref, g2_ref, wr_ref, br_ref, x1_ref, h2_ref, route_ref)
        for h in range(2)]
    for _ in range(MERGE_MATMUL_STAGES):
        next(first)
    for stage in first:
        next(second)
    for stage in second:
        pass
    carry[...] = counts["total"]
    cnt_ref[...] = jnp.broadcast_to(counts["total"], cnt_ref.shape).astype(jnp.int32)


MERGE_MATMUL_STAGES = 4


def _merge_stages(rows, counts, yr_ref, ya_ref, gr_ref, ga_ref, x_ref, mod_ref, wb_ref, wo_ref, g2_ref, wr_ref,
                  br_ref, x1_ref, h2_ref, route_ref):
    tm = rows.stop - rows.start
    zr = jnp.dot(yr_ref[rows, :], wb_ref[0], preferred_element_type=F32)
    yield
    za = jnp.dot(ya_ref[rows, :], wb_ref[1], preferred_element_type=F32)
    mix = (gr_ref[rows, :] * zr + ga_ref[rows, :] * za).astype(BF16)
    yield
    mixed = jnp.dot(mix, wo_ref[...], preferred_element_type=F32)
    x1 = x_ref[rows, :] + mod_ref[0, 2:3, :] * mixed
    x1_ref[rows, :] = x1
    h2 = _norm_mod(x1, g2_ref[...], mod_ref[0, 3:4, :], mod_ref[0, 4:5, :])
    h2_ref[rows, :] = h2
    yield

    nt = (((1,), (1,)), ((), ()))
    ne = N_EXPERTS
    hi = h2.astype(BF16)
    lo = (h2 - hi.astype(F32)).astype(BF16)
    both = lax.dot_general(wr_ref[...], hi, nt, preferred_element_type=F32)
    lg = (both[:ne] + both[ne:]
          + lax.dot_general(wr_ref[0:ne, :], lo, nt, preferred_element_type=F32) + br_ref[...])
    yield

    eidx = lax.broadcasted_iota(jnp.int32, lg.shape, 0).astype(F32)
    vals, idxs = [], []
    onehot = jnp.zeros(lg.shape, F32)
    for _ in range(TOP_K):
        m = jnp.max(lg, axis=0, keepdims=True)
        idx = jnp.min(jnp.where(lg == m, eidx, float(ne)), axis=0, keepdims=True)
        pick = eidx == idx
        onehot = jnp.where(pick, 1.0, onehot)
        lg = jnp.where(pick, -jnp.inf, lg)
        vals.append(m)
        idxs.append(idx)
        yield
    es = [jnp.exp(v - vals[0]) for v in vals]
    den = es[0] + es[1] + es[2] + es[3]
    before0 = counts["total"]

    rr = lax.broadcasted_iota(jnp.int32, (tm, tm), 0)
    cc = lax.broadcasted_iota(jnp.int32, (tm, tm), 1)
    earlier = jnp.where(rr < cc, 1.0, 0.0).astype(BF16)
    before = jnp.dot(onehot.astype(BF16), earlier, preferred_element_type=F32) + before0
    ranks = [jnp.sum(jnp.where(eidx == idxs[k], before, 0.0), axis=0, keepdims=True) for k in range(TOP_K)]

    rows_t = jnp.concatenate(idxs + [e / den for e in es] + ranks
                             + [jnp.zeros((ROUTE_COLS - 3 * TOP_K, tm), F32)], axis=0)
    route_ref[rows, :] = rows_t.T
    counts["total"] = before0 + jnp.sum(onehot, axis=1, keepdims=True)
    yield


def _merge(y_rnn, y_att, gl, x2, mod3, w_branch, w_out, norm2_g, w_router, b_router, rows_per_batch):
    n, D = x2.shape
    tm = 2 * MERGE_SUB
    tpb = rows_per_batch // tm
    wr_t = w_router.T
    wr_hi = wr_t.astype(BF16)
    wr = jnp.concatenate([wr_hi, (wr_t - wr_hi.astype(F32)).astype(BF16)], axis=0)
    br = b_router.reshape(N_EXPERTS, 1)
    row = lambda i: (i, 0)
    fixed = lambda i: (0, 0)
    once = pl.Buffered(1)
    wide = jax.ShapeDtypeStruct((n, D), F32)
    return pl.pallas_call(
        _merge_body,
        grid=(n // tm,),
        in_specs=[pl.BlockSpec((tm, D), row),
                  pl.BlockSpec((tm, D), row),
                  pl.BlockSpec((tm, D), lambda i: (i, 0)),
                  pl.BlockSpec((tm, D), lambda i: (i, 1)),
                  pl.BlockSpec((tm, D), row),
                  pl.BlockSpec((1, 6, D), lambda i: (i // tpb, 0, 0)),
                  pl.BlockSpec((2, D, D), lambda i: (0, 0, 0), pipeline_mode=once),
                  pl.BlockSpec((D, D), fixed, pipeline_mode=once),
                  pl.BlockSpec((1, D), fixed),
                  pl.BlockSpec((2 * N_EXPERTS, D), fixed, pipeline_mode=once),
                  pl.BlockSpec((N_EXPERTS, 1), fixed)],
        out_specs=[pl.BlockSpec((tm, D), row),
                   pl.BlockSpec((tm, D), row),
                   pl.BlockSpec((tm, ROUTE_COLS), row),
                   pl.BlockSpec((N_EXPERTS, LANES), fixed)],
        out_shape=[wide, wide, jax.ShapeDtypeStruct((n, ROUTE_COLS), F32),
                   jax.ShapeDtypeStruct((N_EXPERTS, LANES), jnp.int32)],
        scratch_shapes=[pltpu.VMEM((N_EXPERTS, 1), F32)],
        compiler_params=_params("arbitrary"),
        name="merge",
    )(y_rnn, y_att, gl, gl, x2, mod3, w_branch.astype(BF16), w_out.astype(BF16),
      norm2_g.reshape(1, D), wr, br)


DISPATCH_TOKENS = 256
ROW_DMA_UNROLL = 8


def _dispatch_body(dest_ref, zblk_ref, src_ref, dst_hbm, zbuf, stage, sems, zsem):
    t = DISPATCH_TOKENS
    c = pl.program_id(0)

    @pl.when(c == 0)
    def _():
        zbuf[...] = jnp.zeros_like(zbuf)

        def zero_copy(b):
            r0 = pl.multiple_of(b * EXPERT_ROWS, EXPERT_ROWS)
            return pltpu.make_async_copy(zbuf, dst_hbm.at[pl.ds(r0, EXPERT_ROWS)], zsem)

        def zissue(j, carry):
            @pl.when(zblk_ref[j] >= 0)
            def _():
                zero_copy(zblk_ref[j]).start()
            return carry

        def zwait(j, carry):
            @pl.when(zblk_ref[j] >= 0)
            def _():
                zero_copy(zblk_ref[j]).wait()
            return carry

        lax.fori_loop(0, zblk_ref.shape[0], zissue, 0)
        lax.fori_loop(0, zblk_ref.shape[0], zwait, 0)

    slot = c % 2

    def drain(s):
        for k in range(TOP_K):
            pltpu.make_async_copy(stage.at[s], dst_hbm.at[pl.ds(0, t)], sems.at[s]).wait()

    @pl.when(c >= 2)
    def _():
        drain(slot)

    stage[slot] = src_ref[...].reshape(stage.shape[1:])

    def issue(g, carry):
        for u in range(ROW_DMA_UNROLL):
            r = g * ROW_DMA_UNROLL + u
            for k in range(TOP_K):
                d = dest_ref[(c * t + r) * TOP_K + k]
                pltpu.make_async_copy(stage.at[slot, r], dst_hbm.at[d], sems.at[slot]).start()
        return carry

    lax.fori_loop(0, t // ROW_DMA_UNROLL, issue, 0)

    @pl.when(c == pl.num_programs(0) - 1)
    def _():
        drain(1 - slot)
        drain(slot)


def _dispatch(h2, dest, zero_blocks, n_rows):
    n, D = h2.shape
    return pl.pallas_call(
        _dispatch_body,
        grid_spec=pltpu.PrefetchScalarGridSpec(
            num_scalar_prefetch=2,
            grid=(n // DISPATCH_TOKENS,),
            in_specs=[pl.BlockSpec((DISPATCH_TOKENS, D), lambda i, d, z: (i, 0))],
            out_specs=pl.BlockSpec(memory_space=pl.ANY),
            scratch_shapes=[pltpu.VMEM((EXPERT_ROWS, D // LANES, LANES), F32),
                            pltpu.VMEM((2, DISPATCH_TOKENS, D // LANES, LANES), F32),
                            pltpu.SemaphoreType.DMA((2,)), pltpu.SemaphoreType.DMA],
        ),
        out_shape=jax.ShapeDtypeStruct((n_rows, D // LANES, LANES), F32),
        compiler_params=_params("arbitrary"),
        name="dispatch",
    )(dest, zero_blocks, h2)


def _experts_body(bexp_ref, nused_ref, first_ref, slot_ref, next_ref, x_ref, wu_hbm, bu_ref, wd_hbm, bd_ref,
                  o_ref, wu_buf, wd_buf, wu_bf, wd_bf, sems):
    i = pl.program_id(0)
    dff = wd_hbm.shape[1]
    used = i < nused_ref[0]

    def weight_copies(e, s):
        return (pltpu.make_async_copy(wu_hbm.at[e], wu_buf.at[s], sems.at[0, s]),
                pltpu.make_async_copy(wd_hbm.at[e], wd_buf.at[s], sems.at[1, s]))

    @pl.when(i == 0)
    def _():
        for cp in weight_copies(bexp_ref[0], 0):
            cp.start()

    @pl.when(used & (first_ref[i] == 1))
    def _():
        s = slot_ref[i]
        for cp in weight_copies(bexp_ref[i], s):
            cp.wait()
        wu_bf[...] = wu_buf[s].astype(BF16)
        wd_bf[...] = wd_buf[s].astype(BF16)

        @pl.when(next_ref[i] >= 0)
        def _():
            for cp in weight_copies(next_ref[i], 1 - s):
                cp.start()

    @pl.when(used)
    def _():
        x = x_ref[...].reshape(x_ref.shape[0], -1)
        hc = jnp.dot(x.astype(BF16), wu_bf[...], preferred_element_type=F32) + bu_ref[0]
        g = jnp.minimum(hc[:, :dff], SWIGLU_LIMIT)
        lin = jnp.clip(hc[:, dff:], -SWIGLU_LIMIT, SWIGLU_LIMIT)
        act = (lin + 1.0) * g * jax.nn.sigmoid(SWIGLU_ALPHA * g)
        y = jnp.dot(act.astype(BF16), wd_bf[...], preferred_element_type=F32) + bd_ref[0]
        o_ref[...] = y.reshape(o_ref.shape)

    @pl.when(i >= nused_ref[0])
    def _():
        o_ref[...] = jnp.zeros_like(o_ref)


def _experts(x_rows, block_exp, n_used, w_up, b_up, w_down, b_down, n_blocks):
    R = x_rows.shape[0]
    E, D, F2 = w_up.shape
    dff = w_down.shape[1]
    bidx = jnp.arange(n_blocks, dtype=jnp.int32)
    first = ((bidx == 0) | (block_exp != jnp.roll(block_exp, 1))) & (bidx < n_used[0])
    slot = (jnp.cumsum(first) - 1) % 2
    first_pos = jnp.where(first, bidx, n_blocks)
    next_pos = jnp.flip(lax.cummin(jnp.flip(jnp.roll(first_pos, -1).at[-1].set(n_blocks))))
    next_exp = jnp.where(next_pos < n_blocks, block_exp[jnp.minimum(next_pos, n_blocks - 1)], -1)
    blk = lambda i, *_: (i, 0, 0)
    bsel = lambda i, be, *_: (be[i], 0, 0)
    return pl.pallas_call(
        _experts_body,
        grid_spec=pltpu.PrefetchScalarGridSpec(
            num_scalar_prefetch=5,
            grid=(n_blocks,),
            in_specs=[pl.BlockSpec((EXPERT_ROWS, D // LANES, LANES), blk),
                      pl.BlockSpec(memory_space=pl.ANY),
                      pl.BlockSpec((1, 1, F2), bsel),
                      pl.BlockSpec(memory_space=pl.ANY),
                      pl.BlockSpec((1, 1, D), bsel)],
            out_specs=pl.BlockSpec((EXPERT_ROWS, D // LANES, LANES), blk),
            scratch_shapes=[pltpu.VMEM((2, D, F2), F32), pltpu.VMEM((2, dff, D), F32),
                            pltpu.VMEM((D, F2), BF16), pltpu.VMEM((dff, D), BF16),
                            pltpu.SemaphoreType.DMA((2, 2))],
        ),
        out_shape=jax.ShapeDtypeStruct((R, D // LANES, LANES), F32),
        compiler_params=_params("arbitrary"),
        name="experts",
    )(block_exp, n_used, first.astype(jnp.int32), slot.astype(jnp.int32), next_exp.astype(jnp.int32),
      x_rows, w_up, b_up.reshape(E, 1, F2), w_down, b_down.reshape(E, 1, D))


COMBINE_ROWS = 128


def _combine_body(dest_ref, y_hbm, x1_ref, gate_ref, mod_ref, o_ref, buf, sems):
    i = pl.program_id(0)
    t = COMBINE_ROWS
    cur = i % 2

    def request(tile, slot):
        def issue(g, carry):
            for u in range(ROW_DMA_UNROLL):
                r = g * ROW_DMA_UNROLL + u
                for k in range(TOP_K):
                    d = dest_ref[(tile * t + r) * TOP_K + k]
                    pltpu.make_async_copy(y_hbm.at[d], buf.at[slot, k, r], sems.at[slot]).start()
            return carry
        lax.fori_loop(0, t // ROW_DMA_UNROLL, issue, 0)

    @pl.when(i == 0)
    def _():
        request(0, 0)

    @pl.when(i + 1 < pl.num_programs(0))
    def _():
        request(i + 1, 1 - cur)

    for k in range(TOP_K):
        pltpu.make_async_copy(y_hbm.at[pl.ds(0, t)], buf.at[cur, k], sems.at[cur]).wait()
    gates = gate_ref[:, TOP_K:2 * TOP_K]
    y = gates[:, 0:1] * buf[cur, 0].reshape(t, -1)
    for k in range(1, TOP_K):
        y = y + gates[:, k:k + 1] * buf[cur, k].reshape(t, -1)
    o_ref[...] = x1_ref[...] + mod_ref[0, 5:6, :] * y


def _combine(y_rows, dest, x1, gates, mod3, rows_per_batch):
    n, D = x1.shape
    t = COMBINE_ROWS
    tpb = rows_per_batch // t
    return pl.pallas_call(
        _combine_body,
        grid_spec=pltpu.PrefetchScalarGridSpec(
            num_scalar_prefetch=1,
            grid=(n // t,),
            in_specs=[pl.BlockSpec(memory_space=pl.ANY),
                      pl.BlockSpec((t, D), lambda i, d: (i, 0)),
                      pl.BlockSpec((t, ROUTE_COLS), lambda i, d: (i, 0)),
                      pl.BlockSpec((1, 6, D), lambda i, d: (i // tpb, 0, 0))],
            out_specs=pl.BlockSpec((t, D), lambda i, d: (i, 0)),
            scratch_shapes=[pltpu.VMEM((2, TOP_K, t, D // LANES, LANES), F32), pltpu.SemaphoreType.DMA((2,))],
        ),
        out_shape=jax.ShapeDtypeStruct((n, D), F32),
        compiler_params=_params("arbitrary"),
        name="combine",
    )(dest, y_rows, x1, gates, mod3)


def kernel(x, c, positions, ada_w, ada_b, norm1_g, norm2_g, w_in, conv_w, conv_b, w_rg_a, b_rg_a, w_rg_x, b_rg_x, lru_lambda, q_norm_g, k_norm_g, b_gate, w_branch, w_out, w_router, b_router, w_up, b_up, w_down, b_down):
    B, S, D = x.shape
    n = B * S
    depth = ada_w.shape[0]
    x2 = x.reshape(n, D)
    cos_t, sin_t = _rope_tables(positions)
    perm = _rope_head_perm()
    head_cols = (jnp.arange(2 * N_HEADS)[:, None] * HEAD_DIM + perm[None, :]).reshape(-1)
    for l in range(depth):
        mod3 = _adaln(c, ada_w[l], ada_b[l]).reshape(B, 6, D)
        h1 = _norm_mod_call(x2, mod3, norm1_g[l], S)
        w = w_in[l].astype(BF16)
        xr = _proj("plain", h1, w[:, 0:D], [], F32, "proj_xr")
        gg = _proj("gelu", h1, w[:, D:2 * D], [], F32, "proj_gr")
        qk_g = jnp.concatenate([jnp.tile(q_norm_g[l][perm], N_HEADS),
                                jnp.tile(k_norm_g[l][perm], N_HEADS)]).reshape(1, 2 * D)
        qk = _proj("qk", h1, w[:, 2 * D:4 * D][:, head_cols], [qk_g, cos_t, sin_t], F32, "proj_qk")
        v = _proj("plain", h1, w[:, 4 * D:5 * D], [], BF16, "proj_v")
        gl = _proj("gate", h1, w[:, 5 * D:7 * D], [b_gate[l].reshape(1, 2 * D)], F32, "proj_gl")

        y_rnn = _rglru(xr, gg, conv_w[l], conv_b[l], w_rg_a[l], b_rg_a[l], w_rg_x[l], b_rg_x[l],
                       lru_lambda[l], B, S)
        y_att = _moba(qk, v, B, S)

        x1, h2, route, cnt = _merge(y_rnn, y_att, gl, x2, mod3, w_branch[l], w_out[l],
                                    norm2_g[l], w_router[l], b_router[l], S)

        counts = cnt[:, 0]
        padded = (counts + EXPERT_ROWS - 1) // EXPERT_ROWS * EXPERT_ROWS
        pad_end = jnp.cumsum(padded)
        pad_start = pad_end - padded
        top_idx = route[:, 0:TOP_K].astype(jnp.int32)
        rank = route[:, 2 * TOP_K:3 * TOP_K].astype(jnp.int32)
        dest = (pad_start[top_idx] + rank).reshape(n * TOP_K).astype(jnp.int32)
        n_blocks = (n * TOP_K) // EXPERT_ROWS + N_EXPERTS
        block_start = jnp.arange(n_blocks, dtype=jnp.int32) * EXPERT_ROWS
        block_exp = jnp.minimum(jnp.sum(block_start[:, None] >= pad_end[None, :], axis=1),
                                N_EXPERTS - 1).astype(jnp.int32)
        n_used = (pad_end[-1] // EXPERT_ROWS).astype(jnp.int32)
        last_blk = jnp.where(padded > 0, pad_end // EXPERT_ROWS - 1, -1)
        spare = n_used + jnp.arange(N_EXPERTS, dtype=jnp.int32)
        spare = jnp.where(spare < n_blocks, spare, -1)
        zero_blocks = jnp.concatenate([last_blk, spare]).astype(jnp.int32)

        x_rows = _dispatch(h2, dest, zero_blocks, n_blocks * EXPERT_ROWS)
        y_rows = _experts(x_rows, block_exp, n_used.reshape(1), w_up[l], b_up[l], w_down[l], b_down[l], n_blocks)
        x2 = _combine(y_rows, dest, x1, route, mod3, S)
    return x2.reshape(B, S, D)
```

```python
import functools

import jax
import jax.numpy as jnp
from jax import lax
from jax.experimental import pallas as pl
from jax.experimental.pallas import tpu as pltpu

F32 = jnp.float32
BF16 = jnp.bfloat16
HIGHEST = lax.Precision.HIGHEST

EPS = 1e-6
LANES = 128
N_HEADS = 8
HEAD_DIM = 128
RNN_BLOCKS = 8
RNN_BW = 128
CONV_W = 4
LRU_C = 8.0
MOBA_BLOCK = 256
MOBA_TOPK = 3
MOBA_CHUNK = 4
MOBA_HEADS = 4
MERGE_SUB = 256
LOG2_E = 1.4426950408889634
ROPE_DIMS = HEAD_DIM // 4
ROPE_THETA = 500000.0
N_EXPERTS = 32
TOP_K = 4
SWIGLU_LIMIT = 7.0
SWIGLU_ALPHA = 1.702
EXPERT_ROWS = 256
NEG_BIG = -1e30

VMEM_LIMIT = 56 * 1024 * 1024


def _params(*sem):
    return pltpu.CompilerParams(dimension_semantics=sem, vmem_limit_bytes=VMEM_LIMIT)


def _adaln_body(c_ref, w_ref, b_ref, o_ref):
    cs = c_ref[...]
    cs = cs * jax.nn.sigmoid(cs)
    o_ref[...] = jnp.dot(cs, w_ref[...], preferred_element_type=F32, precision=HIGHEST) + b_ref[...]


def _adaln(c, ada_w, ada_b):
    B, D = c.shape
    W = ada_w.shape[1]
    cpad = jnp.zeros((8, D), F32).at[:B].set(c)
    tn = 1024
    mod = pl.pallas_call(
        _adaln_body,
        grid=(W // tn,),
        in_specs=[pl.BlockSpec((8, D), lambda j: (0, 0)),
                  pl.BlockSpec((D, tn), lambda j: (0, j)),
                  pl.BlockSpec((1, tn), lambda j: (0, j))],
        out_specs=pl.BlockSpec((8, tn), lambda j: (0, j)),
        out_shape=jax.ShapeDtypeStruct((8, W), F32),
        compiler_params=_params("parallel"),
        name="adaln",
    )(cpad, ada_w, ada_b.reshape(1, W))
    return mod[:B]


def _rope_head_perm():
    half = ROPE_DIMS // 2
    mid = HEAD_DIM // 2
    return jnp.concatenate([jnp.arange(0, half), jnp.arange(ROPE_DIMS, mid + half),
                            jnp.arange(half, ROPE_DIMS), jnp.arange(mid + half, HEAD_DIM)])


def _rope_body(pos_ref, freq_ref, c_ref, s_ref):
    ang = pos_ref[...].astype(F32) * freq_ref[...]
    lane = lax.broadcasted_iota(jnp.int32, ang.shape, 1)
    half = ROPE_DIMS // 2
    mid = HEAD_DIM // 2
    s = jnp.sin(ang)
    c_ref[...] = jnp.cos(ang)
    s_ref[...] = jnp.where(lane < half, -s, jnp.where((lane >= mid) & (lane < mid + half), s, 0.0))


def _rope_tables(positions):
    n = positions.size
    half = ROPE_DIMS // 2
    mid = HEAD_DIM // 2
    freqs = ROPE_THETA ** (-jnp.arange(half, dtype=F32) / half)
    freq_lane = jnp.zeros((1, LANES), F32).at[0, :half].set(freqs).at[0, mid:mid + half].set(freqs)
    tm = 1024
    tab = jax.ShapeDtypeStruct((n, LANES), F32)
    return pl.pallas_call(
        _rope_body,
        grid=(n // tm,),
        in_specs=[pl.BlockSpec((tm, 1), lambda i: (i, 0)),
                  pl.BlockSpec((1, LANES), lambda i: (0, 0))],
        out_specs=[pl.BlockSpec((tm, LANES), lambda i: (i, 0))] * 2,
        out_shape=[tab, tab],
        compiler_params=_params("parallel"),
        name="rope_tab",
    )(positions.reshape(n, 1), freq_lane)


def _norm_mod(x, g, sh, sc):
    ms = jnp.mean(x * x, axis=-1, keepdims=True)
    y = x * lax.rsqrt(ms + EPS)
    return (y * g) * (1.0 + sc) + sh


def _norm_mod_body(x_ref, mod_ref, g_ref, o_ref):
    o_ref[...] = _norm_mod(x_ref[...], g_ref[...], mod_ref[0, 0:1, :], mod_ref[0, 1:2, :]).astype(o_ref.dtype)


def _norm_mod_call(x2, mod3, g, rows_per_batch):
    n, D = x2.shape
    tm = 512
    tpb = rows_per_batch // tm
    return pl.pallas_call(
        _norm_mod_body,
        grid=(n // tm,),
        in_specs=[pl.BlockSpec((tm, D), lambda i: (i, 0)),
                  pl.BlockSpec((1, 6, D), lambda i: (i // tpb, 0, 0)),
                  pl.BlockSpec((1, D), lambda i: (0, 0))],
        out_specs=pl.BlockSpec((tm, D), lambda i: (i, 0)),
        out_shape=jax.ShapeDtypeStruct((n, D), BF16),
        compiler_params=_params("parallel"),
        name="norm_mod",
    )(x2, mod3, g.reshape(1, D))


def _gelu_tanh(x):
    return 0.5 * x * (1.0 + jnp.tanh(0.7978845608028654 * (x + 0.044715 * (x * x * x))))


PROJ_SUB = 256
PROJ_PIECE = 256


def _qk_head(seg, g, cos, sin):
    ms = jnp.mean(seg * seg, axis=-1, keepdims=True)
    y = seg * lax.rsqrt(ms + EPS) * g
    return y * cos + pltpu.roll(y, HEAD_DIM // 2, axis=1) * sin


def _proj_stages(kind, rows, h_ref, w_ref, extras, o_ref):
    n_pieces = w_ref.shape[1] // PROJ_PIECE
    pieces = []
    for j in range(n_pieces):
        pieces.append(jnp.dot(h_ref[rows, :], w_ref[:, j * PROJ_PIECE:(j + 1) * PROJ_PIECE],
                              preferred_element_type=F32))
        yield
    for j in range(n_pieces):
        cols = slice(j * PROJ_PIECE, (j + 1) * PROJ_PIECE)
        acc = pieces[j]
        if kind == "gelu":
            acc = _gelu_tanh(acc)
        elif kind == "gate":
            (b_ref,) = extras
            acc = jax.nn.sigmoid(acc + b_ref[:, cols])
        elif kind == "qk":
            g_ref, c_ref, s_ref = extras
            cos, sin = c_ref[rows, :], s_ref[rows, :]
            acc = jnp.concatenate(
                [_qk_head(acc[:, o:o + HEAD_DIM], g_ref[:, cols.start + o:cols.start + o + HEAD_DIM], cos, sin)
                 for o in range(0, PROJ_PIECE, HEAD_DIM)], axis=1)
        o_ref[rows, cols] = acc.astype(o_ref.dtype)
        yield


def _proj_body(kind, h_ref, w_ref, *rest):
    *extras, o_ref = rest
    first, second = [_proj_stages(kind, slice(t * PROJ_SUB, (t + 1) * PROJ_SUB), h_ref, w_ref, extras, o_ref)
                     for t in range(2)]
    n_pieces = w_ref.shape[1] // PROJ_PIECE
    for _ in range(n_pieces):
        next(first)
    for _ in range(n_pieces):
        next(second)
        next(first)
    for stage in second:
        pass


def _proj(kind, h, w, extras, out_dtype, name):
    n, D = h.shape
    W = w.shape[1]
    tm, tn = 512, 1024
    in_specs = [pl.BlockSpec((tm, D), lambda j, i: (i, 0)),
                pl.BlockSpec((D, tn), lambda j, i: (0, j))]
    args = [h, w]
    for arr in extras:
        if arr.shape[0] == 1:
            in_specs.append(pl.BlockSpec((1, tn), lambda j, i: (0, j)))
        else:
            in_specs.append(pl.BlockSpec((tm, LANES), lambda j, i: (i, 0)))
        args.append(arr)
    return pl.pallas_call(
        functools.partial(_proj_body, kind),
        grid=(W // tn, n // tm),
        in_specs=in_specs,
        out_specs=pl.BlockSpec((tm, tn), lambda j, i: (i, j)),
        out_shape=jax.ShapeDtypeStruct((n, W), out_dtype),
        compiler_params=_params("parallel", "parallel"),
        name=name,
    )(*args)


def _rglru_body(xr_ref, gg_ref, cw_ref, cb_ref, wa_ref, ba_ref, wx_ref, bx_ref, lam_ref,
                o_ref, xbuf, hcar, a_s, u_s):
    s = pl.program_id(1)
    ts, D = xr_ref.shape

    @pl.when(s == 0)
    def _():
        xbuf[0:8, :] = jnp.zeros((8, D), F32)
        hcar[...] = jnp.zeros_like(hcar)

    @pl.when(s > 0)
    def _():
        xbuf[0:8, :] = xbuf[ts:ts + 8, :]

    xbuf[8:8 + ts, :] = xr_ref[...]
    xc = cb_ref[...] + cw_ref[0:1, :] * xbuf[8:8 + ts, :]
    for i in range(1, CONV_W):
        xc = xc + cw_ref[i:i + 1, :] * xbuf[8 - i:8 - i + ts, :]

    ra, rx = [], []
    for n in range(RNN_BLOCKS):
        xb = xc[:, n * RNN_BW:(n + 1) * RNN_BW].astype(BF16)
        ra.append(jnp.dot(xb, wa_ref[n], preferred_element_type=F32))
        rx.append(jnp.dot(xb, wx_ref[n], preferred_element_type=F32))
    r = jax.nn.sigmoid(jnp.concatenate(ra, axis=1) + ba_ref[...])
    ig = jax.nn.sigmoid(jnp.concatenate(rx, axis=1) + bx_ref[...])

    z = -lam_ref[...]
    softplus = jnp.maximum(z, 0.0) + jnp.log1p(jnp.exp(-jnp.abs(z)))
    log_a = -LRU_C * r * softplus
    a = jnp.exp(log_a)
    mult = jnp.sqrt(1.0 - a * a)
    row = lax.broadcasted_iota(jnp.int32, (ts, D), 0)
    mult = jnp.where((row == 0) & (s == 0), 1.0, mult)
    u = mult * (ig * xc)

    rm = row & 7
    for d in (1, 2, 4):
        keep = rm >= d
        a_sh = pltpu.roll(a, d, axis=0)
        u_sh = pltpu.roll(u, d, axis=0)
        u = jnp.where(keep, a * u_sh + u, u)
        a = jnp.where(keep, a * a_sh, a)
    a_s[...] = a
    u_s[...] = u

    def group(g, h):
        r0 = pl.multiple_of(g * 8, 8)
        hg = u_s[pl.ds(r0, 8), :] + a_s[pl.ds(r0, 8), :] * h
        u_s[pl.ds(r0, 8), :] = hg
        return hg[7:8, :]

    hcar[...] = lax.fori_loop(0, ts // 8, group, hcar[...])
    o_ref[...] = (u_s[...] * gg_ref[...]).astype(o_ref.dtype)


def _rglru(xr, gg, conv_w, conv_b, w_a, b_a, w_x, b_x, lam, B, S):
    n, D = xr.shape
    ts = 256
    spb = S // ts
    row = lambda b, s: (b * spb + s, 0)
    vec = lambda b, s: (0, 0)
    return pl.pallas_call(
        _rglru_body,
        grid=(B, spb),
        in_specs=[pl.BlockSpec((ts, D), row),
                  pl.BlockSpec((ts, D), row),
                  pl.BlockSpec((CONV_W, D), vec),
                  pl.BlockSpec((1, D), vec),
                  pl.BlockSpec((RNN_BLOCKS, RNN_BW, RNN_BW), lambda b, s: (0, 0, 0)),
                  pl.BlockSpec((1, D), vec),
                  pl.BlockSpec((RNN_BLOCKS, RNN_BW, RNN_BW), lambda b, s: (0, 0, 0)),
                  pl.BlockSpec((1, D), vec),
                  pl.BlockSpec((1, D), vec)],
        out_specs=pl.BlockSpec((ts, D), row),
        out_shape=jax.ShapeDtypeStruct((n, D), BF16),
        scratch_shapes=[pltpu.VMEM((ts + 8, D), F32), pltpu.VMEM((1, D), F32),
                        pltpu.VMEM((ts, D), F32), pltpu.VMEM((ts, D), F32)],
        compiler_params=_params("arbitrary", "arbitrary"),
        name="rglru",
    )(xr, gg, conv_w, conv_b.reshape(1, D), w_a.astype(BF16), b_a.reshape(1, D),
      w_x.astype(BF16), b_x.reshape(1, D), lam.reshape(1, D))


def _moba_body(q_ref, k_ref, v_ref, o_ref, kaug, vaug, kmean, m_s, acc_s):
    qb = pl.program_id(2)
    S = k_ref.shape[0]
    nb = S // MOBA_BLOCK
    bs = MOBA_BLOCK
    dh = HEAD_DIM
    heads = range(MOBA_HEADS)
    hcol = lambda hh: slice(hh * dh, (hh + 1) * dh)

    @pl.when(qb == 0)
    def _():
        blk = lax.broadcasted_iota(jnp.int32, (S, dh), 0) // bs
        col = lax.broadcasted_iota(jnp.int32, (S, dh), 1)
        onehot = jnp.where(col == blk, 1.0, 0.0).astype(BF16)
        for hh in heads:
            kaug[hh, :, 0:dh] = k_ref[:, hcol(hh)].astype(BF16)
            kaug[hh, :, dh:2 * dh] = onehot
            vaug[hh, :, 0:dh] = v_ref[:, hcol(hh)]
            vaug[hh, :, dh:2 * dh] = jnp.ones((S, dh), BF16)
            kmean[hh] = jnp.zeros((LANES, dh), F32)
            for j in range(nb):
                kmean[hh, j:j + 1, :] = jnp.mean(k_ref[j * bs:(j + 1) * bs, hcol(hh)], axis=0, keepdims=True)

    scale = HEAD_DIM ** -0.5
    nt = (((1,), (1,)), ((), ()))
    c2 = scale * LOG2_E
    r0 = pl.multiple_of(qb * bs, bs)
    nb_pad = -(-nb // 8) * 8
    rr = lax.broadcasted_iota(jnp.int32, (bs, bs), 0)
    cc = lax.broadcasted_iota(jnp.int32, (bs, bs), 1)

    qs = [q_ref[:, hcol(hh)] for hh in heads]
    qbfs = [q.astype(BF16) for q in qs]

    blk_i = lax.broadcasted_iota(jnp.int32, (nb_pad, bs), 0)
    blk_f = blk_i.astype(F32)
    past = blk_i < qb
    gs = [jnp.where(past, lax.dot_general(kmean[hh, 0:nb_pad, :], qs[hh], nt, preferred_element_type=F32,
                                          precision=HIGHEST), -jnp.inf) for hh in heads]
    ss = [jnp.where(cc <= rr, lax.dot_general(qbfs[hh], kaug[hh, pl.ds(r0, bs), 0:dh], nt,
                                              preferred_element_type=F32), NEG_BIG) for hh in heads]
    biases = [jnp.full((nb_pad, bs), NEG_BIG, F32) for _ in heads]
    for _ in range(MOBA_TOPK):
        for hh in heads:
            m = jnp.max(gs[hh], axis=0, keepdims=True)
            idx = jnp.min(jnp.where(gs[hh] == m, blk_f, float(LANES)), axis=0, keepdims=True)
            pick = (blk_f == idx) & past
            biases[hh] = jnp.where(pick, 0.0, biases[hh])
            gs[hh] = jnp.where(pick, -jnp.inf, gs[hh])
    fill = jnp.full((bs, dh - nb_pad), NEG_BIG, BF16)
    qaugs = [jnp.concatenate([qbfs[hh], biases[hh].T.astype(BF16), fill], axis=1) for hh in heads]

    ps = []
    for hh in heads:
        m0 = jnp.max(ss[hh], axis=1, keepdims=True)
        ps.append(jnp.exp2((ss[hh] - m0) * c2).astype(BF16))
        m_s[hh] = m0
    for hh in heads:
        acc_s[hh] = jnp.dot(ps[hh], vaug[hh, pl.ds(r0, bs), :], preferred_element_type=F32)

    cw = MOBA_CHUNK * bs
    for c in range(nb // MOBA_CHUNK):
        @pl.when(c * MOBA_CHUNK < qb)
        def _(c=c):
            scs = [lax.dot_general(qaugs[hh], kaug[hh, c * cw:(c + 1) * cw, :], nt, preferred_element_type=F32)
                   for hh in heads]
            pcs, alphas = [], []
            for hh in heads:
                m_old = m_s[hh]
                m_new = jnp.maximum(m_old, jnp.max(scs[hh], axis=1, keepdims=True))
                alphas.append(jnp.exp2((m_old - m_new) * c2))
                pcs.append(jnp.exp2((scs[hh] - m_new) * c2).astype(BF16))
                m_s[hh] = m_new
            for hh in heads:
                acc_s[hh] = alphas[hh] * acc_s[hh] + jnp.dot(pcs[hh], vaug[hh, c * cw:(c + 1) * cw, :],
                                                             preferred_element_type=F32)

    for hh in heads:
        o_ref[:, hcol(hh)] = (acc_s[hh, :, 0:dh] / acc_s[hh, :, dh:dh + 1]).astype(o_ref.dtype)


def _moba(qk, v, B, S):
    n, D = v.shape
    nq = S // MOBA_BLOCK
    hs = MOBA_HEADS
    w = hs * HEAD_DIM
    return pl.pallas_call(
        _moba_body,
        grid=(B, N_HEADS // hs, nq),
        in_specs=[pl.BlockSpec((MOBA_BLOCK, w), lambda b, h, i: (b * nq + i, h)),
                  pl.BlockSpec((S, w), lambda b, h, i: (b, N_HEADS // hs + h)),
                  pl.BlockSpec((S, w), lambda b, h, i: (b, h))],
        out_specs=pl.BlockSpec((MOBA_BLOCK, w), lambda b, h, i: (b * nq + i, h)),
        out_shape=jax.ShapeDtypeStruct((n, D), BF16),
        scratch_shapes=[pltpu.VMEM((hs, S, 2 * HEAD_DIM), BF16), pltpu.VMEM((hs, S, 2 * HEAD_DIM), BF16),
                        pltpu.VMEM((hs, LANES, HEAD_DIM), F32), pltpu.VMEM((hs, MOBA_BLOCK, 1), F32),
                        pltpu.VMEM((hs, MOBA_BLOCK, 2 * HEAD_DIM), F32)],
        compiler_params=_params("parallel", "parallel", "arbitrary"),
        name="moba",
    )(qk, qk, v)


ROUTE_COLS = 16


def _merge_body(yr_ref, ya_ref, gr_ref, ga_ref, x_ref, mod_ref, wb_ref, wo_ref, g2_ref, wr_ref, br_ref,
                x1_ref, h2_ref, route_ref, cnt_ref, carry):
    i = pl.program_id(0)

    @pl.when(i == 0)
    def _():
        carry[...] = jnp.zeros_like(carry)

    tm = MERGE_SUB
    counts = {"total": carry[...]}
    first, second = [
        _merge_stages(slice(h * tm, (h + 1) * tm), counts, yr_ref, ya_ref, gr_ref, ga_ref, x_ref, mod_ref,
                      wb_ref, wo_ref, g2_ref, wr_ref, br_ref, x1_ref, h2_ref, route_ref)
        for h in range(2)]
    for _ in range(MERGE_MATMUL_STAGES):
        next(first)
    for stage in first:
        next(second)
    for stage in second:
        pass
    carry[...] = counts["total"]
    cnt_ref[...] = jnp.broadcast_to(counts["total"], cnt_ref.shape).astype(jnp.int32)


MERGE_MATMUL_STAGES = 4


def _merge_stages(rows, counts, yr_ref, ya_ref, gr_ref, ga_ref, x_ref, mod_ref, wb_ref, wo_ref, g2_ref, wr_ref,
                  br_ref, x1_ref, h2_ref, route_ref):
    tm = rows.stop - rows.start
    zr = jnp.dot(yr_ref[rows, :], wb_ref[0], preferred_element_type=F32)
    yield
    za = jnp.dot(ya_ref[rows, :], wb_ref[1], preferred_element_type=F32)
    mix = (gr_ref[rows, :] * zr + ga_ref[rows, :] * za).astype(BF16)
    yield
    mixed = jnp.dot(mix, wo_ref[...], preferred_element_type=F32)
    x1 = x_ref[rows, :] + mod_ref[0, 2:3, :] * mixed
    x1_ref[rows, :] = x1
    h2 = _norm_mod(x1, g2_ref[...], mod_ref[0, 3:4, :], mod_ref[0, 4:5, :])
    h2_ref[rows, :] = h2
    yield

    nt = (((1,), (1,)), ((), ()))
    ne = N_EXPERTS
    hi = h2.astype(BF16)
    lo = (h2 - hi.astype(F32)).astype(BF16)
    both = lax.dot_general(wr_ref[...], hi, nt, preferred_element_type=F32)
    lg = (both[:ne] + both[ne:]
          + lax.dot_general(wr_ref[0:ne, :], lo, nt, preferred_element_type=F32) + br_ref[...])
    yield

    eidx = lax.broadcasted_iota(jnp.int32, lg.shape, 0).astype(F32)
    vals, idxs = [], []
    onehot = jnp.zeros(lg.shape, F32)
    for _ in range(TOP_K):
        m = jnp.max(lg, axis=0, keepdims=True)
        idx = jnp.min(jnp.where(lg == m, eidx, float(ne)), axis=0, keepdims=True)
        pick = eidx == idx
        onehot = jnp.where(pick, 1.0, onehot)
        lg = jnp.where(pick, -jnp.inf, lg)
        vals.append(m)
        idxs.append(idx)
        yield
    es = [jnp.exp(v - vals[0]) for v in vals]
    den = es[0] + es[1] + es[2] + es[3]
    before0 = counts["total"]

    rr = lax.broadcasted_iota(jnp.int32, (tm, tm), 0)
    cc = lax.broadcasted_iota(jnp.int32, (tm, tm), 1)
    earlier = jnp.where(rr < cc, 1.0, 0.0).astype(BF16)
    before = jnp.dot(onehot.astype(BF16), earlier, preferred_element_type=F32) + before0
    ranks = [jnp.sum(jnp.where(eidx == idxs[k], before, 0.0), axis=0, keepdims=True) for k in range(TOP_K)]

    rows_t = jnp.concatenate(idxs + [e / den for e in es] + ranks
                             + [jnp.zeros((ROUTE_COLS - 3 * TOP_K, tm), F32)], axis=0)
    route_ref[rows, :] = rows_t.T
    counts["total"] = before0 + jnp.sum(onehot, axis=1, keepdims=True)
    yield


def _merge(y_rnn, y_att, gl, x2, mod3, w_branch, w_out, norm2_g, w_router, b_router, rows_per_batch):
    n, D = x2.shape
    tm = 2 * MERGE_SUB
    tpb = rows_per_batch // tm
    wr_t = w_router.T
    wr_hi = wr_t.astype(BF16)
    wr = jnp.concatenate([wr_hi, (wr_t - wr_hi.astype(F32)).astype(BF16)], axis=0)
    br = b_router.reshape(N_EXPERTS, 1)
    row = lambda i: (i, 0)
    fixed = lambda i: (0, 0)
    once = pl.Buffered(1)
    wide = jax.ShapeDtypeStruct((n, D), F32)
    return pl.pallas_call(
        _merge_body,
        grid=(n // tm,),
        in_specs=[pl.BlockSpec((tm, D), row),
                  pl.BlockSpec((tm, D), row),
                  pl.BlockSpec((tm, D), lambda i: (i, 0)),
                  pl.BlockSpec((tm, D), lambda i: (i, 1)),
                  pl.BlockSpec((tm, D), row),
                  pl.BlockSpec((1, 6, D), lambda i: (i // tpb, 0, 0)),
                  pl.BlockSpec((2, D, D), lambda i: (0, 0, 0), pipeline_mode=once),
                  pl.BlockSpec((D, D), fixed, pipeline_mode=once),
                  pl.BlockSpec((1, D), fixed),
                  pl.BlockSpec((2 * N_EXPERTS, D), fixed, pipeline_mode=once),
                  pl.BlockSpec((N_EXPERTS, 1), fixed)],
        out_specs=[pl.BlockSpec((tm, D), row),
                   pl.BlockSpec((tm, D), row),
                   pl.BlockSpec((tm, ROUTE_COLS), row),
                   pl.BlockSpec((N_EXPERTS, LANES), fixed)],
        out_shape=[wide, wide, jax.ShapeDtypeStruct((n, ROUTE_COLS), F32),
                   jax.ShapeDtypeStruct((N_EXPERTS, LANES), jnp.int32)],
        scratch_shapes=[pltpu.VMEM((N_EXPERTS, 1), F32)],
        compiler_params=_params("arbitrary"),
        name="merge",
    )(y_rnn, y_att, gl, gl, x2, mod3, w_branch.astype(BF16), w_out.astype(BF16),
      norm2_g.reshape(1, D), wr, br)


DISPATCH_TOKENS = 256
ROW_DMA_UNROLL = 8


def _dispatch_body(dest_ref, zblk_ref, src_ref, dst_hbm, zbuf, stage, sems, zsem):
    t = DISPATCH_TOKENS
    c = pl.program_id(0)

    @pl.when(c == 0)
    def _():
        zbuf[...] = jnp.zeros_like(zbuf)

        def zero_copy(b):
            r0 = pl.multiple_of(b * EXPERT_ROWS, EXPERT_ROWS)
            return pltpu.make_async_copy(zbuf, dst_hbm.at[pl.ds(r0, EXPERT_ROWS)], zsem)

        def zissue(j, carry):
            @pl.when(zblk_ref[j] >= 0)
            def _():
                zero_copy(zblk_ref[j]).start()
            return carry

        def zwait(j, carry):
            @pl.when(zblk_ref[j] >= 0)
            def _():
                zero_copy(zblk_ref[j]).wait()
            return carry

        lax.fori_loop(0, zblk_ref.shape[0], zissue, 0)
        lax.fori_loop(0, zblk_ref.shape[0], zwait, 0)

    slot = c % 2

    def drain(s):
        for k in range(TOP_K):
            pltpu.make_async_copy(stage.at[s], dst_hbm.at[pl.ds(0, t)], sems.at[s]).wait()

    @pl.when(c >= 2)
    def _():
        drain(slot)

    stage[slot] = src_ref[...].reshape(stage.shape[1:])

    def issue(g, carry):
        for u in range(ROW_DMA_UNROLL):
            r = g * ROW_DMA_UNROLL + u
            for k in range(TOP_K):
                d = dest_ref[(c * t + r) * TOP_K + k]
                pltpu.make_async_copy(stage.at[slot, r], dst_hbm.at[d], sems.at[slot]).start()
        return carry

    lax.fori_loop(0, t // ROW_DMA_UNROLL, issue, 0)

    @pl.when(c == pl.num_programs(0) - 1)
    def _():
        drain(1 - slot)
        drain(slot)


def _dispatch(h2, dest, zero_blocks, n_rows):
    n, D = h2.shape
    return pl.pallas_call(
        _dispatch_body,
        grid_spec=pltpu.PrefetchScalarGridSpec(
            num_scalar_prefetch=2,
            grid=(n // DISPATCH_TOKENS,),
            in_specs=[pl.BlockSpec((DISPATCH_TOKENS, D), lambda i, d, z: (i, 0))],
            out_specs=pl.BlockSpec(memory_space=pl.ANY),
            scratch_shapes=[pltpu.VMEM((EXPERT_ROWS, D // LANES, LANES), F32),
                            pltpu.VMEM((2, DISPATCH_TOKENS, D // LANES, LANES), F32),
                            pltpu.SemaphoreType.DMA((2,)), pltpu.SemaphoreType.DMA],
        ),
        out_shape=jax.ShapeDtypeStruct((n_rows, D // LANES, LANES), F32),
        compiler_params=_params("arbitrary"),
        name="dispatch",
    )(dest, zero_blocks, h2)


def _experts_body(bexp_ref, nused_ref, first_ref, slot_ref, next_ref, x_ref, wu_hbm, bu_ref, wd_hbm, bd_ref,
                  o_ref, wu_buf, wd_buf, wu_bf, wd_bf, sems):
    i = pl.program_id(0)
    dff = wd_hbm.shape[1]
    used = i < nused_ref[0]

    def weight_copies(e, s):
        return (pltpu.make_async_copy(wu_hbm.at[e], wu_buf.at[s], sems.at[0, s]),
                pltpu.make_async_copy(wd_hbm.at[e], wd_buf.at[s], sems.at[1, s]))

    @pl.when(i == 0)
    def _():
        for cp in weight_copies(bexp_ref[0], 0):
            cp.start()

    @pl.when(used & (first_ref[i] == 1))
    def _():
        s = slot_ref[i]
        for cp in weight_copies(bexp_ref[i], s):
            cp.wait()
        wu_bf[...] = wu_buf[s].astype(BF16)
        wd_bf[...] = wd_buf[s].astype(BF16)

        @pl.when(next_ref[i] >= 0)
        def _():
            for cp in weight_copies(next_ref[i], 1 - s):
                cp.start()

    @pl.when(used)
    def _():
        x = x_ref[...].reshape(x_ref.shape[0], -1)
        hc = jnp.dot(x.astype(BF16), wu_bf[...], preferred_element_type=F32) + bu_ref[0]
        g = jnp.minimum(hc[:, :dff], SWIGLU_LIMIT)
        lin = jnp.clip(hc[:, dff:], -SWIGLU_LIMIT, SWIGLU_LIMIT)
        act = (lin + 1.0) * g * jax.nn.sigmoid(SWIGLU_ALPHA * g)
        y = jnp.dot(act.astype(BF16), wd_bf[...], preferred_element_type=F32) + bd_ref[0]
        o_ref[...] = y.reshape(o_ref.shape)

    @pl.when(i >= nused_ref[0])
    def _():
        o_ref[...] = jnp.zeros_like(o_ref)


def _experts(x_rows, block_exp, n_used, w_up, b_up, w_down, b_down, n_blocks):
    R = x_rows.shape[0]
    E, D, F2 = w_up.shape
    dff = w_down.shape[1]
    bidx = jnp.arange(n_blocks, dtype=jnp.int32)
    first = ((bidx == 0) | (block_exp != jnp.roll(block_exp, 1))) & (bidx < n_used[0])
    slot = (jnp.cumsum(first) - 1) % 2
    first_pos = jnp.where(first, bidx, n_blocks)
    next_pos = jnp.flip(lax.cummin(jnp.flip(jnp.roll(first_pos, -1).at[-1].set(n_blocks))))
    next_exp = jnp.where(next_pos < n_blocks, block_exp[jnp.minimum(next_pos, n_blocks - 1)], -1)
    blk = lambda i, *_: (i, 0, 0)
    bsel = lambda i, be, *_: (be[i], 0, 0)
    return pl.pallas_call(
        _experts_body,
        grid_spec=pltpu.PrefetchScalarGridSpec(
            num_scalar_prefetch=5,
            grid=(n_blocks,),
            in_specs=[pl.BlockSpec((EXPERT_ROWS, D // LANES, LANES), blk),
                      pl.BlockSpec(memory_space=pl.ANY),
                      pl.BlockSpec((1, 1, F2), bsel),
                      pl.BlockSpec(memory_space=pl.ANY),
                      pl.BlockSpec((1, 1, D), bsel)],
            out_specs=pl.BlockSpec((EXPERT_ROWS, D // LANES, LANES), blk),
            scratch_shapes=[pltpu.VMEM((2, D, F2), F32), pltpu.VMEM((2, dff, D), F32),
                            pltpu.VMEM((D, F2), BF16), pltpu.VMEM((dff, D), BF16),
                            pltpu.SemaphoreType.DMA((2, 2))],
        ),
        out_shape=jax.ShapeDtypeStruct((R, D // LANES, LANES), F32),
        compiler_params=_params("arbitrary"),
        name="experts",
    )(block_exp, n_used, first.astype(jnp.int32), slot.astype(jnp.int32), next_exp.astype(jnp.int32),
      x_rows, w_up, b_up.reshape(E, 1, F2), w_down, b_down.reshape(E, 1, D))


COMBINE_ROWS = 128


def _combine_body(dest_ref, y_hbm, x1_ref, gate_ref, mod_ref, o_ref, buf, sems):
    i = pl.program_id(0)
    t = COMBINE_ROWS
    cur = i % 2

    def request(tile, slot):
        def issue(g, carry):
            for u in range(ROW_DMA_UNROLL):
                r = g * ROW_DMA_UNROLL + u
                for k in range(TOP_K):
                    d = dest_ref[(tile * t + r) * TOP_K + k]
                    pltpu.make_async_copy(y_hbm.at[d], buf.at[slot, k, r], sems.at[slot]).start()
            return carry
        lax.fori_loop(0, t // ROW_DMA_UNROLL, issue, 0)

    @pl.when(i == 0)
    def _():
        request(0, 0)

    @pl.when(i + 1 < pl.num_programs(0))
    def _():
        request(i + 1, 1 - cur)

    for k in range(TOP_K):
        pltpu.make_async_copy(y_hbm.at[pl.ds(0, t)], buf.at[cur, k], sems.at[cur]).wait()
    gates = gate_ref[:, TOP_K:2 * TOP_K]
    y = gates[:, 0:1] * buf[cur, 0].reshape(t, -1)
    for k in range(1, TOP_K):
        y = y + gates[:, k:k + 1] * buf[cur, k].reshape(t, -1)
    o_ref[...] = x1_ref[...] + mod_ref[0, 5:6, :] * y


def _combine(y_rows, dest, x1, gates, mod3, rows_per_batch):
    n, D = x1.shape
    t = COMBINE_ROWS
    tpb = rows_per_batch // t
    return pl.pallas_call(
        _combine_body,
        grid_spec=pltpu.PrefetchScalarGridSpec(
            num_scalar_prefetch=1,
            grid=(n // t,),
            in_specs=[pl.BlockSpec(memory_space=pl.ANY),
                      pl.BlockSpec((t, D), lambda i, d: (i, 0)),
                      pl.BlockSpec((t, ROUTE_COLS), lambda i, d: (i, 0)),
                      pl.BlockSpec((1, 6, D), lambda i, d: (i // tpb, 0, 0))],
            out_specs=pl.BlockSpec((t, D), lambda i, d: (i, 0)),
            scratch_shapes=[pltpu.VMEM((2, TOP_K, t, D // LANES, LANES), F32), pltpu.SemaphoreType.DMA((2,))],
        ),
        out_shape=jax.ShapeDtypeStruct((n, D), F32),
        compiler_params=_params("arbitrary"),
        name="combine",
    )(dest, y_rows, x1, gates, mod3)


def kernel(x, c, positions, ada_w, ada_b, norm1_g, norm2_g, w_in, conv_w, conv_b, w_rg_a, b_rg_a, w_rg_x, b_rg_x, lru_lambda, q_norm_g, k_norm_g, b_gate, w_branch, w_out, w_router, b_router, w_up, b_up, w_down, b_down):
    B, S, D = x.shape
    n = B * S
    depth = ada_w.shape[0]
    x2 = x.reshape(n, D)
    cos_t, sin_t = _rope_tables(positions)
    perm = _rope_head_perm()
    head_cols = (jnp.arange(2 * N_HEADS)[:, None] * HEAD_DIM + perm[None, :]).reshape(-1)
    for l in range(depth):
        mod3 = _adaln(c, ada_w[l], ada_b[l]).reshape(B, 6, D)
        h1 = _norm_mod_call(x2, mod3, norm1_g[l], S)
        w = w_in[l].astype(BF16)
        xr = _proj("plain", h1, w[:, 0:D], [], F32, "proj_xr")
        gg = _proj("gelu", h1, w[:, D:2 * D], [], F32, "proj_gr")
        qk_g = jnp.concatenate([jnp.tile(q_norm_g[l][perm], N_HEADS),
                                jnp.tile(k_norm_g[l][perm], N_HEADS)]).reshape(1, 2 * D)
        qk = _proj("qk", h1, w[:, 2 * D:4 * D][:, head_cols], [qk_g, cos_t, sin_t], F32, "proj_qk")
        v = _proj("plain", h1, w[:, 4 * D:5 * D], [], BF16, "proj_v")
        gl = _proj("gate", h1, w[:, 5 * D:7 * D], [b_gate[l].reshape(1, 2 * D)], F32, "proj_gl")

        y_rnn = _rglru(xr, gg, conv_w[l], conv_b[l], w_rg_a[l], b_rg_a[l], w_rg_x[l], b_rg_x[l],
                       lru_lambda[l], B, S)
        y_att = _moba(qk, v, B, S)

        x1, h2, route, cnt = _merge(y_rnn, y_att, gl, x2, mod3, w_branch[l], w_out[l],
                                    norm2_g[l], w_router[l], b_router[l], S)

        counts = cnt[:, 0]
        padded = (counts + EXPERT_ROWS - 1) // EXPERT_ROWS * EXPERT_ROWS
        pad_end = jnp.cumsum(padded)
        pad_start = pad_end - padded
        top_idx = route[:, 0:TOP_K].astype(jnp.int32).reshape(n * TOP_K)
        rank = route[:, 2 * TOP_K:3 * TOP_K].astype(jnp.int32).reshape(n * TOP_K)
        dest = (pad_start[top_idx] + rank).astype(jnp.int32)
        n_blocks = (n * TOP_K) // EXPERT_ROWS + N_EXPERTS
        block_start = jnp.arange(n_blocks, dtype=jnp.int32) * EXPERT_ROWS
        block_exp = jnp.minimum(jnp.sum(block_start[:, None] >= pad_end[None, :], axis=1),
                                N_EXPERTS - 1).astype(jnp.int32)
        n_used = (pad_end[-1] // EXPERT_ROWS).astype(jnp.int32)
        last_blk = jnp.where(padded > 0, pad_end // EXPERT_ROWS - 1, -1)
        spare = n_used + jnp.arange(N_EXPERTS, dtype=jnp.int32)
        spare = jnp.where(spare < n_blocks, spare, -1)
        zero_blocks = jnp.concatenate([last_blk, spare]).astype(jnp.int32)

        x_rows = _dispatch(h2, dest, zero_blocks, n_blocks * EXPERT_ROWS)
        y_rows = _experts(x_rows, block_exp, n_used.reshape(1), w_up[l], b_up[l], w_down[l], b_down[l], n_blocks)
        x2 = _combine(y_rows, dest, x1, route, mod3, S)
    return x2.reshape(B, S, D)
```
